```python
import jax, jax.numpy as jnp
from jax import lax
import numpy as np

D_MODEL = 1024
BATCH = 2
SEQ = 8192
DEPTH = 2

CHUNK = 64
EPS = 1e-6
F_MIN = 1e-6
DKA = 128
HA = D_MODEL // DKA
WA = HA * DKA
HB = 8
DKB = D_MODEL // HB
DVB = 2 * DKB
WB_QK = HB * DKB
WB_V = HB * DVB
ROPE_BASE = 10000.0
RET_GN_EPS = 1e-5
NC = 64
HC = D_MODEL // NC
WC = HC * NC
W_LORA = 64
A_LORA = 64
RWKV_GN_EPS = 64e-5
C_SHIFT = 3 * WC + 2 * W_LORA + 2 * A_LORA

IN_SIZES = (WA, WA, WA, WA, WA,
            WB_QK, WB_QK, WB_V, WB_V,
            C_SHIFT, WC,
            3 * D_MODEL)
IN_OFFSETS = tuple(int(v) for v in np.cumsum(IN_SIZES)[:-1])
N_IN = int(sum(IN_SIZES))

kernel_name = "hybrid_hgrn2_retention_rwkv7_bidir"


def rmsnorm(x, g, eps=EPS):
    xf = x.astype(jnp.float32)
    y = xf * lax.rsqrt(jnp.mean(xf * xf, axis=-1, keepdims=True) + eps)
    return y * g


def head_groupnorm(o, g, b, eps):
    of = o.astype(jnp.float32)
    mu = jnp.mean(of, axis=-1, keepdims=True)
    var = jnp.mean(jnp.square(of - mu), axis=-1, keepdims=True)
    return (of - mu) * lax.rsqrt(var + eps) * g + b


def heads(z, h):
    b, t, w = z.shape
    return z.reshape(b, t, h, w // h).transpose(0, 2, 1, 3)


def chunk_gla(q, k, v, log_f, strict):
    b, h, t, dk = q.shape
    dv = v.shape[-1]
    n = t // CHUNK

    def chunks(a):
        return a.reshape(b, h, n, CHUNK, a.shape[-1]).transpose(2, 0, 1, 3, 4)

    qc, kc, vc = chunks(q), chunks(k), chunks(v)
    bc = jnp.cumsum(chunks(log_f.astype(jnp.float32)), axis=3)
    mask = jnp.tril(jnp.ones((CHUNK, CHUNK), dtype=bool), -1 if strict else 0)[:, :, None]

    def step(state, inp):
        qi, ki, vi, bi = inp
        diff = bi[:, :, :, None, :] - bi[:, :, None, :, :]
        decay = jnp.where(mask, jnp.exp(jnp.where(mask, diff, 0.0)), 0.0)
        scores = jnp.sum(qi[:, :, :, None, :] * ki[:, :, None, :, :] * decay, axis=-1)
        o = (jnp.einsum("bhij,bhjv->bhiv", scores, vi)
             + jnp.einsum("bhik,bhkv->bhiv", qi * jnp.exp(bi), state))
        b_last = bi[:, :, -1:, :]
        state = (jnp.exp(b_last[:, :, 0, :])[..., None] * state
                 + jnp.einsum("bhjk,bhjv->bhkv", ki * jnp.exp(b_last - bi), vi))
        return state, o

    s0 = jnp.zeros((b, h, dk, dv), jnp.float32)
    _, o = lax.scan(step, s0, (qc, kc, vc, bc))
    return o.transpose(1, 2, 0, 3, 4).reshape(b, h, t, dv)


def hgrn2_mixer(q, f_fwd, f_bwd, i, lb, norm_g):
    b, t, _ = q.shape
    qh = heads(jax.nn.silu(q), HA)
    ih = heads(i, HA)
    o = 0.0
    for d, f_raw in enumerate((f_fwd, f_bwd)):
        f_raw = f_raw.astype(jnp.float32)
        f = lb[d] + (1.0 - lb[d]) * jax.nn.sigmoid(f_raw)
        log_f = jnp.log(jnp.maximum(f, F_MIN))
        k = (1.0 - lb[d]) * jax.nn.sigmoid(-f_raw)
        args = (qh, heads(k, HA), ih, heads(log_f, HA))
        if d == 0:
            o = o + chunk_gla(*args, strict=False)
        else:
            o = o + jnp.flip(chunk_gla(*[jnp.flip(z, 2) for z in args], strict=False), 2)
    o = rmsnorm(o.transpose(0, 2, 1, 3), norm_g)
    return o.reshape(b, t, WA).astype(q.dtype)


def rotary(z):
    t, dh = z.shape[2], z.shape[3]
    inv = ROPE_BASE ** (-jnp.arange(0, dh, 2, dtype=jnp.float32) / dh)
    ang = jnp.arange(t, dtype=jnp.float32)[:, None] * inv
    cos, sin = jnp.cos(ang), jnp.sin(ang)
    z1, z2 = z[..., : dh // 2], z[..., dh // 2:]
    return jnp.concatenate([z1 * cos - z2 * sin, z2 * cos + z1 * sin], axis=-1).astype(z.dtype)


def retention_mixer(q, k, v, gn_g, gn_b):
    b, t, _ = q.shape
    qh = rotary(heads(q, HB))
    kh = rotary(heads(k, HB)) * (DKB ** -0.5)
    vh = heads(v, HB)
    log_gamma = jnp.log1p(-jnp.exp2(-5.0 - jnp.arange(HB, dtype=jnp.float32)))
    log_f = jnp.broadcast_to(log_gamma[None, :, None, None], (b, HB, t, 1))
    fwd = chunk_gla(qh, kh, vh, log_f, strict=False)
    bwd = jnp.flip(chunk_gla(jnp.flip(qh, 2), jnp.flip(kh, 2), jnp.flip(vh, 2), log_f, strict=True), 2)
    o = head_groupnorm((fwd + bwd).transpose(0, 2, 1, 3), gn_g, gn_b, RET_GN_EPS)
    return o.reshape(b, t, WB_V).astype(q.dtype)


def centred_shift(p, mu):
    prev = jnp.pad(p[:, :-1], ((0, 0), (1, 0), (0, 0)))
    nxt = jnp.pad(p[:, 1:], ((0, 0), (0, 1), (0, 0)))
    return p + mu[0] * (prev - p) + mu[1] * (nxt - p)


def rwkv7_scan(r, decay, k, v, kk, a, reverse):
    def tm(z):
        return jnp.moveaxis(z.astype(jnp.float32), 1, 0)

    def step(S, inp):
        r_t, w_t, k_t, v_t, kk_t, a_t = inp
        sa = jnp.einsum("bhvk,bhk->bhv", S, -kk_t)
        S = (S * w_t[:, :, None, :]
             + sa[..., None] * (kk_t * a_t)[:, :, None, :]
             + v_t[..., None] * k_t[:, :, None, :])
        return S, jnp.einsum("bhvk,bhk->bhv", S, r_t)

    s0 = jnp.zeros((r.shape[0], HC, NC, NC), jnp.float32)
    _, o = lax.scan(step, s0, tuple(tm(z) for z in (r, decay, k, v, kk, a)), reverse=reverse)
    return jnp.moveaxis(o, 0, 1)


def rwkv7_mixer(c_s, mu, w0, w_lora_b, a0, a_lora_b, k_k, k_a, r_k, ln_g, ln_b):
    b, t, _ = c_s.shape
    s = centred_shift(c_s, mu)
    r, k, v, codes = jnp.split(s, (WC, 2 * WC, 3 * WC), axis=-1)
    w_code = codes[..., : 2 * W_LORA].reshape(b, t, 2, W_LORA)
    a_code = codes[..., 2 * W_LORA:].reshape(b, t, 2, A_LORA)

    def hd(z):
        return z.reshape(b, t, HC, NC)

    kk = hd(k * k_k).astype(jnp.float32)
    kk = kk * lax.rsqrt(jnp.sum(kk * kk, axis=-1, keepdims=True) + 1e-12)
    o = 0.0
    bonus = 0.0
    for d in range(2):
        w = (w0[d] + jnp.tanh(w_code[:, :, d]) @ w_lora_b[d]).astype(jnp.float32)
        decay = jnp.exp(-jnp.exp(-jax.nn.softplus(-w) - 0.5))
        a = jax.nn.sigmoid((a0[d] + a_code[:, :, d] @ a_lora_b[d]).astype(jnp.float32))
        k_d = k * (1.0 + (a - 1.0) * k_a)
        o = o + rwkv7_scan(hd(r), hd(decay), hd(k_d), hd(v), kk, hd(a), reverse=(d == 1))
        bonus = bonus + jnp.sum(hd(r * k_d * r_k), axis=-1, keepdims=True) * hd(v)
    y = head_groupnorm(o, ln_g, ln_b, RWKV_GN_EPS) + bonus
    return y.reshape(b, t, WC).astype(c_s.dtype)


def setup_inputs(seed: int = 0) -> dict:
    key = jax.random.key(seed)
    ks = jax.random.split(key, 24)
    f32 = jnp.float32

    def nrm(k, shape, scale):
        return scale * jax.random.normal(k, shape, f32)

    def gain(k, shape):
        return 1.0 + nrm(k, shape, 0.02)

    return {
        "x": nrm(ks[0], (BATCH, SEQ, D_MODEL), 1.0),
        "norm_g": gain(ks[1], (DEPTH, D_MODEL)),
        "w_in": nrm(ks[2], (DEPTH, D_MODEL, N_IN), D_MODEL ** -0.5),
        "hgrn_lb_logits": nrm(ks[3], (DEPTH, 2, WA), 0.1),
        "hgrn_norm_g": gain(ks[4], (DEPTH, HA, DKA)),
        "ret_norm_g": gain(ks[5], (DEPTH, HB, DVB)),
        "ret_norm_b": nrm(ks[6], (DEPTH, HB, DVB), 0.02),
        "rwkv_mu": jax.random.uniform(ks[7], (DEPTH, 2, C_SHIFT), f32, 0.0, 0.5),
        "rwkv_w0": jax.random.uniform(ks[8], (DEPTH, 2, WC), f32, -6.0, 1.0),
        "rwkv_w_lora_b": nrm(ks[9], (DEPTH, 2, W_LORA, WC), 0.5 * W_LORA ** -0.5),
        "rwkv_a0": nrm(ks[10], (DEPTH, 2, WC), 0.1),
        "rwkv_a_lora_b": nrm(ks[11], (DEPTH, 2, A_LORA, WC), 0.5 * A_LORA ** -0.5),
        "rwkv_k_k": 0.85 + nrm(ks[12], (DEPTH, WC), 0.02),
        "rwkv_k_a": 1.0 + nrm(ks[13], (DEPTH, WC), 0.02),
        "rwkv_r_k": nrm(ks[14], (DEPTH, WC), 0.1),
        "rwkv_ln_g": gain(ks[15], (DEPTH, HC, NC)),
        "rwkv_ln_b": nrm(ks[16], (DEPTH, HC, NC), 0.02),
        "w_branch_a": nrm(ks[17], (DEPTH, WA, D_MODEL), WA ** -0.5),
        "w_branch_b": nrm(ks[18], (DEPTH, WB_V, D_MODEL), WB_V ** -0.5),
        "w_branch_c": nrm(ks[19], (DEPTH, WC, D_MODEL), WC ** -0.5),
        "w_out": nrm(ks[20], (DEPTH, D_MODEL, D_MODEL), D_MODEL ** -0.5),
        "final_norm_g": gain(ks[21], (D_MODEL,)),
    }


def reference(x, norm_g, w_in, hgrn_lb_logits, hgrn_norm_g, ret_norm_g, ret_norm_b,
              rwkv_mu, rwkv_w0, rwkv_w_lora_b, rwkv_a0, rwkv_a_lora_b, rwkv_k_k, rwkv_k_a,
              rwkv_r_k, rwkv_ln_g, rwkv_ln_b, w_branch_a, w_branch_b, w_branch_c, w_out,
              final_norm_g):
    b, t, _ = x.shape
    p_lb = jax.nn.softmax(hgrn_lb_logits.astype(jnp.float32), axis=0)
    lbs = jnp.cumsum(p_lb, axis=0) - p_lb[0:1]
    for l in range(DEPTH):
        h = rmsnorm(x, norm_g[l]).astype(x.dtype)
        proj = h @ w_in[l]
        (a_q, a_ff, a_fb, a_i, a_g, b_q, b_k, b_v, b_g, c_s, c_g, merge) = jnp.split(
            proj, IN_OFFSETS, axis=-1)
        y_a = hgrn2_mixer(a_q, a_ff, a_fb, a_i, lbs[l], hgrn_norm_g[l]) * jax.nn.silu(a_g)
        y_b = retention_mixer(b_q, b_k, b_v, ret_norm_g[l], ret_norm_b[l]) * jax.nn.silu(b_g)
        y_c = rwkv7_mixer(c_s, rwkv_mu[l], rwkv_w0[l], rwkv_w_lora_b[l], rwkv_a0[l],
                          rwkv_a_lora_b[l], rwkv_k_k[l], rwkv_k_a[l], rwkv_r_k[l],
                          rwkv_ln_g[l], rwkv_ln_b[l]) * jax.nn.silu(c_g)
        gates = jax.nn.sigmoid(merge.reshape(b, t, 3, D_MODEL))
        mixed = (gates[:, :, 0] * (y_a @ w_branch_a[l])
                 + gates[:, :, 1] * (y_b @ w_branch_b[l])
                 + gates[:, :, 2] * (y_c @ w_branch_c[l]))
        x = x + (mixed @ w_out[l]).astype(x.dtype)
    return rmsnorm(x, final_norm_g).astype(x.dtype)
```

```python
import functools

import numpy as np
import jax
import jax.numpy as jnp
from jax import lax
from jax.experimental import pallas as pl
from jax.experimental.pallas import tpu as pltpu

F32 = jnp.float32
BF16 = jnp.bfloat16

D_MODEL = 1024
EPS = 1e-6
F_MIN = 1e-6
DKA = 128
HA = D_MODEL // DKA
WA = HA * DKA
HB = 8
DKB = D_MODEL // HB
DVB = 2 * DKB
WB_QK = HB * DKB
WB_V = HB * DVB
ROPE_BASE = 10000.0
RET_GN_EPS = 1e-5
NC = 64
HC = D_MODEL // NC
WC = HC * NC
W_LORA = 64
A_LORA = 64
RWKV_GN_EPS = 64e-5
N_CODES = 2 * W_LORA + 2 * A_LORA
C_SHIFT = 3 * WC + N_CODES

SRC_A = 0
SRC_B = 5 * WA
SRC_CS = SRC_B + 2 * WB_QK + 2 * WB_V
SRC_CG = SRC_CS + C_SHIFT
SRC_MERGE = SRC_CG + WC
N_IN = SRC_MERGE + 3 * D_MODEL

OFF_MERGE = 0
OFF_AQ = OFF_MERGE + 3 * D_MODEL
OFF_AFF = OFF_AQ + WA
OFF_AFB = OFF_AFF + WA
OFF_AI = OFF_AFB + WA
OFF_AG = OFF_AI + WA
OFF_BQ = OFF_AG + WA
OFF_BK = OFF_BQ + WB_QK
OFF_BV = OFF_BK + WB_QK
OFF_BG = OFF_BV + WB_V
OFF_CG = OFF_BG + WB_V
OFF_CR = OFF_CG + WC
OFF_CK = OFF_CR + WC
OFF_CV = OFF_CK + WC
OFF_CC = OFF_CV + WC
LANE = 128
SUBLANE = 8
PROJ_TN = 512
N_PAD = -(-(OFF_CC + N_CODES) // PROJ_TN) * PROJ_TN

CHUNK_A = 16
CHUNK_B = 256
CHUNK_C = 64
VMEM_LIMIT = 48 * 1024 * 1024

NT_DIMS = (((1,), (1,)), ((), ()))
TN_DIMS = (((0,), (0,)), ((), ()))


def _mm(a, b):
    return jnp.dot(a.astype(BF16), b.astype(BF16), preferred_element_type=F32)


def _mm_nt(a, b):
    return lax.dot_general(a.astype(BF16), b.astype(BF16), NT_DIMS, preferred_element_type=F32)


def _mm_tn(a, b):
    return lax.dot_general(a.astype(BF16), b.astype(BF16), TN_DIMS, preferred_element_type=F32)


def _split_dot(ones_mat, x):
    hi = x.astype(BF16)
    r1 = x - hi.astype(F32)
    mid = r1.astype(BF16)
    lo = (r1 - mid.astype(F32)).astype(BF16)
    out = jnp.dot(ones_mat, hi, preferred_element_type=F32)
    out = out + jnp.dot(ones_mat, mid, preferred_element_type=F32)
    return out + jnp.dot(ones_mat, lo, preferred_element_type=F32)


def _split_dot_right(x, ones_mat):
    hi = x.astype(BF16)
    r1 = x - hi.astype(F32)
    mid = r1.astype(BF16)
    lo = (r1 - mid.astype(F32)).astype(BF16)
    out = jnp.dot(hi, ones_mat, preferred_element_type=F32)
    out = out + jnp.dot(mid, ones_mat, preferred_element_type=F32)
    return out + jnp.dot(lo, ones_mat, preferred_element_type=F32)


def _silu(z):
    return z * jax.nn.sigmoid(z)


def _pick(n, pref):
    t = min(n, pref)
    assert n % t == 0, (n, pref)
    return t


def _params(*sem):
    return pltpu.CompilerParams(dimension_semantics=sem, vmem_limit_bytes=VMEM_LIMIT)


def _inproj_kernel(x_ref, g_ref, w_ref, o_ref, h_ref):
    @pl.when(pl.program_id(1) == 0)
    def _():
        x = x_ref[...]
        ms = jnp.mean(x * x, axis=-1, keepdims=True)
        h_ref[...] = (x * lax.rsqrt(ms + EPS) * g_ref[...]).astype(BF16)

    o_ref[...] = jnp.dot(h_ref[...], w_ref[...], preferred_element_type=F32)


def _inproj(x2, g, w_bf16):
    m = x2.shape[0]
    tm = _pick(m, 2048)
    return pl.pallas_call(
        _inproj_kernel,
        grid=(m // tm, N_PAD // PROJ_TN),
        in_specs=[
            pl.BlockSpec((tm, D_MODEL), lambda i, j: (i, 0)),
            pl.BlockSpec((1, D_MODEL), lambda i, j: (0, 0)),
            pl.BlockSpec((D_MODEL, PROJ_TN), lambda i, j: (0, j)),
        ],
        out_specs=pl.BlockSpec((tm, PROJ_TN), lambda i, j: (i, j)),
        out_shape=jax.ShapeDtypeStruct((m, N_PAD), F32),
        scratch_shapes=[pltpu.VMEM((tm, D_MODEL), BF16)],
        compiler_params=_params("parallel", "arbitrary"),
    )(x2, g.reshape(1, D_MODEL), w_bf16)


def _permute_w_in(w):
    parts = [w[:, SRC_MERGE:N_IN], w[:, SRC_A:SRC_CS], w[:, SRC_CG:SRC_MERGE], w[:, SRC_CS:SRC_CG]]
    wp = jnp.concatenate(parts, axis=1)
    return jnp.pad(wp, ((0, 0), (0, N_PAD - N_IN))).astype(BF16)


def _hgrn_kernel(rev, final, tt, q_ref, f_ref, v_ref, lb_ref, tri_ref, *rest):
    if final:
        ob_ref, gate_ref, ng_ref, y_ref, st_ref, q_s, k_s, b_s, o_s = rest
    else:
        o_ref, st_ref, q_s, k_s, b_s, o_s = rest

    @pl.when(pl.program_id(2) == 0)
    def _():
        st_ref[...] = jnp.zeros_like(st_ref)

    lb = lb_ref[0]
    fr = f_ref[0]
    sig = jax.nn.sigmoid(fr)
    lf = jnp.log(jnp.maximum(lb + (1.0 - lb) * sig, F_MIN))
    k_s[...] = (1.0 - lb) * (1.0 - sig)
    q_s[...] = _silu(q_ref[0])
    for i in range(tt // LANE):
        b_s[i * LANE:(i + 1) * LANE, :] = _split_dot(tri_ref[...], lf[i * LANE:(i + 1) * LANE, :])

    rows = lax.broadcasted_iota(jnp.int32, (CHUNK_A, 1), 0)
    nchunk = tt // CHUNK_A

    def chunk(ci, carry):
        c = (nchunk - 1 - ci) if rev else ci
        r0 = pl.multiple_of(c * CHUNK_A, CHUNK_A)
        qc = q_s[pl.ds(r0, CHUNK_A), :]
        kc = k_s[pl.ds(r0, CHUNK_A), :]
        bc = b_s[pl.ds(r0, CHUNK_A), :]
        vc = v_ref[0, pl.ds(r0, CHUNK_A), :]
        st = st_ref[...]
        o = _mm_nt(qc * jnp.exp(bc), st)
        for j in range(CHUNK_A):
            bj = b_s[pl.ds(r0 + j, 1), :]
            kj = k_s[pl.ds(r0 + j, 1), :]
            vj = v_ref[0, pl.ds(r0 + j, 1), :]
            valid = (rows <= j) if rev else (rows >= j)
            w = jnp.where(valid, jnp.exp(jnp.where(valid, bc - bj, 0.0)), 0.0)
            s = jnp.sum(qc * w * kj, axis=-1, keepdims=True)
            o = o + s * vj
        b_end = b_s[pl.ds(r0 + (0 if rev else CHUNK_A - 1), 1), :]
        ke = kc * jnp.exp(b_end - bc)
        st_ref[...] = st * jnp.exp(b_end) + _mm_tn(vc, ke)
        o_s[pl.ds(r0, CHUNK_A), :] = o
        return carry

    lax.fori_loop(0, nchunk, chunk, 0)

    if final:
        o = o_s[...] + ob_ref[0]
        ms = jnp.mean(o * o, axis=-1, keepdims=True)
        y = o * lax.rsqrt(ms + EPS) * ng_ref[0] * _silu(gate_ref[0])
        y_ref[0] = y.astype(y_ref.dtype)
    else:
        o_ref[0] = o_s[...]


def _tri_blockdiag(n, blk, rev):
    i = np.arange(n)
    same = (i[:, None] // blk) == (i[None, :] // blk)
    tri = (i[None, :] >= i[:, None]) if rev else (i[None, :] <= i[:, None])
    return jnp.asarray((same & tri).astype(np.float32), dtype=BF16)


def _hgrn_pass(proj3, lb, rev, final, ob=None, norm_g=None):
    b, t, _ = proj3.shape
    tt = _pick(t, 512)
    nt = t // tt
    tmap = (lambda i: nt - 1 - i) if rev else (lambda i: i)

    def col(off):
        return pl.BlockSpec((1, tt, DKA), lambda bi, h, ti: (bi, tmap(ti), off // DKA + h))

    head_vec = pl.BlockSpec((1, 1, DKA), lambda bi, h, ti: (h, 0, 0))
    in_specs = [col(OFF_AQ), col(OFF_AFB if rev else OFF_AFF), col(OFF_AI), head_vec,
                pl.BlockSpec((LANE, LANE), lambda bi, h, ti: (0, 0))]
    args = [proj3, proj3, proj3, lb.reshape(HA, 1, DKA), _tri_blockdiag(LANE, CHUNK_A, rev)]
    out_block = pl.BlockSpec((1, tt, DKA), lambda bi, h, ti: (bi, tmap(ti), h))
    if final:
        in_specs += [out_block, col(OFF_AG), head_vec]
        args += [ob, proj3, norm_g.reshape(HA, 1, DKA)]
        out_dtype = BF16
    else:
        out_dtype = F32
    return pl.pallas_call(
        functools.partial(_hgrn_kernel, rev, final, tt),
        grid=(b, HA, nt),
        in_specs=in_specs,
        out_specs=out_block,
        out_shape=jax.ShapeDtypeStruct((b, t, WA), out_dtype),
        scratch_shapes=[pltpu.VMEM((DKA, DKA), F32)] + [pltpu.VMEM((tt, DKA), F32)] * 4,
        compiler_params=_params("parallel", "parallel", "arbitrary"),
    )(*args)


def _hgrn_mixer(proj3, lbs, norm_g):
    ob = _hgrn_pass(proj3, lbs[1], rev=True, final=False)
    return _hgrn_pass(proj3, lbs[0], rev=False, final=True, ob=ob, norm_g=norm_g)


def _rotary(z, cos2, sin2):
    return z * cos2 + pltpu.roll(z, DKB // 2, axis=1) * sin2


def _ret_state_kernel(k_ref, v_ref, cos_ref, sin_ref, lg_ref, sb_ref, st_ref):
    @pl.when(pl.program_id(2) == 0)
    def _():
        st_ref[...] = jnp.zeros_like(st_ref)

    st = st_ref[...]
    sb_ref[0, 0, 0] = st
    lg = lg_ref[0][:, :DKB]
    pos = lax.broadcasted_iota(jnp.int32, (CHUNK_B, DKB), 0).astype(F32)
    k = _rotary(k_ref[0], cos_ref[...], sin_ref[...]) * (DKB ** -0.5)
    kd = k * jnp.exp(pos * lg)
    st_ref[...] = st * jnp.exp(CHUNK_B * lg_ref[0]) + _mm_tn(kd, v_ref[0])


def _ret_out_kernel(q_ref, k_ref, v_ref, cos_ref, sin_ref, lg_ref, sb_ref, gate_ref, gg_ref,
                    gb_ref, y_ref, st_ref):
    @pl.when(pl.program_id(2) == 0)
    def _():
        st_ref[...] = jnp.zeros_like(st_ref)

    lg2 = lg_ref[0]
    lg = lg2[:, :DKB]
    cos2 = cos_ref[...]
    sin2 = sin_ref[...]
    q = _rotary(q_ref[0], cos2, sin2)
    k = _rotary(k_ref[0], cos2, sin2) * (DKB ** -0.5)
    v = v_ref[0]
    pos = lax.broadcasted_iota(jnp.int32, (CHUNK_B, DKB), 0).astype(F32)
    ri = lax.broadcasted_iota(jnp.int32, (CHUNK_B, CHUNK_B), 0)
    ci = lax.broadcasted_iota(jnp.int32, (CHUNK_B, CHUNK_B), 1)
    dist = jnp.abs(ri - ci).astype(F32)
    scores = _mm_nt(q, k) * jnp.exp(dist * lg2)
    sf = st_ref[...]
    o = _mm(scores, v)
    o = o + _mm(q * jnp.exp((pos + 1.0) * lg), sf)
    o = o + _mm(q * jnp.exp((CHUNK_B - pos) * lg), sb_ref[0, 0, 0])
    st_ref[...] = sf * jnp.exp(CHUNK_B * lg2) + _mm_tn(k * jnp.exp((CHUNK_B - 1.0 - pos) * lg), v)

    mu = jnp.mean(o, axis=-1, keepdims=True)
    cen = o - mu
    var = jnp.mean(cen * cen, axis=-1, keepdims=True)
    y = (cen * lax.rsqrt(var + RET_GN_EPS) * gg_ref[0] + gb_ref[0]) * _silu(gate_ref[0])
    y_ref[0] = y.astype(y_ref.dtype)


def _ret_mixer(proj3, gn_g, gn_b):
    b, t, _ = proj3.shape
    cb = CHUNK_B
    assert t % cb == 0
    nc = t // cb
    half = DKB // 2
    inv = ROPE_BASE ** (-jnp.arange(0, DKB, 2, dtype=F32) / DKB)
    ang = jnp.arange(t, dtype=F32)[:, None] * inv
    cos2 = jnp.concatenate([jnp.cos(ang), jnp.cos(ang)], axis=-1)
    sin2 = jnp.concatenate([-jnp.sin(ang), jnp.sin(ang)], axis=-1)
    assert cos2.shape == (t, 2 * half)
    log_gamma = jnp.log1p(-jnp.exp2(-5.0 - jnp.arange(HB, dtype=F32)))
    lg = jnp.broadcast_to(log_gamma[:, None, None], (HB, 1, DVB))

    def col(off, width, tmap):
        return pl.BlockSpec((1, cb, width), lambda bi, h, c: (bi, tmap(c), off // width + h))

    def tab(tmap):
        return pl.BlockSpec((cb, DKB), lambda bi, h, c: (tmap(c), 0))

    lg_spec = pl.BlockSpec((1, 1, DVB), lambda bi, h, c: (h, 0, 0))
    rmap = lambda c: nc - 1 - c
    fmap = lambda c: c

    sb = pl.pallas_call(
        _ret_state_kernel,
        grid=(b, HB, nc),
        in_specs=[col(OFF_BK, DKB, rmap), col(OFF_BV, DVB, rmap), tab(rmap), tab(rmap), lg_spec],
        out_specs=pl.BlockSpec((1, 1, 1, DKB, DVB), lambda bi, h, c: (bi, h, rmap(c), 0, 0)),
        out_shape=jax.ShapeDtypeStruct((b, HB, nc, DKB, DVB), F32),
        scratch_shapes=[pltpu.VMEM((DKB, DVB), F32)],
        compiler_params=_params("parallel", "parallel", "arbitrary"),
    )(proj3, proj3, cos2, sin2, lg)

    head_vec = pl.BlockSpec((1, 1, DVB), lambda bi, h, c: (h, 0, 0))
    return pl.pallas_call(
        _ret_out_kernel,
        grid=(b, HB, nc),
        in_specs=[col(OFF_BQ, DKB, fmap), col(OFF_BK, DKB, fmap), col(OFF_BV, DVB, fmap),
                  tab(fmap), tab(fmap), lg_spec,
                  pl.BlockSpec((1, 1, 1, DKB, DVB), lambda bi, h, c: (bi, h, c, 0, 0)),
                  col(OFF_BG, DVB, fmap), head_vec, head_vec],
        out_specs=pl.BlockSpec((1, cb, DVB), lambda bi, h, c: (bi, c, h)),
        out_shape=jax.ShapeDtypeStruct((b, t, WB_V), BF16),
        scratch_shapes=[pltpu.VMEM((DKB, DVB), F32)],
        compiler_params=_params("parallel", "parallel", "arbitrary"),
    )(proj3, proj3, proj3, cos2, sin2, lg, sb, proj3,
      gn_g.reshape(HB, 1, DVB), gn_b.reshape(HB, 1, DVB))


def _group_sum(x, gmat):
    cols = [_split_dot_right(x[:, i * LANE:(i + 1) * LANE], gmat) for i in range(x.shape[1] // LANE)]
    return jnp.concatenate(cols, axis=1)


def _rwkv_prep_kernel(tt, r_ref, k_ref, v_ref, c_ref, rp_ref, kp_ref, vp_ref, cp_ref,
                      rn_ref, kn_ref, vn_ref, cn_ref, mur_ref, muk_ref, muv_ref, muc_ref,
                      w0_ref, wl_ref, a0_ref, al_ref, kk_ref, ka_ref, rk_ref, gmat_ref,
                      r_o, v_o, kk_o, bonus_o, lw0_o, lw1_o, k0_o, k1_o, ka0_o, ka1_o):
    ti = pl.program_id(1)
    has_prev = (ti > 0).astype(F32)
    has_next = (ti < pl.num_programs(1) - 1).astype(F32)
    rows = lax.broadcasted_iota(jnp.int32, (tt, 1), 0)

    def shifted(p_ref, prev_ref, next_ref, mu_ref):
        p = p_ref[0]
        prev_row = prev_ref[0, SUBLANE - 1:SUBLANE, :] * has_prev
        next_row = next_ref[0, 0:1, :] * has_next
        before = jnp.where(rows == 0, prev_row, pltpu.roll(p, 1, axis=0))
        after = jnp.where(rows == tt - 1, next_row, pltpu.roll(p, tt - 1, axis=0))
        return p + mu_ref[0:1, :] * (before - p) + mu_ref[1:2, :] * (after - p)

    r = shifted(r_ref, rp_ref, rn_ref, mur_ref)
    k = shifted(k_ref, kp_ref, kn_ref, muk_ref)
    v = shifted(v_ref, vp_ref, vn_ref, muv_ref)
    codes = shifted(c_ref, cp_ref, cn_ref, muc_ref)
    gmat = gmat_ref[...]

    kraw = k * kk_ref[...]
    kk = kraw * lax.rsqrt(_group_sum(kraw * kraw, gmat) + 1e-12)
    r_o[0] = r
    v_o[0] = v
    kk_o[0] = kk
    tcodes = jnp.tanh(codes)
    bonus = jnp.zeros_like(r)
    for d, (lw_o, kd_o, ka_o) in enumerate(((lw0_o, k0_o, ka0_o), (lw1_o, k1_o, ka1_o))):
        w = w0_ref[d:d + 1, :] + _mm(tcodes, wl_ref[d])
        lw_o[0] = -np.exp(-0.5).astype(np.float32) * jax.nn.sigmoid(w)
        a = jax.nn.sigmoid(a0_ref[d:d + 1, :] + _mm(codes, al_ref[d]))
        kd = k * (1.0 + (a - 1.0) * ka_ref[...])
        kd_o[0] = kd
        ka_o[0] = kk * a
        bonus = bonus + _group_sum(r * kd * rk_ref[...], gmat) * v
    bonus_o[0] = bonus


def _rwkv_scan_kernel(rev, final, tt, r_ref, lw_ref, k_ref, v_ref, ka_ref, kk_ref, tri_ref, *rest):
    if final:
        ob_ref, bonus_ref, gate_ref, lng_ref, lnb_ref, y_ref, h_ref, o_s = rest
    else:
        o_ref, h_ref, o_s = rest

    @pl.when(pl.program_id(2) == 0)
    def _():
        h_ref[...] = jnp.zeros_like(h_ref)

    n2 = 2 * CHUNK_C
    lane = lax.broadcasted_iota(jnp.int32, (1, LANE), 1)
    head0 = lane < NC
    ri = lax.broadcasted_iota(jnp.int32, (n2, n2), 0)
    ci = lax.broadcasted_iota(jnp.int32, (n2, n2), 1)
    strict = (ci > ri) if rev else (ci < ri)
    incl = (ci >= ri) if rev else (ci <= ri)
    eye = ri == ci
    nchunk = tt // CHUNK_C

    def stack(z):
        return jnp.concatenate([jnp.where(head0, z, 0.0), jnp.where(head0, 0.0, z)], axis=0)

    def chunk(ci_, carry):
        c = (nchunk - 1 - ci_) if rev else ci_
        r0 = pl.multiple_of(c * CHUNK_C, CHUNK_C)
        sl = pl.ds(r0, CHUNK_C)
        lw = lw_ref[0, sl, :]
        g = _split_dot(tri_ref[...], lw)
        g_end = g[0:1, :] if rev else g[CHUNK_C - 1:CHUNK_C, :]
        e_neg = jnp.exp(-g)
        kk = kk_ref[0, sl, :]
        bs = stack(-kk * jnp.exp(g - lw))
        as_ = stack(ka_ref[0, sl, :] * e_neg)
        ks = stack(k_ref[0, sl, :] * e_neg)
        rs = stack(r_ref[0, sl, :] * jnp.exp(g))
        vs = stack(v_ref[0, sl, :])

        a_ab = jnp.where(strict, _mm_nt(bs, as_), 0.0)
        a_ak = jnp.where(strict, _mm_nt(bs, ks), 0.0)
        m_ra = jnp.where(incl, _mm_nt(rs, as_), 0.0)
        m_rk = jnp.where(incl, _mm_nt(rs, ks), 0.0)
        tinv = jnp.where(eye, 1.0, 0.0) + a_ab
        pw = a_ab
        for _ in range(int(np.log2(CHUNK_C)) - 1):
            pw = _mm(pw, pw)
            tinv = tinv + _mm(tinv, pw)
        p = _mm(tinv, bs)
        wv = _mm(tinv, _mm(a_ak, vs))
        q_eff = rs + _mm(m_ra, p)
        o_in = _mm(m_ra, wv) + _mm(m_rk, vs)
        e_end = jnp.exp(g_end)
        as_end = as_ * e_end
        gmat = jnp.where(eye, e_end, 0.0) + _mm_tn(as_end, p)
        dmat = _mm_tn(as_end, wv) + _mm_tn(ks * e_end, vs)
        h = h_ref[...]
        os_ = _mm(q_eff, h) + o_in
        h_ref[...] = _mm(gmat, h) + dmat
        o_s[sl, :] = os_[:CHUNK_C, :] + os_[CHUNK_C:, :]
        return carry

    lax.fori_loop(0, nchunk, chunk, 0)

    if final:
        o = o_s[...] + ob_ref[0]
        inv_n = 1.0 / NC
        s0 = jnp.sum(jnp.where(head0, o, 0.0), axis=-1, keepdims=True)
        s1 = jnp.sum(jnp.where(head0, 0.0, o), axis=-1, keepdims=True)
        cen = o - jnp.where(head0, s0, s1) * inv_n
        c2 = cen * cen
        v0 = jnp.sum(jnp.where(head0, c2, 0.0), axis=-1, keepdims=True)
        v1 = jnp.sum(jnp.where(head0, 0.0, c2), axis=-1, keepdims=True)
        var = jnp.where(head0, v0, v1) * inv_n
        y = cen * lax.rsqrt(var + RWKV_GN_EPS) * lng_ref[...] + lnb_ref[...] + bonus_ref[0]
        y_ref[0] = (y * _silu(gate_ref[0])).astype(y_ref.dtype)
    else:
        o_ref[0] = o_s[...]


def _rwkv_mixer(proj3, mu, w0, w_lora_b, a0, a_lora_b, k_k, k_a, r_k, ln_g, ln_b):
    b, t, _ = proj3.shape
    tt = _pick(t, 256)
    nt = t // tt
    sub = tt // SUBLANE

    def cur(off, width):
        return pl.BlockSpec((1, tt, width), lambda bi, ti: (bi, ti, off // width))

    def prev(off, width):
        return pl.BlockSpec((1, SUBLANE, width),
                            lambda bi, ti: (bi, jnp.maximum(ti * sub - 1, 0), off // width))

    def nxt(off, width):
        return pl.BlockSpec((1, SUBLANE, width),
                            lambda bi, ti: (bi, jnp.minimum((ti + 1) * sub, nt * sub - 1), off // width))

    def full(shape):
        return pl.BlockSpec(shape, lambda bi, ti: (0,) * len(shape))

    segs = ((OFF_CR, WC), (OFF_CK, WC), (OFF_CV, WC), (OFF_CC, N_CODES))
    in_specs = ([cur(o, w) for o, w in segs] + [prev(o, w) for o, w in segs] + [nxt(o, w) for o, w in segs]
                + [full((2, WC))] * 3 + [full((2, N_CODES))]
                + [full((2, WC)), full((2, N_CODES, WC)), full((2, WC)), full((2, N_CODES, WC))]
                + [full((1, WC))] * 3 + [full((LANE, LANE))])
    wl = jnp.zeros((2, N_CODES, WC), F32)
    al = jnp.zeros((2, N_CODES, WC), F32)
    for d in range(2):
        wl = wl.at[d, d * W_LORA:(d + 1) * W_LORA].set(w_lora_b[d])
        al = al.at[d, 2 * W_LORA + d * A_LORA:2 * W_LORA + (d + 1) * A_LORA].set(a_lora_b[d])
    lane_i = np.arange(LANE)
    gmat = jnp.asarray((lane_i[:, None] // NC == lane_i[None, :] // NC).astype(np.float32), dtype=BF16)
    row_out = pl.BlockSpec((1, tt, WC), lambda bi, ti: (bi, ti, 0))
    outs = pl.pallas_call(
        functools.partial(_rwkv_prep_kernel, tt),
        grid=(b, nt),
        in_specs=in_specs,
        out_specs=[row_out] * 10,
        out_shape=[jax.ShapeDtypeStruct((b, t, WC), F32)] * 10,
        compiler_params=_params("parallel", "parallel"),
    )(*([proj3] * 12), mu[:, :WC], mu[:, WC:2 * WC], mu[:, 2 * WC:3 * WC], mu[:, 3 * WC:],
      w0, wl.astype(BF16), a0, al.astype(BF16), k_k.reshape(1, WC), k_a.reshape(1, WC),
      r_k.reshape(1, WC), gmat)
    r, v, kk, bonus, lw0, lw1, k0, k1, ka0, ka1 = outs

    ts = _pick(t, 512)
    ns = t // ts

    def scan(rev, final, lw, kd, ka, ob=None):
        tmap = (lambda i: ns - 1 - i) if rev else (lambda i: i)
        blk = pl.BlockSpec((1, ts, LANE), lambda bi, h, ti: (bi, tmap(ti), h))
        head_vec = pl.BlockSpec((1, LANE), lambda bi, h, ti: (0, h))
        in_specs = [blk] * 6 + [pl.BlockSpec((CHUNK_C, CHUNK_C), lambda bi, h, ti: (0, 0))]
        ii = np.arange(CHUNK_C)
        tri = (ii[None, :] >= ii[:, None]) if rev else (ii[None, :] <= ii[:, None])
        args = [r, lw, kd, v, ka, kk, jnp.asarray(tri.astype(np.float32), dtype=BF16)]
        if final:
            gate = pl.BlockSpec((1, ts, LANE), lambda bi, h, ti: (bi, tmap(ti), OFF_CG // LANE + h))
            in_specs += [blk, blk, gate, head_vec, head_vec]
            args += [ob, bonus, proj3, ln_g.reshape(1, WC), ln_b.reshape(1, WC)]
        return pl.pallas_call(
            functools.partial(_rwkv_scan_kernel, rev, final, ts),
            grid=(b, HC // 2, ns),
            in_specs=in_specs,
            out_specs=blk,
            out_shape=jax.ShapeDtypeStruct((b, t, WC), BF16 if final else F32),
            scratch_shapes=[pltpu.VMEM((LANE, LANE), F32), pltpu.VMEM((ts, LANE), F32)],
            compiler_params=_params("parallel", "parallel", "arbitrary"),
        )(*args)

    ob = scan(True, False, lw1, k1, ka1)
    return scan(False, True, lw0, k0, ka0, ob=ob)


def _merge_kernel(final, ya_ref, yb_ref, yc_ref, mg_ref, x_ref, wa_ref, wb_ref, wc_ref, wo_ref,
                  fg_ref, o_ref):
    za = jnp.dot(ya_ref[...], wa_ref[...], preferred_element_type=F32)
    zb = jnp.dot(yb_ref[...], wb_ref[...], preferred_element_type=F32)
    zc = jnp.dot(yc_ref[...], wc_ref[...], preferred_element_type=F32)
    mg = mg_ref[...]
    mixed = (jax.nn.sigmoid(mg[:, :D_MODEL]) * za
             + jax.nn.sigmoid(mg[:, D_MODEL:2 * D_MODEL]) * zb
             + jax.nn.sigmoid(mg[:, 2 * D_MODEL:]) * zc)
    out = x_ref[...] + jnp.dot(mixed.astype(BF16), wo_ref[...], preferred_element_type=F32)
    if final:
        ms = jnp.mean(out * out, axis=-1, keepdims=True)
        out = out * lax.rsqrt(ms + EPS) * fg_ref[...]
    o_ref[...] = out


def _merge(ya, yb, yc, proj, x2, wa, wb, wc, wo, fg, final):
    m = x2.shape[0]
    tm = _pick(m, 256)

    def rows(width):
        return pl.BlockSpec((tm, width), lambda i: (i, 0))

    def full(shape):
        return pl.BlockSpec(shape, lambda i: (0, 0))

    return pl.pallas_call(
        functools.partial(_merge_kernel, final),
        grid=(m // tm,),
        in_specs=[rows(WA), rows(WB_V), rows(WC), rows(3 * D_MODEL), rows(D_MODEL),
                  full((WA, D_MODEL)), full((WB_V, D_MODEL)), full((WC, D_MODEL)),
                  full((D_MODEL, D_MODEL)), full((1, D_MODEL))],
        out_specs=rows(D_MODEL),
        out_shape=jax.ShapeDtypeStruct((m, D_MODEL), F32),
        compiler_params=_params("parallel"),
    )(ya, yb, yc, proj, x2, wa.astype(BF16), wb.astype(BF16), wc.astype(BF16), wo.astype(BF16),
      fg.reshape(1, D_MODEL))


def kernel(x, norm_g, w_in, hgrn_lb_logits, hgrn_norm_g, ret_norm_g, ret_norm_b, rwkv_mu, rwkv_w0,
           rwkv_w_lora_b, rwkv_a0, rwkv_a_lora_b, rwkv_k_k, rwkv_k_a, rwkv_r_k, rwkv_ln_g,
           rwkv_ln_b, w_branch_a, w_branch_b, w_branch_c, w_out, final_norm_g):
    b, t, d = x.shape
    m = b * t
    depth = w_in.shape[0]
    p_lb = jax.nn.softmax(hgrn_lb_logits.astype(F32), axis=0)
    lbs = jnp.cumsum(p_lb, axis=0) - p_lb[0:1]
    x2 = x.reshape(m, d)
    for l in range(depth):
        proj = _inproj(x2, norm_g[l], _permute_w_in(w_in[l]))
        proj3 = proj.reshape(b, t, N_PAD)
        ya = _hgrn_mixer(proj3, lbs[l], hgrn_norm_g[l])
        yb = _ret_mixer(proj3, ret_norm_g[l], ret_norm_b[l])
        yc = _rwkv_mixer(proj3, rwkv_mu[l], rwkv_w0[l], rwkv_w_lora_b[l], rwkv_a0[l],
                         rwkv_a_lora_b[l], rwkv_k_k[l], rwkv_k_a[l], rwkv_r_k[l],
                         rwkv_ln_g[l], rwkv_ln_b[l])
        x2 = _merge(ya.reshape(m, WA), yb.reshape(m, WB_V), yc.reshape(m, WC), proj, x2,
                    w_branch_a[l], w_branch_b[l], w_branch_c[l], w_out[l], final_norm_g,
                    final=(l == depth - 1))
    return x2.reshape(b, t, d)
```

```python
import functools

import numpy as np
import jax
import jax.numpy as jnp
from jax import lax
from jax.experimental import pallas as pl
from jax.experimental.pallas import tpu as pltpu

F32 = jnp.float32
BF16 = jnp.bfloat16

D_MODEL = 1024
EPS = 1e-6
F_MIN = 1e-6
DKA = 128
HA = D_MODEL // DKA
WA = HA * DKA
HB = 8
DKB = D_MODEL // HB
DVB = 2 * DKB
WB_QK = HB * DKB
WB_V = HB * DVB
ROPE_BASE = 10000.0
RET_GN_EPS = 1e-5
NC = 64
HC = D_MODEL // NC
WC = HC * NC
W_LORA = 64
A_LORA = 64
RWKV_GN_EPS = 64e-5
N_CODES = 2 * W_LORA + 2 * A_LORA
C_SHIFT = 3 * WC + N_CODES

SRC_A = 0
SRC_B = 5 * WA
SRC_CS = SRC_B + 2 * WB_QK + 2 * WB_V
SRC_CG = SRC_CS + C_SHIFT
SRC_MERGE = SRC_CG + WC
N_IN = SRC_MERGE + 3 * D_MODEL

OFF_MERGE = 0
OFF_AQ = OFF_MERGE + 3 * D_MODEL
OFF_AFF = OFF_AQ + WA
OFF_AFB = OFF_AFF + WA
OFF_AI = OFF_AFB + WA
OFF_AG = OFF_AI + WA
OFF_BQ = OFF_AG + WA
OFF_BK = OFF_BQ + WB_QK
OFF_BV = OFF_BK + WB_QK
OFF_BG = OFF_BV + WB_V
OFF_CG = OFF_BG + WB_V
OFF_CR = OFF_CG + WC
OFF_CK = OFF_CR + WC
OFF_CV = OFF_CK + WC
OFF_CC = OFF_CV + WC
LANE = 128
SUBLANE = 8
PROJ_TN = 512
N_PAD = -(-(OFF_CC + N_CODES) // PROJ_TN) * PROJ_TN

CHUNK_A = 16
CHUNK_B = 256
CHUNK_C = 64
UNROLL_C = 8
VMEM_LIMIT = 48 * 1024 * 1024

NT_DIMS = (((1,), (1,)), ((), ()))
TN_DIMS = (((0,), (0,)), ((), ()))


def _mm(a, b):
    return jnp.dot(a.astype(BF16), b.astype(BF16), preferred_element_type=F32)


def _mm_nt(a, b):
    return lax.dot_general(a.astype(BF16), b.astype(BF16), NT_DIMS, preferred_element_type=F32)


def _mm_tn(a, b):
    return lax.dot_general(a.astype(BF16), b.astype(BF16), TN_DIMS, preferred_element_type=F32)


def _split_dot(ones_mat, x):
    hi = x.astype(BF16)
    r1 = x - hi.astype(F32)
    mid = r1.astype(BF16)
    lo = (r1 - mid.astype(F32)).astype(BF16)
    out = jnp.dot(ones_mat, hi, preferred_element_type=F32)
    out = out + jnp.dot(ones_mat, mid, preferred_element_type=F32)
    return out + jnp.dot(ones_mat, lo, preferred_element_type=F32)


def _split_dot_right(x, ones_mat):
    hi = x.astype(BF16)
    r1 = x - hi.astype(F32)
    mid = r1.astype(BF16)
    lo = (r1 - mid.astype(F32)).astype(BF16)
    out = jnp.dot(hi, ones_mat, preferred_element_type=F32)
    out = out + jnp.dot(mid, ones_mat, preferred_element_type=F32)
    return out + jnp.dot(lo, ones_mat, preferred_element_type=F32)


def _silu(z):
    return z * jax.nn.sigmoid(z)


def _pick(n, pref):
    t = min(n, pref)
    assert n % t == 0, (n, pref)
    return t


def _params(*sem):
    return pltpu.CompilerParams(dimension_semantics=sem, vmem_limit_bytes=VMEM_LIMIT)


def _inproj_kernel(x_ref, g_ref, w_ref, o_ref, h_ref):
    @pl.when(pl.program_id(1) == 0)
    def _():
        x = x_ref[...]
        ms = jnp.mean(x * x, axis=-1, keepdims=True)
        h_ref[...] = (x * lax.rsqrt(ms + EPS) * g_ref[...]).astype(BF16)

    o_ref[...] = jnp.dot(h_ref[...], w_ref[...], preferred_element_type=F32)


def _inproj(x2, g, w_bf16):
    m = x2.shape[0]
    tm = _pick(m, 2048)
    return pl.pallas_call(
        _inproj_kernel,
        grid=(m // tm, N_PAD // PROJ_TN),
        in_specs=[
            pl.BlockSpec((tm, D_MODEL), lambda i, j: (i, 0)),
            pl.BlockSpec((1, D_MODEL), lambda i, j: (0, 0)),
            pl.BlockSpec((D_MODEL, PROJ_TN), lambda i, j: (0, j)),
        ],
        out_specs=pl.BlockSpec((tm, PROJ_TN), lambda i, j: (i, j)),
        out_shape=jax.ShapeDtypeStruct((m, N_PAD), F32),
        scratch_shapes=[pltpu.VMEM((tm, D_MODEL), BF16)],
        compiler_params=_params("parallel", "arbitrary"),
    )(x2, g.reshape(1, D_MODEL), w_bf16)


def _permute_w_in(w):
    parts = [w[:, SRC_MERGE:N_IN], w[:, SRC_A:SRC_CS], w[:, SRC_CG:SRC_MERGE], w[:, SRC_CS:SRC_CG]]
    wp = jnp.concatenate(parts, axis=1)
    return jnp.pad(wp, ((0, 0), (0, N_PAD - N_IN))).astype(BF16)


def _hgrn_kernel(rev, final, tt, q_ref, f_ref, v_ref, lb_ref, tri_ref, *rest):
    if final:
        ob_ref, gate_ref, ng_ref, y_ref, st_ref, q_s, k_s, b_s, o_s = rest
    else:
        o_ref, st_ref, q_s, k_s, b_s, o_s = rest

    @pl.when(pl.program_id(2) == 0)
    def _():
        st_ref[...] = jnp.zeros_like(st_ref)

    lb = lb_ref[0]
    fr = f_ref[0]
    sig = jax.nn.sigmoid(fr)
    lf = jnp.log(jnp.maximum(lb + (1.0 - lb) * sig, F_MIN))
    k_s[...] = (1.0 - lb) * (1.0 - sig)
    q_s[...] = _silu(q_ref[0])
    for i in range(tt // LANE):
        b_s[i * LANE:(i + 1) * LANE, :] = _split_dot(tri_ref[...], lf[i * LANE:(i + 1) * LANE, :])

    rows = lax.broadcasted_iota(jnp.int32, (CHUNK_A, 1), 0)
    nchunk = tt // CHUNK_A

    def chunk(ci, carry):
        c = (nchunk - 1 - ci) if rev else ci
        r0 = pl.multiple_of(c * CHUNK_A, CHUNK_A)
        qc = q_s[pl.ds(r0, CHUNK_A), :]
        kc = k_s[pl.ds(r0, CHUNK_A), :]
        bc = b_s[pl.ds(r0, CHUNK_A), :]
        vc = v_ref[0, pl.ds(r0, CHUNK_A), :]
        st = st_ref[...]
        o = _mm_nt(qc * jnp.exp(bc), st)
        for j in range(CHUNK_A):
            bj = b_s[pl.ds(r0 + j, 1), :]
            kj = k_s[pl.ds(r0 + j, 1), :]
            vj = v_ref[0, pl.ds(r0 + j, 1), :]
            valid = (rows <= j) if rev else (rows >= j)
            w = jnp.where(valid, jnp.exp(jnp.where(valid, bc - bj, 0.0)), 0.0)
            s = jnp.sum(qc * w * kj, axis=-1, keepdims=True)
            o = o + s * vj
        b_end = b_s[pl.ds(r0 + (0 if rev else CHUNK_A - 1), 1), :]
        ke = kc * jnp.exp(b_end - bc)
        st_ref[...] = st * jnp.exp(b_end) + _mm_tn(vc, ke)
        o_s[pl.ds(r0, CHUNK_A), :] = o
        return carry

    lax.fori_loop(0, nchunk, chunk, 0)

    if final:
        o = o_s[...] + ob_ref[0]
        ms = jnp.mean(o * o, axis=-1, keepdims=True)
        y = o * lax.rsqrt(ms + EPS) * ng_ref[0] * _silu(gate_ref[0])
        y_ref[0] = y.astype(y_ref.dtype)
    else:
        o_ref[0] = o_s[...]


def _tri_blockdiag(n, blk, rev):
    i = np.arange(n)
    same = (i[:, None] // blk) == (i[None, :] // blk)
    tri = (i[None, :] >= i[:, None]) if rev else (i[None, :] <= i[:, None])
    return jnp.asarray((same & tri).astype(np.float32), dtype=BF16)


def _hgrn_pass(proj3, lb, rev, final, ob=None, norm_g=None):
    b, t, _ = proj3.shape
    tt = _pick(t, 512)
    nt = t // tt
    tmap = (lambda i: nt - 1 - i) if rev else (lambda i: i)

    def col(off):
        return pl.BlockSpec((1, tt, DKA), lambda bi, h, ti: (bi, tmap(ti), off // DKA + h))

    head_vec = pl.BlockSpec((1, 1, DKA), lambda bi, h, ti: (h, 0, 0))
    in_specs = [col(OFF_AQ), col(OFF_AFB if rev else OFF_AFF), col(OFF_AI), head_vec,
                pl.BlockSpec((LANE, LANE), lambda bi, h, ti: (0, 0))]
    args = [proj3, proj3, proj3, lb.reshape(HA, 1, DKA), _tri_blockdiag(LANE, CHUNK_A, rev)]
    out_block = pl.BlockSpec((1, tt, DKA), lambda bi, h, ti: (bi, tmap(ti), h))
    if final:
        in_specs += [out_block, col(OFF_AG), head_vec]
        args += [ob, proj3, norm_g.reshape(HA, 1, DKA)]
        out_dtype = BF16
    else:
        out_dtype = F32
    return pl.pallas_call(
        functools.partial(_hgrn_kernel, rev, final, tt),
        grid=(b, HA, nt),
        in_specs=in_specs,
        out_specs=out_block,
        out_shape=jax.ShapeDtypeStruct((b, t, WA), out_dtype),
        scratch_shapes=[pltpu.VMEM((DKA, DKA), F32)] + [pltpu.VMEM((tt, DKA), F32)] * 4,
        compiler_params=_params("parallel", "parallel", "arbitrary"),
    )(*args)


def _hgrn_mixer(proj3, lbs, norm_g):
    ob = _hgrn_pass(proj3, lbs[1], rev=True, final=False)
    return _hgrn_pass(proj3, lbs[0], rev=False, final=True, ob=ob, norm_g=norm_g)


def _rotary(z, cos2, sin2):
    return z * cos2 + pltpu.roll(z, DKB // 2, axis=1) * sin2


def _ret_state_kernel(k_ref, v_ref, cos_ref, sin_ref, lg_ref, sb_ref, st_ref):
    @pl.when(pl.program_id(2) == 0)
    def _():
        st_ref[...] = jnp.zeros_like(st_ref)

    st = st_ref[...]
    sb_ref[0, 0, 0] = st
    lg = lg_ref[0][:, :DKB]
    pos = lax.broadcasted_iota(jnp.int32, (CHUNK_B, DKB), 0).astype(F32)
    k = _rotary(k_ref[0], cos_ref[...], sin_ref[...]) * (DKB ** -0.5)
    kd = k * jnp.exp(pos * lg)
    st_ref[...] = st * jnp.exp(CHUNK_B * lg_ref[0]) + _mm_tn(kd, v_ref[0])


def _ret_out_kernel(q_ref, k_ref, v_ref, cos_ref, sin_ref, lg_ref, sb_ref, gate_ref, gg_ref,
                    gb_ref, y_ref, st_ref):
    @pl.when(pl.program_id(2) == 0)
    def _():
        st_ref[...] = jnp.zeros_like(st_ref)

    lg2 = lg_ref[0]
    lg = lg2[:, :DKB]
    cos2 = cos_ref[...]
    sin2 = sin_ref[...]
    q = _rotary(q_ref[0], cos2, sin2)
    k = _rotary(k_ref[0], cos2, sin2) * (DKB ** -0.5)
    v = v_ref[0]
    pos = lax.broadcasted_iota(jnp.int32, (CHUNK_B, DKB), 0).astype(F32)
    ri = lax.broadcasted_iota(jnp.int32, (CHUNK_B, CHUNK_B), 0)
    ci = lax.broadcasted_iota(jnp.int32, (CHUNK_B, CHUNK_B), 1)
    dist = jnp.abs(ri - ci).astype(F32)
    scores = _mm_nt(q, k) * jnp.exp(dist * lg2)
    sf = st_ref[...]
    o = _mm(scores, v)
    o = o + _mm(q * jnp.exp((pos + 1.0) * lg), sf)
    o = o + _mm(q * jnp.exp((CHUNK_B - pos) * lg), sb_ref[0, 0, 0])
    st_ref[...] = sf * jnp.exp(CHUNK_B * lg2) + _mm_tn(k * jnp.exp((CHUNK_B - 1.0 - pos) * lg), v)

    mu = jnp.mean(o, axis=-1, keepdims=True)
    cen = o - mu
    var = jnp.mean(cen * cen, axis=-1, keepdims=True)
    y = (cen * lax.rsqrt(var + RET_GN_EPS) * gg_ref[0] + gb_ref[0]) * _silu(gate_ref[0])
    y_ref[0] = y.astype(y_ref.dtype)


def _ret_mixer(proj3, gn_g, gn_b):
    b, t, _ = proj3.shape
    cb = CHUNK_B
    assert t % cb == 0
    nc = t // cb
    half = DKB // 2
    inv = ROPE_BASE ** (-jnp.arange(0, DKB, 2, dtype=F32) / DKB)
    ang = jnp.arange(t, dtype=F32)[:, None] * inv
    cos2 = jnp.concatenate([jnp.cos(ang), jnp.cos(ang)], axis=-1)
    sin2 = jnp.concatenate([-jnp.sin(ang), jnp.sin(ang)], axis=-1)
    assert cos2.shape == (t, 2 * half)
    log_gamma = jnp.log1p(-jnp.exp2(-5.0 - jnp.arange(HB, dtype=F32)))
    lg = jnp.broadcast_to(log_gamma[:, None, None], (HB, 1, DVB))

    def col(off, width, tmap):
        return pl.BlockSpec((1, cb, width), lambda bi, h, c: (bi, tmap(c), off // width + h))

    def tab(tmap):
        return pl.BlockSpec((cb, DKB), lambda bi, h, c: (tmap(c), 0))

    lg_spec = pl.BlockSpec((1, 1, DVB), lambda bi, h, c: (h, 0, 0))
    rmap = lambda c: nc - 1 - c
    fmap = lambda c: c

    sb = pl.pallas_call(
        _ret_state_kernel,
        grid=(b, HB, nc),
        in_specs=[col(OFF_BK, DKB, rmap), col(OFF_BV, DVB, rmap), tab(rmap), tab(rmap), lg_spec],
        out_specs=pl.BlockSpec((1, 1, 1, DKB, DVB), lambda bi, h, c: (bi, h, rmap(c), 0, 0)),
        out_shape=jax.ShapeDtypeStruct((b, HB, nc, DKB, DVB), F32),
        scratch_shapes=[pltpu.VMEM((DKB, DVB), F32)],
        compiler_params=_params("parallel", "parallel", "arbitrary"),
    )(proj3, proj3, cos2, sin2, lg)

    head_vec = pl.BlockSpec((1, 1, DVB), lambda bi, h, c: (h, 0, 0))
    return pl.pallas_call(
        _ret_out_kernel,
        grid=(b, HB, nc),
        in_specs=[col(OFF_BQ, DKB, fmap), col(OFF_BK, DKB, fmap), col(OFF_BV, DVB, fmap),
                  tab(fmap), tab(fmap), lg_spec,
                  pl.BlockSpec((1, 1, 1, DKB, DVB), lambda bi, h, c: (bi, h, c, 0, 0)),
                  col(OFF_BG, DVB, fmap), head_vec, head_vec],
        out_specs=pl.BlockSpec((1, cb, DVB), lambda bi, h, c: (bi, c, h)),
        out_shape=jax.ShapeDtypeStruct((b, t, WB_V), BF16),
        scratch_shapes=[pltpu.VMEM((DKB, DVB), F32)],
        compiler_params=_params("parallel", "parallel", "arbitrary"),
    )(proj3, proj3, proj3, cos2, sin2, lg, sb, proj3,
      gn_g.reshape(HB, 1, DVB), gn_b.reshape(HB, 1, DVB))


def _group_sum(x, gmat):
    cols = [_split_dot_right(x[:, i * LANE:(i + 1) * LANE], gmat) for i in range(x.shape[1] // LANE)]
    return jnp.concatenate(cols, axis=1)


def _rwkv_prep_kernel(tt, r_ref, k_ref, v_ref, c_ref, rp_ref, kp_ref, vp_ref, cp_ref,
                      rn_ref, kn_ref, vn_ref, cn_ref, mur_ref, muk_ref, muv_ref, muc_ref,
                      w0_ref, wl_ref, a0_ref, al_ref, kk_ref, ka_ref, rk_ref, gmat_ref,
                      r_o, v_o, kk_o, bonus_o, lw0_o, lw1_o, k0_o, k1_o, ka0_o, ka1_o):
    ti = pl.program_id(1)
    has_prev = (ti > 0).astype(F32)
    has_next = (ti < pl.num_programs(1) - 1).astype(F32)
    rows = lax.broadcasted_iota(jnp.int32, (tt, 1), 0)

    def shifted(p_ref, prev_ref, next_ref, mu_ref):
        p = p_ref[0]
        prev_row = prev_ref[0, SUBLANE - 1:SUBLANE, :] * has_prev
        next_row = next_ref[0, 0:1, :] * has_next
        before = jnp.where(rows == 0, prev_row, pltpu.roll(p, 1, axis=0))
        after = jnp.where(rows == tt - 1, next_row, pltpu.roll(p, tt - 1, axis=0))
        return p + mu_ref[0:1, :] * (before - p) + mu_ref[1:2, :] * (after - p)

    r = shifted(r_ref, rp_ref, rn_ref, mur_ref)
    k = shifted(k_ref, kp_ref, kn_ref, muk_ref)
    v = shifted(v_ref, vp_ref, vn_ref, muv_ref)
    codes = shifted(c_ref, cp_ref, cn_ref, muc_ref)
    gmat = gmat_ref[...]

    kraw = k * kk_ref[...]
    kk = kraw * lax.rsqrt(_group_sum(kraw * kraw, gmat) + 1e-12)
    r_o[0] = r
    v_o[0] = v
    kk_o[0] = kk
    tcodes = jnp.tanh(codes)
    bonus = jnp.zeros_like(r)
    for d, (lw_o, kd_o, ka_o) in enumerate(((lw0_o, k0_o, ka0_o), (lw1_o, k1_o, ka1_o))):
        w = w0_ref[d:d + 1, :] + _mm(tcodes, wl_ref[d])
        lw_o[0] = -np.exp(-0.5).astype(np.float32) * jax.nn.sigmoid(w)
        a = jax.nn.sigmoid(a0_ref[d:d + 1, :] + _mm(codes, al_ref[d]))
        kd = k * (1.0 + (a - 1.0) * ka_ref[...])
        kd_o[0] = kd
        ka_o[0] = kk * a
        bonus = bonus + _group_sum(r * kd * rk_ref[...], gmat) * v
    bonus_o[0] = bonus


def _rwkv_scan_kernel(rev, final, tt, r_ref, lw_ref, k_ref, v_ref, ka_ref, kk_ref, tri_ref, *rest):
    if final:
        ob_ref, bonus_ref, gate_ref, lng_ref, lnb_ref, y_ref, h_ref, o_s = rest
    else:
        o_ref, h_ref, o_s = rest

    @pl.when(pl.program_id(2) == 0)
    def _():
        h_ref[...] = jnp.zeros_like(h_ref)

    n2 = 2 * CHUNK_C
    lane = lax.broadcasted_iota(jnp.int32, (1, LANE), 1)
    head0 = lane < NC
    ri = lax.broadcasted_iota(jnp.int32, (n2, n2), 0)
    ci = lax.broadcasted_iota(jnp.int32, (n2, n2), 1)
    strict = (ci > ri) if rev else (ci < ri)
    incl = (ci >= ri) if rev else (ci <= ri)
    eye = ri == ci
    nchunk = tt // CHUNK_C

    def stack(z):
        return jnp.concatenate([jnp.where(head0, z, 0.0), jnp.where(head0, 0.0, z)], axis=0)

    def each(fn, *lists):
        return [fn(*items) for items in zip(*lists)]

    def prepare(first):
        sls, lws = [], []
        for u in range(UNROLL_C):
            ci_ = first + u
            c = (nchunk - 1 - ci_) if rev else ci_
            sls.append(pl.ds(pl.multiple_of(c * CHUNK_C, CHUNK_C), CHUNK_C))
            lws.append(lw_ref[0, sls[-1], :])
        g = each(lambda lw: _split_dot(tri_ref[...], lw), lws)
        e_end = each(lambda gu: jnp.exp(gu[0:1, :] if rev else gu[CHUNK_C - 1:CHUNK_C, :]), g)
        e_neg = each(lambda gu: jnp.exp(-gu), g)
        bs = each(lambda sl, gu, lw: stack(-kk_ref[0, sl, :] * jnp.exp(gu - lw)), sls, g, lws)
        as_ = each(lambda sl, en: stack(ka_ref[0, sl, :] * en), sls, e_neg)
        ks = each(lambda sl, en: stack(k_ref[0, sl, :] * en), sls, e_neg)
        rs = each(lambda sl, gu: stack(r_ref[0, sl, :] * jnp.exp(gu)), sls, g)
        vs = each(lambda sl: stack(v_ref[0, sl, :]), sls)

        ak = each(lambda a, k: jnp.concatenate([a, k], axis=0), as_, ks)
        sc_b = each(_mm_nt, bs, ak)
        sc_r = each(_mm_nt, rs, ak)
        a_ab = each(lambda z: jnp.where(strict, z[:, :n2], 0.0), sc_b)
        a_ak = each(lambda z: jnp.where(strict, z[:, n2:], 0.0), sc_b)
        m_ra = each(lambda z: jnp.where(incl, z[:, :n2], 0.0), sc_r)
        m_rk = each(lambda z: jnp.where(incl, z[:, n2:], 0.0), sc_r)
        tinv = each(lambda a: jnp.where(eye, 1.0, 0.0) + a, a_ab)
        pw = each(lambda a: _mm(a, a), a_ab)
        akv = each(_mm, a_ak, vs)
        nsteps = int(np.log2(CHUNK_C)) - 1
        for i in range(nsteps):
            if i < nsteps - 1:
                both = each(lambda t, p: _mm(jnp.concatenate([t, p], axis=0), p), tinv, pw)
                tinv = each(lambda t, b: t + b[:n2], tinv, both)
                pw = each(lambda b: b[n2:], both)
            else:
                tinv = each(lambda t, p: t + _mm(t, p), tinv, pw)
        pwv = each(lambda t, b, x: _mm(t, jnp.concatenate([b, x], axis=1)), tinv, bs, akv)
        x1 = each(_mm, m_ra, pwv)
        x2 = each(lambda a, e, z: _mm_tn(a * e, z), as_, e_end, pwv)
        x3 = each(_mm, m_rk, vs)
        x4 = each(lambda k, e, v: _mm_tn(k * e, v), ks, e_end, vs)
        q_eff = each(lambda r, z: r + z[:, :LANE], rs, x1)
        o_in = each(lambda z, y: z[:, LANE:] + y, x1, x3)
        gmat = each(lambda e, z: jnp.where(eye, e, 0.0) + z[:, :LANE], e_end, x2)
        dmat = each(lambda z, y: z[:, LANE:] + y, x2, x4)
        return list(zip(sls, q_eff, o_in, gmat, dmat))

    def apply(sl, q_eff, o_in, gmat, dmat):
        both = _mm(jnp.concatenate([q_eff, gmat], axis=0), h_ref[...])
        h_ref[...] = both[n2:] + dmat
        os_ = both[:n2] + o_in
        o_s[sl, :] = os_[:CHUNK_C, :] + os_[CHUNK_C:, :]

    def chunk_group(gi, carry):
        for item in prepare(gi * UNROLL_C):
            apply(*item)
        return carry

    lax.fori_loop(0, nchunk // UNROLL_C, chunk_group, 0)

    if final:
        o = o_s[...] + ob_ref[0]
        inv_n = 1.0 / NC
        s0 = jnp.sum(jnp.where(head0, o, 0.0), axis=-1, keepdims=True)
        s1 = jnp.sum(jnp.where(head0, 0.0, o), axis=-1, keepdims=True)
        cen = o - jnp.where(head0, s0, s1) * inv_n
        c2 = cen * cen
        v0 = jnp.sum(jnp.where(head0, c2, 0.0), axis=-1, keepdims=True)
        v1 = jnp.sum(jnp.where(head0, 0.0, c2), axis=-1, keepdims=True)
        var = jnp.where(head0, v0, v1) * inv_n
        y = cen * lax.rsqrt(var + RWKV_GN_EPS) * lng_ref[...] + lnb_ref[...] + bonus_ref[0]
        y_ref[0] = (y * _silu(gate_ref[0])).astype(y_ref.dtype)
    else:
        o_ref[0] = o_s[...]


def _rwkv_mixer(proj3, mu, w0, w_lora_b, a0, a_lora_b, k_k, k_a, r_k, ln_g, ln_b):
    b, t, _ = proj3.shape
    tt = _pick(t, 256)
    nt = t // tt
    sub = tt // SUBLANE

    def cur(off, width):
        return pl.BlockSpec((1, tt, width), lambda bi, ti: (bi, ti, off // width))

    def prev(off, width):
        return pl.BlockSpec((1, SUBLANE, width),
                            lambda bi, ti: (bi, jnp.maximum(ti * sub - 1, 0), off // width))

    def nxt(off, width):
        return pl.BlockSpec((1, SUBLANE, width),
                            lambda bi, ti: (bi, jnp.minimum((ti + 1) * sub, nt * sub - 1), off // width))

    def full(shape):
        return pl.BlockSpec(shape, lambda bi, ti: (0,) * len(shape))

    segs = ((OFF_CR, WC), (OFF_CK, WC), (OFF_CV, WC), (OFF_CC, N_CODES))
    in_specs = ([cur(o, w) for o, w in segs] + [prev(o, w) for o, w in segs] + [nxt(o, w) for o, w in segs]
                + [full((2, WC))] * 3 + [full((2, N_CODES))]
                + [full((2, WC)), full((2, N_CODES, WC)), full((2, WC)), full((2, N_CODES, WC))]
                + [full((1, WC))] * 3 + [full((LANE, LANE))])
    wl = jnp.zeros((2, N_CODES, WC), F32)
    al = jnp.zeros((2, N_CODES, WC), F32)
    for d in range(2):
        wl = wl.at[d, d * W_LORA:(d + 1) * W_LORA].set(w_lora_b[d])
        al = al.at[d, 2 * W_LORA + d * A_LORA:2 * W_LORA + (d + 1) * A_LORA].set(a_lora_b[d])
    lane_i = np.arange(LANE)
    gmat = jnp.asarray((lane_i[:, None] // NC == lane_i[None, :] // NC).astype(np.float32), dtype=BF16)
    row_out = pl.BlockSpec((1, tt, WC), lambda bi, ti: (bi, ti, 0))
    outs = pl.pallas_call(
        functools.partial(_rwkv_prep_kernel, tt),
        grid=(b, nt),
        in_specs=in_specs,
        out_specs=[row_out] * 10,
        out_shape=[jax.ShapeDtypeStruct((b, t, WC), F32)] * 10,
        compiler_params=_params("parallel", "parallel"),
    )(*([proj3] * 12), mu[:, :WC], mu[:, WC:2 * WC], mu[:, 2 * WC:3 * WC], mu[:, 3 * WC:],
      w0, wl.astype(BF16), a0, al.astype(BF16), k_k.reshape(1, WC), k_a.reshape(1, WC),
      r_k.reshape(1, WC), gmat)
    r, v, kk, bonus, lw0, lw1, k0, k1, ka0, ka1 = outs

    ts = _pick(t, 512)
    ns = t // ts

    def scan(rev, final, lw, kd, ka, ob=None):
        tmap = (lambda i: ns - 1 - i) if rev else (lambda i: i)
        blk = pl.BlockSpec((1, ts, LANE), lambda bi, h, ti: (bi, tmap(ti), h))
        head_vec = pl.BlockSpec((1, LANE), lambda bi, h, ti: (0, h))
        in_specs = [blk] * 6 + [pl.BlockSpec((CHUNK_C, CHUNK_C), lambda bi, h, ti: (0, 0))]
        ii = np.arange(CHUNK_C)
        tri = (ii[None, :] >= ii[:, None]) if rev else (ii[None, :] <= ii[:, None])
        args = [r, lw, kd, v, ka, kk, jnp.asarray(tri.astype(np.float32), dtype=BF16)]
        if final:
            gate = pl.BlockSpec((1, ts, LANE), lambda bi, h, ti: (bi, tmap(ti), OFF_CG // LANE + h))
            in_specs += [blk, blk, gate, head_vec, head_vec]
            args += [ob, bonus, proj3, ln_g.reshape(1, WC), ln_b.reshape(1, WC)]
        return pl.pallas_call(
            functools.partial(_rwkv_scan_kernel, rev, final, ts),
            grid=(b, HC // 2, ns),
            in_specs=in_specs,
            out_specs=blk,
            out_shape=jax.ShapeDtypeStruct((b, t, WC), BF16 if final else F32),
            scratch_shapes=[pltpu.VMEM((LANE, LANE), F32), pltpu.VMEM((ts, LANE), F32)],
            compiler_params=_params("parallel", "parallel", "arbitrary"),
        )(*args)

    ob = scan(True, False, lw1, k1, ka1)
    return scan(False, True, lw0, k0, ka0, ob=ob)


def _merge_kernel(final, ya_ref, yb_ref, yc_ref, mg_ref, x_ref, wa_ref, wb_ref, wc_ref, wo_ref,
                  fg_ref, o_ref):
    za = jnp.dot(ya_ref[...], wa_ref[...], preferred_element_type=F32)
    zb = jnp.dot(yb_ref[...], wb_ref[...], preferred_element_type=F32)
    zc = jnp.dot(yc_ref[...], wc_ref[...], preferred_element_type=F32)
    mg = mg_ref[...]
    mixed = (jax.nn.sigmoid(mg[:, :D_MODEL]) * za
             + jax.nn.sigmoid(mg[:, D_MODEL:2 * D_MODEL]) * zb
             + jax.nn.sigmoid(mg[:, 2 * D_MODEL:]) * zc)
    out = x_ref[...] + jnp.dot(mixed.astype(BF16), wo_ref[...], preferred_element_type=F32)
    if final:
        ms = jnp.mean(out * out, axis=-1, keepdims=True)
        out = out * lax.rsqrt(ms + EPS) * fg_ref[...]
    o_ref[...] = out


def _merge(ya, yb, yc, proj, x2, wa, wb, wc, wo, fg, final):
    m = x2.shape[0]
    tm = _pick(m, 256)

    def rows(width):
        return pl.BlockSpec((tm, width), lambda i: (i, 0))

    def full(shape):
        return pl.BlockSpec(shape, lambda i: (0, 0))

    return pl.pallas_call(
        functools.partial(_merge_kernel, final),
        grid=(m // tm,),
        in_specs=[rows(WA), rows(WB_V), rows(WC), rows(3 * D_MODEL), rows(D_MODEL),
                  full((WA, D_MODEL)), full((WB_V, D_MODEL)), full((WC, D_MODEL)),
                  full((D_MODEL, D_MODEL)), full((1, D_MODEL))],
        out_specs=rows(D_MODEL),
        out_shape=jax.ShapeDtypeStruct((m, D_MODEL), F32),
        compiler_params=_params("parallel"),
    )(ya, yb, yc, proj, x2, wa.astype(BF16), wb.astype(BF16), wc.astype(BF16), wo.astype(BF16),
      fg.reshape(1, D_MODEL))


def kernel(x, norm_g, w_in, hgrn_lb_logits, hgrn_norm_g, ret_norm_g, ret_norm_b, rwkv_mu, rwkv_w0,
           rwkv_w_lora_b, rwkv_a0, rwkv_a_lora_b, rwkv_k_k, rwkv_k_a, rwkv_r_k, rwkv_ln_g,
           rwkv_ln_b, w_branch_a, w_branch_b, w_branch_c, w_out, final_norm_g):
    b, t, d = x.shape
    m = b * t
    depth = w_in.shape[0]
    p_lb = jax.nn.softmax(hgrn_lb_logits.astype(F32), axis=0)
    lbs = jnp.cumsum(p_lb, axis=0) - p_lb[0:1]
    x2 = x.reshape(m, d)
    for l in range(depth):
        proj = _inproj(x2, norm_g[l], _permute_w_in(w_in[l]))
        proj3 = proj.reshape(b, t, N_PAD)
        ya = _hgrn_mixer(proj3, lbs[l], hgrn_norm_g[l])
        yb = _ret_mixer(proj3, ret_norm_g[l], ret_norm_b[l])
        yc = _rwkv_mixer(proj3, rwkv_mu[l], rwkv_w0[l], rwkv_w_lora_b[l], rwkv_a0[l],
                         rwkv_a_lora_b[l], rwkv_k_k[l], rwkv_k_a[l], rwkv_r_k[l],
                         rwkv_ln_g[l], rwkv_ln_b[l])
        x2 = _merge(ya.reshape(m, WA), yb.reshape(m, WB_V), yc.reshape(m, WC), proj, x2,
                    w_branch_a[l], w_branch_b[l], w_branch_c[l], w_out[l], final_norm_g,
                    final=(l == depth - 1))
    return x2.reshape(b, t, d)
```

```python
import functools

import numpy as np
import jax
import jax.numpy as jnp
from jax import lax
from jax.experimental import pallas as pl
from jax.experimental.pallas import tpu as pltpu

F32 = jnp.float32
BF16 = jnp.bfloat16

D_MODEL = 1024
EPS = 1e-6
F_MIN = 1e-6
DKA = 128
HA = D_MODEL // DKA
WA = HA * DKA
HB = 8
DKB = D_MODEL // HB
DVB = 2 * DKB
WB_QK = HB * DKB
WB_V = HB * DVB
ROPE_BASE = 10000.0
RET_GN_EPS = 1e-5
NC = 64
HC = D_MODEL // NC
WC = HC * NC
W_LORA = 64
A_LORA = 64
RWKV_GN_EPS = 64e-5
N_CODES = 2 * W_LORA + 2 * A_LORA
C_SHIFT = 3 * WC + N_CODES

SRC_A = 0
SRC_B = 5 * WA
SRC_CS = SRC_B + 2 * WB_QK + 2 * WB_V
SRC_CG = SRC_CS + C_SHIFT
SRC_MERGE = SRC_CG + WC
N_IN = SRC_MERGE + 3 * D_MODEL

OFF_MERGE = 0
OFF_AQ = OFF_MERGE + 3 * D_MODEL
OFF_AFF = OFF_AQ + WA
OFF_AFB = OFF_AFF + WA
OFF_AI = OFF_AFB + WA
OFF_AG = OFF_AI + WA
OFF_BQ = OFF_AG + WA
OFF_BK = OFF_BQ + WB_QK
OFF_BV = OFF_BK + WB_QK
OFF_BG = OFF_BV + WB_V
OFF_CG = OFF_BG + WB_V
OFF_CR = OFF_CG + WC
OFF_CK = OFF_CR + WC
OFF_CV = OFF_CK + WC
OFF_CC = OFF_CV + WC
LANE = 128
SUBLANE = 8
PROJ_TN = 512
N_PAD = -(-(OFF_CC + N_CODES) // PROJ_TN) * PROJ_TN

CHUNK_A = 128
GROUP_A = 8
CHUNK_B = 256
CHUNK_C = 64
UNROLL_C = 8
VMEM_LIMIT = 48 * 1024 * 1024

NT_DIMS = (((1,), (1,)), ((), ()))
TN_DIMS = (((0,), (0,)), ((), ()))


def _mm(a, b):
    return jnp.dot(a.astype(BF16), b.astype(BF16), preferred_element_type=F32)


def _mm_nt(a, b):
    return lax.dot_general(a.astype(BF16), b.astype(BF16), NT_DIMS, preferred_element_type=F32)


def _mm_tn(a, b):
    return lax.dot_general(a.astype(BF16), b.astype(BF16), TN_DIMS, preferred_element_type=F32)


def _split_dot(ones_mat, x):
    hi = x.astype(BF16)
    r1 = x - hi.astype(F32)
    mid = r1.astype(BF16)
    lo = (r1 - mid.astype(F32)).astype(BF16)
    out = jnp.dot(ones_mat, hi, preferred_element_type=F32)
    out = out + jnp.dot(ones_mat, mid, preferred_element_type=F32)
    return out + jnp.dot(ones_mat, lo, preferred_element_type=F32)


def _split_dot_right(x, ones_mat):
    hi = x.astype(BF16)
    r1 = x - hi.astype(F32)
    mid = r1.astype(BF16)
    lo = (r1 - mid.astype(F32)).astype(BF16)
    out = jnp.dot(hi, ones_mat, preferred_element_type=F32)
    out = out + jnp.dot(mid, ones_mat, preferred_element_type=F32)
    return out + jnp.dot(lo, ones_mat, preferred_element_type=F32)


def _silu(z):
    return z * jax.nn.sigmoid(z)


def _pick(n, pref):
    t = min(n, pref)
    assert n % t == 0, (n, pref)
    return t


def _params(*sem):
    return pltpu.CompilerParams(dimension_semantics=sem, vmem_limit_bytes=VMEM_LIMIT)


def _inproj_kernel(x_ref, g_ref, w_ref, o_ref, h_ref):
    @pl.when(pl.program_id(1) == 0)
    def _():
        x = x_ref[...]
        ms = jnp.mean(x * x, axis=-1, keepdims=True)
        h_ref[...] = (x * lax.rsqrt(ms + EPS) * g_ref[...]).astype(BF16)

    o_ref[...] = jnp.dot(h_ref[...], w_ref[...], preferred_element_type=F32)


def _inproj(x2, g, w_bf16):
    m = x2.shape[0]
    tm = _pick(m, 2048)
    return pl.pallas_call(
        _inproj_kernel,
        grid=(m // tm, N_PAD // PROJ_TN),
        in_specs=[
            pl.BlockSpec((tm, D_MODEL), lambda i, j: (i, 0)),
            pl.BlockSpec((1, D_MODEL), lambda i, j: (0, 0)),
            pl.BlockSpec((D_MODEL, PROJ_TN), lambda i, j: (0, j)),
        ],
        out_specs=pl.BlockSpec((tm, PROJ_TN), lambda i, j: (i, j)),
        out_shape=jax.ShapeDtypeStruct((m, N_PAD), F32),
        scratch_shapes=[pltpu.VMEM((tm, D_MODEL), BF16)],
        compiler_params=_params("parallel", "arbitrary"),
    )(x2, g.reshape(1, D_MODEL), w_bf16)


def _permute_w_in(w):
    parts = [w[:, SRC_MERGE:N_IN], w[:, SRC_A:SRC_CS], w[:, SRC_CG:SRC_MERGE], w[:, SRC_CS:SRC_CG]]
    wp = jnp.concatenate(parts, axis=1)
    return jnp.pad(wp, ((0, 0), (0, N_PAD - N_IN))).astype(BF16)


def _each(fn, *lists):
    return [fn(*items) for items in zip(*lists)]


def _hgrn_kernel(rev, final, tt, q_ref, f_ref, v_ref, lb_ref, tri_ref, *rest):
    if final:
        ob_ref, gate_ref, ng_ref, y_ref, st_ref = rest
    else:
        o_ref, st_ref = rest

    @pl.when(pl.program_id(2) == 0)
    def _():
        st_ref[...] = jnp.zeros_like(st_ref)

    n = CHUNK_A
    lb = lb_ref[0]
    ri = lax.broadcasted_iota(jnp.int32, (n, n), 0)
    ci = lax.broadcasted_iota(jnp.int32, (n, n), 1)
    g_bits = int(np.log2(GROUP_A))
    diag_mask = ((ri >> g_bits) == (ci >> g_bits)) & ((ci >= ri) if rev else (ci <= ri))
    halves = [GROUP_A << i for i in range(int(np.log2(n // GROUP_A)))]
    level_masks = []
    for half in halves:
        hb = int(np.log2(half))
        q_side, k_side = (0, 1) if rev else (1, 0)
        level_masks.append(((ri >> (hb + 1)) == (ci >> (hb + 1)))
                           & (((ri >> hb) & 1) == q_side) & (((ci >> hb) & 1) == k_side))

    def ref_rows(b, blk, idx):
        picked = b.reshape(n // blk, blk, DKA)[:, idx:idx + 1, :]
        return jnp.broadcast_to(picked, (n // blk, blk, DKA)).reshape(n, DKA)

    nchunk = tt // n
    order = [(nchunk - 1 - i) if rev else i for i in range(nchunk)]
    sls = [pl.ds(c * n, n) for c in order]
    sig = _each(lambda sl: jax.nn.sigmoid(f_ref[0, sl, :]), sls)
    lf = _each(lambda z: jnp.log(jnp.maximum(lb + (1.0 - lb) * z, F_MIN)), sig)
    k = _each(lambda z: (1.0 - lb) * (1.0 - z), sig)
    q = _each(lambda sl: _silu(q_ref[0, sl, :]), sls)
    v = _each(lambda sl: v_ref[0, sl, :], sls)
    b = _each(lambda z: _split_dot(tri_ref[...], z), lf)
    d0 = _each(lambda z: z - ref_rows(z, GROUP_A, GROUP_A // 2 if rev else GROUP_A // 2 - 1), b)
    scores = _each(lambda qq, kk, d: jnp.where(diag_mask, _mm_nt(qq * jnp.exp(d), kk * jnp.exp(-d)), 0.0),
                   q, k, d0)
    for half, mask in zip(halves, level_masks):
        e = _each(lambda z: jnp.exp(-jnp.abs(z - ref_rows(z, 2 * half, half if rev else half - 1))), b)
        scores = _each(lambda sc, qq, kk, ee: jnp.where(mask, _mm_nt(qq * ee, kk * ee), sc), scores, q, k, e)
    o_intra = _each(_mm, scores, v)
    b_end = _each(lambda z: z[0:1, :] if rev else z[n - 1:n, :], b)
    q_in = _each(lambda qq, z: qq * jnp.exp(z), q, b)
    kv = _each(lambda vv, kk, z, ze: _mm_tn(vv, kk * jnp.exp(ze - z)), v, k, b, b_end)
    e_end = _each(jnp.exp, b_end)

    outs = []
    for sl, oi, qi, kvi, ee in zip(sls, o_intra, q_in, kv, e_end):
        st = st_ref[...]
        outs.append((sl, oi + _mm_nt(qi, st)))
        st_ref[...] = st * ee + kvi

    for sl, o in outs:
        if final:
            o = o + ob_ref[0, sl, :]
            ms = jnp.mean(o * o, axis=-1, keepdims=True)
            y = o * lax.rsqrt(ms + EPS) * ng_ref[0] * _silu(gate_ref[0, sl, :])
            y_ref[0, sl, :] = y.astype(y_ref.dtype)
        else:
            o_ref[0, sl, :] = o


def _tri_blockdiag(n, blk, rev):
    i = np.arange(n)
    same = (i[:, None] // blk) == (i[None, :] // blk)
    tri = (i[None, :] >= i[:, None]) if rev else (i[None, :] <= i[:, None])
    return jnp.asarray((same & tri).astype(np.float32), dtype=BF16)


def _hgrn_pass(proj3, lb, rev, final, ob=None, norm_g=None):
    b, t, _ = proj3.shape
    tt = _pick(t, 512)
    nt = t // tt
    tmap = (lambda i: nt - 1 - i) if rev else (lambda i: i)

    def col(off):
        return pl.BlockSpec((1, tt, DKA), lambda bi, h, ti: (bi, tmap(ti), off // DKA + h))

    head_vec = pl.BlockSpec((1, 1, DKA), lambda bi, h, ti: (h, 0, 0))
    in_specs = [col(OFF_AQ), col(OFF_AFB if rev else OFF_AFF), col(OFF_AI), head_vec,
                pl.BlockSpec((LANE, LANE), lambda bi, h, ti: (0, 0))]
    args = [proj3, proj3, proj3, lb.reshape(HA, 1, DKA), _tri_blockdiag(LANE, CHUNK_A, rev)]
    out_block = pl.BlockSpec((1, tt, DKA), lambda bi, h, ti: (bi, tmap(ti), h))
    if final:
        in_specs += [out_block, col(OFF_AG), head_vec]
        args += [ob, proj3, norm_g.reshape(HA, 1, DKA)]
        out_dtype = BF16
    else:
        out_dtype = F32
    return pl.pallas_call(
        functools.partial(_hgrn_kernel, rev, final, tt),
        grid=(b, HA, nt),
        in_specs=in_specs,
        out_specs=out_block,
        out_shape=jax.ShapeDtypeStruct((b, t, WA), out_dtype),
        scratch_shapes=[pltpu.VMEM((DKA, DKA), F32)],
        compiler_params=_params("parallel", "parallel", "arbitrary"),
    )(*args)


def _hgrn_mixer(proj3, lbs, norm_g):
    ob = _hgrn_pass(proj3, lbs[1], rev=True, final=False)
    return _hgrn_pass(proj3, lbs[0], rev=False, final=True, ob=ob, norm_g=norm_g)


def _rotary(z, cos2, sin2):
    return z * cos2 + pltpu.roll(z, DKB // 2, axis=1) * sin2


def _ret_state_kernel(nsub, k_ref, v_ref, cos_ref, sin_ref, lg_ref, sb_ref, st_ref):
    @pl.when(pl.program_id(2) == 0)
    def _():
        st_ref[...] = jnp.zeros_like(st_ref)

    cb = CHUNK_B
    lg2 = lg_ref[0]
    lg = lg2[:, :DKB]
    pos = lax.broadcasted_iota(jnp.int32, (cb, DKB), 0).astype(F32)
    dec = jnp.exp(pos * lg) * (DKB ** -0.5)
    order = list(range(nsub - 1, -1, -1))
    sls = [pl.ds(c * cb, cb) for c in order]
    kd = _each(lambda sl: _rotary(k_ref[0, sl, :], cos_ref[sl, :], sin_ref[sl, :]) * dec, sls)
    kv = _each(lambda kk, sl: _mm_tn(kk, v_ref[0, sl, :]), kd, sls)
    e_chunk = jnp.exp(cb * lg2)
    st = st_ref[...]
    for c, kvi in zip(order, kv):
        sb_ref[0, 0, c] = st
        st = st * e_chunk + kvi
    st_ref[...] = st


def _ret_out_kernel(nsub, q_ref, k_ref, v_ref, cos_ref, sin_ref, lg_ref, sb_ref, gate_ref, gg_ref,
                    gb_ref, y_ref, st_ref):
    @pl.when(pl.program_id(2) == 0)
    def _():
        st_ref[...] = jnp.zeros_like(st_ref)

    cb = CHUNK_B
    lg2 = lg_ref[0]
    lg = lg2[:, :DKB]
    pos = lax.broadcasted_iota(jnp.int32, (cb, DKB), 0).astype(F32)
    ri = lax.broadcasted_iota(jnp.int32, (cb, cb), 0)
    ci = lax.broadcasted_iota(jnp.int32, (cb, cb), 1)
    decay = jnp.exp(jnp.abs(ri - ci).astype(F32) * lg2)
    dec_fwd = jnp.exp((pos + 1.0) * lg)
    dec_bwd = jnp.exp((cb - pos) * lg)
    dec_key = jnp.exp((cb - 1.0 - pos) * lg)
    e_chunk = jnp.exp(cb * lg2)

    sls = [pl.ds(c * cb, cb) for c in range(nsub)]
    q = _each(lambda sl: _rotary(q_ref[0, sl, :], cos_ref[sl, :], sin_ref[sl, :]), sls)
    k = _each(lambda sl: _rotary(k_ref[0, sl, :], cos_ref[sl, :], sin_ref[sl, :]) * (DKB ** -0.5), sls)
    v = _each(lambda sl: v_ref[0, sl, :], sls)
    scores = _each(lambda qq, kk: _mm_nt(qq, kk) * decay, q, k)
    o = _each(_mm, scores, v)
    o = _each(lambda oo, qq, c: oo + _mm(qq * dec_bwd, sb_ref[0, 0, c]), o, q, list(range(nsub)))
    kv = _each(lambda kk, vv: _mm_tn(kk * dec_key, vv), k, v)
    q_fwd = _each(lambda qq: qq * dec_fwd, q)
    st = st_ref[...]
    outs = []
    for oo, qf, kvi in zip(o, q_fwd, kv):
        outs.append(oo + _mm(qf, st))
        st = st * e_chunk + kvi
    st_ref[...] = st

    for sl, oo in zip(sls, outs):
        mu = jnp.mean(oo, axis=-1, keepdims=True)
        cen = oo - mu
        var = jnp.mean(cen * cen, axis=-1, keepdims=True)
        y = (cen * lax.rsqrt(var + RET_GN_EPS) * gg_ref[0] + gb_ref[0]) * _silu(gate_ref[0, sl, :])
        y_ref[0, sl, :] = y.astype(y_ref.dtype)


def _ret_mixer(proj3, gn_g, gn_b):
    b, t, _ = proj3.shape
    cb = CHUNK_B
    tt = _pick(t, 4 * cb)
    nsub = tt // cb
    nt = t // tt
    half = DKB // 2
    inv = ROPE_BASE ** (-jnp.arange(0, DKB, 2, dtype=F32) / DKB)
    ang = jnp.arange(t, dtype=F32)[:, None] * inv
    cos2 = jnp.concatenate([jnp.cos(ang), jnp.cos(ang)], axis=-1)
    sin2 = jnp.concatenate([-jnp.sin(ang), jnp.sin(ang)], axis=-1)
    assert cos2.shape == (t, 2 * half)
    log_gamma = jnp.log1p(-jnp.exp2(-5.0 - jnp.arange(HB, dtype=F32)))
    lg = jnp.broadcast_to(log_gamma[:, None, None], (HB, 1, DVB))

    def col(off, width, tmap):
        return pl.BlockSpec((1, tt, width), lambda bi, h, c: (bi, tmap(c), off // width + h))

    def tab(tmap):
        return pl.BlockSpec((tt, DKB), lambda bi, h, c: (tmap(c), 0))

    lg_spec = pl.BlockSpec((1, 1, DVB), lambda bi, h, c: (h, 0, 0))
    rmap = lambda c: nt - 1 - c
    fmap = lambda c: c

    sb = pl.pallas_call(
        functools.partial(_ret_state_kernel, nsub),
        grid=(b, HB, nt),
        in_specs=[col(OFF_BK, DKB, rmap), col(OFF_BV, DVB, rmap), tab(rmap), tab(rmap), lg_spec],
        out_specs=pl.BlockSpec((1, 1, nsub, DKB, DVB), lambda bi, h, c: (bi, h, rmap(c), 0, 0)),
        out_shape=jax.ShapeDtypeStruct((b, HB, t // cb, DKB, DVB), F32),
        scratch_shapes=[pltpu.VMEM((DKB, DVB), F32)],
        compiler_params=_params("parallel", "parallel", "arbitrary"),
    )(proj3, proj3, cos2, sin2, lg)

    head_vec = pl.BlockSpec((1, 1, DVB), lambda bi, h, c: (h, 0, 0))
    return pl.pallas_call(
        functools.partial(_ret_out_kernel, nsub),
        grid=(b, HB, nt),
        in_specs=[col(OFF_BQ, DKB, fmap), col(OFF_BK, DKB, fmap), col(OFF_BV, DVB, fmap),
                  tab(fmap), tab(fmap), lg_spec,
                  pl.BlockSpec((1, 1, nsub, DKB, DVB), lambda bi, h, c: (bi, h, c, 0, 0)),
                  col(OFF_BG, DVB, fmap), head_vec, head_vec],
        out_specs=pl.BlockSpec((1, tt, DVB), lambda bi, h, c: (bi, c, h)),
        out_shape=jax.ShapeDtypeStruct((b, t, WB_V), BF16),
        scratch_shapes=[pltpu.VMEM((DKB, DVB), F32)],
        compiler_params=_params("parallel", "parallel", "arbitrary"),
    )(proj3, proj3, proj3, cos2, sin2, lg, sb, proj3,
      gn_g.reshape(HB, 1, DVB), gn_b.reshape(HB, 1, DVB))


def _group_sum(x, gmat):
    cols = [_split_dot_right(x[:, i * LANE:(i + 1) * LANE], gmat) for i in range(x.shape[1] // LANE)]
    return jnp.concatenate(cols, axis=1)


def _rwkv_prep_kernel(tt, r_ref, k_ref, v_ref, c_ref, rp_ref, kp_ref, vp_ref, cp_ref,
                      rn_ref, kn_ref, vn_ref, cn_ref, mur_ref, muk_ref, muv_ref, muc_ref,
                      w0_ref, wl_ref, a0_ref, al_ref, kk_ref, ka_ref, rk_ref, gmat_ref,
                      r_o, v_o, kk_o, bonus_o, lw0_o, lw1_o, k0_o, k1_o, ka0_o, ka1_o):
    ti = pl.program_id(1)
    has_prev = (ti > 0).astype(F32)
    has_next = (ti < pl.num_programs(1) - 1).astype(F32)
    rows = lax.broadcasted_iota(jnp.int32, (tt, 1), 0)

    def shifted(p_ref, prev_ref, next_ref, mu_ref):
        p = p_ref[0]
        prev_row = prev_ref[0, SUBLANE - 1:SUBLANE, :] * has_prev
        next_row = next_ref[0, 0:1, :] * has_next
        before = jnp.where(rows == 0, prev_row, pltpu.roll(p, 1, axis=0))
        after = jnp.where(rows == tt - 1, next_row, pltpu.roll(p, tt - 1, axis=0))
        return p + mu_ref[0:1, :] * (before - p) + mu_ref[1:2, :] * (after - p)

    r = shifted(r_ref, rp_ref, rn_ref, mur_ref)
    k = shifted(k_ref, kp_ref, kn_ref, muk_ref)
    v = shifted(v_ref, vp_ref, vn_ref, muv_ref)
    codes = shifted(c_ref, cp_ref, cn_ref, muc_ref)
    gmat = gmat_ref[...]

    kraw = k * kk_ref[...]
    kk = kraw * lax.rsqrt(_group_sum(kraw * kraw, gmat) + 1e-12)
    r_o[0] = r
    v_o[0] = v
    kk_o[0] = kk
    tcodes = jnp.tanh(codes)
    bonus = jnp.zeros_like(r)
    for d, (lw_o, kd_o, ka_o) in enumerate(((lw0_o, k0_o, ka0_o), (lw1_o, k1_o, ka1_o))):
        w = w0_ref[d:d + 1, :] + _mm(tcodes, wl_ref[d])
        lw_o[0] = -np.exp(-0.5).astype(np.float32) * jax.nn.sigmoid(w)
        a = jax.nn.sigmoid(a0_ref[d:d + 1, :] + _mm(codes, al_ref[d]))
        kd = k * (1.0 + (a - 1.0) * ka_ref[...])
        kd_o[0] = kd
        ka_o[0] = kk * a
        bonus = bonus + _group_sum(r * kd * rk_ref[...], gmat) * v
    bonus_o[0] = bonus


def _rwkv_scan_kernel(rev, final, tt, r_ref, lw_ref, k_ref, v_ref, ka_ref, kk_ref, tri_ref, *rest):
    if final:
        ob_ref, bonus_ref, gate_ref, lng_ref, lnb_ref, y_ref, h_ref, o_s = rest
    else:
        o_ref, h_ref, o_s = rest

    @pl.when(pl.program_id(2) == 0)
    def _():
        h_ref[...] = jnp.zeros_like(h_ref)

    n2 = 2 * CHUNK_C
    lane = lax.broadcasted_iota(jnp.int32, (1, LANE), 1)
    head0 = lane < NC
    ri = lax.broadcasted_iota(jnp.int32, (n2, n2), 0)
    ci = lax.broadcasted_iota(jnp.int32, (n2, n2), 1)
    strict = (ci > ri) if rev else (ci < ri)
    incl = (ci >= ri) if rev else (ci <= ri)
    eye = ri == ci
    nchunk = tt // CHUNK_C

    def stack(z):
        return jnp.concatenate([jnp.where(head0, z, 0.0), jnp.where(head0, 0.0, z)], axis=0)

    def each(fn, *lists):
        return [fn(*items) for items in zip(*lists)]

    def prepare(first):
        sls, lws = [], []
        for u in range(UNROLL_C):
            ci_ = first + u
            c = (nchunk - 1 - ci_) if rev else ci_
            sls.append(pl.ds(pl.multiple_of(c * CHUNK_C, CHUNK_C), CHUNK_C))
            lws.append(lw_ref[0, sls[-1], :])
        g = each(lambda lw: _split_dot(tri_ref[...], lw), lws)
        e_end = each(lambda gu: jnp.exp(gu[0:1, :] if rev else gu[CHUNK_C - 1:CHUNK_C, :]), g)
        e_neg = each(lambda gu: jnp.exp(-gu), g)
        bs = each(lambda sl, gu, lw: stack(-kk_ref[0, sl, :] * jnp.exp(gu - lw)), sls, g, lws)
        as_ = each(lambda sl, en: stack(ka_ref[0, sl, :] * en), sls, e_neg)
        ks = each(lambda sl, en: stack(k_ref[0, sl, :] * en), sls, e_neg)
        rs = each(lambda sl, gu: stack(r_ref[0, sl, :] * jnp.exp(gu)), sls, g)
        vs = each(lambda sl: stack(v_ref[0, sl, :]), sls)

        ak = each(lambda a, k: jnp.concatenate([a, k], axis=0), as_, ks)
        sc_b = each(_mm_nt, bs, ak)
        sc_r = each(_mm_nt, rs, ak)
        a_ab = each(lambda z: jnp.where(strict, z[:, :n2], 0.0), sc_b)
        a_ak = each(lambda z: jnp.where(strict, z[:, n2:], 0.0), sc_b)
        m_ra = each(lambda z: jnp.where(incl, z[:, :n2], 0.0), sc_r)
        m_rk = each(lambda z: jnp.where(incl, z[:, n2:], 0.0), sc_r)
        tinv = each(lambda a: jnp.where(eye, 1.0, 0.0) + a, a_ab)
        pw = each(lambda a: _mm(a, a), a_ab)
        akv = each(_mm, a_ak, vs)
        nsteps = int(np.log2(CHUNK_C)) - 1
        for i in range(nsteps):
            if i < nsteps - 1:
                both = each(lambda t, p: _mm(jnp.concatenate([t, p], axis=0), p), tinv, pw)
                tinv = each(lambda t, b: t + b[:n2], tinv, both)
                pw = each(lambda b: b[n2:], both)
            else:
                tinv = each(lambda t, p: t + _mm(t, p), tinv, pw)
        pwv = each(lambda t, b, x: _mm(t, jnp.concatenate([b, x], axis=1)), tinv, bs, akv)
        x1 = each(_mm, m_ra, pwv)
        x2 = each(lambda a, e, z: _mm_tn(a * e, z), as_, e_end, pwv)
        x3 = each(_mm, m_rk, vs)
        x4 = each(lambda k, e, v: _mm_tn(k * e, v), ks, e_end, vs)
        q_eff = each(lambda r, z: r + z[:, :LANE], rs, x1)
        o_in = each(lambda z, y: z[:, LANE:] + y, x1, x3)
        gmat = each(lambda e, z: jnp.where(eye, e, 0.0) + z[:, :LANE], e_end, x2)
        dmat = each(lambda z, y: z[:, LANE:] + y, x2, x4)
        return list(zip(sls, q_eff, o_in, gmat, dmat))

    def apply(sl, q_eff, o_in, gmat, dmat):
        both = _mm(jnp.concatenate([q_eff, gmat], axis=0), h_ref[...])
        h_ref[...] = both[n2:] + dmat
        os_ = both[:n2] + o_in
        o_s[sl, :] = os_[:CHUNK_C, :] + os_[CHUNK_C:, :]

    def chunk_group(gi, carry):
        for item in prepare(gi * UNROLL_C):
            apply(*item)
        return carry

    lax.fori_loop(0, nchunk // UNROLL_C, chunk_group, 0)

    if final:
        o = o_s[...] + ob_ref[0]
        inv_n = 1.0 / NC
        s0 = jnp.sum(jnp.where(head0, o, 0.0), axis=-1, keepdims=True)
        s1 = jnp.sum(jnp.where(head0, 0.0, o), axis=-1, keepdims=True)
        cen = o - jnp.where(head0, s0, s1) * inv_n
        c2 = cen * cen
        v0 = jnp.sum(jnp.where(head0, c2, 0.0), axis=-1, keepdims=True)
        v1 = jnp.sum(jnp.where(head0, 0.0, c2), axis=-1, keepdims=True)
        var = jnp.where(head0, v0, v1) * inv_n
        y = cen * lax.rsqrt(var + RWKV_GN_EPS) * lng_ref[...] + lnb_ref[...] + bonus_ref[0]
        y_ref[0] = (y * _silu(gate_ref[0])).astype(y_ref.dtype)
    else:
        o_ref[0] = o_s[...]


def _rwkv_mixer(proj3, mu, w0, w_lora_b, a0, a_lora_b, k_k, k_a, r_k, ln_g, ln_b):
    b, t, _ = proj3.shape
    tt = _pick(t, 256)
    nt = t // tt
    sub = tt // SUBLANE

    def cur(off, width):
        return pl.BlockSpec((1, tt, width), lambda bi, ti: (bi, ti, off // width))

    def prev(off, width):
        return pl.BlockSpec((1, SUBLANE, width),
                            lambda bi, ti: (bi, jnp.maximum(ti * sub - 1, 0), off // width))

    def nxt(off, width):
        return pl.BlockSpec((1, SUBLANE, width),
                            lambda bi, ti: (bi, jnp.minimum((ti + 1) * sub, nt * sub - 1), off // width))

    def full(shape):
        return pl.BlockSpec(shape, lambda bi, ti: (0,) * len(shape))

    segs = ((OFF_CR, WC), (OFF_CK, WC), (OFF_CV, WC), (OFF_CC, N_CODES))
    in_specs = ([cur(o, w) for o, w in segs] + [prev(o, w) for o, w in segs] + [nxt(o, w) for o, w in segs]
                + [full((2, WC))] * 3 + [full((2, N_CODES))]
                + [full((2, WC)), full((2, N_CODES, WC)), full((2, WC)), full((2, N_CODES, WC))]
                + [full((1, WC))] * 3 + [full((LANE, LANE))])
    wl = jnp.zeros((2, N_CODES, WC), F32)
    al = jnp.zeros((2, N_CODES, WC), F32)
    for d in range(2):
        wl = wl.at[d, d * W_LORA:(d + 1) * W_LORA].set(w_lora_b[d])
        al = al.at[d, 2 * W_LORA + d * A_LORA:2 * W_LORA + (d + 1) * A_LORA].set(a_lora_b[d])
    lane_i = np.arange(LANE)
    gmat = jnp.asarray((lane_i[:, None] // NC == lane_i[None, :] // NC).astype(np.float32), dtype=BF16)
    row_out = pl.BlockSpec((1, tt, WC), lambda bi, ti: (bi, ti, 0))
    outs = pl.pallas_call(
        functools.partial(_rwkv_prep_kernel, tt),
        grid=(b, nt),
        in_specs=in_specs,
        out_specs=[row_out] * 10,
        out_shape=[jax.ShapeDtypeStruct((b, t, WC), F32)] * 10,
        compiler_params=_params("parallel", "parallel"),
    )(*([proj3] * 12), mu[:, :WC], mu[:, WC:2 * WC], mu[:, 2 * WC:3 * WC], mu[:, 3 * WC:],
      w0, wl.astype(BF16), a0, al.astype(BF16), k_k.reshape(1, WC), k_a.reshape(1, WC),
      r_k.reshape(1, WC), gmat)
    r, v, kk, bonus, lw0, lw1, k0, k1, ka0, ka1 = outs

    ts = _pick(t, 512)
    ns = t // ts

    def scan(rev, final, lw, kd, ka, ob=None):
        tmap = (lambda i: ns - 1 - i) if rev else (lambda i: i)
        blk = pl.BlockSpec((1, ts, LANE), lambda bi, h, ti: (bi, tmap(ti), h))
        head_vec = pl.BlockSpec((1, LANE), lambda bi, h, ti: (0, h))
        in_specs = [blk] * 6 + [pl.BlockSpec((CHUNK_C, CHUNK_C), lambda bi, h, ti: (0, 0))]
        ii = np.arange(CHUNK_C)
        tri = (ii[None, :] >= ii[:, None]) if rev else (ii[None, :] <= ii[:, None])
        args = [r, lw, kd, v, ka, kk, jnp.asarray(tri.astype(np.float32), dtype=BF16)]
        if final:
            gate = pl.BlockSpec((1, ts, LANE), lambda bi, h, ti: (bi, tmap(ti), OFF_CG // LANE + h))
            in_specs += [blk, blk, gate, head_vec, head_vec]
            args += [ob, bonus, proj3, ln_g.reshape(1, WC), ln_b.reshape(1, WC)]
        return pl.pallas_call(
            functools.partial(_rwkv_scan_kernel, rev, final, ts),
            grid=(b, HC // 2, ns),
            in_specs=in_specs,
            out_specs=blk,
            out_shape=jax.ShapeDtypeStruct((b, t, WC), BF16 if final else F32),
            scratch_shapes=[pltpu.VMEM((LANE, LANE), F32), pltpu.VMEM((ts, LANE), F32)],
            compiler_params=_params("parallel", "parallel", "arbitrary"),
        )(*args)

    ob = scan(True, False, lw1, k1, ka1)
    return scan(False, True, lw0, k0, ka0, ob=ob)


def _merge_kernel(final, ya_ref, yb_ref, yc_ref, mg_ref, x_ref, wa_ref, wb_ref, wc_ref, wo_ref,
                  fg_ref, o_ref):
    za = jnp.dot(ya_ref[...], wa_ref[...], preferred_element_type=F32)
    zb = jnp.dot(yb_ref[...], wb_ref[...], preferred_element_type=F32)
    zc = jnp.dot(yc_ref[...], wc_ref[...], preferred_element_type=F32)
    mg = mg_ref[...]
    mixed = (jax.nn.sigmoid(mg[:, :D_MODEL]) * za
             + jax.nn.sigmoid(mg[:, D_MODEL:2 * D_MODEL]) * zb
             + jax.nn.sigmoid(mg[:, 2 * D_MODEL:]) * zc)
    out = x_ref[...] + jnp.dot(mixed.astype(BF16), wo_ref[...], preferred_element_type=F32)
    if final:
        ms = jnp.mean(out * out, axis=-1, keepdims=True)
        out = out * lax.rsqrt(ms + EPS) * fg_ref[...]
    o_ref[...] = out


def _merge(ya, yb, yc, proj, x2, wa, wb, wc, wo, fg, final):
    m = x2.shape[0]
    tm = _pick(m, 256)

    def rows(width):
        return pl.BlockSpec((tm, width), lambda i: (i, 0))

    def full(shape):
        return pl.BlockSpec(shape, lambda i: (0, 0))

    return pl.pallas_call(
        functools.partial(_merge_kernel, final),
        grid=(m // tm,),
        in_specs=[rows(WA), rows(WB_V), rows(WC), rows(3 * D_MODEL), rows(D_MODEL),
                  full((WA, D_MODEL)), full((WB_V, D_MODEL)), full((WC, D_MODEL)),
                  full((D_MODEL, D_MODEL)), full((1, D_MODEL))],
        out_specs=rows(D_MODEL),
        out_shape=jax.ShapeDtypeStruct((m, D_MODEL), F32),
        compiler_params=_params("parallel"),
    )(ya, yb, yc, proj, x2, wa.astype(BF16), wb.astype(BF16), wc.astype(BF16), wo.astype(BF16),
      fg.reshape(1, D_MODEL))


def kernel(x, norm_g, w_in, hgrn_lb_logits, hgrn_norm_g, ret_norm_g, ret_norm_b, rwkv_mu, rwkv_w0,
           rwkv_w_lora_b, rwkv_a0, rwkv_a_lora_b, rwkv_k_k, rwkv_k_a, rwkv_r_k, rwkv_ln_g,
           rwkv_ln_b, w_branch_a, w_branch_b, w_branch_c, w_out, final_norm_g):
    b, t, d = x.shape
    m = b * t
    depth = w_in.shape[0]
    p_lb = jax.nn.softmax(hgrn_lb_logits.astype(F32), axis=0)
    lbs = jnp.cumsum(p_lb, axis=0) - p_lb[0:1]
    x2 = x.reshape(m, d)
    for l in range(depth):
        proj = _inproj(x2, norm_g[l], _permute_w_in(w_in[l]))
        proj3 = proj.reshape(b, t, N_PAD)
        ya = _hgrn_mixer(proj3, lbs[l], hgrn_norm_g[l])
        yb = _ret_mixer(proj3, ret_norm_g[l], ret_norm_b[l])
        yc = _rwkv_mixer(proj3, rwkv_mu[l], rwkv_w0[l], rwkv_w_lora_b[l], rwkv_a0[l],
                         rwkv_a_lora_b[l], rwkv_k_k[l], rwkv_k_a[l], rwkv_r_k[l],
                         rwkv_ln_g[l], rwkv_ln_b[l])
        x2 = _merge(ya.reshape(m, WA), yb.reshape(m, WB_V), yc.reshape(m, WC), proj, x2,
                    w_branch_a[l], w_branch_b[l], w_branch_c[l], w_out[l], final_norm_g,
                    final=(l == depth - 1))
    return x2.reshape(b, t, d)
```

```python
import functools

import numpy as np
import jax
import jax.numpy as jnp
from jax import lax
from jax.experimental import pallas as pl
from jax.experimental.pallas import tpu as pltpu

F32 = jnp.float32
BF16 = jnp.bfloat16

D_MODEL = 1024
EPS = 1e-6
F_MIN = 1e-6
DKA = 128
HA = D_MODEL // DKA
WA = HA * DKA
HB = 8
DKB = D_MODEL // HB
DVB = 2 * DKB
WB_QK = HB * DKB
WB_V = HB * DVB
ROPE_BASE = 10000.0
RET_GN_EPS = 1e-5
NC = 64
HC = D_MODEL // NC
WC = HC * NC
W_LORA = 64
A_LORA = 64
RWKV_GN_EPS = 64e-5
N_CODES = 2 * W_LORA + 2 * A_LORA
C_SHIFT = 3 * WC + N_CODES

SRC_A = 0
SRC_B = 5 * WA
SRC_CS = SRC_B + 2 * WB_QK + 2 * WB_V
SRC_CG = SRC_CS + C_SHIFT
SRC_MERGE = SRC_CG + WC
N_IN = SRC_MERGE + 3 * D_MODEL

OFF_MERGE = 0
OFF_AQ = OFF_MERGE + 3 * D_MODEL
OFF_AFF = OFF_AQ + WA
OFF_AFB = OFF_AFF + WA
OFF_AI = OFF_AFB + WA
OFF_AG = OFF_AI + WA
OFF_BQ = OFF_AG + WA
OFF_BK = OFF_BQ + WB_QK
OFF_BV = OFF_BK + WB_QK
OFF_BG = OFF_BV + WB_V
OFF_CG = OFF_BG + WB_V
OFF_CR = OFF_CG + WC
OFF_CK = OFF_CR + WC
OFF_CV = OFF_CK + WC
OFF_CC = OFF_CV + WC
LANE = 128
SUBLANE = 8
PROJ_TN = 512
N_PAD = -(-(OFF_CC + N_CODES) // PROJ_TN) * PROJ_TN

CHUNK_A = 128
GROUP_A = 8
CHUNK_B = 256
CHUNK_C = 64
PAIRS_C = 4
VMEM_LIMIT = 48 * 1024 * 1024

NT_DIMS = (((1,), (1,)), ((), ()))
TN_DIMS = (((0,), (0,)), ((), ()))


def _mm(a, b):
    return jnp.dot(a.astype(BF16), b.astype(BF16), preferred_element_type=F32)


def _mm_nt(a, b):
    return lax.dot_general(a.astype(BF16), b.astype(BF16), NT_DIMS, preferred_element_type=F32)


def _mm_tn(a, b):
    return lax.dot_general(a.astype(BF16), b.astype(BF16), TN_DIMS, preferred_element_type=F32)


def _split_dot(ones_mat, x):
    hi = x.astype(BF16)
    r1 = x - hi.astype(F32)
    mid = r1.astype(BF16)
    lo = (r1 - mid.astype(F32)).astype(BF16)
    out = jnp.dot(ones_mat, hi, preferred_element_type=F32)
    out = out + jnp.dot(ones_mat, mid, preferred_element_type=F32)
    return out + jnp.dot(ones_mat, lo, preferred_element_type=F32)


def _split_dot_right(x, ones_mat):
    hi = x.astype(BF16)
    r1 = x - hi.astype(F32)
    mid = r1.astype(BF16)
    lo = (r1 - mid.astype(F32)).astype(BF16)
    out = jnp.dot(hi, ones_mat, preferred_element_type=F32)
    out = out + jnp.dot(mid, ones_mat, preferred_element_type=F32)
    return out + jnp.dot(lo, ones_mat, preferred_element_type=F32)


def _silu(z):
    return z * jax.nn.sigmoid(z)


def _pick(n, pref):
    t = min(n, pref)
    assert n % t == 0, (n, pref)
    return t


def _params(*sem):
    return pltpu.CompilerParams(dimension_semantics=sem, vmem_limit_bytes=VMEM_LIMIT)


def _inproj_kernel(x_ref, g_ref, w_ref, o_ref, h_ref):
    @pl.when(pl.program_id(1) == 0)
    def _():
        x = x_ref[...]
        ms = jnp.mean(x * x, axis=-1, keepdims=True)
        h_ref[...] = (x * lax.rsqrt(ms + EPS) * g_ref[...]).astype(BF16)

    o_ref[...] = jnp.dot(h_ref[...], w_ref[...], preferred_element_type=F32)


def _inproj(x2, g, w_bf16):
    m = x2.shape[0]
    tm = _pick(m, 2048)
    return pl.pallas_call(
        _inproj_kernel,
        grid=(m // tm, N_PAD // PROJ_TN),
        in_specs=[
            pl.BlockSpec((tm, D_MODEL), lambda i, j: (i, 0)),
            pl.BlockSpec((1, D_MODEL), lambda i, j: (0, 0)),
            pl.BlockSpec((D_MODEL, PROJ_TN), lambda i, j: (0, j)),
        ],
        out_specs=pl.BlockSpec((tm, PROJ_TN), lambda i, j: (i, j)),
        out_shape=jax.ShapeDtypeStruct((m, N_PAD), F32),
        scratch_shapes=[pltpu.VMEM((tm, D_MODEL), BF16)],
        compiler_params=_params("parallel", "arbitrary"),
    )(x2, g.reshape(1, D_MODEL), w_bf16)


def _permute_w_in(w):
    parts = [w[:, SRC_MERGE:N_IN], w[:, SRC_A:SRC_CS], w[:, SRC_CG:SRC_MERGE], w[:, SRC_CS:SRC_CG]]
    wp = jnp.concatenate(parts, axis=1)
    return jnp.pad(wp, ((0, 0), (0, N_PAD - N_IN))).astype(BF16)


def _each(fn, *lists):
    return [fn(*items) for items in zip(*lists)]


def _hgrn_kernel(rev, final, tt, q_ref, f_ref, v_ref, lb_ref, tri_ref, *rest):
    if final:
        ob_ref, gate_ref, ng_ref, y_ref, st_ref = rest
    else:
        o_ref, st_ref = rest

    @pl.when(pl.program_id(2) == 0)
    def _():
        st_ref[...] = jnp.zeros_like(st_ref)

    n = CHUNK_A
    lb = lb_ref[0]
    ri = lax.broadcasted_iota(jnp.int32, (n, n), 0)
    ci = lax.broadcasted_iota(jnp.int32, (n, n), 1)
    g_bits = int(np.log2(GROUP_A))
    diag_mask = ((ri >> g_bits) == (ci >> g_bits)) & ((ci >= ri) if rev else (ci <= ri))
    halves = [GROUP_A << i for i in range(int(np.log2(n // GROUP_A)))]
    level_masks = []
    for half in halves:
        hb = int(np.log2(half))
        q_side, k_side = (0, 1) if rev else (1, 0)
        level_masks.append(((ri >> (hb + 1)) == (ci >> (hb + 1)))
                           & (((ri >> hb) & 1) == q_side) & (((ci >> hb) & 1) == k_side))

    def ref_rows(b, blk, idx):
        picked = b.reshape(n // blk, blk, DKA)[:, idx:idx + 1, :]
        return jnp.broadcast_to(picked, (n // blk, blk, DKA)).reshape(n, DKA)

    nchunk = tt // n
    order = [(nchunk - 1 - i) if rev else i for i in range(nchunk)]
    sls = [pl.ds(c * n, n) for c in order]
    sig = _each(lambda sl: jax.nn.sigmoid(f_ref[0, sl, :]), sls)
    lf = _each(lambda z: jnp.log(jnp.maximum(lb + (1.0 - lb) * z, F_MIN)), sig)
    k = _each(lambda z: (1.0 - lb) * (1.0 - z), sig)
    q = _each(lambda sl: _silu(q_ref[0, sl, :]), sls)
    v = _each(lambda sl: v_ref[0, sl, :], sls)
    b = _each(lambda z: _split_dot(tri_ref[...], z), lf)
    d0 = _each(lambda z: z - ref_rows(z, GROUP_A, GROUP_A // 2 if rev else GROUP_A // 2 - 1), b)
    scores = _each(lambda qq, kk, d: jnp.where(diag_mask, _mm_nt(qq * jnp.exp(d), kk * jnp.exp(-d)), 0.0),
                   q, k, d0)
    for half, mask in zip(halves, level_masks):
        e = _each(lambda z: jnp.exp(-jnp.abs(z - ref_rows(z, 2 * half, half if rev else half - 1))), b)
        scores = _each(lambda sc, qq, kk, ee: jnp.where(mask, _mm_nt(qq * ee, kk * ee), sc), scores, q, k, e)
    o_intra = _each(_mm, scores, v)
    b_end = _each(lambda z: z[0:1, :] if rev else z[n - 1:n, :], b)
    q_in = _each(lambda qq, z: qq * jnp.exp(z), q, b)
    kv = _each(lambda vv, kk, z, ze: _mm_tn(vv, kk * jnp.exp(ze - z)), v, k, b, b_end)
    e_end = _each(jnp.exp, b_end)

    outs = []
    for sl, oi, qi, kvi, ee in zip(sls, o_intra, q_in, kv, e_end):
        st = st_ref[...]
        outs.append((sl, oi + _mm_nt(qi, st)))
        st_ref[...] = st * ee + kvi

    for sl, o in outs:
        if final:
            o = o + ob_ref[0, sl, :]
            ms = jnp.mean(o * o, axis=-1, keepdims=True)
            y = o * lax.rsqrt(ms + EPS) * ng_ref[0] * _silu(gate_ref[0, sl, :])
            y_ref[0, sl, :] = y.astype(y_ref.dtype)
        else:
            o_ref[0, sl, :] = o


def _tri_blockdiag(n, blk, rev):
    i = np.arange(n)
    same = (i[:, None] // blk) == (i[None, :] // blk)
    tri = (i[None, :] >= i[:, None]) if rev else (i[None, :] <= i[:, None])
    return jnp.asarray((same & tri).astype(np.float32), dtype=BF16)


def _hgrn_pass(proj3, lb, rev, final, ob=None, norm_g=None):
    b, t, _ = proj3.shape
    tt = _pick(t, 512)
    nt = t // tt
    tmap = (lambda i: nt - 1 - i) if rev else (lambda i: i)

    def col(off):
        return pl.BlockSpec((1, tt, DKA), lambda bi, h, ti: (bi, tmap(ti), off // DKA + h))

    head_vec = pl.BlockSpec((1, 1, DKA), lambda bi, h, ti: (h, 0, 0))
    in_specs = [col(OFF_AQ), col(OFF_AFB if rev else OFF_AFF), col(OFF_AI), head_vec,
                pl.BlockSpec((LANE, LANE), lambda bi, h, ti: (0, 0))]
    args = [proj3, proj3, proj3, lb.reshape(HA, 1, DKA), _tri_blockdiag(LANE, CHUNK_A, rev)]
    out_block = pl.BlockSpec((1, tt, DKA), lambda bi, h, ti: (bi, tmap(ti), h))
    if final:
        in_specs += [out_block, col(OFF_AG), head_vec]
        args += [ob, proj3, norm_g.reshape(HA, 1, DKA)]
        out_dtype = BF16
    else:
        out_dtype = F32
    return pl.pallas_call(
        functools.partial(_hgrn_kernel, rev, final, tt),
        grid=(b, HA, nt),
        in_specs=in_specs,
        out_specs=out_block,
        out_shape=jax.ShapeDtypeStruct((b, t, WA), out_dtype),
        scratch_shapes=[pltpu.VMEM((DKA, DKA), F32)],
        compiler_params=_params("parallel", "parallel", "arbitrary"),
    )(*args)


def _hgrn_mixer(proj3, lbs, norm_g):
    ob = _hgrn_pass(proj3, lbs[1], rev=True, final=False)
    return _hgrn_pass(proj3, lbs[0], rev=False, final=True, ob=ob, norm_g=norm_g)


def _rotary(z, cos2, sin2):
    return z * cos2 + pltpu.roll(z, DKB // 2, axis=1) * sin2


def _ret_state_kernel(nsub, k_ref, v_ref, cos_ref, sin_ref, lg_ref, sb_ref, st_ref):
    @pl.when(pl.program_id(2) == 0)
    def _():
        st_ref[...] = jnp.zeros_like(st_ref)

    cb = CHUNK_B
    lg2 = lg_ref[0]
    lg = lg2[:, :DKB]
    pos = lax.broadcasted_iota(jnp.int32, (cb, DKB), 0).astype(F32)
    dec = jnp.exp(pos * lg) * (DKB ** -0.5)
    order = list(range(nsub - 1, -1, -1))
    sls = [pl.ds(c * cb, cb) for c in order]
    kd = _each(lambda sl: _rotary(k_ref[0, sl, :], cos_ref[sl, :], sin_ref[sl, :]) * dec, sls)
    kv = _each(lambda kk, sl: _mm_tn(kk, v_ref[0, sl, :]), kd, sls)
    e_chunk = jnp.exp(cb * lg2)
    st = st_ref[...]
    for c, kvi in zip(order, kv):
        sb_ref[0, 0, c] = st
        st = st * e_chunk + kvi
    st_ref[...] = st


def _ret_out_kernel(nsub, q_ref, k_ref, v_ref, cos_ref, sin_ref, lg_ref, sb_ref, gate_ref, gg_ref,
                    gb_ref, y_ref, st_ref):
    @pl.when(pl.program_id(2) == 0)
    def _():
        st_ref[...] = jnp.zeros_like(st_ref)

    cb = CHUNK_B
    lg2 = lg_ref[0]
    lg = lg2[:, :DKB]
    pos = lax.broadcasted_iota(jnp.int32, (cb, DKB), 0).astype(F32)
    ri = lax.broadcasted_iota(jnp.int32, (cb, cb), 0)
    ci = lax.broadcasted_iota(jnp.int32, (cb, cb), 1)
    decay = jnp.exp(jnp.abs(ri - ci).astype(F32) * lg2)
    dec_fwd = jnp.exp((pos + 1.0) * lg)
    dec_bwd = jnp.exp((cb - pos) * lg)
    dec_key = jnp.exp((cb - 1.0 - pos) * lg)
    e_chunk = jnp.exp(cb * lg2)

    sls = [pl.ds(c * cb, cb) for c in range(nsub)]
    q = _each(lambda sl: _rotary(q_ref[0, sl, :], cos_ref[sl, :], sin_ref[sl, :]), sls)
    k = _each(lambda sl: _rotary(k_ref[0, sl, :], cos_ref[sl, :], sin_ref[sl, :]) * (DKB ** -0.5), sls)
    v = _each(lambda sl: v_ref[0, sl, :], sls)
    scores = _each(lambda qq, kk: _mm_nt(qq, kk) * decay, q, k)
    o = _each(_mm, scores, v)
    o = _each(lambda oo, qq, c: oo + _mm(qq * dec_bwd, sb_ref[0, 0, c]), o, q, list(range(nsub)))
    kv = _each(lambda kk, vv: _mm_tn(kk * dec_key, vv), k, v)
    q_fwd = _each(lambda qq: qq * dec_fwd, q)
    st = st_ref[...]
    outs = []
    for oo, qf, kvi in zip(o, q_fwd, kv):
        outs.append(oo + _mm(qf, st))
        st = st * e_chunk + kvi
    st_ref[...] = st

    for sl, oo in zip(sls, outs):
        mu = jnp.mean(oo, axis=-1, keepdims=True)
        cen = oo - mu
        var = jnp.mean(cen * cen, axis=-1, keepdims=True)
        y = (cen * lax.rsqrt(var + RET_GN_EPS) * gg_ref[0] + gb_ref[0]) * _silu(gate_ref[0, sl, :])
        y_ref[0, sl, :] = y.astype(y_ref.dtype)


def _ret_mixer(proj3, gn_g, gn_b):
    b, t, _ = proj3.shape
    cb = CHUNK_B
    tt = _pick(t, 4 * cb)
    nsub = tt // cb
    nt = t // tt
    half = DKB // 2
    inv = ROPE_BASE ** (-jnp.arange(0, DKB, 2, dtype=F32) / DKB)
    ang = jnp.arange(t, dtype=F32)[:, None] * inv
    cos2 = jnp.concatenate([jnp.cos(ang), jnp.cos(ang)], axis=-1)
    sin2 = jnp.concatenate([-jnp.sin(ang), jnp.sin(ang)], axis=-1)
    assert cos2.shape == (t, 2 * half)
    log_gamma = jnp.log1p(-jnp.exp2(-5.0 - jnp.arange(HB, dtype=F32)))
    lg = jnp.broadcast_to(log_gamma[:, None, None], (HB, 1, DVB))

    def col(off, width, tmap):
        return pl.BlockSpec((1, tt, width), lambda bi, h, c: (bi, tmap(c), off // width + h))

    def tab(tmap):
        return pl.BlockSpec((tt, DKB), lambda bi, h, c: (tmap(c), 0))

    lg_spec = pl.BlockSpec((1, 1, DVB), lambda bi, h, c: (h, 0, 0))
    rmap = lambda c: nt - 1 - c
    fmap = lambda c: c

    sb = pl.pallas_call(
        functools.partial(_ret_state_kernel, nsub),
        grid=(b, HB, nt),
        in_specs=[col(OFF_BK, DKB, rmap), col(OFF_BV, DVB, rmap), tab(rmap), tab(rmap), lg_spec],
        out_specs=pl.BlockSpec((1, 1, nsub, DKB, DVB), lambda bi, h, c: (bi, h, rmap(c), 0, 0)),
        out_shape=jax.ShapeDtypeStruct((b, HB, t // cb, DKB, DVB), F32),
        scratch_shapes=[pltpu.VMEM((DKB, DVB), F32)],
        compiler_params=_params("parallel", "parallel", "arbitrary"),
    )(proj3, proj3, cos2, sin2, lg)

    head_vec = pl.BlockSpec((1, 1, DVB), lambda bi, h, c: (h, 0, 0))
    return pl.pallas_call(
        functools.partial(_ret_out_kernel, nsub),
        grid=(b, HB, nt),
        in_specs=[col(OFF_BQ, DKB, fmap), col(OFF_BK, DKB, fmap), col(OFF_BV, DVB, fmap),
                  tab(fmap), tab(fmap), lg_spec,
                  pl.BlockSpec((1, 1, nsub, DKB, DVB), lambda bi, h, c: (bi, h, c, 0, 0)),
                  col(OFF_BG, DVB, fmap), head_vec, head_vec],
        out_specs=pl.BlockSpec((1, tt, DVB), lambda bi, h, c: (bi, c, h)),
        out_shape=jax.ShapeDtypeStruct((b, t, WB_V), BF16),
        scratch_shapes=[pltpu.VMEM((DKB, DVB), F32)],
        compiler_params=_params("parallel", "parallel", "arbitrary"),
    )(proj3, proj3, proj3, cos2, sin2, lg, sb, proj3,
      gn_g.reshape(HB, 1, DVB), gn_b.reshape(HB, 1, DVB))


def _group_sum(x, gmat):
    cols = [_split_dot_right(x[:, i * LANE:(i + 1) * LANE], gmat) for i in range(x.shape[1] // LANE)]
    return jnp.concatenate(cols, axis=1)


def _rwkv_prep_kernel(tt, r_ref, k_ref, v_ref, c_ref, rp_ref, kp_ref, vp_ref, cp_ref,
                      rn_ref, kn_ref, vn_ref, cn_ref, mur_ref, muk_ref, muv_ref, muc_ref,
                      w0_ref, wl_ref, a0_ref, al_ref, kk_ref, ka_ref, rk_ref, gmat_ref,
                      r_o, v_o, kk_o, bonus_o, lw0_o, lw1_o, k0_o, k1_o, ka0_o, ka1_o):
    ti = pl.program_id(1)
    has_prev = (ti > 0).astype(F32)
    has_next = (ti < pl.num_programs(1) - 1).astype(F32)
    rows = lax.broadcasted_iota(jnp.int32, (tt, 1), 0)

    def shifted(p_ref, prev_ref, next_ref, mu_ref):
        p = p_ref[0]
        prev_row = prev_ref[0, SUBLANE - 1:SUBLANE, :] * has_prev
        next_row = next_ref[0, 0:1, :] * has_next
        before = jnp.where(rows == 0, prev_row, pltpu.roll(p, 1, axis=0))
        after = jnp.where(rows == tt - 1, next_row, pltpu.roll(p, tt - 1, axis=0))
        return p + mu_ref[0:1, :] * (before - p) + mu_ref[1:2, :] * (after - p)

    r = shifted(r_ref, rp_ref, rn_ref, mur_ref)
    k = shifted(k_ref, kp_ref, kn_ref, muk_ref)
    v = shifted(v_ref, vp_ref, vn_ref, muv_ref)
    codes = shifted(c_ref, cp_ref, cn_ref, muc_ref)
    gmat = gmat_ref[...]

    kraw = k * kk_ref[...]
    kk = kraw * lax.rsqrt(_group_sum(kraw * kraw, gmat) + 1e-12)
    r_o[0] = r
    v_o[0] = v
    kk_o[0] = kk
    tcodes = jnp.tanh(codes)
    bonus = jnp.zeros_like(r)
    for d, (lw_o, kd_o, ka_o) in enumerate(((lw0_o, k0_o, ka0_o), (lw1_o, k1_o, ka1_o))):
        w = w0_ref[d:d + 1, :] + _mm(tcodes, wl_ref[d])
        lw_o[0] = -np.exp(-0.5).astype(np.float32) * jax.nn.sigmoid(w)
        a = jax.nn.sigmoid(a0_ref[d:d + 1, :] + _mm(codes, al_ref[d]))
        kd = k * (1.0 + (a - 1.0) * ka_ref[...])
        kd_o[0] = kd
        ka_o[0] = kk * a
        bonus = bonus + _group_sum(r * kd * rk_ref[...], gmat) * v
    bonus_o[0] = bonus


def _rwkv_scan_kernel(rev, final, tt, r_ref, lw_ref, k_ref, v_ref, ka_ref, kk_ref, tri_ref, *rest):
    if final:
        ob_ref, bonus_ref, gate_ref, lng_ref, lnb_ref, y_ref, h_ref = rest
    else:
        o_ref, h_ref = rest

    @pl.when(pl.program_id(2) == 0)
    def _():
        h_ref[...] = jnp.zeros_like(h_ref)

    n2 = 2 * CHUNK_C
    lane = lax.broadcasted_iota(jnp.int32, (1, LANE), 1)
    head0 = lane < NC
    ri = lax.broadcasted_iota(jnp.int32, (n2, n2), 0)
    ci = lax.broadcasted_iota(jnp.int32, (n2, n2), 1)
    strict = (ci > ri) if rev else (ci < ri)
    incl = (ci >= ri) if rev else (ci <= ri)
    eye = ri == ci
    nchunk = tt // CHUNK_C

    def stack(z):
        return jnp.concatenate([jnp.where(head0, z, 0.0), jnp.where(head0, 0.0, z)], axis=0)

    order = [(nchunk - 1 - i) if rev else i for i in range(nchunk)]
    items = [(pl.ds(c * CHUNK_C, CHUNK_C), pl.ds(p * LANE, LANE), p) for c in order for p in range(PAIRS_C)]
    rows = [it[0] for it in items]
    cols = [it[1] for it in items]

    lws = _each(lambda sl, cl: lw_ref[0, sl, cl], rows, cols)
    g = _each(lambda lw: _split_dot(tri_ref[...], lw), lws)
    e_end = _each(lambda gu: jnp.exp(gu[0:1, :] if rev else gu[CHUNK_C - 1:CHUNK_C, :]), g)
    e_neg = _each(lambda gu: jnp.exp(-gu), g)
    bs = _each(lambda sl, cl, gu, lw: stack(-kk_ref[0, sl, cl] * jnp.exp(gu - lw)), rows, cols, g, lws)
    as_ = _each(lambda sl, cl, en: stack(ka_ref[0, sl, cl] * en), rows, cols, e_neg)
    ks = _each(lambda sl, cl, en: stack(k_ref[0, sl, cl] * en), rows, cols, e_neg)
    rs = _each(lambda sl, cl, gu: stack(r_ref[0, sl, cl] * jnp.exp(gu)), rows, cols, g)
    vs = _each(lambda sl, cl: stack(v_ref[0, sl, cl]), rows, cols)

    ak = _each(lambda a, k: jnp.concatenate([a, k], axis=0), as_, ks)
    sc_b = _each(_mm_nt, bs, ak)
    sc_r = _each(_mm_nt, rs, ak)
    a_ab = _each(lambda z: jnp.where(strict, z[:, :n2], 0.0), sc_b)
    a_ak = _each(lambda z: jnp.where(strict, z[:, n2:], 0.0), sc_b)
    m_rak = _each(lambda z: jnp.where(jnp.concatenate([incl, incl], axis=1), z, 0.0), sc_r)
    tinv = _each(lambda a: jnp.where(eye, 1.0, 0.0) + a, a_ab)
    pw = _each(lambda a: _mm(a, a), a_ab)
    akv = _each(_mm, a_ak, vs)
    nsteps = int(np.log2(CHUNK_C)) - 1
    for i in range(nsteps):
        if i < nsteps - 1:
            both = _each(lambda t, p: _mm(jnp.concatenate([t, p], axis=0), p), tinv, pw)
            tinv = _each(lambda t, b: t + b[:n2], tinv, both)
            pw = _each(lambda b: b[n2:], both)
        else:
            tinv = _each(lambda t, p: t + _mm(t, p), tinv, pw)
    pwv = _each(lambda t, b, x: _mm(t, jnp.concatenate([b, x], axis=1)), tinv, bs, akv)
    low = _each(lambda z, v: jnp.concatenate([z, jnp.concatenate([jnp.zeros_like(v), v], axis=1)], axis=0),
                pwv, vs)
    x1 = _each(_mm, m_rak, low)
    x2 = _each(lambda a, k, e, z: _mm_tn(jnp.concatenate([a * e, k * e], axis=0), z), as_, ks, e_end, low)
    top = _each(lambda r, z, e, y: jnp.concatenate([r + z[:, :LANE], jnp.where(eye, e, 0.0) + y[:, :LANE]], axis=0),
                rs, x1, e_end, x2)
    o_in = _each(lambda z: z[:, LANE:], x1)
    dmat = _each(lambda y: y[:, LANE:], x2)

    outs = []
    h = [h_ref[p] for p in range(PAIRS_C)]
    for (sl, cl, p), tp, oi, dm in zip(items, top, o_in, dmat):
        both = _mm(tp, h[p])
        h[p] = both[n2:] + dm
        os_ = both[:n2] + oi
        outs.append((sl, cl, os_[:CHUNK_C, :] + os_[CHUNK_C:, :]))
    for p in range(PAIRS_C):
        h_ref[p] = h[p]

    for sl, cl, o in outs:
        if final:
            o = o + ob_ref[0, sl, cl]
            inv_n = 1.0 / NC
            s0 = jnp.sum(jnp.where(head0, o, 0.0), axis=-1, keepdims=True)
            s1 = jnp.sum(jnp.where(head0, 0.0, o), axis=-1, keepdims=True)
            cen = o - jnp.where(head0, s0, s1) * inv_n
            c2 = cen * cen
            v0 = jnp.sum(jnp.where(head0, c2, 0.0), axis=-1, keepdims=True)
            v1 = jnp.sum(jnp.where(head0, 0.0, c2), axis=-1, keepdims=True)
            var = jnp.where(head0, v0, v1) * inv_n
            y = cen * lax.rsqrt(var + RWKV_GN_EPS) * lng_ref[:, cl] + lnb_ref[:, cl] + bonus_ref[0, sl, cl]
            y_ref[0, sl, cl] = (y * _silu(gate_ref[0, sl, cl])).astype(y_ref.dtype)
        else:
            o_ref[0, sl, cl] = o


def _rwkv_mixer(proj3, mu, w0, w_lora_b, a0, a_lora_b, k_k, k_a, r_k, ln_g, ln_b):
    b, t, _ = proj3.shape
    tt = _pick(t, 256)
    nt = t // tt
    sub = tt // SUBLANE

    def cur(off, width):
        return pl.BlockSpec((1, tt, width), lambda bi, ti: (bi, ti, off // width))

    def prev(off, width):
        return pl.BlockSpec((1, SUBLANE, width),
                            lambda bi, ti: (bi, jnp.maximum(ti * sub - 1, 0), off // width))

    def nxt(off, width):
        return pl.BlockSpec((1, SUBLANE, width),
                            lambda bi, ti: (bi, jnp.minimum((ti + 1) * sub, nt * sub - 1), off // width))

    def full(shape):
        return pl.BlockSpec(shape, lambda bi, ti: (0,) * len(shape))

    segs = ((OFF_CR, WC), (OFF_CK, WC), (OFF_CV, WC), (OFF_CC, N_CODES))
    in_specs = ([cur(o, w) for o, w in segs] + [prev(o, w) for o, w in segs] + [nxt(o, w) for o, w in segs]
                + [full((2, WC))] * 3 + [full((2, N_CODES))]
                + [full((2, WC)), full((2, N_CODES, WC)), full((2, WC)), full((2, N_CODES, WC))]
                + [full((1, WC))] * 3 + [full((LANE, LANE))])
    wl = jnp.zeros((2, N_CODES, WC), F32)
    al = jnp.zeros((2, N_CODES, WC), F32)
    for d in range(2):
        wl = wl.at[d, d * W_LORA:(d + 1) * W_LORA].set(w_lora_b[d])
        al = al.at[d, 2 * W_LORA + d * A_LORA:2 * W_LORA + (d + 1) * A_LORA].set(a_lora_b[d])
    lane_i = np.arange(LANE)
    gmat = jnp.asarray((lane_i[:, None] // NC == lane_i[None, :] // NC).astype(np.float32), dtype=BF16)
    row_out = pl.BlockSpec((1, tt, WC), lambda bi, ti: (bi, ti, 0))
    outs = pl.pallas_call(
        functools.partial(_rwkv_prep_kernel, tt),
        grid=(b, nt),
        in_specs=in_specs,
        out_specs=[row_out] * 10,
        out_shape=[jax.ShapeDtypeStruct((b, t, WC), F32)] * 10,
        compiler_params=_params("parallel", "parallel"),
    )(*([proj3] * 12), mu[:, :WC], mu[:, WC:2 * WC], mu[:, 2 * WC:3 * WC], mu[:, 3 * WC:],
      w0, wl.astype(BF16), a0, al.astype(BF16), k_k.reshape(1, WC), k_a.reshape(1, WC),
      r_k.reshape(1, WC), gmat)
    r, v, kk, bonus, lw0, lw1, k0, k1, ka0, ka1 = outs

    ts = _pick(t, 512)
    ns = t // ts

    def scan(rev, final, lw, kd, ka, ob=None):
        wblk = PAIRS_C * LANE
        tmap = (lambda i: ns - 1 - i) if rev else (lambda i: i)
        blk = pl.BlockSpec((1, ts, wblk), lambda bi, h, ti: (bi, tmap(ti), h))
        head_vec = pl.BlockSpec((1, wblk), lambda bi, h, ti: (0, h))
        in_specs = [blk] * 6 + [pl.BlockSpec((CHUNK_C, CHUNK_C), lambda bi, h, ti: (0, 0))]
        ii = np.arange(CHUNK_C)
        tri = (ii[None, :] >= ii[:, None]) if rev else (ii[None, :] <= ii[:, None])
        args = [r, lw, kd, v, ka, kk, jnp.asarray(tri.astype(np.float32), dtype=BF16)]
        if final:
            gate = pl.BlockSpec((1, ts, wblk), lambda bi, h, ti: (bi, tmap(ti), OFF_CG // wblk + h))
            in_specs += [blk, blk, gate, head_vec, head_vec]
            args += [ob, bonus, proj3, ln_g.reshape(1, WC), ln_b.reshape(1, WC)]
        return pl.pallas_call(
            functools.partial(_rwkv_scan_kernel, rev, final, ts),
            grid=(b, WC // wblk, ns),
            in_specs=in_specs,
            out_specs=blk,
            out_shape=jax.ShapeDtypeStruct((b, t, WC), BF16 if final else F32),
            scratch_shapes=[pltpu.VMEM((PAIRS_C, LANE, LANE), F32)],
            compiler_params=_params("parallel", "parallel", "arbitrary"),
        )(*args)

    ob = scan(True, False, lw1, k1, ka1)
    return scan(False, True, lw0, k0, ka0, ob=ob)


def _merge_kernel(final, ya_ref, yb_ref, yc_ref, mg_ref, x_ref, wa_ref, wb_ref, wc_ref, wo_ref,
                  fg_ref, o_ref):
    za = jnp.dot(ya_ref[...], wa_ref[...], preferred_element_type=F32)
    zb = jnp.dot(yb_ref[...], wb_ref[...], preferred_element_type=F32)
    zc = jnp.dot(yc_ref[...], wc_ref[...], preferred_element_type=F32)
    mg = mg_ref[...]
    mixed = (jax.nn.sigmoid(mg[:, :D_MODEL]) * za
             + jax.nn.sigmoid(mg[:, D_MODEL:2 * D_MODEL]) * zb
             + jax.nn.sigmoid(mg[:, 2 * D_MODEL:]) * zc)
    out = x_ref[...] + jnp.dot(mixed.astype(BF16), wo_ref[...], preferred_element_type=F32)
    if final:
        ms = jnp.mean(out * out, axis=-1, keepdims=True)
        out = out * lax.rsqrt(ms + EPS) * fg_ref[...]
    o_ref[...] = out


def _merge(ya, yb, yc, proj, x2, wa, wb, wc, wo, fg, final):
    m = x2.shape[0]
    tm = _pick(m, 256)

    def rows(width):
        return pl.BlockSpec((tm, width), lambda i: (i, 0))

    def full(shape):
        return pl.BlockSpec(shape, lambda i: (0, 0))

    return pl.pallas_call(
        functools.partial(_merge_kernel, final),
        grid=(m // tm,),
        in_specs=[rows(WA), rows(WB_V), rows(WC), rows(3 * D_MODEL), rows(D_MODEL),
                  full((WA, D_MODEL)), full((WB_V, D_MODEL)), full((WC, D_MODEL)),
                  full((D_MODEL, D_MODEL)), full((1, D_MODEL))],
        out_specs=rows(D_MODEL),
        out_shape=jax.ShapeDtypeStruct((m, D_MODEL), F32),
        compiler_params=_params("parallel"),
    )(ya, yb, yc, proj, x2, wa.astype(BF16), wb.astype(BF16), wc.astype(BF16), wo.astype(BF16),
      fg.reshape(1, D_MODEL))


def kernel(x, norm_g, w_in, hgrn_lb_logits, hgrn_norm_g, ret_norm_g, ret_norm_b, rwkv_mu, rwkv_w0,
           rwkv_w_lora_b, rwkv_a0, rwkv_a_lora_b, rwkv_k_k, rwkv_k_a, rwkv_r_k, rwkv_ln_g,
           rwkv_ln_b, w_branch_a, w_branch_b, w_branch_c, w_out, final_norm_g):
    b, t, d = x.shape
    m = b * t
    depth = w_in.shape[0]
    p_lb = jax.nn.softmax(hgrn_lb_logits.astype(F32), axis=0)
    lbs = jnp.cumsum(p_lb, axis=0) - p_lb[0:1]
    x2 = x.reshape(m, d)
    for l in range(depth):
        proj = _inproj(x2, norm_g[l], _permute_w_in(w_in[l]))
        proj3 = proj.reshape(b, t, N_PAD)
        ya = _hgrn_mixer(proj3, lbs[l], hgrn_norm_g[l])
        yb = _ret_mixer(proj3, ret_norm_g[l], ret_norm_b[l])
        yc = _rwkv_mixer(proj3, rwkv_mu[l], rwkv_w0[l], rwkv_w_lora_b[l], rwkv_a0[l],
                         rwkv_a_lora_b[l], rwkv_k_k[l], rwkv_k_a[l], rwkv_r_k[l],
                         rwkv_ln_g[l], rwkv_ln_b[l])
        x2 = _merge(ya.reshape(m, WA), yb.reshape(m, WB_V), yc.reshape(m, WC), proj, x2,
                    w_branch_a[l], w_branch_b[l], w_branch_c[l], w_out[l], final_norm_g,
                    final=(l == depth - 1))
    return x2.reshape(b, t, d)
```

```python
import functools

import numpy as np
import jax
import jax.numpy as jnp
from jax import lax
from jax.experimental import pallas as pl
from jax.experimental.pallas import tpu as pltpu

F32 = jnp.float32
BF16 = jnp.bfloat16

D_MODEL = 1024
EPS = 1e-6
F_MIN = 1e-6
DKA = 128
HA = D_MODEL // DKA
WA = HA * DKA
HB = 8
DKB = D_MODEL // HB
DVB = 2 * DKB
WB_QK = HB * DKB
WB_V = HB * DVB
ROPE_BASE = 10000.0
RET_GN_EPS = 1e-5
NC = 64
HC = D_MODEL // NC
WC = HC * NC
W_LORA = 64
A_LORA = 64
RWKV_GN_EPS = 64e-5
N_CODES = 2 * W_LORA + 2 * A_LORA
C_SHIFT = 3 * WC + N_CODES

SRC_A = 0
SRC_B = 5 * WA
SRC_CS = SRC_B + 2 * WB_QK + 2 * WB_V
SRC_CG = SRC_CS + C_SHIFT
SRC_MERGE = SRC_CG + WC
N_IN = SRC_MERGE + 3 * D_MODEL

OFF_MERGE = 0
OFF_AQ = OFF_MERGE + 3 * D_MODEL
OFF_AFF = OFF_AQ + WA
OFF_AFB = OFF_AFF + WA
OFF_AI = OFF_AFB + WA
OFF_AG = OFF_AI + WA
OFF_BQ = OFF_AG + WA
OFF_BK = OFF_BQ + WB_QK
OFF_BV = OFF_BK + WB_QK
OFF_BG = OFF_BV + WB_V
OFF_CG = OFF_BG + WB_V
OFF_CR = OFF_CG + WC
OFF_CK = OFF_CR + WC
OFF_CV = OFF_CK + WC
OFF_CC = OFF_CV + WC
LANE = 128
PROJ_DTYPE = BF16
HALO = 16
PROJ_TN = 512
N_PAD = -(-(OFF_CC + N_CODES) // PROJ_TN) * PROJ_TN

CHUNK_A = 128
GROUP_A = 8
CHUNK_B = 256
CHUNK_C = 64
PAIRS_C = 4
VMEM_LIMIT = 48 * 1024 * 1024

NT_DIMS = (((1,), (1,)), ((), ()))
TN_DIMS = (((0,), (0,)), ((), ()))


def _mm(a, b):
    return jnp.dot(a.astype(BF16), b.astype(BF16), preferred_element_type=F32)


def _mm_nt(a, b):
    return lax.dot_general(a.astype(BF16), b.astype(BF16), NT_DIMS, preferred_element_type=F32)


def _mm_tn(a, b):
    return lax.dot_general(a.astype(BF16), b.astype(BF16), TN_DIMS, preferred_element_type=F32)


def _split_dot(ones_mat, x):
    hi = x.astype(BF16)
    r1 = x - hi.astype(F32)
    mid = r1.astype(BF16)
    lo = (r1 - mid.astype(F32)).astype(BF16)
    out = jnp.dot(ones_mat, hi, preferred_element_type=F32)
    out = out + jnp.dot(ones_mat, mid, preferred_element_type=F32)
    return out + jnp.dot(ones_mat, lo, preferred_element_type=F32)


def _split_dot_right(x, ones_mat):
    hi = x.astype(BF16)
    r1 = x - hi.astype(F32)
    mid = r1.astype(BF16)
    lo = (r1 - mid.astype(F32)).astype(BF16)
    out = jnp.dot(hi, ones_mat, preferred_element_type=F32)
    out = out + jnp.dot(mid, ones_mat, preferred_element_type=F32)
    return out + jnp.dot(lo, ones_mat, preferred_element_type=F32)


def _f32(z):
    return z.astype(F32)


def _silu(z):
    return z * jax.nn.sigmoid(z)


def _pick(n, pref):
    t = min(n, pref)
    assert n % t == 0, (n, pref)
    return t


def _params(*sem):
    return pltpu.CompilerParams(dimension_semantics=sem, vmem_limit_bytes=VMEM_LIMIT)


def _inproj_kernel(x_ref, g_ref, w_ref, o_ref, h_ref):
    @pl.when(pl.program_id(1) == 0)
    def _():
        x = x_ref[...]
        ms = jnp.mean(x * x, axis=-1, keepdims=True)
        h_ref[...] = (x * lax.rsqrt(ms + EPS) * g_ref[...]).astype(BF16)

    o_ref[...] = jnp.dot(h_ref[...], w_ref[...], preferred_element_type=F32).astype(o_ref.dtype)


def _inproj(x2, g, w_bf16):
    m = x2.shape[0]
    tm = _pick(m, 2048)
    return pl.pallas_call(
        _inproj_kernel,
        grid=(m // tm, N_PAD // PROJ_TN),
        in_specs=[
            pl.BlockSpec((tm, D_MODEL), lambda i, j: (i, 0)),
            pl.BlockSpec((1, D_MODEL), lambda i, j: (0, 0)),
            pl.BlockSpec((D_MODEL, PROJ_TN), lambda i, j: (0, j)),
        ],
        out_specs=pl.BlockSpec((tm, PROJ_TN), lambda i, j: (i, j)),
        out_shape=jax.ShapeDtypeStruct((m, N_PAD), PROJ_DTYPE),
        scratch_shapes=[pltpu.VMEM((tm, D_MODEL), BF16)],
        compiler_params=_params("parallel", "arbitrary"),
    )(x2, g.reshape(1, D_MODEL), w_bf16)


def _permute_w_in(w):
    parts = [w[:, SRC_MERGE:N_IN], w[:, SRC_A:SRC_CS], w[:, SRC_CG:SRC_MERGE], w[:, SRC_CS:SRC_CG]]
    wp = jnp.concatenate(parts, axis=1)
    return jnp.pad(wp, ((0, 0), (0, N_PAD - N_IN))).astype(BF16)


def _each(fn, *lists):
    return [fn(*items) for items in zip(*lists)]


def _hgrn_kernel(rev, final, tt, q_ref, f_ref, v_ref, lb_ref, tri_ref, *rest):
    if final:
        ob_ref, gate_ref, ng_ref, y_ref, st_ref = rest
    else:
        o_ref, st_ref = rest

    @pl.when(pl.program_id(2) == 0)
    def _():
        st_ref[...] = jnp.zeros_like(st_ref)

    n = CHUNK_A
    lb = lb_ref[0]
    ri = lax.broadcasted_iota(jnp.int32, (n, n), 0)
    ci = lax.broadcasted_iota(jnp.int32, (n, n), 1)
    g_bits = int(np.log2(GROUP_A))
    diag_mask = ((ri >> g_bits) == (ci >> g_bits)) & ((ci >= ri) if rev else (ci <= ri))
    halves = [GROUP_A << i for i in range(int(np.log2(n // GROUP_A)))]
    level_masks = []
    for half in halves:
        hb = int(np.log2(half))
        q_side, k_side = (0, 1) if rev else (1, 0)
        level_masks.append(((ri >> (hb + 1)) == (ci >> (hb + 1)))
                           & (((ri >> hb) & 1) == q_side) & (((ci >> hb) & 1) == k_side))

    def ref_rows(b, blk, idx):
        picked = b.reshape(n // blk, blk, DKA)[:, idx:idx + 1, :]
        return jnp.broadcast_to(picked, (n // blk, blk, DKA)).reshape(n, DKA)

    nchunk = tt // n
    order = [(nchunk - 1 - i) if rev else i for i in range(nchunk)]
    sls = [pl.ds(c * n, n) for c in order]
    sig = _each(lambda sl: jax.nn.sigmoid(_f32(f_ref[0, sl, :])), sls)
    lf = _each(lambda z: jnp.log(jnp.maximum(lb + (1.0 - lb) * z, F_MIN)), sig)
    k = _each(lambda z: (1.0 - lb) * (1.0 - z), sig)
    q = _each(lambda sl: _silu(_f32(q_ref[0, sl, :])), sls)
    v = _each(lambda sl: v_ref[0, sl, :], sls)
    b = _each(lambda z: _split_dot(tri_ref[...], z), lf)
    d0 = _each(lambda z: z - ref_rows(z, GROUP_A, GROUP_A // 2 if rev else GROUP_A // 2 - 1), b)
    scores = _each(lambda qq, kk, d: jnp.where(diag_mask, _mm_nt(qq * jnp.exp(d), kk * jnp.exp(-d)), 0.0),
                   q, k, d0)
    for half, mask in zip(halves, level_masks):
        e = _each(lambda z: jnp.exp(-jnp.abs(z - ref_rows(z, 2 * half, half if rev else half - 1))), b)
        scores = _each(lambda sc, qq, kk, ee: jnp.where(mask, _mm_nt(qq * ee, kk * ee), sc), scores, q, k, e)
    o_intra = _each(_mm, scores, v)
    b_end = _each(lambda z: z[0:1, :] if rev else z[n - 1:n, :], b)
    q_in = _each(lambda qq, z: qq * jnp.exp(z), q, b)
    kv = _each(lambda vv, kk, z, ze: _mm_tn(vv, kk * jnp.exp(ze - z)), v, k, b, b_end)
    e_end = _each(jnp.exp, b_end)

    outs = []
    for sl, oi, qi, kvi, ee in zip(sls, o_intra, q_in, kv, e_end):
        st = st_ref[...]
        outs.append((sl, oi + _mm_nt(qi, st)))
        st_ref[...] = st * ee + kvi

    for sl, o in outs:
        if final:
            o = o + ob_ref[0, sl, :]
            ms = jnp.mean(o * o, axis=-1, keepdims=True)
            y = o * lax.rsqrt(ms + EPS) * ng_ref[0] * _silu(_f32(gate_ref[0, sl, :]))
            y_ref[0, sl, :] = y.astype(y_ref.dtype)
        else:
            o_ref[0, sl, :] = o


def _tri_blockdiag(n, blk, rev):
    i = np.arange(n)
    same = (i[:, None] // blk) == (i[None, :] // blk)
    tri = (i[None, :] >= i[:, None]) if rev else (i[None, :] <= i[:, None])
    return jnp.asarray((same & tri).astype(np.float32), dtype=BF16)


def _hgrn_pass(proj3, lb, rev, final, ob=None, norm_g=None):
    b, t, _ = proj3.shape
    tt = _pick(t, 512)
    nt = t // tt
    tmap = (lambda i: nt - 1 - i) if rev else (lambda i: i)

    def col(off):
        return pl.BlockSpec((1, tt, DKA), lambda bi, h, ti: (bi, tmap(ti), off // DKA + h))

    head_vec = pl.BlockSpec((1, 1, DKA), lambda bi, h, ti: (h, 0, 0))
    in_specs = [col(OFF_AQ), col(OFF_AFB if rev else OFF_AFF), col(OFF_AI), head_vec,
                pl.BlockSpec((LANE, LANE), lambda bi, h, ti: (0, 0))]
    args = [proj3, proj3, proj3, lb.reshape(HA, 1, DKA), _tri_blockdiag(LANE, CHUNK_A, rev)]
    out_block = pl.BlockSpec((1, tt, DKA), lambda bi, h, ti: (bi, tmap(ti), h))
    if final:
        in_specs += [out_block, col(OFF_AG), head_vec]
        args += [ob, proj3, norm_g.reshape(HA, 1, DKA)]
        out_dtype = BF16
    else:
        out_dtype = F32
    return pl.pallas_call(
        functools.partial(_hgrn_kernel, rev, final, tt),
        grid=(b, HA, nt),
        in_specs=in_specs,
        out_specs=out_block,
        out_shape=jax.ShapeDtypeStruct((b, t, WA), out_dtype),
        scratch_shapes=[pltpu.VMEM((DKA, DKA), F32)],
        compiler_params=_params("parallel", "parallel", "arbitrary"),
    )(*args)


def _hgrn_mixer(proj3, lbs, norm_g):
    ob = _hgrn_pass(proj3, lbs[1], rev=True, final=False)
    return _hgrn_pass(proj3, lbs[0], rev=False, final=True, ob=ob, norm_g=norm_g)


def _rotary(z, cos2, sin2):
    return z * cos2 + pltpu.roll(z, DKB // 2, axis=1) * sin2


def _ret_state_kernel(nsub, k_ref, v_ref, cos_ref, sin_ref, lg_ref, sb_ref, st_ref):
    @pl.when(pl.program_id(2) == 0)
    def _():
        st_ref[...] = jnp.zeros_like(st_ref)

    cb = CHUNK_B
    lg2 = lg_ref[0]
    lg = lg2[:, :DKB]
    pos = lax.broadcasted_iota(jnp.int32, (cb, DKB), 0).astype(F32)
    dec = jnp.exp(pos * lg) * (DKB ** -0.5)
    order = list(range(nsub - 1, -1, -1))
    sls = [pl.ds(c * cb, cb) for c in order]
    kd = _each(lambda sl: _rotary(_f32(k_ref[0, sl, :]), cos_ref[sl, :], sin_ref[sl, :]) * dec, sls)
    kv = _each(lambda kk, sl: _mm_tn(kk, v_ref[0, sl, :]), kd, sls)
    e_chunk = jnp.exp(cb * lg2)
    st = st_ref[...]
    for c, kvi in zip(order, kv):
        sb_ref[0, 0, c] = st
        st = st * e_chunk + kvi
    st_ref[...] = st


def _ret_out_kernel(nsub, q_ref, k_ref, v_ref, cos_ref, sin_ref, lg_ref, sb_ref, gate_ref, gg_ref,
                    gb_ref, y_ref, st_ref):
    @pl.when(pl.program_id(2) == 0)
    def _():
        st_ref[...] = jnp.zeros_like(st_ref)

    cb = CHUNK_B
    lg2 = lg_ref[0]
    lg = lg2[:, :DKB]
    pos = lax.broadcasted_iota(jnp.int32, (cb, DKB), 0).astype(F32)
    ri = lax.broadcasted_iota(jnp.int32, (cb, cb), 0)
    ci = lax.broadcasted_iota(jnp.int32, (cb, cb), 1)
    decay = jnp.exp(jnp.abs(ri - ci).astype(F32) * lg2)
    dec_fwd = jnp.exp((pos + 1.0) * lg)
    dec_bwd = jnp.exp((cb - pos) * lg)
    dec_key = jnp.exp((cb - 1.0 - pos) * lg)
    e_chunk = jnp.exp(cb * lg2)

    sls = [pl.ds(c * cb, cb) for c in range(nsub)]
    q = _each(lambda sl: _rotary(_f32(q_ref[0, sl, :]), cos_ref[sl, :], sin_ref[sl, :]), sls)
    k = _each(lambda sl: _rotary(_f32(k_ref[0, sl, :]), cos_ref[sl, :], sin_ref[sl, :]) * (DKB ** -0.5), sls)
    v = _each(lambda sl: v_ref[0, sl, :], sls)
    scores = _each(lambda qq, kk: _mm_nt(qq, kk) * decay, q, k)
    o = _each(_mm, scores, v)
    o = _each(lambda oo, qq, c: oo + _mm(qq * dec_bwd, sb_ref[0, 0, c]), o, q, list(range(nsub)))
    kv = _each(lambda kk, vv: _mm_tn(kk * dec_key, vv), k, v)
    q_fwd = _each(lambda qq: qq * dec_fwd, q)
    st = st_ref[...]
    outs = []
    for oo, qf, kvi in zip(o, q_fwd, kv):
        outs.append(oo + _mm(qf, st))
        st = st * e_chunk + kvi
    st_ref[...] = st

    for sl, oo in zip(sls, outs):
        mu = jnp.mean(oo, axis=-1, keepdims=True)
        cen = oo - mu
        var = jnp.mean(cen * cen, axis=-1, keepdims=True)
        y = (cen * lax.rsqrt(var + RET_GN_EPS) * gg_ref[0] + gb_ref[0]) * _silu(_f32(gate_ref[0, sl, :]))
        y_ref[0, sl, :] = y.astype(y_ref.dtype)


def _ret_mixer(proj3, gn_g, gn_b):
    b, t, _ = proj3.shape
    cb = CHUNK_B
    tt = _pick(t, 4 * cb)
    nsub = tt // cb
    nt = t // tt
    half = DKB // 2
    inv = ROPE_BASE ** (-jnp.arange(0, DKB, 2, dtype=F32) / DKB)
    ang = jnp.arange(t, dtype=F32)[:, None] * inv
    cos2 = jnp.concatenate([jnp.cos(ang), jnp.cos(ang)], axis=-1)
    sin2 = jnp.concatenate([-jnp.sin(ang), jnp.sin(ang)], axis=-1)
    assert cos2.shape == (t, 2 * half)
    log_gamma = jnp.log1p(-jnp.exp2(-5.0 - jnp.arange(HB, dtype=F32)))
    lg = jnp.broadcast_to(log_gamma[:, None, None], (HB, 1, DVB))

    def col(off, width, tmap):
        return pl.BlockSpec((1, tt, width), lambda bi, h, c: (bi, tmap(c), off // width + h))

    def tab(tmap):
        return pl.BlockSpec((tt, DKB), lambda bi, h, c: (tmap(c), 0))

    lg_spec = pl.BlockSpec((1, 1, DVB), lambda bi, h, c: (h, 0, 0))
    rmap = lambda c: nt - 1 - c
    fmap = lambda c: c

    sb = pl.pallas_call(
        functools.partial(_ret_state_kernel, nsub),
        grid=(b, HB, nt),
        in_specs=[col(OFF_BK, DKB, rmap), col(OFF_BV, DVB, rmap), tab(rmap), tab(rmap), lg_spec],
        out_specs=pl.BlockSpec((1, 1, nsub, DKB, DVB), lambda bi, h, c: (bi, h, rmap(c), 0, 0)),
        out_shape=jax.ShapeDtypeStruct((b, HB, t // cb, DKB, DVB), F32),
        scratch_shapes=[pltpu.VMEM((DKB, DVB), F32)],
        compiler_params=_params("parallel", "parallel", "arbitrary"),
    )(proj3, proj3, cos2, sin2, lg)

    head_vec = pl.BlockSpec((1, 1, DVB), lambda bi, h, c: (h, 0, 0))
    return pl.pallas_call(
        functools.partial(_ret_out_kernel, nsub),
        grid=(b, HB, nt),
        in_specs=[col(OFF_BQ, DKB, fmap), col(OFF_BK, DKB, fmap), col(OFF_BV, DVB, fmap),
                  tab(fmap), tab(fmap), lg_spec,
                  pl.BlockSpec((1, 1, nsub, DKB, DVB), lambda bi, h, c: (bi, h, c, 0, 0)),
                  col(OFF_BG, DVB, fmap), head_vec, head_vec],
        out_specs=pl.BlockSpec((1, tt, DVB), lambda bi, h, c: (bi, c, h)),
        out_shape=jax.ShapeDtypeStruct((b, t, WB_V), BF16),
        scratch_shapes=[pltpu.VMEM((DKB, DVB), F32)],
        compiler_params=_params("parallel", "parallel", "arbitrary"),
    )(proj3, proj3, proj3, cos2, sin2, lg, sb, proj3,
      gn_g.reshape(HB, 1, DVB), gn_b.reshape(HB, 1, DVB))


def _group_sum(x, gmat):
    cols = [_split_dot_right(x[:, i * LANE:(i + 1) * LANE], gmat) for i in range(x.shape[1] // LANE)]
    return jnp.concatenate(cols, axis=1)


def _rwkv_prep_kernel(tt, r_ref, k_ref, v_ref, c_ref, rp_ref, kp_ref, vp_ref, cp_ref,
                      rn_ref, kn_ref, vn_ref, cn_ref, mur_ref, muk_ref, muv_ref, muc_ref,
                      w0_ref, wl_ref, a0_ref, al_ref, kk_ref, ka_ref, rk_ref, gmat_ref,
                      r_o, v_o, kk_o, bonus_o, lw0_o, lw1_o, k0_o, k1_o, ka0_o, ka1_o):
    ti = pl.program_id(1)
    has_prev = (ti > 0).astype(F32)
    has_next = (ti < pl.num_programs(1) - 1).astype(F32)
    rows = lax.broadcasted_iota(jnp.int32, (tt, 1), 0)

    def shifted(p_ref, prev_ref, next_ref, mu_ref):
        p = _f32(p_ref[0])
        prev_row = _f32(prev_ref[0, HALO - 1:HALO, :]) * has_prev
        next_row = _f32(next_ref[0, 0:1, :]) * has_next
        before = jnp.where(rows == 0, prev_row, pltpu.roll(p, 1, axis=0))
        after = jnp.where(rows == tt - 1, next_row, pltpu.roll(p, tt - 1, axis=0))
        return p + mu_ref[0:1, :] * (before - p) + mu_ref[1:2, :] * (after - p)

    r = shifted(r_ref, rp_ref, rn_ref, mur_ref)
    k = shifted(k_ref, kp_ref, kn_ref, muk_ref)
    v = shifted(v_ref, vp_ref, vn_ref, muv_ref)
    codes = shifted(c_ref, cp_ref, cn_ref, muc_ref)
    gmat = gmat_ref[...]

    kraw = k * kk_ref[...]
    kk = kraw * lax.rsqrt(_group_sum(kraw * kraw, gmat) + 1e-12)
    r_o[0] = r
    v_o[0] = v
    kk_o[0] = kk
    tcodes = jnp.tanh(codes)
    bonus = jnp.zeros_like(r)
    for d, (lw_o, kd_o, ka_o) in enumerate(((lw0_o, k0_o, ka0_o), (lw1_o, k1_o, ka1_o))):
        w = w0_ref[d:d + 1, :] + _mm(tcodes, wl_ref[d])
        lw_o[0] = -np.exp(-0.5).astype(np.float32) * jax.nn.sigmoid(w)
        a = jax.nn.sigmoid(a0_ref[d:d + 1, :] + _mm(codes, al_ref[d]))
        kd = k * (1.0 + (a - 1.0) * ka_ref[...])
        kd_o[0] = kd
        ka_o[0] = kk * a
        bonus = bonus + _group_sum(r * kd * rk_ref[...], gmat) * v
    bonus_o[0] = bonus


def _rwkv_scan_kernel(rev, final, tt, r_ref, lw_ref, k_ref, v_ref, ka_ref, kk_ref, tri_ref, *rest):
    if final:
        ob_ref, bonus_ref, gate_ref, lng_ref, lnb_ref, y_ref, h_ref = rest
    else:
        o_ref, h_ref = rest

    @pl.when(pl.program_id(2) == 0)
    def _():
        h_ref[...] = jnp.zeros_like(h_ref)

    n2 = 2 * CHUNK_C
    lane = lax.broadcasted_iota(jnp.int32, (1, LANE), 1)
    head0 = lane < NC
    ri = lax.broadcasted_iota(jnp.int32, (n2, n2), 0)
    ci = lax.broadcasted_iota(jnp.int32, (n2, n2), 1)
    strict = (ci > ri) if rev else (ci < ri)
    incl = (ci >= ri) if rev else (ci <= ri)
    eye = ri == ci
    nchunk = tt // CHUNK_C

    def stack(z):
        return jnp.concatenate([jnp.where(head0, z, 0.0), jnp.where(head0, 0.0, z)], axis=0)

    order = [(nchunk - 1 - i) if rev else i for i in range(nchunk)]
    items = [(pl.ds(c * CHUNK_C, CHUNK_C), pl.ds(p * LANE, LANE), p) for c in order for p in range(PAIRS_C)]
    rows = [it[0] for it in items]
    cols = [it[1] for it in items]

    lws = _each(lambda sl, cl: lw_ref[0, sl, cl], rows, cols)
    g = _each(lambda lw: _split_dot(tri_ref[...], lw), lws)
    e_end = _each(lambda gu: jnp.exp(gu[0:1, :] if rev else gu[CHUNK_C - 1:CHUNK_C, :]), g)
    e_neg = _each(lambda gu: jnp.exp(-gu), g)
    bs = _each(lambda sl, cl, gu, lw: stack(-kk_ref[0, sl, cl] * jnp.exp(gu - lw)), rows, cols, g, lws)
    as_ = _each(lambda sl, cl, en: stack(ka_ref[0, sl, cl] * en), rows, cols, e_neg)
    ks = _each(lambda sl, cl, en: stack(k_ref[0, sl, cl] * en), rows, cols, e_neg)
    rs = _each(lambda sl, cl, gu: stack(r_ref[0, sl, cl] * jnp.exp(gu)), rows, cols, g)
    vs = _each(lambda sl, cl: stack(v_ref[0, sl, cl]), rows, cols)

    ak = _each(lambda a, k: jnp.concatenate([a, k], axis=0), as_, ks)
    sc_b = _each(_mm_nt, bs, ak)
    sc_r = _each(_mm_nt, rs, ak)
    a_ab = _each(lambda z: jnp.where(strict, z[:, :n2], 0.0), sc_b)
    a_ak = _each(lambda z: jnp.where(strict, z[:, n2:], 0.0), sc_b)
    m_rak = _each(lambda z: jnp.where(jnp.concatenate([incl, incl], axis=1), z, 0.0), sc_r)
    tinv = _each(lambda a: jnp.where(eye, 1.0, 0.0) + a, a_ab)
    pw = _each(lambda a: _mm(a, a), a_ab)
    akv = _each(_mm, a_ak, vs)
    nsteps = int(np.log2(CHUNK_C)) - 1
    for i in range(nsteps):
        if i < nsteps - 1:
            both = _each(lambda t, p: _mm(jnp.concatenate([t, p], axis=0), p), tinv, pw)
            tinv = _each(lambda t, b: t + b[:n2], tinv, both)
            pw = _each(lambda b: b[n2:], both)
        else:
            tinv = _each(lambda t, p: t + _mm(t, p), tinv, pw)
    pwv = _each(lambda t, b, x: _mm(t, jnp.concatenate([b, x], axis=1)), tinv, bs, akv)
    low = _each(lambda z, v: jnp.concatenate([z, jnp.concatenate([jnp.zeros_like(v), v], axis=1)], axis=0),
                pwv, vs)
    x1 = _each(_mm, m_rak, low)
    x2 = _each(lambda a, k, e, z: _mm_tn(jnp.concatenate([a * e, k * e], axis=0), z), as_, ks, e_end, low)
    top = _each(lambda r, z, e, y: jnp.concatenate([r + z[:, :LANE], jnp.where(eye, e, 0.0) + y[:, :LANE]], axis=0),
                rs, x1, e_end, x2)
    o_in = _each(lambda z: z[:, LANE:], x1)
    dmat = _each(lambda y: y[:, LANE:], x2)

    outs = []
    h = [h_ref[p] for p in range(PAIRS_C)]
    for (sl, cl, p), tp, oi, dm in zip(items, top, o_in, dmat):
        both = _mm(tp, h[p])
        h[p] = both[n2:] + dm
        os_ = both[:n2] + oi
        outs.append((sl, cl, os_[:CHUNK_C, :] + os_[CHUNK_C:, :]))
    for p in range(PAIRS_C):
        h_ref[p] = h[p]

    for sl, cl, o in outs:
        if final:
            o = o + ob_ref[0, sl, cl]
            inv_n = 1.0 / NC
            s0 = jnp.sum(jnp.where(head0, o, 0.0), axis=-1, keepdims=True)
            s1 = jnp.sum(jnp.where(head0, 0.0, o), axis=-1, keepdims=True)
            cen = o - jnp.where(head0, s0, s1) * inv_n
            c2 = cen * cen
            v0 = jnp.sum(jnp.where(head0, c2, 0.0), axis=-1, keepdims=True)
            v1 = jnp.sum(jnp.where(head0, 0.0, c2), axis=-1, keepdims=True)
            var = jnp.where(head0, v0, v1) * inv_n
            y = cen * lax.rsqrt(var + RWKV_GN_EPS) * lng_ref[:, cl] + lnb_ref[:, cl] + bonus_ref[0, sl, cl]
            y_ref[0, sl, cl] = (y * _silu(_f32(gate_ref[0, sl, cl]))).astype(y_ref.dtype)
        else:
            o_ref[0, sl, cl] = o


def _rwkv_mixer(proj3, mu, w0, w_lora_b, a0, a_lora_b, k_k, k_a, r_k, ln_g, ln_b):
    b, t, _ = proj3.shape
    tt = _pick(t, 256)
    nt = t // tt
    sub = tt // HALO

    def cur(off, width):
        return pl.BlockSpec((1, tt, width), lambda bi, ti: (bi, ti, off // width))

    def prev(off, width):
        return pl.BlockSpec((1, HALO, width),
                            lambda bi, ti: (bi, jnp.maximum(ti * sub - 1, 0), off // width))

    def nxt(off, width):
        return pl.BlockSpec((1, HALO, width),
                            lambda bi, ti: (bi, jnp.minimum((ti + 1) * sub, nt * sub - 1), off // width))

    def full(shape):
        return pl.BlockSpec(shape, lambda bi, ti: (0,) * len(shape))

    segs = ((OFF_CR, WC), (OFF_CK, WC), (OFF_CV, WC), (OFF_CC, N_CODES))
    in_specs = ([cur(o, w) for o, w in segs] + [prev(o, w) for o, w in segs] + [nxt(o, w) for o, w in segs]
                + [full((2, WC))] * 3 + [full((2, N_CODES))]
                + [full((2, WC)), full((2, N_CODES, WC)), full((2, WC)), full((2, N_CODES, WC))]
                + [full((1, WC))] * 3 + [full((LANE, LANE))])
    wl = jnp.zeros((2, N_CODES, WC), F32)
    al = jnp.zeros((2, N_CODES, WC), F32)
    for d in range(2):
        wl = wl.at[d, d * W_LORA:(d + 1) * W_LORA].set(w_lora_b[d])
        al = al.at[d, 2 * W_LORA + d * A_LORA:2 * W_LORA + (d + 1) * A_LORA].set(a_lora_b[d])
    lane_i = np.arange(LANE)
    gmat = jnp.asarray((lane_i[:, None] // NC == lane_i[None, :] // NC).astype(np.float32), dtype=BF16)
    row_out = pl.BlockSpec((1, tt, WC), lambda bi, ti: (bi, ti, 0))
    outs = pl.pallas_call(
        functools.partial(_rwkv_prep_kernel, tt),
        grid=(b, nt),
        in_specs=in_specs,
        out_specs=[row_out] * 10,
        out_shape=[jax.ShapeDtypeStruct((b, t, WC), F32)] * 10,
        compiler_params=_params("parallel", "parallel"),
    )(*([proj3] * 12), mu[:, :WC], mu[:, WC:2 * WC], mu[:, 2 * WC:3 * WC], mu[:, 3 * WC:],
      w0, wl.astype(BF16), a0, al.astype(BF16), k_k.reshape(1, WC), k_a.reshape(1, WC),
      r_k.reshape(1, WC), gmat)
    r, v, kk, bonus, lw0, lw1, k0, k1, ka0, ka1 = outs

    ts = _pick(t, 512)
    ns = t // ts

    def scan(rev, final, lw, kd, ka, ob=None):
        wblk = PAIRS_C * LANE
        tmap = (lambda i: ns - 1 - i) if rev else (lambda i: i)
        blk = pl.BlockSpec((1, ts, wblk), lambda bi, h, ti: (bi, tmap(ti), h))
        head_vec = pl.BlockSpec((1, wblk), lambda bi, h, ti: (0, h))
        in_specs = [blk] * 6 + [pl.BlockSpec((CHUNK_C, CHUNK_C), lambda bi, h, ti: (0, 0))]
        ii = np.arange(CHUNK_C)
        tri = (ii[None, :] >= ii[:, None]) if rev else (ii[None, :] <= ii[:, None])
        args = [r, lw, kd, v, ka, kk, jnp.asarray(tri.astype(np.float32), dtype=BF16)]
        if final:
            gate = pl.BlockSpec((1, ts, wblk), lambda bi, h, ti: (bi, tmap(ti), OFF_CG // wblk + h))
            in_specs += [blk, blk, gate, head_vec, head_vec]
            args += [ob, bonus, proj3, ln_g.reshape(1, WC), ln_b.reshape(1, WC)]
        return pl.pallas_call(
            functools.partial(_rwkv_scan_kernel, rev, final, ts),
            grid=(b, WC // wblk, ns),
            in_specs=in_specs,
            out_specs=blk,
            out_shape=jax.ShapeDtypeStruct((b, t, WC), BF16 if final else F32),
            scratch_shapes=[pltpu.VMEM((PAIRS_C, LANE, LANE), F32)],
            compiler_params=_params("parallel", "parallel", "arbitrary"),
        )(*args)

    ob = scan(True, False, lw1, k1, ka1)
    return scan(False, True, lw0, k0, ka0, ob=ob)


def _merge_kernel(final, ya_ref, yb_ref, yc_ref, mg_ref, x_ref, wa_ref, wb_ref, wc_ref, wo_ref,
                  fg_ref, o_ref):
    za = jnp.dot(ya_ref[...], wa_ref[...], preferred_element_type=F32)
    zb = jnp.dot(yb_ref[...], wb_ref[...], preferred_element_type=F32)
    zc = jnp.dot(yc_ref[...], wc_ref[...], preferred_element_type=F32)
    mg = _f32(mg_ref[...])
    mixed = (jax.nn.sigmoid(mg[:, :D_MODEL]) * za
             + jax.nn.sigmoid(mg[:, D_MODEL:2 * D_MODEL]) * zb
             + jax.nn.sigmoid(mg[:, 2 * D_MODEL:]) * zc)
    out = x_ref[...] + jnp.dot(mixed.astype(BF16), wo_ref[...], preferred_element_type=F32)
    if final:
        ms = jnp.mean(out * out, axis=-1, keepdims=True)
        out = out * lax.rsqrt(ms + EPS) * fg_ref[...]
    o_ref[...] = out


def _merge(ya, yb, yc, proj, x2, wa, wb, wc, wo, fg, final):
    m = x2.shape[0]
    tm = _pick(m, 256)

    def rows(width):
        return pl.BlockSpec((tm, width), lambda i: (i, 0))

    def full(shape):
        return pl.BlockSpec(shape, lambda i: (0, 0))

    return pl.pallas_call(
        functools.partial(_merge_kernel, final),
        grid=(m // tm,),
        in_specs=[rows(WA), rows(WB_V), rows(WC), rows(3 * D_MODEL), rows(D_MODEL),
                  full((WA, D_MODEL)), full((WB_V, D_MODEL)), full((WC, D_MODEL)),
                  full((D_MODEL, D_MODEL)), full((1, D_MODEL))],
        out_specs=rows(D_MODEL),
        out_shape=jax.ShapeDtypeStruct((m, D_MODEL), F32),
        compiler_params=_params("parallel"),
    )(ya, yb, yc, proj, x2, wa.astype(BF16), wb.astype(BF16), wc.astype(BF16), wo.astype(BF16),
      fg.reshape(1, D_MODEL))


def kernel(x, norm_g, w_in, hgrn_lb_logits, hgrn_norm_g, ret_norm_g, ret_norm_b, rwkv_mu, rwkv_w0,
           rwkv_w_lora_b, rwkv_a0, rwkv_a_lora_b, rwkv_k_k, rwkv_k_a, rwkv_r_k, rwkv_ln_g,
           rwkv_ln_b, w_branch_a, w_branch_b, w_branch_c, w_out, final_norm_g):
    b, t, d = x.shape
    m = b * t
    depth = w_in.shape[0]
    p_lb = jax.nn.softmax(hgrn_lb_logits.astype(F32), axis=0)
    lbs = jnp.cumsum(p_lb, axis=0) - p_lb[0:1]
    x2 = x.reshape(m, d)
    for l in range(depth):
        proj = _inproj(x2, norm_g[l], _permute_w_in(w_in[l]))
        proj3 = proj.reshape(b, t, N_PAD)
        ya = _hgrn_mixer(proj3, lbs[l], hgrn_norm_g[l])
        yb = _ret_mixer(proj3, ret_norm_g[l], ret_norm_b[l])
        yc = _rwkv_mixer(proj3, rwkv_mu[l], rwkv_w0[l], rwkv_w_lora_b[l], rwkv_a0[l],
                         rwkv_a_lora_b[l], rwkv_k_k[l], rwkv_k_a[l], rwkv_r_k[l],
                         rwkv_ln_g[l], rwkv_ln_b[l])
        x2 = _merge(ya.reshape(m, WA), yb.reshape(m, WB_V), yc.reshape(m, WC), proj, x2,
                    w_branch_a[l], w_branch_b[l], w_branch_c[l], w_out[l], final_norm_g,
                    final=(l == depth - 1))
    return x2.reshape(b, t, d)
```

```python
import functools

import numpy as np
import jax
import jax.numpy as jnp
from jax import lax
from jax.experimental import pallas as pl
from jax.experimental.pallas import tpu as pltpu

F32 = jnp.float32
BF16 = jnp.bfloat16

D_MODEL = 1024
EPS = 1e-6
F_MIN = 1e-6
DKA = 128
HA = D_MODEL // DKA
WA = HA * DKA
HB = 8
DKB = D_MODEL // HB
DVB = 2 * DKB
WB_QK = HB * DKB
WB_V = HB * DVB
ROPE_BASE = 10000.0
RET_GN_EPS = 1e-5
NC = 64
HC = D_MODEL // NC
WC = HC * NC
W_LORA = 64
A_LORA = 64
RWKV_GN_EPS = 64e-5
N_CODES = 2 * W_LORA + 2 * A_LORA
C_SHIFT = 3 * WC + N_CODES

SRC_A = 0
SRC_B = 5 * WA
SRC_CS = SRC_B + 2 * WB_QK + 2 * WB_V
SRC_CG = SRC_CS + C_SHIFT
SRC_MERGE = SRC_CG + WC
N_IN = SRC_MERGE + 3 * D_MODEL

OFF_MERGE = 0
OFF_AQ = OFF_MERGE + 3 * D_MODEL
OFF_AFF = OFF_AQ + WA
OFF_AFB = OFF_AFF + WA
OFF_AI = OFF_AFB + WA
OFF_AG = OFF_AI + WA
OFF_BQ = OFF_AG + WA
OFF_BK = OFF_BQ + WB_QK
OFF_BV = OFF_BK + WB_QK
OFF_BG = OFF_BV + WB_V
OFF_CG = OFF_BG + WB_V
OFF_CR = OFF_CG + WC
OFF_CK = OFF_CR + WC
OFF_CV = OFF_CK + WC
OFF_CC = OFF_CV + WC
LANE = 128
PROJ_DTYPE = BF16
HALO = 16
PROJ_TN = 512
N_PAD = -(-(OFF_CC + N_CODES) // PROJ_TN) * PROJ_TN

CHUNK_A = 128
GROUP_A = 8
HEADS_A = 2
CHUNK_B = 256
CHUNK_C = 64
PAIRS_C = 4
VMEM_LIMIT = 48 * 1024 * 1024

NT_DIMS = (((1,), (1,)), ((), ()))
TN_DIMS = (((0,), (0,)), ((), ()))


def _mm(a, b):
    return jnp.dot(a.astype(BF16), b.astype(BF16), preferred_element_type=F32)


def _mm_nt(a, b):
    return lax.dot_general(a.astype(BF16), b.astype(BF16), NT_DIMS, preferred_element_type=F32)


def _mm_tn(a, b):
    return lax.dot_general(a.astype(BF16), b.astype(BF16), TN_DIMS, preferred_element_type=F32)


def _split_dot(ones_mat, x):
    hi = x.astype(BF16)
    r1 = x - hi.astype(F32)
    mid = r1.astype(BF16)
    lo = (r1 - mid.astype(F32)).astype(BF16)
    out = jnp.dot(ones_mat, hi, preferred_element_type=F32)
    out = out + jnp.dot(ones_mat, mid, preferred_element_type=F32)
    return out + jnp.dot(ones_mat, lo, preferred_element_type=F32)


def _split_dot_right(x, ones_mat):
    hi = x.astype(BF16)
    r1 = x - hi.astype(F32)
    mid = r1.astype(BF16)
    lo = (r1 - mid.astype(F32)).astype(BF16)
    out = jnp.dot(hi, ones_mat, preferred_element_type=F32)
    out = out + jnp.dot(mid, ones_mat, preferred_element_type=F32)
    return out + jnp.dot(lo, ones_mat, preferred_element_type=F32)


def _f32(z):
    return z.astype(F32)


def _sigmoid(z):
    return 0.5 * jnp.tanh(0.5 * z) + 0.5


def _silu(z):
    return z * _sigmoid(z)


def _pick(n, pref):
    t = min(n, pref)
    assert n % t == 0, (n, pref)
    return t


def _params(*sem):
    return pltpu.CompilerParams(dimension_semantics=sem, vmem_limit_bytes=VMEM_LIMIT)


def _inproj_kernel(x_ref, g_ref, w_ref, o_ref, h_ref):
    @pl.when(pl.program_id(1) == 0)
    def _():
        x = x_ref[...]
        ms = jnp.mean(x * x, axis=-1, keepdims=True)
        h_ref[...] = (x * lax.rsqrt(ms + EPS) * g_ref[...]).astype(BF16)

    o_ref[...] = jnp.dot(h_ref[...], w_ref[...], preferred_element_type=F32).astype(o_ref.dtype)


def _inproj(x2, g, w_bf16):
    m = x2.shape[0]
    tm = _pick(m, 2048)
    return pl.pallas_call(
        _inproj_kernel,
        grid=(m // tm, N_PAD // PROJ_TN),
        in_specs=[
            pl.BlockSpec((tm, D_MODEL), lambda i, j: (i, 0)),
            pl.BlockSpec((1, D_MODEL), lambda i, j: (0, 0)),
            pl.BlockSpec((D_MODEL, PROJ_TN), lambda i, j: (0, j)),
        ],
        out_specs=pl.BlockSpec((tm, PROJ_TN), lambda i, j: (i, j)),
        out_shape=jax.ShapeDtypeStruct((m, N_PAD), PROJ_DTYPE),
        scratch_shapes=[pltpu.VMEM((tm, D_MODEL), BF16)],
        compiler_params=_params("parallel", "arbitrary"),
    )(x2, g.reshape(1, D_MODEL), w_bf16)


def _permute_w_in(w):
    parts = [w[:, SRC_MERGE:N_IN], w[:, SRC_A:SRC_CS], w[:, SRC_CG:SRC_MERGE], w[:, SRC_CS:SRC_CG]]
    wp = jnp.concatenate(parts, axis=1)
    return jnp.pad(wp, ((0, 0), (0, N_PAD - N_IN))).astype(BF16)


def _each(fn, *lists):
    return [fn(*items) for items in zip(*lists)]


def _hgrn_kernel(rev, final, tt, q_ref, f_ref, v_ref, lb_ref, tri_ref, *rest):
    if final:
        ob_ref, gate_ref, ng_ref, y_ref, st_ref = rest
    else:
        o_ref, st_ref = rest

    @pl.when(pl.program_id(2) == 0)
    def _():
        st_ref[...] = jnp.zeros_like(st_ref)

    n = CHUNK_A
    ri = lax.broadcasted_iota(jnp.int32, (n, n), 0)
    ci = lax.broadcasted_iota(jnp.int32, (n, n), 1)
    g_bits = int(np.log2(GROUP_A))
    diag_mask = ((ri >> g_bits) == (ci >> g_bits)) & ((ci >= ri) if rev else (ci <= ri))
    halves = [GROUP_A << i for i in range(int(np.log2(n // GROUP_A)))]
    level_masks = []
    for half in halves:
        hb = int(np.log2(half))
        q_side, k_side = (0, 1) if rev else (1, 0)
        level_masks.append(((ri >> (hb + 1)) == (ci >> (hb + 1)))
                           & (((ri >> hb) & 1) == q_side) & (((ci >> hb) & 1) == k_side))

    def ref_rows(b, blk, idx):
        picked = b.reshape(n // blk, blk, DKA)[:, idx:idx + 1, :]
        return jnp.broadcast_to(picked, (n // blk, blk, DKA)).reshape(n, DKA)

    nchunk = tt // n
    order = [(nchunk - 1 - i) if rev else i for i in range(nchunk)]
    items = [(pl.ds(c * n, n), pl.ds(hh * DKA, DKA), hh) for c in order for hh in range(HEADS_A)]
    sls = [it[0] for it in items]
    cls = [it[1] for it in items]
    lbs = [lb_ref[it[2]] for it in items]
    sig = _each(lambda sl, cl: _sigmoid(_f32(f_ref[0, sl, cl])), sls, cls)
    lf = _each(lambda z, lb: jnp.log(jnp.maximum(lb + (1.0 - lb) * z, F_MIN)), sig, lbs)
    k = _each(lambda z, lb: (1.0 - lb) * (1.0 - z), sig, lbs)
    q = _each(lambda sl, cl: _silu(_f32(q_ref[0, sl, cl])), sls, cls)
    v = _each(lambda sl, cl: v_ref[0, sl, cl], sls, cls)
    b = _each(lambda z: _split_dot(tri_ref[...], z), lf)
    d0 = _each(lambda z: z - ref_rows(z, GROUP_A, GROUP_A // 2 if rev else GROUP_A // 2 - 1), b)
    qb = _each(lambda z: z.astype(BF16), q)
    kb = _each(lambda z: z.astype(BF16), k)
    scores = _each(lambda qq, kk, d: jnp.where(
        diag_mask, _mm_nt(qq * jnp.exp(d).astype(BF16), kk * jnp.exp(-d).astype(BF16)), 0.0), qb, kb, d0)
    for half, mask in zip(halves, level_masks):
        e = _each(lambda z: jnp.exp(-jnp.abs(z - ref_rows(z, 2 * half, half if rev else half - 1))).astype(BF16),
                  b)
        scores = _each(lambda sc, qq, kk, ee: jnp.where(mask, _mm_nt(qq * ee, kk * ee), sc), scores, qb, kb, e)
    o_intra = _each(_mm, scores, v)
    b_end = _each(lambda z: z[0:1, :] if rev else z[n - 1:n, :], b)
    q_in = _each(lambda qq, z: qq * jnp.exp(z), q, b)
    kv = _each(lambda vv, kk, z, ze: _mm_tn(vv, kk * jnp.exp(ze - z)), v, k, b, b_end)
    e_end = _each(jnp.exp, b_end)

    outs = []
    st = [st_ref[hh] for hh in range(HEADS_A)]
    for (sl, cl, hh), oi, qi, kvi, ee in zip(items, o_intra, q_in, kv, e_end):
        outs.append((sl, cl, hh, oi + _mm_nt(qi, st[hh])))
        st[hh] = st[hh] * ee + kvi
    for hh in range(HEADS_A):
        st_ref[hh] = st[hh]

    for sl, cl, hh, o in outs:
        if final:
            o = o + ob_ref[0, sl, cl]
            ms = jnp.mean(o * o, axis=-1, keepdims=True)
            y = o * lax.rsqrt(ms + EPS) * ng_ref[hh] * _silu(_f32(gate_ref[0, sl, cl]))
            y_ref[0, sl, cl] = y.astype(y_ref.dtype)
        else:
            o_ref[0, sl, cl] = o


def _tri_blockdiag(n, blk, rev):
    i = np.arange(n)
    same = (i[:, None] // blk) == (i[None, :] // blk)
    tri = (i[None, :] >= i[:, None]) if rev else (i[None, :] <= i[:, None])
    return jnp.asarray((same & tri).astype(np.float32), dtype=BF16)


def _hgrn_pass(proj3, lb, rev, final, ob=None, norm_g=None):
    b, t, _ = proj3.shape
    tt = _pick(t, 512)
    nt = t // tt
    tmap = (lambda i: nt - 1 - i) if rev else (lambda i: i)

    wblk = HEADS_A * DKA

    def col(off):
        return pl.BlockSpec((1, tt, wblk), lambda bi, h, ti: (bi, tmap(ti), off // wblk + h))

    head_vec = pl.BlockSpec((HEADS_A, 1, DKA), lambda bi, h, ti: (h, 0, 0))
    in_specs = [col(OFF_AQ), col(OFF_AFB if rev else OFF_AFF), col(OFF_AI), head_vec,
                pl.BlockSpec((LANE, LANE), lambda bi, h, ti: (0, 0))]
    args = [proj3, proj3, proj3, lb.reshape(HA, 1, DKA), _tri_blockdiag(LANE, CHUNK_A, rev)]
    out_block = pl.BlockSpec((1, tt, wblk), lambda bi, h, ti: (bi, tmap(ti), h))
    if final:
        in_specs += [out_block, col(OFF_AG), head_vec]
        args += [ob, proj3, norm_g.reshape(HA, 1, DKA)]
        out_dtype = BF16
    else:
        out_dtype = F32
    return pl.pallas_call(
        functools.partial(_hgrn_kernel, rev, final, tt),
        grid=(b, HA // HEADS_A, nt),
        in_specs=in_specs,
        out_specs=out_block,
        out_shape=jax.ShapeDtypeStruct((b, t, WA), out_dtype),
        scratch_shapes=[pltpu.VMEM((HEADS_A, DKA, DKA), F32)],
        compiler_params=_params("parallel", "parallel", "arbitrary"),
    )(*args)


def _hgrn_mixer(proj3, lbs, norm_g):
    ob = _hgrn_pass(proj3, lbs[1], rev=True, final=False)
    return _hgrn_pass(proj3, lbs[0], rev=False, final=True, ob=ob, norm_g=norm_g)


def _rotary(z, cos2, sin2):
    return z * cos2 + pltpu.roll(z, DKB // 2, axis=1) * sin2


def _ret_state_kernel(nsub, k_ref, v_ref, cos_ref, sin_ref, lg_ref, sb_ref, st_ref):
    @pl.when(pl.program_id(2) == 0)
    def _():
        st_ref[...] = jnp.zeros_like(st_ref)

    cb = CHUNK_B
    lg2 = lg_ref[0]
    lg = lg2[:, :DKB]
    pos = lax.broadcasted_iota(jnp.int32, (cb, DKB), 0).astype(F32)
    dec = jnp.exp(pos * lg) * (DKB ** -0.5)
    order = list(range(nsub - 1, -1, -1))
    sls = [pl.ds(c * cb, cb) for c in order]
    kd = _each(lambda sl: _rotary(_f32(k_ref[0, sl, :]), cos_ref[sl, :], sin_ref[sl, :]) * dec, sls)
    kv = _each(lambda kk, sl: _mm_tn(kk, v_ref[0, sl, :]), kd, sls)
    e_chunk = jnp.exp(cb * lg2)
    st = st_ref[...]
    for c, kvi in zip(order, kv):
        sb_ref[0, 0, c] = st
        st = st * e_chunk + kvi
    st_ref[...] = st


def _ret_out_kernel(nsub, q_ref, k_ref, v_ref, cos_ref, sin_ref, lg_ref, sb_ref, gate_ref, gg_ref,
                    gb_ref, y_ref, st_ref):
    @pl.when(pl.program_id(2) == 0)
    def _():
        st_ref[...] = jnp.zeros_like(st_ref)

    cb = CHUNK_B
    lg2 = lg_ref[0]
    lg = lg2[:, :DKB]
    pos = lax.broadcasted_iota(jnp.int32, (cb, DKB), 0).astype(F32)
    ri = lax.broadcasted_iota(jnp.int32, (cb, cb), 0)
    ci = lax.broadcasted_iota(jnp.int32, (cb, cb), 1)
    decay = jnp.exp(jnp.abs(ri - ci).astype(F32) * lg2)
    dec_fwd = jnp.exp((pos + 1.0) * lg)
    dec_bwd = jnp.exp((cb - pos) * lg)
    dec_key = jnp.exp((cb - 1.0 - pos) * lg)
    e_chunk = jnp.exp(cb * lg2)

    sls = [pl.ds(c * cb, cb) for c in range(nsub)]
    q = _each(lambda sl: _rotary(_f32(q_ref[0, sl, :]), cos_ref[sl, :], sin_ref[sl, :]), sls)
    k = _each(lambda sl: _rotary(_f32(k_ref[0, sl, :]), cos_ref[sl, :], sin_ref[sl, :]) * (DKB ** -0.5), sls)
    v = _each(lambda sl: v_ref[0, sl, :], sls)
    scores = _each(lambda qq, kk: _mm_nt(qq, kk) * decay, q, k)
    o = _each(_mm, scores, v)
    o = _each(lambda oo, qq, c: oo + _mm(qq * dec_bwd, sb_ref[0, 0, c]), o, q, list(range(nsub)))
    kv = _each(lambda kk, vv: _mm_tn(kk * dec_key, vv), k, v)
    q_fwd = _each(lambda qq: qq * dec_fwd, q)
    st = st_ref[...]
    outs = []
    for oo, qf, kvi in zip(o, q_fwd, kv):
        outs.append(oo + _mm(qf, st))
        st = st * e_chunk + kvi
    st_ref[...] = st

    for sl, oo in zip(sls, outs):
        mu = jnp.mean(oo, axis=-1, keepdims=True)
        cen = oo - mu
        var = jnp.mean(cen * cen, axis=-1, keepdims=True)
        y = (cen * lax.rsqrt(var + RET_GN_EPS) * gg_ref[0] + gb_ref[0]) * _silu(_f32(gate_ref[0, sl, :]))
        y_ref[0, sl, :] = y.astype(y_ref.dtype)


def _ret_mixer(proj3, gn_g, gn_b):
    b, t, _ = proj3.shape
    cb = CHUNK_B
    tt = _pick(t, 4 * cb)
    nsub = tt // cb
    nt = t // tt
    half = DKB // 2
    inv = ROPE_BASE ** (-jnp.arange(0, DKB, 2, dtype=F32) / DKB)
    ang_hi = (jnp.arange(t // cb, dtype=F32) * cb)[:, None, None] * inv
    ang_lo = jnp.arange(cb, dtype=F32)[None, :, None] * inv
    cos = (jnp.cos(ang_hi) * jnp.cos(ang_lo) - jnp.sin(ang_hi) * jnp.sin(ang_lo)).reshape(t, half)
    sin = (jnp.sin(ang_hi) * jnp.cos(ang_lo) + jnp.cos(ang_hi) * jnp.sin(ang_lo)).reshape(t, half)
    cos2 = jnp.concatenate([cos, cos], axis=-1)
    sin2 = jnp.concatenate([-sin, sin], axis=-1)
    log_gamma = jnp.log1p(-jnp.exp2(-5.0 - jnp.arange(HB, dtype=F32)))
    lg = jnp.broadcast_to(log_gamma[:, None, None], (HB, 1, DVB))

    def col(off, width, tmap):
        return pl.BlockSpec((1, tt, width), lambda bi, h, c: (bi, tmap(c), off // width + h))

    def tab(tmap):
        return pl.BlockSpec((tt, DKB), lambda bi, h, c: (tmap(c), 0))

    lg_spec = pl.BlockSpec((1, 1, DVB), lambda bi, h, c: (h, 0, 0))
    rmap = lambda c: nt - 1 - c
    fmap = lambda c: c

    sb = pl.pallas_call(
        functools.partial(_ret_state_kernel, nsub),
        grid=(b, HB, nt),
        in_specs=[col(OFF_BK, DKB, rmap), col(OFF_BV, DVB, rmap), tab(rmap), tab(rmap), lg_spec],
        out_specs=pl.BlockSpec((1, 1, nsub, DKB, DVB), lambda bi, h, c: (bi, h, rmap(c), 0, 0)),
        out_shape=jax.ShapeDtypeStruct((b, HB, t // cb, DKB, DVB), F32),
        scratch_shapes=[pltpu.VMEM((DKB, DVB), F32)],
        compiler_params=_params("parallel", "parallel", "arbitrary"),
    )(proj3, proj3, cos2, sin2, lg)

    head_vec = pl.BlockSpec((1, 1, DVB), lambda bi, h, c: (h, 0, 0))
    return pl.pallas_call(
        functools.partial(_ret_out_kernel, nsub),
        grid=(b, HB, nt),
        in_specs=[col(OFF_BQ, DKB, fmap), col(OFF_BK, DKB, fmap), col(OFF_BV, DVB, fmap),
                  tab(fmap), tab(fmap), lg_spec,
                  pl.BlockSpec((1, 1, nsub, DKB, DVB), lambda bi, h, c: (bi, h, c, 0, 0)),
                  col(OFF_BG, DVB, fmap), head_vec, head_vec],
        out_specs=pl.BlockSpec((1, tt, DVB), lambda bi, h, c: (bi, c, h)),
        out_shape=jax.ShapeDtypeStruct((b, t, WB_V), BF16),
        scratch_shapes=[pltpu.VMEM((DKB, DVB), F32)],
        compiler_params=_params("parallel", "parallel", "arbitrary"),
    )(proj3, proj3, proj3, cos2, sin2, lg, sb, proj3,
      gn_g.reshape(HB, 1, DVB), gn_b.reshape(HB, 1, DVB))


def _group_sum(x, gmat):
    cols = [_split_dot_right(x[:, i * LANE:(i + 1) * LANE], gmat) for i in range(x.shape[1] // LANE)]
    return jnp.concatenate(cols, axis=1)


def _rwkv_prep_kernel(tt, r_ref, k_ref, v_ref, c_ref, rp_ref, kp_ref, vp_ref, cp_ref,
                      rn_ref, kn_ref, vn_ref, cn_ref, mur_ref, muk_ref, muv_ref, muc_ref,
                      w0_ref, wl_ref, a0_ref, al_ref, kk_ref, ka_ref, rk_ref, gmat_ref,
                      r_o, v_o, kk_o, bonus_o, lw0_o, lw1_o, k0_o, k1_o, ka0_o, ka1_o):
    ti = pl.program_id(1)
    has_prev = (ti > 0).astype(F32)
    has_next = (ti < pl.num_programs(1) - 1).astype(F32)
    rows = lax.broadcasted_iota(jnp.int32, (tt, 1), 0)

    def shifted(p_ref, prev_ref, next_ref, mu_ref):
        p = _f32(p_ref[0])
        prev_row = _f32(prev_ref[0, HALO - 1:HALO, :]) * has_prev
        next_row = _f32(next_ref[0, 0:1, :]) * has_next
        before = jnp.where(rows == 0, prev_row, pltpu.roll(p, 1, axis=0))
        after = jnp.where(rows == tt - 1, next_row, pltpu.roll(p, tt - 1, axis=0))
        return p + mu_ref[0:1, :] * (before - p) + mu_ref[1:2, :] * (after - p)

    r = shifted(r_ref, rp_ref, rn_ref, mur_ref)
    k = shifted(k_ref, kp_ref, kn_ref, muk_ref)
    v = shifted(v_ref, vp_ref, vn_ref, muv_ref)
    codes = shifted(c_ref, cp_ref, cn_ref, muc_ref)
    gmat = gmat_ref[...]

    kraw = k * kk_ref[...]
    kk = kraw * lax.rsqrt(_group_sum(kraw * kraw, gmat) + 1e-12)
    r_o[0] = r
    v_o[0] = v
    kk_o[0] = kk
    tcodes = jnp.tanh(codes)
    bonus = jnp.zeros_like(r)
    for d, (lw_o, kd_o, ka_o) in enumerate(((lw0_o, k0_o, ka0_o), (lw1_o, k1_o, ka1_o))):
        w = w0_ref[d:d + 1, :] + _mm(tcodes, wl_ref[d])
        lw_o[0] = -np.exp(-0.5).astype(np.float32) * _sigmoid(w)
        a = _sigmoid(a0_ref[d:d + 1, :] + _mm(codes, al_ref[d]))
        kd = k * (1.0 + (a - 1.0) * ka_ref[...])
        kd_o[0] = kd
        ka_o[0] = kk * a
        bonus = bonus + _group_sum(r * kd * rk_ref[...], gmat) * v
    bonus_o[0] = bonus


def _rwkv_scan_kernel(rev, final, tt, r_ref, lw_ref, k_ref, v_ref, ka_ref, kk_ref, tri_ref, *rest):
    if final:
        ob_ref, bonus_ref, gate_ref, lng_ref, lnb_ref, y_ref, h_ref = rest
    else:
        o_ref, h_ref = rest

    @pl.when(pl.program_id(2) == 0)
    def _():
        h_ref[...] = jnp.zeros_like(h_ref)

    n2 = 2 * CHUNK_C
    lane = lax.broadcasted_iota(jnp.int32, (1, LANE), 1)
    head0 = lane < NC
    ri = lax.broadcasted_iota(jnp.int32, (n2, n2), 0)
    ci = lax.broadcasted_iota(jnp.int32, (n2, n2), 1)
    strict = (ci > ri) if rev else (ci < ri)
    incl = (ci >= ri) if rev else (ci <= ri)
    eye = ri == ci
    nchunk = tt // CHUNK_C

    def stack(z):
        return jnp.concatenate([jnp.where(head0, z, 0.0), jnp.where(head0, 0.0, z)], axis=0)

    order = [(nchunk - 1 - i) if rev else i for i in range(nchunk)]
    items = [(pl.ds(c * CHUNK_C, CHUNK_C), pl.ds(p * LANE, LANE), p) for c in order for p in range(PAIRS_C)]
    rows = [it[0] for it in items]
    cols = [it[1] for it in items]

    lws = _each(lambda sl, cl: lw_ref[0, sl, cl], rows, cols)
    g = _each(lambda lw: _split_dot(tri_ref[...], lw), lws)
    e_end = _each(lambda gu: jnp.exp(gu[0:1, :] if rev else gu[CHUNK_C - 1:CHUNK_C, :]), g)
    e_neg = _each(lambda gu: jnp.exp(-gu), g)
    bs = _each(lambda sl, cl, gu, lw: stack(-kk_ref[0, sl, cl] * jnp.exp(gu - lw)), rows, cols, g, lws)
    as_ = _each(lambda sl, cl, en: stack(ka_ref[0, sl, cl] * en), rows, cols, e_neg)
    ks = _each(lambda sl, cl, en: stack(k_ref[0, sl, cl] * en), rows, cols, e_neg)
    rs = _each(lambda sl, cl, gu: stack(r_ref[0, sl, cl] * jnp.exp(gu)), rows, cols, g)
    vs = _each(lambda sl, cl: stack(v_ref[0, sl, cl]), rows, cols)

    ak = _each(lambda a, k: jnp.concatenate([a, k], axis=0), as_, ks)
    sc_b = _each(_mm_nt, bs, ak)
    sc_r = _each(_mm_nt, rs, ak)
    a_ab = _each(lambda z: jnp.where(strict, z[:, :n2], 0.0), sc_b)
    a_ak = _each(lambda z: jnp.where(strict, z[:, n2:], 0.0), sc_b)
    m_rak = _each(lambda z: jnp.where(jnp.concatenate([incl, incl], axis=1), z, 0.0), sc_r)
    tinv = _each(lambda a: jnp.where(eye, 1.0, 0.0) + a, a_ab)
    pw = _each(lambda a: _mm(a, a), a_ab)
    akv = _each(_mm, a_ak, vs)
    nsteps = int(np.log2(CHUNK_C)) - 1
    for i in range(nsteps):
        if i < nsteps - 1:
            both = _each(lambda t, p: _mm(jnp.concatenate([t, p], axis=0), p), tinv, pw)
            tinv = _each(lambda t, b: t + b[:n2], tinv, both)
            pw = _each(lambda b: b[n2:], both)
        else:
            tinv = _each(lambda t, p: t + _mm(t, p), tinv, pw)
    pwv = _each(lambda t, b, x: _mm(t, jnp.concatenate([b, x], axis=1)), tinv, bs, akv)
    low = _each(lambda z, v: jnp.concatenate([z, jnp.concatenate([jnp.zeros_like(v), v], axis=1)], axis=0),
                pwv, vs)
    x1 = _each(_mm, m_rak, low)
    x2 = _each(lambda a, k, e, z: _mm_tn(jnp.concatenate([a * e, k * e], axis=0), z), as_, ks, e_end, low)
    top = _each(lambda r, z, e, y: jnp.concatenate([r + z[:, :LANE], jnp.where(eye, e, 0.0) + y[:, :LANE]], axis=0),
                rs, x1, e_end, x2)
    o_in = _each(lambda z: z[:, LANE:], x1)
    dmat = _each(lambda y: y[:, LANE:], x2)

    outs = []
    h = [h_ref[p] for p in range(PAIRS_C)]
    for (sl, cl, p), tp, oi, dm in zip(items, top, o_in, dmat):
        both = _mm(tp, h[p])
        h[p] = both[n2:] + dm
        os_ = both[:n2] + oi
        outs.append((sl, cl, os_[:CHUNK_C, :] + os_[CHUNK_C:, :]))
    for p in range(PAIRS_C):
        h_ref[p] = h[p]

    for sl, cl, o in outs:
        if final:
            o = o + ob_ref[0, sl, cl]
            inv_n = 1.0 / NC
            s0 = jnp.sum(jnp.where(head0, o, 0.0), axis=-1, keepdims=True)
            s1 = jnp.sum(jnp.where(head0, 0.0, o), axis=-1, keepdims=True)
            cen = o - jnp.where(head0, s0, s1) * inv_n
            c2 = cen * cen
            v0 = jnp.sum(jnp.where(head0, c2, 0.0), axis=-1, keepdims=True)
            v1 = jnp.sum(jnp.where(head0, 0.0, c2), axis=-1, keepdims=True)
            var = jnp.where(head0, v0, v1) * inv_n
            y = cen * lax.rsqrt(var + RWKV_GN_EPS) * lng_ref[:, cl] + lnb_ref[:, cl] + bonus_ref[0, sl, cl]
            y_ref[0, sl, cl] = (y * _silu(_f32(gate_ref[0, sl, cl]))).astype(y_ref.dtype)
        else:
            o_ref[0, sl, cl] = o


def _rwkv_mixer(proj3, mu, w0, w_lora_b, a0, a_lora_b, k_k, k_a, r_k, ln_g, ln_b):
    b, t, _ = proj3.shape
    tt = _pick(t, 256)
    nt = t // tt
    sub = tt // HALO

    def cur(off, width):
        return pl.BlockSpec((1, tt, width), lambda bi, ti: (bi, ti, off // width))

    def prev(off, width):
        return pl.BlockSpec((1, HALO, width),
                            lambda bi, ti: (bi, jnp.maximum(ti * sub - 1, 0), off // width))

    def nxt(off, width):
        return pl.BlockSpec((1, HALO, width),
                            lambda bi, ti: (bi, jnp.minimum((ti + 1) * sub, nt * sub - 1), off // width))

    def full(shape):
        return pl.BlockSpec(shape, lambda bi, ti: (0,) * len(shape))

    segs = ((OFF_CR, WC), (OFF_CK, WC), (OFF_CV, WC), (OFF_CC, N_CODES))
    in_specs = ([cur(o, w) for o, w in segs] + [prev(o, w) for o, w in segs] + [nxt(o, w) for o, w in segs]
                + [full((2, WC))] * 3 + [full((2, N_CODES))]
                + [full((2, WC)), full((2, N_CODES, WC)), full((2, WC)), full((2, N_CODES, WC))]
                + [full((1, WC))] * 3 + [full((LANE, LANE))])
    wl = jnp.zeros((2, N_CODES, WC), F32)
    al = jnp.zeros((2, N_CODES, WC), F32)
    for d in range(2):
        wl = wl.at[d, d * W_LORA:(d + 1) * W_LORA].set(w_lora_b[d])
        al = al.at[d, 2 * W_LORA + d * A_LORA:2 * W_LORA + (d + 1) * A_LORA].set(a_lora_b[d])
    lane_i = np.arange(LANE)
    gmat = jnp.asarray((lane_i[:, None] // NC == lane_i[None, :] // NC).astype(np.float32), dtype=BF16)
    row_out = pl.BlockSpec((1, tt, WC), lambda bi, ti: (bi, ti, 0))
    outs = pl.pallas_call(
        functools.partial(_rwkv_prep_kernel, tt),
        grid=(b, nt),
        in_specs=in_specs,
        out_specs=[row_out] * 10,
        out_shape=[jax.ShapeDtypeStruct((b, t, WC), F32)] * 10,
        compiler_params=_params("parallel", "parallel"),
    )(*([proj3] * 12), mu[:, :WC], mu[:, WC:2 * WC], mu[:, 2 * WC:3 * WC], mu[:, 3 * WC:],
      w0, wl.astype(BF16), a0, al.astype(BF16), k_k.reshape(1, WC), k_a.reshape(1, WC),
      r_k.reshape(1, WC), gmat)
    r, v, kk, bonus, lw0, lw1, k0, k1, ka0, ka1 = outs

    ts = _pick(t, 512)
    ns = t // ts

    def scan(rev, final, lw, kd, ka, ob=None):
        wblk = PAIRS_C * LANE
        tmap = (lambda i: ns - 1 - i) if rev else (lambda i: i)
        blk = pl.BlockSpec((1, ts, wblk), lambda bi, h, ti: (bi, tmap(ti), h))
        head_vec = pl.BlockSpec((1, wblk), lambda bi, h, ti: (0, h))
        in_specs = [blk] * 6 + [pl.BlockSpec((CHUNK_C, CHUNK_C), lambda bi, h, ti: (0, 0))]
        ii = np.arange(CHUNK_C)
        tri = (ii[None, :] >= ii[:, None]) if rev else (ii[None, :] <= ii[:, None])
        args = [r, lw, kd, v, ka, kk, jnp.asarray(tri.astype(np.float32), dtype=BF16)]
        if final:
            gate = pl.BlockSpec((1, ts, wblk), lambda bi, h, ti: (bi, tmap(ti), OFF_CG // wblk + h))
            in_specs += [blk, blk, gate, head_vec, head_vec]
            args += [ob, bonus, proj3, ln_g.reshape(1, WC), ln_b.reshape(1, WC)]
        return pl.pallas_call(
            functools.partial(_rwkv_scan_kernel, rev, final, ts),
            grid=(b, WC // wblk, ns),
            in_specs=in_specs,
            out_specs=blk,
            out_shape=jax.ShapeDtypeStruct((b, t, WC), BF16 if final else F32),
            scratch_shapes=[pltpu.VMEM((PAIRS_C, LANE, LANE), F32)],
            compiler_params=_params("parallel", "parallel", "arbitrary"),
        )(*args)

    ob = scan(True, False, lw1, k1, ka1)
    return scan(False, True, lw0, k0, ka0, ob=ob)


def _merge_kernel(final, ya_ref, yb_ref, yc_ref, mg_ref, x_ref, wa_ref, wb_ref, wc_ref, wo_ref,
                  fg_ref, o_ref):
    za = jnp.dot(ya_ref[...], wa_ref[...], preferred_element_type=F32)
    zb = jnp.dot(yb_ref[...], wb_ref[...], preferred_element_type=F32)
    zc = jnp.dot(yc_ref[...], wc_ref[...], preferred_element_type=F32)
    mg = _f32(mg_ref[...])
    mixed = (_sigmoid(mg[:, :D_MODEL]) * za
             + _sigmoid(mg[:, D_MODEL:2 * D_MODEL]) * zb
             + _sigmoid(mg[:, 2 * D_MODEL:]) * zc)
    out = x_ref[...] + jnp.dot(mixed.astype(BF16), wo_ref[...], preferred_element_type=F32)
    if final:
        ms = jnp.mean(out * out, axis=-1, keepdims=True)
        out = out * lax.rsqrt(ms + EPS) * fg_ref[...]
    o_ref[...] = out


def _merge(ya, yb, yc, proj, x2, wa, wb, wc, wo, fg, final):
    m = x2.shape[0]
    tm = _pick(m, 256)

    def rows(width):
        return pl.BlockSpec((tm, width), lambda i: (i, 0))

    def full(shape):
        return pl.BlockSpec(shape, lambda i: (0, 0))

    return pl.pallas_call(
        functools.partial(_merge_kernel, final),
        grid=(m // tm,),
        in_specs=[rows(WA), rows(WB_V), rows(WC), rows(3 * D_MODEL), rows(D_MODEL),
                  full((WA, D_MODEL)), full((WB_V, D_MODEL)), full((WC, D_MODEL)),
                  full((D_MODEL, D_MODEL)), full((1, D_MODEL))],
        out_specs=rows(D_MODEL),
        out_shape=jax.ShapeDtypeStruct((m, D_MODEL), F32),
        compiler_params=_params("parallel"),
    )(ya, yb, yc, proj, x2, wa.astype(BF16), wb.astype(BF16), wc.astype(BF16), wo.astype(BF16),
      fg.reshape(1, D_MODEL))


def kernel(x, norm_g, w_in, hgrn_lb_logits, hgrn_norm_g, ret_norm_g, ret_norm_b, rwkv_mu, rwkv_w0,
           rwkv_w_lora_b, rwkv_a0, rwkv_a_lora_b, rwkv_k_k, rwkv_k_a, rwkv_r_k, rwkv_ln_g,
           rwkv_ln_b, w_branch_a, w_branch_b, w_branch_c, w_out, final_norm_g):
    b, t, d = x.shape
    m = b * t
    depth = w_in.shape[0]
    p_lb = jax.nn.softmax(hgrn_lb_logits.astype(F32), axis=0)
    lbs = jnp.cumsum(p_lb, axis=0) - p_lb[0:1]
    x2 = x.reshape(m, d)
    for l in range(depth):
        proj = _inproj(x2, norm_g[l], _permute_w_in(w_in[l]))
        proj3 = proj.reshape(b, t, N_PAD)
        ya = _hgrn_mixer(proj3, lbs[l], hgrn_norm_g[l])
        yb = _ret_mixer(proj3, ret_norm_g[l], ret_norm_b[l])
        yc = _rwkv_mixer(proj3, rwkv_mu[l], rwkv_w0[l], rwkv_w_lora_b[l], rwkv_a0[l],
                         rwkv_a_lora_b[l], rwkv_k_k[l], rwkv_k_a[l], rwkv_r_k[l],
                         rwkv_ln_g[l], rwkv_ln_b[l])
        x2 = _merge(ya.reshape(m, WA), yb.reshape(m, WB_V), yc.reshape(m, WC), proj, x2,
                    w_branch_a[l], w_branch_b[l], w_branch_c[l], w_out[l], final_norm_g,
                    final=(l == depth - 1))
    return x2.reshape(b, t, d)
```

```python
import functools

import numpy as np
import jax
import jax.numpy as jnp
from jax import lax
from jax.experimental import pallas as pl
from jax.experimental.pallas import tpu as pltpu

F32 = jnp.float32
BF16 = jnp.bfloat16

D_MODEL = 1024
EPS = 1e-6
F_MIN = 1e-6
DKA = 128
HA = D_MODEL // DKA
WA = HA * DKA
HB = 8
DKB = D_MODEL // HB
DVB = 2 * DKB
WB_QK = HB * DKB
WB_V = HB * DVB
ROPE_BASE = 10000.0
RET_GN_EPS = 1e-5
NC = 64
HC = D_MODEL // NC
WC = HC * NC
W_LORA = 64
A_LORA = 64
RWKV_GN_EPS = 64e-5
N_CODES = 2 * W_LORA + 2 * A_LORA
C_SHIFT = 3 * WC + N_CODES

SRC_A = 0
SRC_B = 5 * WA
SRC_CS = SRC_B + 2 * WB_QK + 2 * WB_V
SRC_CG = SRC_CS + C_SHIFT
SRC_MERGE = SRC_CG + WC
N_IN = SRC_MERGE + 3 * D_MODEL

OFF_MERGE = 0
OFF_AQ = OFF_MERGE + 3 * D_MODEL
OFF_AFF = OFF_AQ + WA
OFF_AFB = OFF_AFF + WA
OFF_AI = OFF_AFB + WA
OFF_AG = OFF_AI + WA
OFF_BQ = OFF_AG + WA
OFF_BK = OFF_BQ + WB_QK
OFF_BV = OFF_BK + WB_QK
OFF_BG = OFF_BV + WB_V
OFF_CG = OFF_BG + WB_V
OFF_CR = OFF_CG + WC
OFF_CK = OFF_CR + WC
OFF_CV = OFF_CK + WC
OFF_CC = OFF_CV + WC
LANE = 128
PROJ_DTYPE = BF16
HALO = 16
PROJ_SPLIT = 4
N_PAD = -(-(OFF_CC + N_CODES) // (PROJ_SPLIT * LANE)) * PROJ_SPLIT * LANE
PROJ_TN = N_PAD // PROJ_SPLIT
MXU_N = 256
WPERM_TN = 256

CHUNK_A = 128
GROUP_A = 8
HEADS_A = 2
CHUNK_B = 256
CHUNK_C = 64
PAIRS_C = 4
VMEM_LIMIT = 48 * 1024 * 1024
PROJ_VMEM_LIMIT = 56 * 1024 * 1024

NT_DIMS = (((1,), (1,)), ((), ()))
TN_DIMS = (((0,), (0,)), ((), ()))


def _mm(a, b):
    return jnp.dot(a.astype(BF16), b.astype(BF16), preferred_element_type=F32)


def _mm_nt(a, b):
    return lax.dot_general(a.astype(BF16), b.astype(BF16), NT_DIMS, preferred_element_type=F32)


def _mm_tn(a, b):
    return lax.dot_general(a.astype(BF16), b.astype(BF16), TN_DIMS, preferred_element_type=F32)


def _split_dot(ones_mat, x):
    hi = x.astype(BF16)
    r1 = x - hi.astype(F32)
    mid = r1.astype(BF16)
    lo = (r1 - mid.astype(F32)).astype(BF16)
    out = jnp.dot(ones_mat, hi, preferred_element_type=F32)
    out = out + jnp.dot(ones_mat, mid, preferred_element_type=F32)
    return out + jnp.dot(ones_mat, lo, preferred_element_type=F32)


def _split_dot_right(x, ones_mat):
    hi = x.astype(BF16)
    r1 = x - hi.astype(F32)
    mid = r1.astype(BF16)
    lo = (r1 - mid.astype(F32)).astype(BF16)
    out = jnp.dot(hi, ones_mat, preferred_element_type=F32)
    out = out + jnp.dot(mid, ones_mat, preferred_element_type=F32)
    return out + jnp.dot(lo, ones_mat, preferred_element_type=F32)


def _f32(z):
    return z.astype(F32)


def _sigmoid(z):
    return 0.5 * jnp.tanh(0.5 * z) + 0.5


def _silu(z):
    return z * _sigmoid(z)


def _pick(n, pref):
    t = min(n, pref)
    assert n % t == 0, (n, pref)
    return t


def _params(*sem, vmem=VMEM_LIMIT):
    return pltpu.CompilerParams(dimension_semantics=sem, vmem_limit_bytes=vmem)


def _inproj_kernel(x_ref, g_ref, w_ref, o_ref, h_ref):
    @pl.when(pl.program_id(1) == 0)
    def _():
        x = x_ref[...]
        ms = jnp.mean(x * x, axis=-1, keepdims=True)
        h_ref[...] = (x * lax.rsqrt(ms + EPS) * g_ref[...]).astype(BF16)

    def tile(cols):
        o_ref[:, cols] = jnp.dot(h_ref[...], w_ref[:, cols], preferred_element_type=F32).astype(o_ref.dtype)

    for start in range(0, PROJ_TN, MXU_N):
        tile(pl.ds(start, min(MXU_N, PROJ_TN - start)))


def _inproj(x2, g, w_bf16):
    m = x2.shape[0]
    tm = _pick(m, 1024)
    return pl.pallas_call(
        _inproj_kernel,
        grid=(m // tm, N_PAD // PROJ_TN),
        in_specs=[
            pl.BlockSpec((tm, D_MODEL), lambda i, j: (i, 0)),
            pl.BlockSpec((1, D_MODEL), lambda i, j: (0, 0)),
            pl.BlockSpec((D_MODEL, PROJ_TN), lambda i, j: (0, j)),
        ],
        out_specs=pl.BlockSpec((tm, PROJ_TN), lambda i, j: (i, j)),
        out_shape=jax.ShapeDtypeStruct((m, N_PAD), PROJ_DTYPE),
        scratch_shapes=[pltpu.VMEM((tm, D_MODEL), BF16)],
        compiler_params=_params("parallel", "arbitrary", vmem=PROJ_VMEM_LIMIT),
    )(x2, g.reshape(1, D_MODEL), w_bf16)


def _wperm_kernel(nvalid, a_ref, b_ref, o_ref):
    j = pl.program_id(0)
    for idx, src in enumerate((a_ref, b_ref)):
        cols = pl.ds(idx * WPERM_TN, WPERM_TN)
        blk = src[...].astype(BF16)
        o_ref[:, cols] = jnp.where(2 * j + idx < nvalid, blk, jnp.zeros_like(blk))


def _src_block(jd):
    n_merge = (N_IN - SRC_MERGE) // WPERM_TN
    n_ab = SRC_CS // WPERM_TN
    n_cg = (SRC_MERGE - SRC_CG) // WPERM_TN
    n_src = N_IN // WPERM_TN
    return jnp.where(jd < n_merge, jd + SRC_MERGE // WPERM_TN,
                     jnp.where(jd < n_merge + n_ab, jd - n_merge,
                               jnp.where(jd < n_merge + n_ab + n_cg, jd - n_merge - n_ab + SRC_CG // WPERM_TN,
                                         jnp.minimum(jd - n_merge - n_cg, n_src - 1))))


def _permute_w_in(w_in, layer):
    assert all(off % WPERM_TN == 0 for off in (SRC_CS, SRC_CG, SRC_MERGE, N_IN)) and N_PAD % (2 * WPERM_TN) == 0

    def src(idx):
        return pl.BlockSpec((None, D_MODEL, WPERM_TN), lambda j: (layer, 0, _src_block(2 * j + idx)))

    return pl.pallas_call(
        functools.partial(_wperm_kernel, N_IN // WPERM_TN),
        grid=(N_PAD // (2 * WPERM_TN),),
        in_specs=[src(0), src(1)],
        out_specs=pl.BlockSpec((D_MODEL, 2 * WPERM_TN), lambda j: (0, j)),
        out_shape=jax.ShapeDtypeStruct((D_MODEL, N_PAD), BF16),
        compiler_params=_params("parallel"),
    )(w_in, w_in)


def _each(fn, *lists):
    return [fn(*items) for items in zip(*lists)]


def _hgrn_kernel(rev, final, tt, q_ref, f_ref, v_ref, lb_ref, tri_ref, *rest):
    if final:
        ob_ref, gate_ref, ng_ref, y_ref, st_ref = rest
    else:
        o_ref, st_ref = rest

    @pl.when(pl.program_id(2) == 0)
    def _():
        st_ref[...] = jnp.zeros_like(st_ref)

    n = CHUNK_A
    ri = lax.broadcasted_iota(jnp.int32, (n, n), 0)
    ci = lax.broadcasted_iota(jnp.int32, (n, n), 1)
    g_bits = int(np.log2(GROUP_A))
    diag_mask = ((ri >> g_bits) == (ci >> g_bits)) & ((ci >= ri) if rev else (ci <= ri))
    halves = [GROUP_A << i for i in range(int(np.log2(n // GROUP_A)))]
    level_masks = []
    for half in halves:
        hb = int(np.log2(half))
        q_side, k_side = (0, 1) if rev else (1, 0)
        level_masks.append(((ri >> (hb + 1)) == (ci >> (hb + 1)))
                           & (((ri >> hb) & 1) == q_side) & (((ci >> hb) & 1) == k_side))

    def ref_rows(b, blk, idx):
        picked = b.reshape(n // blk, blk, DKA)[:, idx:idx + 1, :]
        return jnp.broadcast_to(picked, (n // blk, blk, DKA)).reshape(n, DKA)

    nchunk = tt // n
    order = [(nchunk - 1 - i) if rev else i for i in range(nchunk)]
    items = [(pl.ds(c * n, n), pl.ds(hh * DKA, DKA), hh) for c in order for hh in range(HEADS_A)]
    sls = [it[0] for it in items]
    cls = [it[1] for it in items]
    lbs = [lb_ref[it[2]] for it in items]
    sig = _each(lambda sl, cl: _sigmoid(_f32(f_ref[0, sl, cl])), sls, cls)
    lf = _each(lambda z, lb: jnp.log(jnp.maximum(lb + (1.0 - lb) * z, F_MIN)), sig, lbs)
    k = _each(lambda z, lb: (1.0 - lb) * (1.0 - z), sig, lbs)
    q = _each(lambda sl, cl: _silu(_f32(q_ref[0, sl, cl])), sls, cls)
    v = _each(lambda sl, cl: v_ref[0, sl, cl], sls, cls)
    b = _each(lambda z: _split_dot(tri_ref[...], z), lf)
    d0 = _each(lambda z: z - ref_rows(z, GROUP_A, GROUP_A // 2 if rev else GROUP_A // 2 - 1), b)
    qb = _each(lambda z: z.astype(BF16), q)
    kb = _each(lambda z: z.astype(BF16), k)
    scores = _each(lambda qq, kk, d: jnp.where(
        diag_mask, _mm_nt(qq * jnp.exp(d).astype(BF16), kk * jnp.exp(-d).astype(BF16)), 0.0), qb, kb, d0)
    for half, mask in zip(halves, level_masks):
        e = _each(lambda z: jnp.exp(-jnp.abs(z - ref_rows(z, 2 * half, half if rev else half - 1))).astype(BF16),
                  b)
        scores = _each(lambda sc, qq, kk, ee: jnp.where(mask, _mm_nt(qq * ee, kk * ee), sc), scores, qb, kb, e)
    o_intra = _each(_mm, scores, v)
    b_end = _each(lambda z: z[0:1, :] if rev else z[n - 1:n, :], b)
    q_in = _each(lambda qq, z: qq * jnp.exp(z), q, b)
    kv = _each(lambda vv, kk, z, ze: _mm_tn(vv, kk * jnp.exp(ze - z)), v, k, b, b_end)
    e_end = _each(jnp.exp, b_end)

    outs = []
    st = [st_ref[hh] for hh in range(HEADS_A)]
    for (sl, cl, hh), oi, qi, kvi, ee in zip(items, o_intra, q_in, kv, e_end):
        outs.append((sl, cl, hh, oi + _mm_nt(qi, st[hh])))
        st[hh] = st[hh] * ee + kvi
    for hh in range(HEADS_A):
        st_ref[hh] = st[hh]

    for sl, cl, hh, o in outs:
        if final:
            o = o + ob_ref[0, sl, cl]
            ms = jnp.mean(o * o, axis=-1, keepdims=True)
            y = o * lax.rsqrt(ms + EPS) * ng_ref[hh] * _silu(_f32(gate_ref[0, sl, cl]))
            y_ref[0, sl, cl] = y.astype(y_ref.dtype)
        else:
            o_ref[0, sl, cl] = o


def _tri_blockdiag(n, blk, rev):
    i = np.arange(n)
    same = (i[:, None] // blk) == (i[None, :] // blk)
    tri = (i[None, :] >= i[:, None]) if rev else (i[None, :] <= i[:, None])
    return jnp.asarray((same & tri).astype(np.float32), dtype=BF16)


def _hgrn_pass(proj3, lb, rev, final, ob=None, norm_g=None):
    b, t, _ = proj3.shape
    tt = _pick(t, 512)
    nt = t // tt
    tmap = (lambda i: nt - 1 - i) if rev else (lambda i: i)

    wblk = HEADS_A * DKA

    def col(off):
        return pl.BlockSpec((1, tt, wblk), lambda bi, h, ti: (bi, tmap(ti), off // wblk + h))

    head_vec = pl.BlockSpec((HEADS_A, 1, DKA), lambda bi, h, ti: (h, 0, 0))
    in_specs = [col(OFF_AQ), col(OFF_AFB if rev else OFF_AFF), col(OFF_AI), head_vec,
                pl.BlockSpec((LANE, LANE), lambda bi, h, ti: (0, 0))]
    args = [proj3, proj3, proj3, lb.reshape(HA, 1, DKA), _tri_blockdiag(LANE, CHUNK_A, rev)]
    out_block = pl.BlockSpec((1, tt, wblk), lambda bi, h, ti: (bi, tmap(ti), h))
    if final:
        in_specs += [out_block, col(OFF_AG), head_vec]
        args += [ob, proj3, norm_g.reshape(HA, 1, DKA)]
        out_dtype = BF16
    else:
        out_dtype = F32
    return pl.pallas_call(
        functools.partial(_hgrn_kernel, rev, final, tt),
        grid=(b, HA // HEADS_A, nt),
        in_specs=in_specs,
        out_specs=out_block,
        out_shape=jax.ShapeDtypeStruct((b, t, WA), out_dtype),
        scratch_shapes=[pltpu.VMEM((HEADS_A, DKA, DKA), F32)],
        compiler_params=_params("parallel", "parallel", "arbitrary"),
    )(*args)


def _hgrn_mixer(proj3, lbs, norm_g):
    ob = _hgrn_pass(proj3, lbs[1], rev=True, final=False)
    return _hgrn_pass(proj3, lbs[0], rev=False, final=True, ob=ob, norm_g=norm_g)


def _rotary(z, cos2, sin2):
    return z * cos2 + pltpu.roll(z, DKB // 2, axis=1) * sin2


def _ret_state_kernel(nsub, k_ref, v_ref, cos_ref, sin_ref, lg_ref, sb_ref, st_ref):
    @pl.when(pl.program_id(2) == 0)
    def _():
        st_ref[...] = jnp.zeros_like(st_ref)

    cb = CHUNK_B
    lg2 = lg_ref[0]
    lg = lg2[:, :DKB]
    pos = lax.broadcasted_iota(jnp.int32, (cb, DKB), 0).astype(F32)
    dec = jnp.exp(pos * lg) * (DKB ** -0.5)
    order = list(range(nsub - 1, -1, -1))
    sls = [pl.ds(c * cb, cb) for c in order]
    kd = _each(lambda sl: _rotary(_f32(k_ref[0, sl, :]), cos_ref[sl, :], sin_ref[sl, :]) * dec, sls)
    kv = _each(lambda kk, sl: _mm_tn(kk, v_ref[0, sl, :]), kd, sls)
    e_chunk = jnp.exp(cb * lg2)
    st = st_ref[...]
    for c, kvi in zip(order, kv):
        sb_ref[0, 0, c] = st
        st = st * e_chunk + kvi
    st_ref[...] = st


def _ret_out_kernel(nsub, q_ref, k_ref, v_ref, cos_ref, sin_ref, lg_ref, sb_ref, gate_ref, gg_ref,
                    gb_ref, y_ref, st_ref):
    @pl.when(pl.program_id(2) == 0)
    def _():
        st_ref[...] = jnp.zeros_like(st_ref)

    cb = CHUNK_B
    lg2 = lg_ref[0]
    lg = lg2[:, :DKB]
    pos = lax.broadcasted_iota(jnp.int32, (cb, DKB), 0).astype(F32)
    ri = lax.broadcasted_iota(jnp.int32, (cb, cb), 0)
    ci = lax.broadcasted_iota(jnp.int32, (cb, cb), 1)
    decay = jnp.exp(jnp.abs(ri - ci).astype(F32) * lg2)
    dec_fwd = jnp.exp((pos + 1.0) * lg)
    dec_bwd = jnp.exp((cb - pos) * lg)
    dec_key = jnp.exp((cb - 1.0 - pos) * lg)
    e_chunk = jnp.exp(cb * lg2)

    sls = [pl.ds(c * cb, cb) for c in range(nsub)]
    q = _each(lambda sl: _rotary(_f32(q_ref[0, sl, :]), cos_ref[sl, :], sin_ref[sl, :]), sls)
    k = _each(lambda sl: _rotary(_f32(k_ref[0, sl, :]), cos_ref[sl, :], sin_ref[sl, :]) * (DKB ** -0.5), sls)
    v = _each(lambda sl: v_ref[0, sl, :], sls)
    scores = _each(lambda qq, kk: _mm_nt(qq, kk) * decay, q, k)
    o = _each(_mm, scores, v)
    o = _each(lambda oo, qq, c: oo + _mm(qq * dec_bwd, sb_ref[0, 0, c]), o, q, list(range(nsub)))
    kv = _each(lambda kk, vv: _mm_tn(kk * dec_key, vv), k, v)
    q_fwd = _each(lambda qq: qq * dec_fwd, q)
    st = st_ref[...]
    outs = []
    for oo, qf, kvi in zip(o, q_fwd, kv):
        outs.append(oo + _mm(qf, st))
        st = st * e_chunk + kvi
    st_ref[...] = st

    for sl, oo in zip(sls, outs):
        mu = jnp.mean(oo, axis=-1, keepdims=True)
        cen = oo - mu
        var = jnp.mean(cen * cen, axis=-1, keepdims=True)
        y = (cen * lax.rsqrt(var + RET_GN_EPS) * gg_ref[0] + gb_ref[0]) * _silu(_f32(gate_ref[0, sl, :]))
        y_ref[0, sl, :] = y.astype(y_ref.dtype)


def _ret_mixer(proj3, gn_g, gn_b):
    b, t, _ = proj3.shape
    cb = CHUNK_B
    tt = _pick(t, 8 * cb)
    nsub = tt // cb
    nt = t // tt
    half = DKB // 2
    inv = ROPE_BASE ** (-jnp.arange(0, DKB, 2, dtype=F32) / DKB)
    ang_hi = (jnp.arange(t // cb, dtype=F32) * cb)[:, None, None] * inv
    ang_lo = jnp.arange(cb, dtype=F32)[None, :, None] * inv
    cos = (jnp.cos(ang_hi) * jnp.cos(ang_lo) - jnp.sin(ang_hi) * jnp.sin(ang_lo)).reshape(t, half)
    sin = (jnp.sin(ang_hi) * jnp.cos(ang_lo) + jnp.cos(ang_hi) * jnp.sin(ang_lo)).reshape(t, half)
    cos2 = jnp.concatenate([cos, cos], axis=-1)
    sin2 = jnp.concatenate([-sin, sin], axis=-1)
    log_gamma = jnp.log1p(-jnp.exp2(-5.0 - jnp.arange(HB, dtype=F32)))
    lg = jnp.broadcast_to(log_gamma[:, None, None], (HB, 1, DVB))

    def col(off, width, tmap):
        return pl.BlockSpec((1, tt, width), lambda bi, h, c: (bi, tmap(c), off // width + h))

    def tab(tmap):
        return pl.BlockSpec((tt, DKB), lambda bi, h, c: (tmap(c), 0))

    lg_spec = pl.BlockSpec((1, 1, DVB), lambda bi, h, c: (h, 0, 0))
    rmap = lambda c: nt - 1 - c
    fmap = lambda c: c

    sb = pl.pallas_call(
        functools.partial(_ret_state_kernel, nsub),
        grid=(b, HB, nt),
        in_specs=[col(OFF_BK, DKB, rmap), col(OFF_BV, DVB, rmap), tab(rmap), tab(rmap), lg_spec],
        out_specs=pl.BlockSpec((1, 1, nsub, DKB, DVB), lambda bi, h, c: (bi, h, rmap(c), 0, 0)),
        out_shape=jax.ShapeDtypeStruct((b, HB, t // cb, DKB, DVB), F32),
        scratch_shapes=[pltpu.VMEM((DKB, DVB), F32)],
        compiler_params=_params("parallel", "parallel", "arbitrary"),
    )(proj3, proj3, cos2, sin2, lg)

    head_vec = pl.BlockSpec((1, 1, DVB), lambda bi, h, c: (h, 0, 0))
    return pl.pallas_call(
        functools.partial(_ret_out_kernel, nsub),
        grid=(b, HB, nt),
        in_specs=[col(OFF_BQ, DKB, fmap), col(OFF_BK, DKB, fmap), col(OFF_BV, DVB, fmap),
                  tab(fmap), tab(fmap), lg_spec,
                  pl.BlockSpec((1, 1, nsub, DKB, DVB), lambda bi, h, c: (bi, h, c, 0, 0)),
                  col(OFF_BG, DVB, fmap), head_vec, head_vec],
        out_specs=pl.BlockSpec((1, tt, DVB), lambda bi, h, c: (bi, c, h)),
        out_shape=jax.ShapeDtypeStruct((b, t, WB_V), BF16),
        scratch_shapes=[pltpu.VMEM((DKB, DVB), F32)],
        compiler_params=_params("parallel", "parallel", "arbitrary"),
    )(proj3, proj3, proj3, cos2, sin2, lg, sb, proj3,
      gn_g.reshape(HB, 1, DVB), gn_b.reshape(HB, 1, DVB))


def _group_sum(x, gmat):
    cols = [_split_dot_right(x[:, i * LANE:(i + 1) * LANE], gmat) for i in range(x.shape[1] // LANE)]
    return jnp.concatenate(cols, axis=1)


def _rwkv_prep_kernel(tt, r_ref, k_ref, v_ref, c_ref, rp_ref, kp_ref, vp_ref, cp_ref,
                      rn_ref, kn_ref, vn_ref, cn_ref, mur_ref, muk_ref, muv_ref, muc_ref,
                      w0_ref, wl_ref, a0_ref, al_ref, kk_ref, ka_ref, rk_ref, gmat_ref,
                      r_o, v_o, kk_o, bonus_o, lw0_o, lw1_o, k0_o, k1_o, ka0_o, ka1_o):
    ti = pl.program_id(1)
    has_prev = (ti > 0).astype(F32)
    has_next = (ti < pl.num_programs(1) - 1).astype(F32)
    rows = lax.broadcasted_iota(jnp.int32, (tt, 1), 0)

    def shifted(p_ref, prev_ref, next_ref, mu_ref):
        p = _f32(p_ref[0])
        prev_row = _f32(prev_ref[0, HALO - 1:HALO, :]) * has_prev
        next_row = _f32(next_ref[0, 0:1, :]) * has_next
        before = jnp.where(rows == 0, prev_row, pltpu.roll(p, 1, axis=0))
        after = jnp.where(rows == tt - 1, next_row, pltpu.roll(p, tt - 1, axis=0))
        return p + mu_ref[0:1, :] * (before - p) + mu_ref[1:2, :] * (after - p)

    r = shifted(r_ref, rp_ref, rn_ref, mur_ref)
    k = shifted(k_ref, kp_ref, kn_ref, muk_ref)
    v = shifted(v_ref, vp_ref, vn_ref, muv_ref)
    codes = shifted(c_ref, cp_ref, cn_ref, muc_ref)
    gmat = gmat_ref[...]

    kraw = k * kk_ref[...]
    kk = kraw * lax.rsqrt(_group_sum(kraw * kraw, gmat) + 1e-12)
    r_o[0] = r
    v_o[0] = v
    kk_o[0] = kk
    tcodes = jnp.tanh(codes)
    bonus = jnp.zeros_like(r)
    for d, (lw_o, kd_o, ka_o) in enumerate(((lw0_o, k0_o, ka0_o), (lw1_o, k1_o, ka1_o))):
        w = w0_ref[d:d + 1, :] + _mm(tcodes, wl_ref[d])
        lw_o[0] = -np.exp(-0.5).astype(np.float32) * _sigmoid(w)
        a = _sigmoid(a0_ref[d:d + 1, :] + _mm(codes, al_ref[d]))
        kd = k * (1.0 + (a - 1.0) * ka_ref[...])
        kd_o[0] = kd
        ka_o[0] = kk * a
        bonus = bonus + _group_sum(r * kd * rk_ref[...], gmat) * v
    bonus_o[0] = bonus


def _rwkv_scan_kernel(rev, final, tt, r_ref, lw_ref, k_ref, v_ref, ka_ref, kk_ref, tri_ref, *rest):
    if final:
        ob_ref, bonus_ref, gate_ref, lng_ref, lnb_ref, y_ref, h_ref = rest
    else:
        o_ref, h_ref = rest

    @pl.when(pl.program_id(2) == 0)
    def _():
        h_ref[...] = jnp.zeros_like(h_ref)

    n2 = 2 * CHUNK_C
    lane = lax.broadcasted_iota(jnp.int32, (1, LANE), 1)
    head0 = lane < NC
    ri = lax.broadcasted_iota(jnp.int32, (n2, n2), 0)
    ci = lax.broadcasted_iota(jnp.int32, (n2, n2), 1)
    strict = (ci > ri) if rev else (ci < ri)
    incl = (ci >= ri) if rev else (ci <= ri)
    eye = ri == ci
    nchunk = tt // CHUNK_C

    def stack(z):
        return jnp.concatenate([jnp.where(head0, z, 0.0), jnp.where(head0, 0.0, z)], axis=0)

    order = [(nchunk - 1 - i) if rev else i for i in range(nchunk)]
    items = [(pl.ds(c * CHUNK_C, CHUNK_C), pl.ds(p * LANE, LANE), p) for c in order for p in range(PAIRS_C)]
    rows = [it[0] for it in items]
    cols = [it[1] for it in items]

    lws = _each(lambda sl, cl: lw_ref[0, sl, cl], rows, cols)
    g = _each(lambda lw: _split_dot(tri_ref[...], lw), lws)
    e_end = _each(lambda gu: jnp.exp(gu[0:1, :] if rev else gu[CHUNK_C - 1:CHUNK_C, :]), g)
    e_neg = _each(lambda gu: jnp.exp(-gu), g)
    bs = _each(lambda sl, cl, gu, lw: stack(-kk_ref[0, sl, cl] * jnp.exp(gu - lw)), rows, cols, g, lws)
    as_ = _each(lambda sl, cl, en: stack(ka_ref[0, sl, cl] * en), rows, cols, e_neg)
    ks = _each(lambda sl, cl, en: stack(k_ref[0, sl, cl] * en), rows, cols, e_neg)
    rs = _each(lambda sl, cl, gu: stack(r_ref[0, sl, cl] * jnp.exp(gu)), rows, cols, g)
    vs = _each(lambda sl, cl: stack(v_ref[0, sl, cl]), rows, cols)

    ak = _each(lambda a, k: jnp.concatenate([a, k], axis=0), as_, ks)
    sc_b = _each(_mm_nt, bs, ak)
    sc_r = _each(_mm_nt, rs, ak)
    a_ab = _each(lambda z: jnp.where(strict, z[:, :n2], 0.0), sc_b)
    a_ak = _each(lambda z: jnp.where(strict, z[:, n2:], 0.0), sc_b)
    m_rak = _each(lambda z: jnp.where(jnp.concatenate([incl, incl], axis=1), z, 0.0), sc_r)
    tinv = _each(lambda a: jnp.where(eye, 1.0, 0.0) + a, a_ab)
    pw = _each(lambda a: _mm(a, a), a_ab)
    akv = _each(_mm, a_ak, vs)
    nsteps = int(np.log2(CHUNK_C)) - 1
    for i in range(nsteps):
        if i < nsteps - 1:
            both = _each(lambda t, p: _mm(jnp.concatenate([t, p], axis=0), p), tinv, pw)
            tinv = _each(lambda t, b: t + b[:n2], tinv, both)
            pw = _each(lambda b: b[n2:], both)
        else:
            tinv = _each(lambda t, p: t + _mm(t, p), tinv, pw)
    pwv = _each(lambda t, b, x: _mm(t, jnp.concatenate([b, x], axis=1)), tinv, bs, akv)
    low = _each(lambda z, v: jnp.concatenate([z, jnp.concatenate([jnp.zeros_like(v), v], axis=1)], axis=0),
                pwv, vs)
    x1 = _each(_mm, m_rak, low)
    x2 = _each(lambda a, k, e, z: _mm_tn(jnp.concatenate([a * e, k * e], axis=0), z), as_, ks, e_end, low)
    top = _each(lambda r, z, e, y: jnp.concatenate([r + z[:, :LANE], jnp.where(eye, e, 0.0) + y[:, :LANE]], axis=0),
                rs, x1, e_end, x2)
    o_in = _each(lambda z: z[:, LANE:], x1)
    dmat = _each(lambda y: y[:, LANE:], x2)

    outs = []
    h = [h_ref[p] for p in range(PAIRS_C)]
    for (sl, cl, p), tp, oi, dm in zip(items, top, o_in, dmat):
        both = _mm(tp, h[p])
        h[p] = both[n2:] + dm
        os_ = both[:n2] + oi
        outs.append((sl, cl, os_[:CHUNK_C, :] + os_[CHUNK_C:, :]))
    for p in range(PAIRS_C):
        h_ref[p] = h[p]

    for sl, cl, o in outs:
        if final:
            o = o + ob_ref[0, sl, cl]
            inv_n = 1.0 / NC
            s0 = jnp.sum(jnp.where(head0, o, 0.0), axis=-1, keepdims=True)
            s1 = jnp.sum(jnp.where(head0, 0.0, o), axis=-1, keepdims=True)
            cen = o - jnp.where(head0, s0, s1) * inv_n
            c2 = cen * cen
            v0 = jnp.sum(jnp.where(head0, c2, 0.0), axis=-1, keepdims=True)
            v1 = jnp.sum(jnp.where(head0, 0.0, c2), axis=-1, keepdims=True)
            var = jnp.where(head0, v0, v1) * inv_n
            y = cen * lax.rsqrt(var + RWKV_GN_EPS) * lng_ref[:, cl] + lnb_ref[:, cl] + bonus_ref[0, sl, cl]
            y_ref[0, sl, cl] = (y * _silu(_f32(gate_ref[0, sl, cl]))).astype(y_ref.dtype)
        else:
            o_ref[0, sl, cl] = o


def _rwkv_mixer(proj3, mu, w0, w_lora_b, a0, a_lora_b, k_k, k_a, r_k, ln_g, ln_b):
    b, t, _ = proj3.shape
    tt = _pick(t, 256)
    nt = t // tt
    sub = tt // HALO

    def cur(off, width):
        return pl.BlockSpec((1, tt, width), lambda bi, ti: (bi, ti, off // width))

    def prev(off, width):
        return pl.BlockSpec((1, HALO, width),
                            lambda bi, ti: (bi, jnp.maximum(ti * sub - 1, 0), off // width))

    def nxt(off, width):
        return pl.BlockSpec((1, HALO, width),
                            lambda bi, ti: (bi, jnp.minimum((ti + 1) * sub, nt * sub - 1), off // width))

    def full(shape):
        return pl.BlockSpec(shape, lambda bi, ti: (0,) * len(shape))

    segs = ((OFF_CR, WC), (OFF_CK, WC), (OFF_CV, WC), (OFF_CC, N_CODES))
    in_specs = ([cur(o, w) for o, w in segs] + [prev(o, w) for o, w in segs] + [nxt(o, w) for o, w in segs]
                + [full((2, WC))] * 3 + [full((2, N_CODES))]
                + [full((2, WC)), full((2, N_CODES, WC)), full((2, WC)), full((2, N_CODES, WC))]
                + [full((1, WC))] * 3 + [full((LANE, LANE))])
    wl = jnp.zeros((2, N_CODES, WC), F32)
    al = jnp.zeros((2, N_CODES, WC), F32)
    for d in range(2):
        wl = wl.at[d, d * W_LORA:(d + 1) * W_LORA].set(w_lora_b[d])
        al = al.at[d, 2 * W_LORA + d * A_LORA:2 * W_LORA + (d + 1) * A_LORA].set(a_lora_b[d])
    lane_i = np.arange(LANE)
    gmat = jnp.asarray((lane_i[:, None] // NC == lane_i[None, :] // NC).astype(np.float32), dtype=BF16)
    row_out = pl.BlockSpec((1, tt, WC), lambda bi, ti: (bi, ti, 0))
    outs = pl.pallas_call(
        functools.partial(_rwkv_prep_kernel, tt),
        grid=(b, nt),
        in_specs=in_specs,
        out_specs=[row_out] * 10,
        out_shape=[jax.ShapeDtypeStruct((b, t, WC), F32)] * 10,
        compiler_params=_params("parallel", "parallel"),
    )(*([proj3] * 12), mu[:, :WC], mu[:, WC:2 * WC], mu[:, 2 * WC:3 * WC], mu[:, 3 * WC:],
      w0, wl.astype(BF16), a0, al.astype(BF16), k_k.reshape(1, WC), k_a.reshape(1, WC),
      r_k.reshape(1, WC), gmat)
    r, v, kk, bonus, lw0, lw1, k0, k1, ka0, ka1 = outs

    ts = _pick(t, 512)
    ns = t // ts

    def scan(rev, final, lw, kd, ka, ob=None):
        wblk = PAIRS_C * LANE
        tmap = (lambda i: ns - 1 - i) if rev else (lambda i: i)
        blk = pl.BlockSpec((1, ts, wblk), lambda bi, h, ti: (bi, tmap(ti), h))
        head_vec = pl.BlockSpec((1, wblk), lambda bi, h, ti: (0, h))
        in_specs = [blk] * 6 + [pl.BlockSpec((CHUNK_C, CHUNK_C), lambda bi, h, ti: (0, 0))]
        ii = np.arange(CHUNK_C)
        tri = (ii[None, :] >= ii[:, None]) if rev else (ii[None, :] <= ii[:, None])
        args = [r, lw, kd, v, ka, kk, jnp.asarray(tri.astype(np.float32), dtype=BF16)]
        if final:
            gate = pl.BlockSpec((1, ts, wblk), lambda bi, h, ti: (bi, tmap(ti), OFF_CG // wblk + h))
            in_specs += [blk, blk, gate, head_vec, head_vec]
            args += [ob, bonus, proj3, ln_g.reshape(1, WC), ln_b.reshape(1, WC)]
        return pl.pallas_call(
            functools.partial(_rwkv_scan_kernel, rev, final, ts),
            grid=(b, WC // wblk, ns),
            in_specs=in_specs,
            out_specs=blk,
            out_shape=jax.ShapeDtypeStruct((b, t, WC), BF16 if final else F32),
            scratch_shapes=[pltpu.VMEM((PAIRS_C, LANE, LANE), F32)],
            compiler_params=_params("parallel", "parallel", "arbitrary"),
        )(*args)

    ob = scan(True, False, lw1, k1, ka1)
    return scan(False, True, lw0, k0, ka0, ob=ob)


def _merge_kernel(final, ya_ref, yb_ref, yc_ref, mg_ref, x_ref, wa_ref, wb_ref, wc_ref, wo_ref,
                  fg_ref, o_ref):
    za = jnp.dot(ya_ref[...], wa_ref[...], preferred_element_type=F32)
    zb = jnp.dot(yb_ref[...], wb_ref[...], preferred_element_type=F32)
    zc = jnp.dot(yc_ref[...], wc_ref[...], preferred_element_type=F32)
    mg = _f32(mg_ref[...])
    mixed = (_sigmoid(mg[:, :D_MODEL]) * za
             + _sigmoid(mg[:, D_MODEL:2 * D_MODEL]) * zb
             + _sigmoid(mg[:, 2 * D_MODEL:]) * zc)
    out = x_ref[...] + jnp.dot(mixed.astype(BF16), wo_ref[...], preferred_element_type=F32)
    if final:
        ms = jnp.mean(out * out, axis=-1, keepdims=True)
        out = out * lax.rsqrt(ms + EPS) * fg_ref[...]
    o_ref[...] = out


def _merge(ya, yb, yc, proj, x2, wa, wb, wc, wo, fg, final):
    m = x2.shape[0]
    tm = _pick(m, 256)

    def rows(width):
        return pl.BlockSpec((tm, width), lambda i: (i, 0))

    def full(shape):
        return pl.BlockSpec(shape, lambda i: (0, 0))

    return pl.pallas_call(
        functools.partial(_merge_kernel, final),
        grid=(m // tm,),
        in_specs=[rows(WA), rows(WB_V), rows(WC), rows(3 * D_MODEL), rows(D_MODEL),
                  full((WA, D_MODEL)), full((WB_V, D_MODEL)), full((WC, D_MODEL)),
                  full((D_MODEL, D_MODEL)), full((1, D_MODEL))],
        out_specs=rows(D_MODEL),
        out_shape=jax.ShapeDtypeStruct((m, D_MODEL), F32),
        compiler_params=_params("parallel"),
    )(ya, yb, yc, proj, x2, wa.astype(BF16), wb.astype(BF16), wc.astype(BF16), wo.astype(BF16),
      fg.reshape(1, D_MODEL))


def kernel(x, norm_g, w_in, hgrn_lb_logits, hgrn_norm_g, ret_norm_g, ret_norm_b, rwkv_mu, rwkv_w0,
           rwkv_w_lora_b, rwkv_a0, rwkv_a_lora_b, rwkv_k_k, rwkv_k_a, rwkv_r_k, rwkv_ln_g,
           rwkv_ln_b, w_branch_a, w_branch_b, w_branch_c, w_out, final_norm_g):
    b, t, d = x.shape
    m = b * t
    depth = w_in.shape[0]
    p_lb = jax.nn.softmax(hgrn_lb_logits.astype(F32), axis=0)
    lbs = jnp.cumsum(p_lb, axis=0) - p_lb[0:1]
    x2 = x.reshape(m, d)
    for l in range(depth):
        proj = _inproj(x2, norm_g[l], _permute_w_in(w_in, l))
        proj3 = proj.reshape(b, t, N_PAD)
        ya = _hgrn_mixer(proj3, lbs[l], hgrn_norm_g[l])
        yb = _ret_mixer(proj3, ret_norm_g[l], ret_norm_b[l])
        yc = _rwkv_mixer(proj3, rwkv_mu[l], rwkv_w0[l], rwkv_w_lora_b[l], rwkv_a0[l],
                         rwkv_a_lora_b[l], rwkv_k_k[l], rwkv_k_a[l], rwkv_r_k[l],
                         rwkv_ln_g[l], rwkv_ln_b[l])
        x2 = _merge(ya.reshape(m, WA), yb.reshape(m, WB_V), yc.reshape(m, WC), proj, x2,
                    w_branch_a[l], w_branch_b[l], w_branch_c[l], w_out[l], final_norm_g,
                    final=(l == depth - 1))
    return x2.reshape(b, t, d)
```

```python
import functools

import numpy as np
import jax
import jax.numpy as jnp
from jax import lax
from jax.experimental import pallas as pl
from jax.experimental.pallas import tpu as pltpu

F32 = jnp.float32
BF16 = jnp.bfloat16

D_MODEL = 1024
EPS = 1e-6
F_MIN = 1e-6
DKA = 128
HA = D_MODEL // DKA
WA = HA * DKA
HB = 8
DKB = D_MODEL // HB
DVB = 2 * DKB
WB_QK = HB * DKB
WB_V = HB * DVB
ROPE_BASE = 10000.0
RET_GN_EPS = 1e-5
NC = 64
HC = D_MODEL // NC
WC = HC * NC
W_LORA = 64
A_LORA = 64
RWKV_GN_EPS = 64e-5
N_CODES = 2 * W_LORA + 2 * A_LORA
C_SHIFT = 3 * WC + N_CODES

SRC_A = 0
SRC_B = 5 * WA
SRC_CS = SRC_B + 2 * WB_QK + 2 * WB_V
SRC_CG = SRC_CS + C_SHIFT
SRC_MERGE = SRC_CG + WC
N_IN = SRC_MERGE + 3 * D_MODEL

OFF_MERGE = 0
OFF_AQ = OFF_MERGE + 3 * D_MODEL
OFF_AFF = OFF_AQ + WA
OFF_AFB = OFF_AFF + WA
OFF_AI = OFF_AFB + WA
OFF_AG = OFF_AI + WA
OFF_BQ = OFF_AG + WA
OFF_BK = OFF_BQ + WB_QK
OFF_BV = OFF_BK + WB_QK
OFF_BG = OFF_BV + WB_V
OFF_CG = OFF_BG + WB_V
OFF_CR = OFF_CG + WC
OFF_CK = OFF_CR + WC
OFF_CV = OFF_CK + WC
OFF_CC = OFF_CV + WC
LANE = 128
PROJ_DTYPE = BF16
SCAN_DTYPE = BF16
HALO = 16
PROJ_SPLIT = 4
N_PAD = -(-(OFF_CC + N_CODES) // (PROJ_SPLIT * LANE)) * PROJ_SPLIT * LANE
PROJ_TN = N_PAD // PROJ_SPLIT
MXU_N = 256
WPERM_TN = 256

CHUNK_A = 128
GROUP_A = 8
HEADS_A = 2
CHUNK_B = 256
CHUNK_C = 64
PAIRS_C = 4
VMEM_LIMIT = 48 * 1024 * 1024
PROJ_VMEM_LIMIT = 56 * 1024 * 1024

NT_DIMS = (((1,), (1,)), ((), ()))
TN_DIMS = (((0,), (0,)), ((), ()))


def _mm(a, b):
    return jnp.dot(a.astype(BF16), b.astype(BF16), preferred_element_type=F32)


def _mm_nt(a, b):
    return lax.dot_general(a.astype(BF16), b.astype(BF16), NT_DIMS, preferred_element_type=F32)


def _mm_tn(a, b):
    return lax.dot_general(a.astype(BF16), b.astype(BF16), TN_DIMS, preferred_element_type=F32)


def _split_dot(ones_mat, x):
    hi = x.astype(BF16)
    r1 = x - hi.astype(F32)
    mid = r1.astype(BF16)
    lo = (r1 - mid.astype(F32)).astype(BF16)
    out = jnp.dot(ones_mat, hi, preferred_element_type=F32)
    out = out + jnp.dot(ones_mat, mid, preferred_element_type=F32)
    return out + jnp.dot(ones_mat, lo, preferred_element_type=F32)


def _split_dot_right(x, ones_mat):
    hi = x.astype(BF16)
    r1 = x - hi.astype(F32)
    mid = r1.astype(BF16)
    lo = (r1 - mid.astype(F32)).astype(BF16)
    out = jnp.dot(hi, ones_mat, preferred_element_type=F32)
    out = out + jnp.dot(mid, ones_mat, preferred_element_type=F32)
    return out + jnp.dot(lo, ones_mat, preferred_element_type=F32)


def _f32(z):
    return z.astype(F32)


def _sigmoid(z):
    return 0.5 * jnp.tanh(0.5 * z) + 0.5


def _cumsum_rows(x, rev):
    n = x.shape[0]
    rows = lax.broadcasted_iota(jnp.int32, (n, 1), 0)
    step = 1
    while step < n:
        if rev:
            x = x + jnp.where(rows < n - step, pltpu.roll(x, n - step, axis=0), 0.0)
        else:
            x = x + jnp.where(rows >= step, pltpu.roll(x, step, axis=0), 0.0)
        step *= 2
    return x


def _silu(z):
    return z * _sigmoid(z)


def _pick(n, pref):
    t = min(n, pref)
    assert n % t == 0, (n, pref)
    return t


def _params(*sem, vmem=VMEM_LIMIT):
    return pltpu.CompilerParams(dimension_semantics=sem, vmem_limit_bytes=vmem)


def _inproj_kernel(x_ref, g_ref, w_ref, o_ref, h_ref):
    @pl.when(pl.program_id(1) == 0)
    def _():
        x = x_ref[...]
        ms = jnp.mean(x * x, axis=-1, keepdims=True)
        h_ref[...] = (x * lax.rsqrt(ms + EPS) * g_ref[...]).astype(BF16)

    def tile(cols):
        o_ref[:, cols] = jnp.dot(h_ref[...], w_ref[:, cols], preferred_element_type=F32).astype(o_ref.dtype)

    for start in range(0, PROJ_TN, MXU_N):
        tile(pl.ds(start, min(MXU_N, PROJ_TN - start)))


def _inproj(x2, g, w_bf16):
    m = x2.shape[0]
    tm = _pick(m, 1024)
    return pl.pallas_call(
        _inproj_kernel,
        grid=(m // tm, N_PAD // PROJ_TN),
        in_specs=[
            pl.BlockSpec((tm, D_MODEL), lambda i, j: (i, 0)),
            pl.BlockSpec((1, D_MODEL), lambda i, j: (0, 0)),
            pl.BlockSpec((D_MODEL, PROJ_TN), lambda i, j: (0, j)),
        ],
        out_specs=pl.BlockSpec((tm, PROJ_TN), lambda i, j: (i, j)),
        out_shape=jax.ShapeDtypeStruct((m, N_PAD), PROJ_DTYPE),
        scratch_shapes=[pltpu.VMEM((tm, D_MODEL), BF16)],
        compiler_params=_params("parallel", "arbitrary", vmem=PROJ_VMEM_LIMIT),
    )(x2, g.reshape(1, D_MODEL), w_bf16)


def _wperm_kernel(nvalid, a_ref, b_ref, o_ref):
    j = pl.program_id(0)
    for idx, src in enumerate((a_ref, b_ref)):
        cols = pl.ds(idx * WPERM_TN, WPERM_TN)
        blk = src[...].astype(BF16)
        o_ref[:, cols] = jnp.where(2 * j + idx < nvalid, blk, jnp.zeros_like(blk))


def _src_block(jd):
    n_merge = (N_IN - SRC_MERGE) // WPERM_TN
    n_ab = SRC_CS // WPERM_TN
    n_cg = (SRC_MERGE - SRC_CG) // WPERM_TN
    n_src = N_IN // WPERM_TN
    return jnp.where(jd < n_merge, jd + SRC_MERGE // WPERM_TN,
                     jnp.where(jd < n_merge + n_ab, jd - n_merge,
                               jnp.where(jd < n_merge + n_ab + n_cg, jd - n_merge - n_ab + SRC_CG // WPERM_TN,
                                         jnp.minimum(jd - n_merge - n_cg, n_src - 1))))


def _permute_w_in(w_in, layer):
    assert all(off % WPERM_TN == 0 for off in (SRC_CS, SRC_CG, SRC_MERGE, N_IN)) and N_PAD % (2 * WPERM_TN) == 0

    def src(idx):
        return pl.BlockSpec((None, D_MODEL, WPERM_TN), lambda j: (layer, 0, _src_block(2 * j + idx)))

    return pl.pallas_call(
        functools.partial(_wperm_kernel, N_IN // WPERM_TN),
        grid=(N_PAD // (2 * WPERM_TN),),
        in_specs=[src(0), src(1)],
        out_specs=pl.BlockSpec((D_MODEL, 2 * WPERM_TN), lambda j: (0, j)),
        out_shape=jax.ShapeDtypeStruct((D_MODEL, N_PAD), BF16),
        compiler_params=_params("parallel"),
    )(w_in, w_in)


def _each(fn, *lists):
    return [fn(*items) for items in zip(*lists)]


def _hgrn_kernel(rev, final, tt, q_ref, f_ref, v_ref, lb_ref, tri_ref, *rest):
    if final:
        ob_ref, gate_ref, ng_ref, y_ref, st_ref = rest
    else:
        o_ref, st_ref = rest

    @pl.when(pl.program_id(2) == 0)
    def _():
        st_ref[...] = jnp.zeros_like(st_ref)

    n = CHUNK_A
    ri = lax.broadcasted_iota(jnp.int32, (n, n), 0)
    ci = lax.broadcasted_iota(jnp.int32, (n, n), 1)
    g_bits = int(np.log2(GROUP_A))
    diag_mask = ((ri >> g_bits) == (ci >> g_bits)) & ((ci >= ri) if rev else (ci <= ri))
    halves = [GROUP_A << i for i in range(int(np.log2(n // GROUP_A)))]
    level_masks = []
    for half in halves:
        hb = int(np.log2(half))
        q_side, k_side = (0, 1) if rev else (1, 0)
        level_masks.append(((ri >> (hb + 1)) == (ci >> (hb + 1)))
                           & (((ri >> hb) & 1) == q_side) & (((ci >> hb) & 1) == k_side))

    def ref_rows(b, blk, idx):
        picked = b.reshape(n // blk, blk, DKA)[:, idx:idx + 1, :]
        return jnp.broadcast_to(picked, (n // blk, blk, DKA)).reshape(n, DKA)

    nchunk = tt // n
    order = [(nchunk - 1 - i) if rev else i for i in range(nchunk)]
    items = [(pl.ds(c * n, n), pl.ds(hh * DKA, DKA), hh) for c in order for hh in range(HEADS_A)]
    sls = [it[0] for it in items]
    cls = [it[1] for it in items]
    lbs = [lb_ref[it[2]] for it in items]
    sig = _each(lambda sl, cl: _sigmoid(_f32(f_ref[0, sl, cl])), sls, cls)
    lf = _each(lambda z, lb: jnp.log(jnp.maximum(lb + (1.0 - lb) * z, F_MIN)), sig, lbs)
    k = _each(lambda z, lb: (1.0 - lb) * (1.0 - z), sig, lbs)
    q = _each(lambda sl, cl: _silu(_f32(q_ref[0, sl, cl])), sls, cls)
    v = _each(lambda sl, cl: v_ref[0, sl, cl], sls, cls)
    b = _each(lambda z: _split_dot(tri_ref[...], z), lf)
    d0 = _each(lambda z: z - ref_rows(z, GROUP_A, GROUP_A // 2 if rev else GROUP_A // 2 - 1), b)
    qb = _each(lambda z: z.astype(BF16), q)
    kb = _each(lambda z: z.astype(BF16), k)
    scores = _each(lambda qq, kk, d: jnp.where(
        diag_mask, _mm_nt(qq * jnp.exp(d).astype(BF16), kk * jnp.exp(-d).astype(BF16)), 0.0), qb, kb, d0)
    for half, mask in zip(halves, level_masks):
        e = _each(lambda z: jnp.exp(-jnp.abs(z - ref_rows(z, 2 * half, half if rev else half - 1))).astype(BF16),
                  b)
        scores = _each(lambda sc, qq, kk, ee: jnp.where(mask, _mm_nt(qq * ee, kk * ee), sc), scores, qb, kb, e)
    o_intra = _each(_mm, scores, v)
    b_end = _each(lambda z: z[0:1, :] if rev else z[n - 1:n, :], b)
    q_in = _each(lambda qq, z: qq * jnp.exp(z), q, b)
    kv = _each(lambda vv, kk, z, ze: _mm_tn(vv, kk * jnp.exp(ze - z)), v, k, b, b_end)
    e_end = _each(jnp.exp, b_end)

    outs = []
    st = [st_ref[hh] for hh in range(HEADS_A)]
    for (sl, cl, hh), oi, qi, kvi, ee in zip(items, o_intra, q_in, kv, e_end):
        outs.append((sl, cl, hh, oi + _mm_nt(qi, st[hh])))
        st[hh] = st[hh] * ee + kvi
    for hh in range(HEADS_A):
        st_ref[hh] = st[hh]

    for sl, cl, hh, o in outs:
        if final:
            o = o + ob_ref[0, sl, cl]
            ms = jnp.mean(o * o, axis=-1, keepdims=True)
            y = o * lax.rsqrt(ms + EPS) * ng_ref[hh] * _silu(_f32(gate_ref[0, sl, cl]))
            y_ref[0, sl, cl] = y.astype(y_ref.dtype)
        else:
            o_ref[0, sl, cl] = o


def _tri_blockdiag(n, blk, rev):
    i = np.arange(n)
    same = (i[:, None] // blk) == (i[None, :] // blk)
    tri = (i[None, :] >= i[:, None]) if rev else (i[None, :] <= i[:, None])
    return jnp.asarray((same & tri).astype(np.float32), dtype=BF16)


def _hgrn_pass(proj3, lb, rev, final, ob=None, norm_g=None):
    b, t, _ = proj3.shape
    tt = _pick(t, 1024)
    nt = t // tt
    tmap = (lambda i: nt - 1 - i) if rev else (lambda i: i)

    wblk = HEADS_A * DKA

    def col(off):
        return pl.BlockSpec((1, tt, wblk), lambda bi, h, ti: (bi, tmap(ti), off // wblk + h))

    head_vec = pl.BlockSpec((HEADS_A, 1, DKA), lambda bi, h, ti: (h, 0, 0))
    in_specs = [col(OFF_AQ), col(OFF_AFB if rev else OFF_AFF), col(OFF_AI), head_vec,
                pl.BlockSpec((LANE, LANE), lambda bi, h, ti: (0, 0))]
    args = [proj3, proj3, proj3, lb.reshape(HA, 1, DKA), _tri_blockdiag(LANE, CHUNK_A, rev)]
    out_block = pl.BlockSpec((1, tt, wblk), lambda bi, h, ti: (bi, tmap(ti), h))
    if final:
        in_specs += [out_block, col(OFF_AG), head_vec]
        args += [ob, proj3, norm_g.reshape(HA, 1, DKA)]
        out_dtype = BF16
    else:
        out_dtype = F32
    return pl.pallas_call(
        functools.partial(_hgrn_kernel, rev, final, tt),
        grid=(b, HA // HEADS_A, nt),
        in_specs=in_specs,
        out_specs=out_block,
        out_shape=jax.ShapeDtypeStruct((b, t, WA), out_dtype),
        scratch_shapes=[pltpu.VMEM((HEADS_A, DKA, DKA), F32)],
        compiler_params=_params("parallel", "parallel", "arbitrary"),
    )(*args)


def _hgrn_mixer(proj3, lbs, norm_g):
    ob = _hgrn_pass(proj3, lbs[1], rev=True, final=False)
    return _hgrn_pass(proj3, lbs[0], rev=False, final=True, ob=ob, norm_g=norm_g)


def _rotary(z, cos2, sin2):
    return z * cos2 + pltpu.roll(z, DKB // 2, axis=1) * sin2


def _ret_state_kernel(nsub, k_ref, v_ref, cos_ref, sin_ref, lg_ref, sb_ref, st_ref):
    @pl.when(pl.program_id(2) == 0)
    def _():
        st_ref[...] = jnp.zeros_like(st_ref)

    cb = CHUNK_B
    lg2 = lg_ref[0]
    lg = lg2[:, :DKB]
    pos = lax.broadcasted_iota(jnp.int32, (cb, DKB), 0).astype(F32)
    dec = jnp.exp(pos * lg) * (DKB ** -0.5)
    order = list(range(nsub - 1, -1, -1))
    sls = [pl.ds(c * cb, cb) for c in order]
    kd = _each(lambda sl: _rotary(_f32(k_ref[0, sl, :]), cos_ref[sl, :], sin_ref[sl, :]) * dec, sls)
    kv = _each(lambda kk, sl: _mm_tn(kk, v_ref[0, sl, :]), kd, sls)
    e_chunk = jnp.exp(cb * lg2)
    st = st_ref[...]
    for c, kvi in zip(order, kv):
        sb_ref[0, 0, c] = st
        st = st * e_chunk + kvi
    st_ref[...] = st


def _ret_out_kernel(nsub, q_ref, k_ref, v_ref, cos_ref, sin_ref, lg_ref, sb_ref, gate_ref, gg_ref,
                    gb_ref, y_ref, st_ref):
    @pl.when(pl.program_id(2) == 0)
    def _():
        st_ref[...] = jnp.zeros_like(st_ref)

    cb = CHUNK_B
    lg2 = lg_ref[0]
    lg = lg2[:, :DKB]
    pos = lax.broadcasted_iota(jnp.int32, (cb, DKB), 0).astype(F32)
    ri = lax.broadcasted_iota(jnp.int32, (cb, cb), 0)
    ci = lax.broadcasted_iota(jnp.int32, (cb, cb), 1)
    decay = jnp.exp(jnp.abs(ri - ci).astype(F32) * lg2)
    dec_fwd = jnp.exp((pos + 1.0) * lg)
    dec_bwd = jnp.exp((cb - pos) * lg)
    dec_key = jnp.exp((cb - 1.0 - pos) * lg)
    e_chunk = jnp.exp(cb * lg2)

    sls = [pl.ds(c * cb, cb) for c in range(nsub)]
    q = _each(lambda sl: _rotary(_f32(q_ref[0, sl, :]), cos_ref[sl, :], sin_ref[sl, :]), sls)
    k = _each(lambda sl: _rotary(_f32(k_ref[0, sl, :]), cos_ref[sl, :], sin_ref[sl, :]) * (DKB ** -0.5), sls)
    v = _each(lambda sl: v_ref[0, sl, :], sls)
    scores = _each(lambda qq, kk: _mm_nt(qq, kk) * decay, q, k)
    o = _each(_mm, scores, v)
    o = _each(lambda oo, qq, c: oo + _mm(qq * dec_bwd, sb_ref[0, 0, c]), o, q, list(range(nsub)))
    kv = _each(lambda kk, vv: _mm_tn(kk * dec_key, vv), k, v)
    q_fwd = _each(lambda qq: qq * dec_fwd, q)
    st = st_ref[...]
    outs = []
    for oo, qf, kvi in zip(o, q_fwd, kv):
        outs.append(oo + _mm(qf, st))
        st = st * e_chunk + kvi
    st_ref[...] = st

    for sl, oo in zip(sls, outs):
        mu = jnp.mean(oo, axis=-1, keepdims=True)
        cen = oo - mu
        var = jnp.mean(cen * cen, axis=-1, keepdims=True)
        y = (cen * lax.rsqrt(var + RET_GN_EPS) * gg_ref[0] + gb_ref[0]) * _silu(_f32(gate_ref[0, sl, :]))
        y_ref[0, sl, :] = y.astype(y_ref.dtype)


def _ret_mixer(proj3, gn_g, gn_b):
    b, t, _ = proj3.shape
    cb = CHUNK_B
    tt = _pick(t, 8 * cb)
    nsub = tt // cb
    nt = t // tt
    half = DKB // 2
    inv = ROPE_BASE ** (-jnp.arange(0, DKB, 2, dtype=F32) / DKB)
    ang_hi = (jnp.arange(t // cb, dtype=F32) * cb)[:, None, None] * inv
    ang_lo = jnp.arange(cb, dtype=F32)[None, :, None] * inv
    cos = (jnp.cos(ang_hi) * jnp.cos(ang_lo) - jnp.sin(ang_hi) * jnp.sin(ang_lo)).reshape(t, half)
    sin = (jnp.sin(ang_hi) * jnp.cos(ang_lo) + jnp.cos(ang_hi) * jnp.sin(ang_lo)).reshape(t, half)
    cos2 = jnp.concatenate([cos, cos], axis=-1)
    sin2 = jnp.concatenate([-sin, sin], axis=-1)
    log_gamma = jnp.log1p(-jnp.exp2(-5.0 - jnp.arange(HB, dtype=F32)))
    lg = jnp.broadcast_to(log_gamma[:, None, None], (HB, 1, DVB))

    def col(off, width, tmap):
        return pl.BlockSpec((1, tt, width), lambda bi, h, c: (bi, tmap(c), off // width + h))

    def tab(tmap):
        return pl.BlockSpec((tt, DKB), lambda bi, h, c: (tmap(c), 0))

    lg_spec = pl.BlockSpec((1, 1, DVB), lambda bi, h, c: (h, 0, 0))
    rmap = lambda c: nt - 1 - c
    fmap = lambda c: c

    sb = pl.pallas_call(
        functools.partial(_ret_state_kernel, nsub),
        grid=(b, HB, nt),
        in_specs=[col(OFF_BK, DKB, rmap), col(OFF_BV, DVB, rmap), tab(rmap), tab(rmap), lg_spec],
        out_specs=pl.BlockSpec((1, 1, nsub, DKB, DVB), lambda bi, h, c: (bi, h, rmap(c), 0, 0)),
        out_shape=jax.ShapeDtypeStruct((b, HB, t // cb, DKB, DVB), F32),
        scratch_shapes=[pltpu.VMEM((DKB, DVB), F32)],
        compiler_params=_params("parallel", "parallel", "arbitrary"),
    )(proj3, proj3, cos2, sin2, lg)

    head_vec = pl.BlockSpec((1, 1, DVB), lambda bi, h, c: (h, 0, 0))
    return pl.pallas_call(
        functools.partial(_ret_out_kernel, nsub),
        grid=(b, HB, nt),
        in_specs=[col(OFF_BQ, DKB, fmap), col(OFF_BK, DKB, fmap), col(OFF_BV, DVB, fmap),
                  tab(fmap), tab(fmap), lg_spec,
                  pl.BlockSpec((1, 1, nsub, DKB, DVB), lambda bi, h, c: (bi, h, c, 0, 0)),
                  col(OFF_BG, DVB, fmap), head_vec, head_vec],
        out_specs=pl.BlockSpec((1, tt, DVB), lambda bi, h, c: (bi, c, h)),
        out_shape=jax.ShapeDtypeStruct((b, t, WB_V), BF16),
        scratch_shapes=[pltpu.VMEM((DKB, DVB), F32)],
        compiler_params=_params("parallel", "parallel", "arbitrary"),
    )(proj3, proj3, proj3, cos2, sin2, lg, sb, proj3,
      gn_g.reshape(HB, 1, DVB), gn_b.reshape(HB, 1, DVB))


def _group_sum(x, gmat):
    cols = [_split_dot_right(x[:, i * LANE:(i + 1) * LANE], gmat) for i in range(x.shape[1] // LANE)]
    return jnp.concatenate(cols, axis=1)


def _rwkv_prep_kernel(tt, r_ref, k_ref, v_ref, c_ref, rp_ref, kp_ref, vp_ref, cp_ref,
                      rn_ref, kn_ref, vn_ref, cn_ref, mur_ref, muk_ref, muv_ref, muc_ref,
                      w0_ref, wl_ref, a0_ref, al_ref, kk_ref, ka_ref, rk_ref, gmat_ref,
                      r_o, v_o, kk_o, bonus_o, lw0_o, lw1_o, k0_o, k1_o, ka0_o, ka1_o):
    ti = pl.program_id(1)
    has_prev = (ti > 0).astype(F32)
    has_next = (ti < pl.num_programs(1) - 1).astype(F32)
    rows = lax.broadcasted_iota(jnp.int32, (tt, 1), 0)

    def shifted(p_ref, prev_ref, next_ref, mu_ref):
        p = _f32(p_ref[0])
        prev_row = _f32(prev_ref[0, HALO - 1:HALO, :]) * has_prev
        next_row = _f32(next_ref[0, 0:1, :]) * has_next
        before = jnp.where(rows == 0, prev_row, pltpu.roll(p, 1, axis=0))
        after = jnp.where(rows == tt - 1, next_row, pltpu.roll(p, tt - 1, axis=0))
        return p + mu_ref[0:1, :] * (before - p) + mu_ref[1:2, :] * (after - p)

    r = shifted(r_ref, rp_ref, rn_ref, mur_ref)
    k = shifted(k_ref, kp_ref, kn_ref, muk_ref)
    v = shifted(v_ref, vp_ref, vn_ref, muv_ref)
    codes = shifted(c_ref, cp_ref, cn_ref, muc_ref)
    gmat = gmat_ref[...]

    kraw = k * kk_ref[...]
    kk = kraw * lax.rsqrt(_group_sum(kraw * kraw, gmat) + 1e-12)
    r_o[0] = r.astype(r_o.dtype)
    v_o[0] = v.astype(v_o.dtype)
    kk_o[0] = kk.astype(kk_o.dtype)
    tcodes = jnp.tanh(codes)
    bonus = jnp.zeros_like(r)
    for d, (lw_o, kd_o, ka_o) in enumerate(((lw0_o, k0_o, ka0_o), (lw1_o, k1_o, ka1_o))):
        w = w0_ref[d:d + 1, :] + _mm(tcodes, wl_ref[d])
        lw_o[0] = -np.exp(-0.5).astype(np.float32) * _sigmoid(w)
        a = _sigmoid(a0_ref[d:d + 1, :] + _mm(codes, al_ref[d]))
        kd = k * (1.0 + (a - 1.0) * ka_ref[...])
        kd_o[0] = kd.astype(kd_o.dtype)
        ka_o[0] = (kk * a).astype(ka_o.dtype)
        bonus = bonus + _group_sum(r * kd * rk_ref[...], gmat) * v
    bonus_o[0] = bonus


def _rwkv_scan_kernel(rev, final, tt, r_ref, lw_ref, k_ref, v_ref, ka_ref, kk_ref, *rest):
    if final:
        ob_ref, bonus_ref, gate_ref, lng_ref, lnb_ref, y_ref, h_ref = rest
    else:
        o_ref, h_ref = rest

    @pl.when(pl.program_id(2) == 0)
    def _():
        h_ref[...] = jnp.zeros_like(h_ref)

    n2 = 2 * CHUNK_C
    lane = lax.broadcasted_iota(jnp.int32, (1, LANE), 1)
    head0 = lane < NC
    ri = lax.broadcasted_iota(jnp.int32, (n2, n2), 0)
    ci = lax.broadcasted_iota(jnp.int32, (n2, n2), 1)
    strict = (ci > ri) if rev else (ci < ri)
    incl = (ci >= ri) if rev else (ci <= ri)
    eye = ri == ci
    nchunk = tt // CHUNK_C

    def stack(z):
        zero = jnp.zeros_like(z)
        return jnp.concatenate([jnp.where(head0, z, zero), jnp.where(head0, zero, z)], axis=0)

    def stack_b(z):
        return stack(z.astype(BF16))

    order = [(nchunk - 1 - i) if rev else i for i in range(nchunk)]
    items = [(pl.ds(c * CHUNK_C, CHUNK_C), pl.ds(p * LANE, LANE), p) for c in order for p in range(PAIRS_C)]
    rows = [it[0] for it in items]
    cols = [it[1] for it in items]

    lws = _each(lambda sl, cl: lw_ref[0, sl, cl], rows, cols)
    g = _each(lambda lw: _cumsum_rows(lw, rev), lws)
    e_end = _each(lambda gu: jnp.exp(gu[0:1, :] if rev else gu[CHUNK_C - 1:CHUNK_C, :]), g)
    e_neg = _each(lambda gu: jnp.exp(-gu), g)
    bs = _each(lambda sl, cl, gu, lw: stack_b(-_f32(kk_ref[0, sl, cl]) * jnp.exp(gu - lw)), rows, cols, g, lws)
    a_sc = _each(lambda sl, cl, en: _f32(ka_ref[0, sl, cl]) * en, rows, cols, e_neg)
    k_sc = _each(lambda sl, cl, en: _f32(k_ref[0, sl, cl]) * en, rows, cols, e_neg)
    as_ = _each(stack_b, a_sc)
    ks = _each(stack_b, k_sc)
    ake = _each(lambda a, k, e: jnp.concatenate([stack_b(a * e), stack_b(k * e)], axis=0), a_sc, k_sc, e_end)
    rs = _each(lambda sl, cl, gu: stack(_f32(r_ref[0, sl, cl]) * jnp.exp(gu)), rows, cols, g)
    vs = _each(lambda sl, cl: stack_b(v_ref[0, sl, cl]), rows, cols)

    ak = _each(lambda a, k: jnp.concatenate([a, k], axis=0), as_, ks)
    sc_b = _each(_mm_nt, bs, ak)
    sc_r = _each(_mm_nt, rs, ak)
    a_ab = _each(lambda z: jnp.where(strict, z[:, :n2], 0.0), sc_b)
    a_ak = _each(lambda z: jnp.where(strict, z[:, n2:], 0.0), sc_b)
    m_rak = _each(lambda z: jnp.where(jnp.concatenate([incl, incl], axis=1), z, 0.0), sc_r)
    tinv = _each(lambda a: jnp.where(eye, 1.0, 0.0) + a, a_ab)
    pw = _each(lambda a: _mm(a, a), a_ab)
    akv = _each(_mm, a_ak, vs)
    nsteps = int(np.log2(CHUNK_C)) - 1
    for i in range(nsteps):
        if i < nsteps - 1:
            both = _each(lambda t, p: _mm(jnp.concatenate([t, p], axis=0), p), tinv, pw)
            tinv = _each(lambda t, b: t + b[:n2], tinv, both)
            pw = _each(lambda b: b[n2:], both)
        else:
            tinv = _each(lambda t, p: t + _mm(t, p), tinv, pw)
    pwv = _each(lambda t, b, x: _mm(t, jnp.concatenate([b, x.astype(BF16)], axis=1)), tinv, bs, akv)
    low = _each(lambda z, v: jnp.concatenate([z.astype(BF16), jnp.concatenate([jnp.zeros_like(v), v], axis=1)],
                                             axis=0), pwv, vs)
    x1 = _each(_mm, m_rak, low)
    x2 = _each(_mm_tn, ake, low)
    top = _each(lambda r, z, e, y: jnp.concatenate([r + z[:, :LANE], jnp.where(eye, e, 0.0) + y[:, :LANE]], axis=0),
                rs, x1, e_end, x2)
    o_in = _each(lambda z: z[:, LANE:], x1)
    dmat = _each(lambda y: y[:, LANE:], x2)

    outs = []
    h = [h_ref[p] for p in range(PAIRS_C)]
    for (sl, cl, p), tp, oi, dm in zip(items, top, o_in, dmat):
        both = _mm(tp, h[p])
        h[p] = both[n2:] + dm
        os_ = both[:n2] + oi
        outs.append((sl, cl, os_[:CHUNK_C, :] + os_[CHUNK_C:, :]))
    for p in range(PAIRS_C):
        h_ref[p] = h[p]

    for sl, cl, o in outs:
        if final:
            o = o + ob_ref[0, sl, cl]
            inv_n = 1.0 / NC
            s0 = jnp.sum(jnp.where(head0, o, 0.0), axis=-1, keepdims=True)
            s1 = jnp.sum(jnp.where(head0, 0.0, o), axis=-1, keepdims=True)
            cen = o - jnp.where(head0, s0, s1) * inv_n
            c2 = cen * cen
            v0 = jnp.sum(jnp.where(head0, c2, 0.0), axis=-1, keepdims=True)
            v1 = jnp.sum(jnp.where(head0, 0.0, c2), axis=-1, keepdims=True)
            var = jnp.where(head0, v0, v1) * inv_n
            y = cen * lax.rsqrt(var + RWKV_GN_EPS) * lng_ref[:, cl] + lnb_ref[:, cl] + bonus_ref[0, sl, cl]
            y_ref[0, sl, cl] = (y * _silu(_f32(gate_ref[0, sl, cl]))).astype(y_ref.dtype)
        else:
            o_ref[0, sl, cl] = o


def _rwkv_mixer(proj3, mu, w0, w_lora_b, a0, a_lora_b, k_k, k_a, r_k, ln_g, ln_b):
    b, t, _ = proj3.shape
    tt = _pick(t, 256)
    nt = t // tt
    sub = tt // HALO

    def cur(off, width):
        return pl.BlockSpec((1, tt, width), lambda bi, ti: (bi, ti, off // width))

    def prev(off, width):
        return pl.BlockSpec((1, HALO, width),
                            lambda bi, ti: (bi, jnp.maximum(ti * sub - 1, 0), off // width))

    def nxt(off, width):
        return pl.BlockSpec((1, HALO, width),
                            lambda bi, ti: (bi, jnp.minimum((ti + 1) * sub, nt * sub - 1), off // width))

    def full(shape):
        return pl.BlockSpec(shape, lambda bi, ti: (0,) * len(shape))

    segs = ((OFF_CR, WC), (OFF_CK, WC), (OFF_CV, WC), (OFF_CC, N_CODES))
    in_specs = ([cur(o, w) for o, w in segs] + [prev(o, w) for o, w in segs] + [nxt(o, w) for o, w in segs]
                + [full((2, WC))] * 3 + [full((2, N_CODES))]
                + [full((2, WC)), full((2, N_CODES, WC)), full((2, WC)), full((2, N_CODES, WC))]
                + [full((1, WC))] * 3 + [full((LANE, LANE))])
    wl = jnp.zeros((2, N_CODES, WC), F32)
    al = jnp.zeros((2, N_CODES, WC), F32)
    for d in range(2):
        wl = wl.at[d, d * W_LORA:(d + 1) * W_LORA].set(w_lora_b[d])
        al = al.at[d, 2 * W_LORA + d * A_LORA:2 * W_LORA + (d + 1) * A_LORA].set(a_lora_b[d])
    lane_i = np.arange(LANE)
    gmat = jnp.asarray((lane_i[:, None] // NC == lane_i[None, :] // NC).astype(np.float32), dtype=BF16)
    row_out = pl.BlockSpec((1, tt, WC), lambda bi, ti: (bi, ti, 0))
    outs = pl.pallas_call(
        functools.partial(_rwkv_prep_kernel, tt),
        grid=(b, nt),
        in_specs=in_specs,
        out_specs=[row_out] * 10,
        out_shape=[jax.ShapeDtypeStruct((b, t, WC), dt) for dt in (SCAN_DTYPE,) * 3 + (F32,) * 3 + (SCAN_DTYPE,) * 4],
        compiler_params=_params("parallel", "parallel"),
    )(*([proj3] * 12), mu[:, :WC], mu[:, WC:2 * WC], mu[:, 2 * WC:3 * WC], mu[:, 3 * WC:],
      w0, wl.astype(BF16), a0, al.astype(BF16), k_k.reshape(1, WC), k_a.reshape(1, WC),
      r_k.reshape(1, WC), gmat)
    r, v, kk, bonus, lw0, lw1, k0, k1, ka0, ka1 = outs

    ts = _pick(t, 512)
    ns = t // ts

    def scan(rev, final, lw, kd, ka, ob=None):
        wblk = PAIRS_C * LANE
        tmap = (lambda i: ns - 1 - i) if rev else (lambda i: i)
        blk = pl.BlockSpec((1, ts, wblk), lambda bi, h, ti: (bi, tmap(ti), h))
        head_vec = pl.BlockSpec((1, wblk), lambda bi, h, ti: (0, h))
        in_specs = [blk] * 6
        args = [r, lw, kd, v, ka, kk]
        if final:
            gate = pl.BlockSpec((1, ts, wblk), lambda bi, h, ti: (bi, tmap(ti), OFF_CG // wblk + h))
            in_specs += [blk, blk, gate, head_vec, head_vec]
            args += [ob, bonus, proj3, ln_g.reshape(1, WC), ln_b.reshape(1, WC)]
        return pl.pallas_call(
            functools.partial(_rwkv_scan_kernel, rev, final, ts),
            grid=(b, WC // wblk, ns),
            in_specs=in_specs,
            out_specs=blk,
            out_shape=jax.ShapeDtypeStruct((b, t, WC), BF16 if final else F32),
            scratch_shapes=[pltpu.VMEM((PAIRS_C, LANE, LANE), F32)],
            compiler_params=_params("parallel", "parallel", "arbitrary"),
        )(*args)

    ob = scan(True, False, lw1, k1, ka1)
    return scan(False, True, lw0, k0, ka0, ob=ob)


def _merge_kernel(final, ya_ref, yb_ref, yc_ref, mg_ref, x_ref, wa_ref, wb_ref, wc_ref, wo_ref,
                  fg_ref, o_ref):
    za = jnp.dot(ya_ref[...], wa_ref[...], preferred_element_type=F32)
    zb = jnp.dot(yb_ref[...], wb_ref[...], preferred_element_type=F32)
    zc = jnp.dot(yc_ref[...], wc_ref[...], preferred_element_type=F32)
    mg = _f32(mg_ref[...])
    mixed = (_sigmoid(mg[:, :D_MODEL]) * za
             + _sigmoid(mg[:, D_MODEL:2 * D_MODEL]) * zb
             + _sigmoid(mg[:, 2 * D_MODEL:]) * zc)
    out = x_ref[...] + jnp.dot(mixed.astype(BF16), wo_ref[...], preferred_element_type=F32)
    if final:
        ms = jnp.mean(out * out, axis=-1, keepdims=True)
        out = out * lax.rsqrt(ms + EPS) * fg_ref[...]
    o_ref[...] = out


def _merge(ya, yb, yc, proj, x2, wa, wb, wc, wo, fg, final):
    m = x2.shape[0]
    tm = _pick(m, 256)

    def rows(width):
        return pl.BlockSpec((tm, width), lambda i: (i, 0))

    def full(shape):
        return pl.BlockSpec(shape, lambda i: (0, 0))

    return pl.pallas_call(
        functools.partial(_merge_kernel, final),
        grid=(m // tm,),
        in_specs=[rows(WA), rows(WB_V), rows(WC), rows(3 * D_MODEL), rows(D_MODEL),
                  full((WA, D_MODEL)), full((WB_V, D_MODEL)), full((WC, D_MODEL)),
                  full((D_MODEL, D_MODEL)), full((1, D_MODEL))],
        out_specs=rows(D_MODEL),
        out_shape=jax.ShapeDtypeStruct((m, D_MODEL), F32),
        compiler_params=_params("parallel"),
    )(ya, yb, yc, proj, x2, wa.astype(BF16), wb.astype(BF16), wc.astype(BF16), wo.astype(BF16),
      fg.reshape(1, D_MODEL))


def kernel(x, norm_g, w_in, hgrn_lb_logits, hgrn_norm_g, ret_norm_g, ret_norm_b, rwkv_mu, rwkv_w0,
           rwkv_w_lora_b, rwkv_a0, rwkv_a_lora_b, rwkv_k_k, rwkv_k_a, rwkv_r_k, rwkv_ln_g,
           rwkv_ln_b, w_branch_a, w_branch_b, w_branch_c, w_out, final_norm_g):
    b, t, d = x.shape
    m = b * t
    depth = w_in.shape[0]
    p_lb = jax.nn.softmax(hgrn_lb_logits.astype(F32), axis=0)
    lbs = jnp.cumsum(p_lb, axis=0) - p_lb[0:1]
    x2 = x.reshape(m, d)
    for l in range(depth):
        proj = _inproj(x2, norm_g[l], _permute_w_in(w_in, l))
        proj3 = proj.reshape(b, t, N_PAD)
        ya = _hgrn_mixer(proj3, lbs[l], hgrn_norm_g[l])
        yb = _ret_mixer(proj3, ret_norm_g[l], ret_norm_b[l])
        yc = _rwkv_mixer(proj3, rwkv_mu[l], rwkv_w0[l], rwkv_w_lora_b[l], rwkv_a0[l],
                         rwkv_a_lora_b[l], rwkv_k_k[l], rwkv_k_a[l], rwkv_r_k[l],
                         rwkv_ln_g[l], rwkv_ln_b[l])
        x2 = _merge(ya.reshape(m, WA), yb.reshape(m, WB_V), yc.reshape(m, WC), proj, x2,
                    w_branch_a[l], w_branch_b[l], w_branch_c[l], w_out[l], final_norm_g,
                    final=(l == depth - 1))
    return x2.reshape(b, t, d)
```

```python
import functools

import numpy as np
import jax
import jax.numpy as jnp
from jax import lax
from jax.experimental import pallas as pl
from jax.experimental.pallas import tpu as pltpu

F32 = jnp.float32
BF16 = jnp.bfloat16

D_MODEL = 1024
EPS = 1e-6
F_MIN = 1e-6
DKA = 128
HA = D_MODEL // DKA
WA = HA * DKA
HB = 8
DKB = D_MODEL // HB
DVB = 2 * DKB
WB_QK = HB * DKB
WB_V = HB * DVB
ROPE_BASE = 10000.0
RET_GN_EPS = 1e-5
NC = 64
HC = D_MODEL // NC
WC = HC * NC
W_LORA = 64
A_LORA = 64
RWKV_GN_EPS = 64e-5
N_CODES = 2 * W_LORA + 2 * A_LORA
C_SHIFT = 3 * WC + N_CODES

SRC_A = 0
SRC_B = 5 * WA
SRC_CS = SRC_B + 2 * WB_QK + 2 * WB_V
SRC_CG = SRC_CS + C_SHIFT
SRC_MERGE = SRC_CG + WC
N_IN = SRC_MERGE + 3 * D_MODEL

OFF_MERGE = 0
OFF_AQ = OFF_MERGE + 3 * D_MODEL
OFF_AFF = OFF_AQ + WA
OFF_AFB = OFF_AFF + WA
OFF_AI = OFF_AFB + WA
OFF_AG = OFF_AI + WA
OFF_BQ = OFF_AG + WA
OFF_BK = OFF_BQ + WB_QK
OFF_BV = OFF_BK + WB_QK
OFF_BG = OFF_BV + WB_V
OFF_CG = OFF_BG + WB_V
OFF_CR = OFF_CG + WC
OFF_CK = OFF_CR + WC
OFF_CV = OFF_CK + WC
OFF_CC = OFF_CV + WC
LANE = 128
PROJ_DTYPE = BF16
SCAN_DTYPE = BF16
HALO = 16
PROJ_SPLIT = 4
N_PAD = -(-(OFF_CC + N_CODES) // (PROJ_SPLIT * LANE)) * PROJ_SPLIT * LANE
PROJ_TN = N_PAD // PROJ_SPLIT
MXU_N = 256
WPERM_TN = 256

CHUNK_A = 128
GROUP_A = 8
HEADS_A = 2
CHUNK_B = 256
CHUNK_C = 64
PAIRS_C = 4
VMEM_LIMIT = 48 * 1024 * 1024
PROJ_VMEM_LIMIT = 56 * 1024 * 1024

NT_DIMS = (((1,), (1,)), ((), ()))
TN_DIMS = (((0,), (0,)), ((), ()))


def _mm(a, b):
    return jnp.dot(a.astype(BF16), b.astype(BF16), preferred_element_type=F32)


def _mm_nt(a, b):
    return lax.dot_general(a.astype(BF16), b.astype(BF16), NT_DIMS, preferred_element_type=F32)


def _mm_tn(a, b):
    return lax.dot_general(a.astype(BF16), b.astype(BF16), TN_DIMS, preferred_element_type=F32)


def _split_dot(ones_mat, x):
    hi = x.astype(BF16)
    lo = (x - hi.astype(F32)).astype(BF16)
    return jnp.dot(ones_mat, hi, preferred_element_type=F32) + jnp.dot(ones_mat, lo, preferred_element_type=F32)


def _split_dot_right(x, ones_mat):
    hi = x.astype(BF16)
    lo = (x - hi.astype(F32)).astype(BF16)
    return jnp.dot(hi, ones_mat, preferred_element_type=F32) + jnp.dot(lo, ones_mat, preferred_element_type=F32)


def _f32(z):
    return z.astype(F32)


def _sigmoid(z):
    return 0.5 * jnp.tanh(0.5 * z) + 0.5


def _cumsum_rows(x, rev):
    n = x.shape[0]
    rows = lax.broadcasted_iota(jnp.int32, (n, 1), 0)
    step = 1
    while step < n:
        if rev:
            x = x + jnp.where(rows < n - step, pltpu.roll(x, n - step, axis=0), 0.0)
        else:
            x = x + jnp.where(rows >= step, pltpu.roll(x, step, axis=0), 0.0)
        step *= 2
    return x


def _silu(z):
    return z * _sigmoid(z)


def _pick(n, pref):
    t = min(n, pref)
    assert n % t == 0, (n, pref)
    return t


def _params(*sem, vmem=VMEM_LIMIT):
    return pltpu.CompilerParams(dimension_semantics=sem, vmem_limit_bytes=vmem)


def _inproj_kernel(x_ref, g_ref, w_ref, o_ref, h_ref):
    @pl.when(pl.program_id(1) == 0)
    def _():
        x = x_ref[...]
        ms = jnp.mean(x * x, axis=-1, keepdims=True)
        h_ref[...] = (x * lax.rsqrt(ms + EPS) * g_ref[...]).astype(BF16)

    def tile(cols):
        o_ref[:, cols] = jnp.dot(h_ref[...], w_ref[:, cols], preferred_element_type=F32).astype(o_ref.dtype)

    for start in range(0, PROJ_TN, MXU_N):
        tile(pl.ds(start, min(MXU_N, PROJ_TN - start)))


def _inproj(x2, g, w_bf16):
    m = x2.shape[0]
    tm = _pick(m, 1024)
    return pl.pallas_call(
        _inproj_kernel,
        grid=(m // tm, N_PAD // PROJ_TN),
        in_specs=[
            pl.BlockSpec((tm, D_MODEL), lambda i, j: (i, 0)),
            pl.BlockSpec((1, D_MODEL), lambda i, j: (0, 0)),
            pl.BlockSpec((D_MODEL, PROJ_TN), lambda i, j: (0, j)),
        ],
        out_specs=pl.BlockSpec((tm, PROJ_TN), lambda i, j: (i, j)),
        out_shape=jax.ShapeDtypeStruct((m, N_PAD), PROJ_DTYPE),
        scratch_shapes=[pltpu.VMEM((tm, D_MODEL), BF16)],
        compiler_params=_params("parallel", "arbitrary", vmem=PROJ_VMEM_LIMIT),
    )(x2, g.reshape(1, D_MODEL), w_bf16)


def _wperm_kernel(nvalid, a_ref, b_ref, o_ref):
    j = pl.program_id(0)
    for idx, src in enumerate((a_ref, b_ref)):
        cols = pl.ds(idx * WPERM_TN, WPERM_TN)
        blk = src[...].astype(BF16)
        o_ref[:, cols] = jnp.where(2 * j + idx < nvalid, blk, jnp.zeros_like(blk))


def _src_block(jd):
    n_merge = (N_IN - SRC_MERGE) // WPERM_TN
    n_ab = SRC_CS // WPERM_TN
    n_cg = (SRC_MERGE - SRC_CG) // WPERM_TN
    n_src = N_IN // WPERM_TN
    return jnp.where(jd < n_merge, jd + SRC_MERGE // WPERM_TN,
                     jnp.where(jd < n_merge + n_ab, jd - n_merge,
                               jnp.where(jd < n_merge + n_ab + n_cg, jd - n_merge - n_ab + SRC_CG // WPERM_TN,
                                         jnp.minimum(jd - n_merge - n_cg, n_src - 1))))


def _permute_w_in(w_in, layer):
    assert all(off % WPERM_TN == 0 for off in (SRC_CS, SRC_CG, SRC_MERGE, N_IN)) and N_PAD % (2 * WPERM_TN) == 0

    def src(idx):
        return pl.BlockSpec((None, D_MODEL, WPERM_TN), lambda j: (layer, 0, _src_block(2 * j + idx)))

    return pl.pallas_call(
        functools.partial(_wperm_kernel, N_IN // WPERM_TN),
        grid=(N_PAD // (2 * WPERM_TN),),
        in_specs=[src(0), src(1)],
        out_specs=pl.BlockSpec((D_MODEL, 2 * WPERM_TN), lambda j: (0, j)),
        out_shape=jax.ShapeDtypeStruct((D_MODEL, N_PAD), BF16),
        compiler_params=_params("parallel"),
    )(w_in, w_in)


def _each(fn, *lists):
    return [fn(*items) for items in zip(*lists)]


def _hgrn_kernel(rev, final, tt, q_ref, f_ref, v_ref, lb_ref, tri_ref, *rest):
    if final:
        ob_ref, gate_ref, ng_ref, y_ref, st_ref = rest
    else:
        o_ref, st_ref = rest

    @pl.when(pl.program_id(2) == 0)
    def _():
        st_ref[...] = jnp.zeros_like(st_ref)

    n = CHUNK_A
    ri = lax.broadcasted_iota(jnp.int32, (n, n), 0)
    ci = lax.broadcasted_iota(jnp.int32, (n, n), 1)
    g_bits = int(np.log2(GROUP_A))
    diag_mask = ((ri >> g_bits) == (ci >> g_bits)) & ((ci >= ri) if rev else (ci <= ri))
    halves = [GROUP_A << i for i in range(int(np.log2(n // GROUP_A)))]
    level_masks = []
    for half in halves:
        hb = int(np.log2(half))
        q_side, k_side = (0, 1) if rev else (1, 0)
        level_masks.append(((ri >> (hb + 1)) == (ci >> (hb + 1)))
                           & (((ri >> hb) & 1) == q_side) & (((ci >> hb) & 1) == k_side))

    def ref_rows(b, blk, idx):
        picked = b.reshape(n // blk, blk, DKA)[:, idx:idx + 1, :]
        return jnp.broadcast_to(picked, (n // blk, blk, DKA)).reshape(n, DKA)

    nchunk = tt // n
    order = [(nchunk - 1 - i) if rev else i for i in range(nchunk)]
    items = [(pl.ds(c * n, n), pl.ds(hh * DKA, DKA), hh) for c in order for hh in range(HEADS_A)]
    sls = [it[0] for it in items]
    cls = [it[1] for it in items]
    lbs = [lb_ref[it[2]] for it in items]
    sig = _each(lambda sl, cl: _sigmoid(_f32(f_ref[0, sl, cl])), sls, cls)
    lf = _each(lambda z, lb: jnp.log(jnp.maximum(lb + (1.0 - lb) * z, F_MIN)), sig, lbs)
    k = _each(lambda z, lb: (1.0 - lb) * (1.0 - z), sig, lbs)
    q = _each(lambda sl, cl: _silu(_f32(q_ref[0, sl, cl])), sls, cls)
    v = _each(lambda sl, cl: v_ref[0, sl, cl], sls, cls)
    b = _each(lambda z: _split_dot(tri_ref[...], z), lf)
    d0 = _each(lambda z: z - ref_rows(z, GROUP_A, GROUP_A // 2 if rev else GROUP_A // 2 - 1), b)
    qb = _each(lambda z: z.astype(BF16), q)
    kb = _each(lambda z: z.astype(BF16), k)
    scores = _each(lambda qq, kk, d: jnp.where(
        diag_mask, _mm_nt(qq * jnp.exp(d).astype(BF16), kk * jnp.exp(-d).astype(BF16)), 0.0), qb, kb, d0)
    for half, mask in zip(halves, level_masks):
        e = _each(lambda z: jnp.exp(-jnp.abs(z - ref_rows(z, 2 * half, half if rev else half - 1))).astype(BF16),
                  b)
        scores = _each(lambda sc, qq, kk, ee: jnp.where(mask, _mm_nt(qq * ee, kk * ee), sc), scores, qb, kb, e)
    o_intra = _each(_mm, scores, v)
    b_end = _each(lambda z: z[0:1, :] if rev else z[n - 1:n, :], b)
    q_in = _each(lambda qq, z: qq * jnp.exp(z), q, b)
    kv = _each(lambda vv, kk, z, ze: _mm_tn(vv, kk * jnp.exp(ze - z)), v, k, b, b_end)
    e_end = _each(jnp.exp, b_end)

    outs = []
    st = [st_ref[hh] for hh in range(HEADS_A)]
    for (sl, cl, hh), oi, qi, kvi, ee in zip(items, o_intra, q_in, kv, e_end):
        outs.append((sl, cl, hh, oi + _mm_nt(qi, st[hh])))
        st[hh] = st[hh] * ee + kvi
    for hh in range(HEADS_A):
        st_ref[hh] = st[hh]

    for sl, cl, hh, o in outs:
        if final:
            o = o + ob_ref[0, sl, cl]
            ms = jnp.mean(o * o, axis=-1, keepdims=True)
            y = o * lax.rsqrt(ms + EPS) * ng_ref[hh] * _silu(_f32(gate_ref[0, sl, cl]))
            y_ref[0, sl, cl] = y.astype(y_ref.dtype)
        else:
            o_ref[0, sl, cl] = o


def _tri_blockdiag(n, blk, rev):
    i = np.arange(n)
    same = (i[:, None] // blk) == (i[None, :] // blk)
    tri = (i[None, :] >= i[:, None]) if rev else (i[None, :] <= i[:, None])
    return jnp.asarray((same & tri).astype(np.float32), dtype=BF16)


def _hgrn_pass(proj3, lb, rev, final, ob=None, norm_g=None):
    b, t, _ = proj3.shape
    tt = _pick(t, 1024)
    nt = t // tt
    tmap = (lambda i: nt - 1 - i) if rev else (lambda i: i)

    wblk = HEADS_A * DKA

    def col(off):
        return pl.BlockSpec((1, tt, wblk), lambda bi, h, ti: (bi, tmap(ti), off // wblk + h))

    head_vec = pl.BlockSpec((HEADS_A, 1, DKA), lambda bi, h, ti: (h, 0, 0))
    in_specs = [col(OFF_AQ), col(OFF_AFB if rev else OFF_AFF), col(OFF_AI), head_vec,
                pl.BlockSpec((LANE, LANE), lambda bi, h, ti: (0, 0))]
    args = [proj3, proj3, proj3, lb.reshape(HA, 1, DKA), _tri_blockdiag(LANE, CHUNK_A, rev)]
    out_block = pl.BlockSpec((1, tt, wblk), lambda bi, h, ti: (bi, tmap(ti), h))
    if final:
        in_specs += [out_block, col(OFF_AG), head_vec]
        args += [ob, proj3, norm_g.reshape(HA, 1, DKA)]
        out_dtype = BF16
    else:
        out_dtype = F32
    return pl.pallas_call(
        functools.partial(_hgrn_kernel, rev, final, tt),
        grid=(b, HA // HEADS_A, nt),
        in_specs=in_specs,
        out_specs=out_block,
        out_shape=jax.ShapeDtypeStruct((b, t, WA), out_dtype),
        scratch_shapes=[pltpu.VMEM((HEADS_A, DKA, DKA), F32)],
        compiler_params=_params("parallel", "parallel", "arbitrary"),
    )(*args)


def _hgrn_mixer(proj3, lbs, norm_g):
    ob = _hgrn_pass(proj3, lbs[1], rev=True, final=False)
    return _hgrn_pass(proj3, lbs[0], rev=False, final=True, ob=ob, norm_g=norm_g)


def _rotary(z, cos2, sin2):
    return z * cos2 + pltpu.roll(z, DKB // 2, axis=1) * sin2


def _ret_state_kernel(nsub, k_ref, v_ref, cos_ref, sin_ref, lg_ref, sb_ref, st_ref):
    @pl.when(pl.program_id(2) == 0)
    def _():
        st_ref[...] = jnp.zeros_like(st_ref)

    cb = CHUNK_B
    lg2 = lg_ref[0]
    lg = lg2[:, :DKB]
    pos = lax.broadcasted_iota(jnp.int32, (cb, DKB), 0).astype(F32)
    dec = jnp.exp(pos * lg) * (DKB ** -0.5)
    order = list(range(nsub - 1, -1, -1))
    sls = [pl.ds(c * cb, cb) for c in order]
    kd = _each(lambda sl: _rotary(_f32(k_ref[0, sl, :]), cos_ref[sl, :], sin_ref[sl, :]) * dec, sls)
    kv = _each(lambda kk, sl: _mm_tn(kk, v_ref[0, sl, :]), kd, sls)
    e_chunk = jnp.exp(cb * lg2)
    st = st_ref[...]
    for c, kvi in zip(order, kv):
        sb_ref[0, 0, c] = st
        st = st * e_chunk + kvi
    st_ref[...] = st


def _ret_out_kernel(nsub, q_ref, k_ref, v_ref, cos_ref, sin_ref, lg_ref, sb_ref, gate_ref, gg_ref,
                    gb_ref, y_ref, st_ref):
    @pl.when(pl.program_id(2) == 0)
    def _():
        st_ref[...] = jnp.zeros_like(st_ref)

    cb = CHUNK_B
    lg2 = lg_ref[0]
    lg = lg2[:, :DKB]
    pos = lax.broadcasted_iota(jnp.int32, (cb, DKB), 0).astype(F32)
    ri = lax.broadcasted_iota(jnp.int32, (cb, cb), 0)
    ci = lax.broadcasted_iota(jnp.int32, (cb, cb), 1)
    decay = jnp.exp(jnp.abs(ri - ci).astype(F32) * lg2)
    dec_fwd = jnp.exp((pos + 1.0) * lg)
    dec_bwd = jnp.exp((cb - pos) * lg)
    dec_key = jnp.exp((cb - 1.0 - pos) * lg)
    e_chunk = jnp.exp(cb * lg2)

    sls = [pl.ds(c * cb, cb) for c in range(nsub)]
    q = _each(lambda sl: _rotary(_f32(q_ref[0, sl, :]), cos_ref[sl, :], sin_ref[sl, :]), sls)
    k = _each(lambda sl: _rotary(_f32(k_ref[0, sl, :]), cos_ref[sl, :], sin_ref[sl, :]) * (DKB ** -0.5), sls)
    v = _each(lambda sl: v_ref[0, sl, :], sls)
    scores = _each(lambda qq, kk: _mm_nt(qq, kk) * decay, q, k)
    o = _each(_mm, scores, v)
    o = _each(lambda oo, qq, c: oo + _mm(qq * dec_bwd, sb_ref[0, 0, c]), o, q, list(range(nsub)))
    kv = _each(lambda kk, vv: _mm_tn(kk * dec_key, vv), k, v)
    q_fwd = _each(lambda qq: qq * dec_fwd, q)
    st = st_ref[...]
    outs = []
    for oo, qf, kvi in zip(o, q_fwd, kv):
        outs.append(oo + _mm(qf, st))
        st = st * e_chunk + kvi
    st_ref[...] = st

    for sl, oo in zip(sls, outs):
        mu = jnp.mean(oo, axis=-1, keepdims=True)
        cen = oo - mu
        var = jnp.mean(cen * cen, axis=-1, keepdims=True)
        y = (cen * lax.rsqrt(var + RET_GN_EPS) * gg_ref[0] + gb_ref[0]) * _silu(_f32(gate_ref[0, sl, :]))
        y_ref[0, sl, :] = y.astype(y_ref.dtype)


def _ret_mixer(proj3, gn_g, gn_b):
    b, t, _ = proj3.shape
    cb = CHUNK_B
    tt = _pick(t, 8 * cb)
    nsub = tt // cb
    nt = t // tt
    half = DKB // 2
    inv = ROPE_BASE ** (-jnp.arange(0, DKB, 2, dtype=F32) / DKB)
    ang_hi = (jnp.arange(t // cb, dtype=F32) * cb)[:, None, None] * inv
    ang_lo = jnp.arange(cb, dtype=F32)[None, :, None] * inv
    cos = (jnp.cos(ang_hi) * jnp.cos(ang_lo) - jnp.sin(ang_hi) * jnp.sin(ang_lo)).reshape(t, half)
    sin = (jnp.sin(ang_hi) * jnp.cos(ang_lo) + jnp.cos(ang_hi) * jnp.sin(ang_lo)).reshape(t, half)
    cos2 = jnp.concatenate([cos, cos], axis=-1)
    sin2 = jnp.concatenate([-sin, sin], axis=-1)
    log_gamma = jnp.log1p(-jnp.exp2(-5.0 - jnp.arange(HB, dtype=F32)))
    lg = jnp.broadcast_to(log_gamma[:, None, None], (HB, 1, DVB))

    def col(off, width, tmap):
        return pl.BlockSpec((1, tt, width), lambda bi, h, c: (bi, tmap(c), off // width + h))

    def tab(tmap):
        return pl.BlockSpec((tt, DKB), lambda bi, h, c: (tmap(c), 0))

    lg_spec = pl.BlockSpec((1, 1, DVB), lambda bi, h, c: (h, 0, 0))
    rmap = lambda c: nt - 1 - c
    fmap = lambda c: c

    sb = pl.pallas_call(
        functools.partial(_ret_state_kernel, nsub),
        grid=(b, HB, nt),
        in_specs=[col(OFF_BK, DKB, rmap), col(OFF_BV, DVB, rmap), tab(rmap), tab(rmap), lg_spec],
        out_specs=pl.BlockSpec((1, 1, nsub, DKB, DVB), lambda bi, h, c: (bi, h, rmap(c), 0, 0)),
        out_shape=jax.ShapeDtypeStruct((b, HB, t // cb, DKB, DVB), F32),
        scratch_shapes=[pltpu.VMEM((DKB, DVB), F32)],
        compiler_params=_params("parallel", "parallel", "arbitrary"),
    )(proj3, proj3, cos2, sin2, lg)

    head_vec = pl.BlockSpec((1, 1, DVB), lambda bi, h, c: (h, 0, 0))
    return pl.pallas_call(
        functools.partial(_ret_out_kernel, nsub),
        grid=(b, HB, nt),
        in_specs=[col(OFF_BQ, DKB, fmap), col(OFF_BK, DKB, fmap), col(OFF_BV, DVB, fmap),
                  tab(fmap), tab(fmap), lg_spec,
                  pl.BlockSpec((1, 1, nsub, DKB, DVB), lambda bi, h, c: (bi, h, c, 0, 0)),
                  col(OFF_BG, DVB, fmap), head_vec, head_vec],
        out_specs=pl.BlockSpec((1, tt, DVB), lambda bi, h, c: (bi, c, h)),
        out_shape=jax.ShapeDtypeStruct((b, t, WB_V), BF16),
        scratch_shapes=[pltpu.VMEM((DKB, DVB), F32)],
        compiler_params=_params("parallel", "parallel", "arbitrary"),
    )(proj3, proj3, proj3, cos2, sin2, lg, sb, proj3,
      gn_g.reshape(HB, 1, DVB), gn_b.reshape(HB, 1, DVB))


def _group_sum(x, gmat):
    cols = [_split_dot_right(x[:, i * LANE:(i + 1) * LANE], gmat) for i in range(x.shape[1] // LANE)]
    return jnp.concatenate(cols, axis=1)


def _rwkv_prep_kernel(tt, r_ref, k_ref, v_ref, c_ref, rp_ref, kp_ref, vp_ref, cp_ref,
                      rn_ref, kn_ref, vn_ref, cn_ref, mur_ref, muk_ref, muv_ref, muc_ref,
                      w0_ref, wl_ref, a0_ref, al_ref, kk_ref, ka_ref, rk_ref, gmat_ref, shift_ref,
                      r_o, v_o, kk_o, bonus_o, lw0_o, lw1_o, k0_o, k1_o, ka0_o, ka1_o):
    ti = pl.program_id(1)
    has_prev = (ti > 0).astype(F32)
    has_next = (ti < pl.num_programs(1) - 1).astype(F32)
    rows = lax.broadcasted_iota(jnp.int32, (tt, 1), 0)

    def shifted(p_ref, prev_ref, next_ref, mu_ref):
        both = jnp.dot(shift_ref[...], p_ref[0].astype(BF16), preferred_element_type=F32)
        p = _f32(p_ref[0])
        prev_row = _f32(prev_ref[0, HALO - 1:HALO, :]) * has_prev
        next_row = _f32(next_ref[0, 0:1, :]) * has_next
        before = jnp.where(rows == 0, prev_row, both[:tt])
        after = jnp.where(rows == tt - 1, next_row, both[tt:])
        return p + mu_ref[0:1, :] * (before - p) + mu_ref[1:2, :] * (after - p)

    r = shifted(r_ref, rp_ref, rn_ref, mur_ref)
    k = shifted(k_ref, kp_ref, kn_ref, muk_ref)
    v = shifted(v_ref, vp_ref, vn_ref, muv_ref)
    codes = shifted(c_ref, cp_ref, cn_ref, muc_ref)
    gmat = gmat_ref[...]

    kraw = k * kk_ref[...]
    kk = kraw * lax.rsqrt(_group_sum(kraw * kraw, gmat) + 1e-12)
    r_o[0] = r.astype(r_o.dtype)
    v_o[0] = v.astype(v_o.dtype)
    kk_o[0] = kk.astype(kk_o.dtype)
    tcodes = jnp.tanh(codes)
    bonus = jnp.zeros_like(r)
    for d, (lw_o, kd_o, ka_o) in enumerate(((lw0_o, k0_o, ka0_o), (lw1_o, k1_o, ka1_o))):
        w = w0_ref[d:d + 1, :] + _mm(tcodes, wl_ref[d])
        lw_o[0] = -np.exp(-0.5).astype(np.float32) * _sigmoid(w)
        a = _sigmoid(a0_ref[d:d + 1, :] + _mm(codes, al_ref[d]))
        kd = k * (1.0 + (a - 1.0) * ka_ref[...])
        kd_o[0] = kd.astype(kd_o.dtype)
        ka_o[0] = (kk * a).astype(ka_o.dtype)
        bonus = bonus + _group_sum(r * kd * rk_ref[...], gmat) * v
    bonus_o[0] = bonus


def _rwkv_scan_kernel(rev, final, tt, r_ref, lw_ref, k_ref, v_ref, ka_ref, kk_ref, *rest):
    if final:
        ob_ref, bonus_ref, gate_ref, lng_ref, lnb_ref, y_ref, h_ref = rest
    else:
        o_ref, h_ref = rest

    @pl.when(pl.program_id(2) == 0)
    def _():
        h_ref[...] = jnp.zeros_like(h_ref)

    n2 = 2 * CHUNK_C
    lane = lax.broadcasted_iota(jnp.int32, (1, LANE), 1)
    head0 = lane < NC
    ri = lax.broadcasted_iota(jnp.int32, (n2, n2), 0)
    ci = lax.broadcasted_iota(jnp.int32, (n2, n2), 1)
    strict = (ci > ri) if rev else (ci < ri)
    incl = (ci >= ri) if rev else (ci <= ri)
    eye = ri == ci
    nchunk = tt // CHUNK_C

    def stack(z):
        zero = jnp.zeros_like(z)
        return jnp.concatenate([jnp.where(head0, z, zero), jnp.where(head0, zero, z)], axis=0)

    def stack_b(z):
        return stack(z.astype(BF16))

    order = [(nchunk - 1 - i) if rev else i for i in range(nchunk)]
    items = [(pl.ds(c * CHUNK_C, CHUNK_C), pl.ds(p * LANE, LANE), p) for c in order for p in range(PAIRS_C)]
    rows = [it[0] for it in items]
    cols = [it[1] for it in items]

    lws = _each(lambda sl, cl: lw_ref[0, sl, cl], rows, cols)
    g = _each(lambda lw: _cumsum_rows(lw, rev), lws)
    e_end = _each(lambda gu: jnp.exp(gu[0:1, :] if rev else gu[CHUNK_C - 1:CHUNK_C, :]), g)
    e_neg = _each(lambda gu: jnp.exp(-gu), g)
    bs = _each(lambda sl, cl, gu, lw: stack_b(-_f32(kk_ref[0, sl, cl]) * jnp.exp(gu - lw)), rows, cols, g, lws)
    a_sc = _each(lambda sl, cl, en: _f32(ka_ref[0, sl, cl]) * en, rows, cols, e_neg)
    k_sc = _each(lambda sl, cl, en: _f32(k_ref[0, sl, cl]) * en, rows, cols, e_neg)
    as_ = _each(stack_b, a_sc)
    ks = _each(stack_b, k_sc)
    ake = _each(lambda a, k, e: jnp.concatenate([stack_b(a * e), stack_b(k * e)], axis=0), a_sc, k_sc, e_end)
    rs = _each(lambda sl, cl, gu: stack(_f32(r_ref[0, sl, cl]) * jnp.exp(gu)), rows, cols, g)
    vs = _each(lambda sl, cl: stack_b(v_ref[0, sl, cl]), rows, cols)

    ak = _each(lambda a, k: jnp.concatenate([a, k], axis=0), as_, ks)
    sc_b = _each(_mm_nt, bs, ak)
    sc_r = _each(_mm_nt, rs, ak)
    a_ab = _each(lambda z: jnp.where(strict, z[:, :n2], 0.0), sc_b)
    a_ak = _each(lambda z: jnp.where(strict, z[:, n2:], 0.0), sc_b)
    m_rak = _each(lambda z: jnp.where(jnp.concatenate([incl, incl], axis=1), z, 0.0), sc_r)
    tinv = _each(lambda a: jnp.where(eye, 1.0, 0.0) + a, a_ab)
    pw = _each(lambda a: _mm(a, a), a_ab)
    akv = _each(_mm, a_ak, vs)
    nsteps = int(np.log2(CHUNK_C)) - 1
    for i in range(nsteps):
        if i < nsteps - 1:
            both = _each(lambda t, p: _mm(jnp.concatenate([t, p], axis=0), p), tinv, pw)
            tinv = _each(lambda t, b: t + b[:n2], tinv, both)
            pw = _each(lambda b: b[n2:], both)
        else:
            tinv = _each(lambda t, p: t + _mm(t, p), tinv, pw)
    pwv = _each(lambda t, b, x: _mm(t, jnp.concatenate([b, x.astype(BF16)], axis=1)), tinv, bs, akv)
    low = _each(lambda z, v: jnp.concatenate([z.astype(BF16), jnp.concatenate([jnp.zeros_like(v), v], axis=1)],
                                             axis=0), pwv, vs)
    x1 = _each(_mm, m_rak, low)
    x2 = _each(_mm_tn, ake, low)
    top = _each(lambda r, z, e, y: jnp.concatenate([r + z[:, :LANE], jnp.where(eye, e, 0.0) + y[:, :LANE]], axis=0),
                rs, x1, e_end, x2)
    o_in = _each(lambda z: z[:, LANE:], x1)
    dmat = _each(lambda y: y[:, LANE:], x2)

    outs = []
    h = [h_ref[p] for p in range(PAIRS_C)]
    for (sl, cl, p), tp, oi, dm in zip(items, top, o_in, dmat):
        both = _mm(tp, h[p])
        h[p] = both[n2:] + dm
        os_ = both[:n2] + oi
        outs.append((sl, cl, os_[:CHUNK_C, :] + os_[CHUNK_C:, :]))
    for p in range(PAIRS_C):
        h_ref[p] = h[p]

    for sl, cl, o in outs:
        if final:
            o = o + ob_ref[0, sl, cl]
            inv_n = 1.0 / NC
            s0 = jnp.sum(jnp.where(head0, o, 0.0), axis=-1, keepdims=True)
            s1 = jnp.sum(jnp.where(head0, 0.0, o), axis=-1, keepdims=True)
            cen = o - jnp.where(head0, s0, s1) * inv_n
            c2 = cen * cen
            v0 = jnp.sum(jnp.where(head0, c2, 0.0), axis=-1, keepdims=True)
            v1 = jnp.sum(jnp.where(head0, 0.0, c2), axis=-1, keepdims=True)
            var = jnp.where(head0, v0, v1) * inv_n
            y = cen * lax.rsqrt(var + RWKV_GN_EPS) * lng_ref[:, cl] + lnb_ref[:, cl] + bonus_ref[0, sl, cl]
            y_ref[0, sl, cl] = (y * _silu(_f32(gate_ref[0, sl, cl]))).astype(y_ref.dtype)
        else:
            o_ref[0, sl, cl] = o


def _rwkv_mixer(proj3, mu, w0, w_lora_b, a0, a_lora_b, k_k, k_a, r_k, ln_g, ln_b):
    b, t, _ = proj3.shape
    tt = _pick(t, 256)
    nt = t // tt
    sub = tt // HALO

    def cur(off, width):
        return pl.BlockSpec((1, tt, width), lambda bi, ti: (bi, ti, off // width))

    def prev(off, width):
        return pl.BlockSpec((1, HALO, width),
                            lambda bi, ti: (bi, jnp.maximum(ti * sub - 1, 0), off // width))

    def nxt(off, width):
        return pl.BlockSpec((1, HALO, width),
                            lambda bi, ti: (bi, jnp.minimum((ti + 1) * sub, nt * sub - 1), off // width))

    def full(shape):
        return pl.BlockSpec(shape, lambda bi, ti: (0,) * len(shape))

    segs = ((OFF_CR, WC), (OFF_CK, WC), (OFF_CV, WC), (OFF_CC, N_CODES))
    in_specs = ([cur(o, w) for o, w in segs] + [prev(o, w) for o, w in segs] + [nxt(o, w) for o, w in segs]
                + [full((2, WC))] * 3 + [full((2, N_CODES))]
                + [full((2, WC)), full((2, N_CODES, WC)), full((2, WC)), full((2, N_CODES, WC))]
                + [full((1, WC))] * 3 + [full((LANE, LANE)), full((2 * tt, tt))])
    wl = jnp.zeros((2, N_CODES, WC), F32)
    al = jnp.zeros((2, N_CODES, WC), F32)
    for d in range(2):
        wl = wl.at[d, d * W_LORA:(d + 1) * W_LORA].set(w_lora_b[d])
        al = al.at[d, 2 * W_LORA + d * A_LORA:2 * W_LORA + (d + 1) * A_LORA].set(a_lora_b[d])
    row_i = np.arange(tt)
    shift = np.concatenate([row_i[:, None] - 1 == row_i[None, :], row_i[:, None] + 1 == row_i[None, :]], axis=0)
    shift = jnp.asarray(shift.astype(np.float32), dtype=BF16)
    lane_i = np.arange(LANE)
    gmat = jnp.asarray((lane_i[:, None] // NC == lane_i[None, :] // NC).astype(np.float32), dtype=BF16)
    row_out = pl.BlockSpec((1, tt, WC), lambda bi, ti: (bi, ti, 0))
    outs = pl.pallas_call(
        functools.partial(_rwkv_prep_kernel, tt),
        grid=(b, nt),
        in_specs=in_specs,
        out_specs=[row_out] * 10,
        out_shape=[jax.ShapeDtypeStruct((b, t, WC), dt) for dt in (SCAN_DTYPE,) * 3 + (F32,) * 3 + (SCAN_DTYPE,) * 4],
        compiler_params=_params("parallel", "parallel"),
    )(*([proj3] * 12), mu[:, :WC], mu[:, WC:2 * WC], mu[:, 2 * WC:3 * WC], mu[:, 3 * WC:],
      w0, wl.astype(BF16), a0, al.astype(BF16), k_k.reshape(1, WC), k_a.reshape(1, WC),
      r_k.reshape(1, WC), gmat, shift)
    r, v, kk, bonus, lw0, lw1, k0, k1, ka0, ka1 = outs

    ts = _pick(t, 512)
    ns = t // ts

    def scan(rev, final, lw, kd, ka, ob=None):
        wblk = PAIRS_C * LANE
        tmap = (lambda i: ns - 1 - i) if rev else (lambda i: i)
        blk = pl.BlockSpec((1, ts, wblk), lambda bi, h, ti: (bi, tmap(ti), h))
        head_vec = pl.BlockSpec((1, wblk), lambda bi, h, ti: (0, h))
        in_specs = [blk] * 6
        args = [r, lw, kd, v, ka, kk]
        if final:
            gate = pl.BlockSpec((1, ts, wblk), lambda bi, h, ti: (bi, tmap(ti), OFF_CG // wblk + h))
            in_specs += [blk, blk, gate, head_vec, head_vec]
            args += [ob, bonus, proj3, ln_g.reshape(1, WC), ln_b.reshape(1, WC)]
        return pl.pallas_call(
            functools.partial(_rwkv_scan_kernel, rev, final, ts),
            grid=(b, WC // wblk, ns),
            in_specs=in_specs,
            out_specs=blk,
            out_shape=jax.ShapeDtypeStruct((b, t, WC), BF16 if final else F32),
            scratch_shapes=[pltpu.VMEM((PAIRS_C, LANE, LANE), F32)],
            compiler_params=_params("parallel", "parallel", "arbitrary"),
        )(*args)

    ob = scan(True, False, lw1, k1, ka1)
    return scan(False, True, lw0, k0, ka0, ob=ob)


def _merge_kernel(final, ya_ref, yb_ref, yc_ref, mg_ref, x_ref, wa_ref, wb_ref, wc_ref, wo_ref,
                  fg_ref, o_ref):
    za = jnp.dot(ya_ref[...], wa_ref[...], preferred_element_type=F32)
    zb = jnp.dot(yb_ref[...], wb_ref[...], preferred_element_type=F32)
    zc = jnp.dot(yc_ref[...], wc_ref[...], preferred_element_type=F32)
    mg = _f32(mg_ref[...])
    mixed = (_sigmoid(mg[:, :D_MODEL]) * za
             + _sigmoid(mg[:, D_MODEL:2 * D_MODEL]) * zb
             + _sigmoid(mg[:, 2 * D_MODEL:]) * zc)
    out = x_ref[...] + jnp.dot(mixed.astype(BF16), wo_ref[...], preferred_element_type=F32)
    if final:
        ms = jnp.mean(out * out, axis=-1, keepdims=True)
        out = out * lax.rsqrt(ms + EPS) * fg_ref[...]
    o_ref[...] = out


def _merge(ya, yb, yc, proj, x2, wa, wb, wc, wo, fg, final):
    m = x2.shape[0]
    tm = _pick(m, 256)

    def rows(width):
        return pl.BlockSpec((tm, width), lambda i: (i, 0))

    def full(shape):
        return pl.BlockSpec(shape, lambda i: (0, 0))

    return pl.pallas_call(
        functools.partial(_merge_kernel, final),
        grid=(m // tm,),
        in_specs=[rows(WA), rows(WB_V), rows(WC), rows(3 * D_MODEL), rows(D_MODEL),
                  full((WA, D_MODEL)), full((WB_V, D_MODEL)), full((WC, D_MODEL)),
                  full((D_MODEL, D_MODEL)), full((1, D_MODEL))],
        out_specs=rows(D_MODEL),
        out_shape=jax.ShapeDtypeStruct((m, D_MODEL), F32),
        compiler_params=_params("parallel"),
    )(ya, yb, yc, proj, x2, wa.astype(BF16), wb.astype(BF16), wc.astype(BF16), wo.astype(BF16),
      fg.reshape(1, D_MODEL))


def kernel(x, norm_g, w_in, hgrn_lb_logits, hgrn_norm_g, ret_norm_g, ret_norm_b, rwkv_mu, rwkv_w0,
           rwkv_w_lora_b, rwkv_a0, rwkv_a_lora_b, rwkv_k_k, rwkv_k_a, rwkv_r_k, rwkv_ln_g,
           rwkv_ln_b, w_branch_a, w_branch_b, w_branch_c, w_out, final_norm_g):
    b, t, d = x.shape
    m = b * t
    depth = w_in.shape[0]
    p_lb = jax.nn.softmax(hgrn_lb_logits.astype(F32), axis=0)
    lbs = jnp.cumsum(p_lb, axis=0) - p_lb[0:1]
    x2 = x.reshape(m, d)
    for l in range(depth):
        proj = _inproj(x2, norm_g[l], _permute_w_in(w_in, l))
        proj3 = proj.reshape(b, t, N_PAD)
        ya = _hgrn_mixer(proj3, lbs[l], hgrn_norm_g[l])
        yb = _ret_mixer(proj3, ret_norm_g[l], ret_norm_b[l])
        yc = _rwkv_mixer(proj3, rwkv_mu[l], rwkv_w0[l], rwkv_w_lora_b[l], rwkv_a0[l],
                         rwkv_a_lora_b[l], rwkv_k_k[l], rwkv_k_a[l], rwkv_r_k[l],
                         rwkv_ln_g[l], rwkv_ln_b[l])
        x2 = _merge(ya.reshape(m, WA), yb.reshape(m, WB_V), yc.reshape(m, WC), proj, x2,
                    w_branch_a[l], w_branch_b[l], w_branch_c[l], w_out[l], final_norm_g,
                    final=(l == depth - 1))
    return x2.reshape(b, t, d)
```

```python
import functools

import numpy as np
import jax
import jax.numpy as jnp
from jax import lax
from jax.experimental import pallas as pl
from jax.experimental.pallas import tpu as pltpu

F32 = jnp.float32
BF16 = jnp.bfloat16

D_MODEL = 1024
EPS = 1e-6
F_MIN = 1e-6
DKA = 128
HA = D_MODEL // DKA
WA = HA * DKA
HB = 8
DKB = D_MODEL // HB
DVB = 2 * DKB
WB_QK = HB * DKB
WB_V = HB * DVB
ROPE_BASE = 10000.0
RET_GN_EPS = 1e-5
NC = 64
HC = D_MODEL // NC
WC = HC * NC
W_LORA = 64
A_LORA = 64
RWKV_GN_EPS = 64e-5
KK_NORM_EPS = 1e-12
N_CODES = 2 * W_LORA + 2 * A_LORA
C_SHIFT = 3 * WC + N_CODES

SRC_B = 5 * WA
SRC_CS = SRC_B + 2 * WB_QK + 2 * WB_V
SRC_CG = SRC_CS + C_SHIFT
SRC_MERGE = SRC_CG + WC
N_IN = SRC_MERGE + 3 * D_MODEL

OFF_MERGE = 0
OFF_AQ = OFF_MERGE + 3 * D_MODEL
OFF_AFF = OFF_AQ + WA
OFF_AFB = OFF_AFF + WA
OFF_AI = OFF_AFB + WA
OFF_AG = OFF_AI + WA
OFF_BQ = OFF_AG + WA
OFF_BK = OFF_BQ + WB_QK
OFF_BV = OFF_BK + WB_QK
OFF_BG = OFF_BV + WB_V
OFF_CG = OFF_BG + WB_V
OFF_CR = OFF_CG + WC
OFF_CK = OFF_CR + WC
OFF_CV = OFF_CK + WC
OFF_CC = OFF_CV + WC
LANE = 128
PROJ_DTYPE = BF16
SCAN_DTYPE = BF16
HALO = 16
PROJ_SPLIT = 4
N_PAD = -(-(OFF_CC + N_CODES) // (PROJ_SPLIT * LANE)) * PROJ_SPLIT * LANE
PROJ_TN = N_PAD // PROJ_SPLIT
MXU_N = 256
WPERM_TN = 256

CHUNK_A = 128
GROUP_A = 8
HEADS_A = 2
CHUNK_B = 256
CHUNK_C = 64
PAIRS_C = 4
VMEM_LIMIT = 48 * 1024 * 1024
PROJ_VMEM_LIMIT = 56 * 1024 * 1024

NT_DIMS = (((1,), (1,)), ((), ()))
TN_DIMS = (((0,), (0,)), ((), ()))


def _mm(a, b):
    return jnp.dot(a.astype(BF16), b.astype(BF16), preferred_element_type=F32)


def _mm_nt(a, b):
    return lax.dot_general(a.astype(BF16), b.astype(BF16), NT_DIMS, preferred_element_type=F32)


def _mm_tn(a, b):
    return lax.dot_general(a.astype(BF16), b.astype(BF16), TN_DIMS, preferred_element_type=F32)


def _split_dot(ones_mat, x):
    hi = x.astype(BF16)
    lo = (x - hi.astype(F32)).astype(BF16)
    return jnp.dot(ones_mat, hi, preferred_element_type=F32) + jnp.dot(ones_mat, lo, preferred_element_type=F32)


def _split_dot_right(x, ones_mat):
    hi = x.astype(BF16)
    lo = (x - hi.astype(F32)).astype(BF16)
    return jnp.dot(hi, ones_mat, preferred_element_type=F32) + jnp.dot(lo, ones_mat, preferred_element_type=F32)


def _f32(z):
    return z.astype(F32)


def _sigmoid(z):
    return 0.5 * jnp.tanh(0.5 * z) + 0.5


def _cumsum_rows(x, rev):
    n = x.shape[0]
    rows = lax.broadcasted_iota(jnp.int32, (n, 1), 0)
    step = 1
    while step < n:
        if rev:
            x = x + jnp.where(rows < n - step, pltpu.roll(x, n - step, axis=0), 0.0)
        else:
            x = x + jnp.where(rows >= step, pltpu.roll(x, step, axis=0), 0.0)
        step *= 2
    return x


def _silu(z):
    return z * _sigmoid(z)


def _pick(n, pref):
    t = min(n, pref)
    assert n % t == 0, (n, pref)
    return t


def _params(*sem, vmem=VMEM_LIMIT):
    return pltpu.CompilerParams(dimension_semantics=sem, vmem_limit_bytes=vmem)


def _inproj_kernel(x_ref, g_ref, w_ref, o_ref, h_ref):
    @pl.when(pl.program_id(1) == 0)
    def _():
        x = x_ref[...]
        ms = jnp.mean(x * x, axis=-1, keepdims=True)
        h_ref[...] = (x * lax.rsqrt(ms + EPS) * g_ref[...]).astype(BF16)

    def tile(cols):
        o_ref[:, cols] = jnp.dot(h_ref[...], w_ref[:, cols], preferred_element_type=F32).astype(o_ref.dtype)

    for start in range(0, PROJ_TN, MXU_N):
        tile(pl.ds(start, min(MXU_N, PROJ_TN - start)))


def _inproj(x2, g, w_bf16):
    m = x2.shape[0]
    tm = _pick(m, 1024)
    return pl.pallas_call(
        _inproj_kernel,
        grid=(m // tm, N_PAD // PROJ_TN),
        in_specs=[
            pl.BlockSpec((tm, D_MODEL), lambda i, j: (i, 0)),
            pl.BlockSpec((1, D_MODEL), lambda i, j: (0, 0)),
            pl.BlockSpec((D_MODEL, PROJ_TN), lambda i, j: (0, j)),
        ],
        out_specs=pl.BlockSpec((tm, PROJ_TN), lambda i, j: (i, j)),
        out_shape=jax.ShapeDtypeStruct((m, N_PAD), PROJ_DTYPE),
        scratch_shapes=[pltpu.VMEM((tm, D_MODEL), BF16)],
        compiler_params=_params("parallel", "arbitrary", vmem=PROJ_VMEM_LIMIT),
    )(x2, g.reshape(1, D_MODEL), w_bf16)


def _wperm_kernel(nvalid, a_ref, b_ref, o_ref):
    j = pl.program_id(0)
    for idx, src in enumerate((a_ref, b_ref)):
        cols = pl.ds(idx * WPERM_TN, WPERM_TN)
        blk = src[...].astype(BF16)
        o_ref[:, cols] = jnp.where(2 * j + idx < nvalid, blk, jnp.zeros_like(blk))


def _src_block(jd):
    n_merge = (N_IN - SRC_MERGE) // WPERM_TN
    n_ab = SRC_CS // WPERM_TN
    n_cg = (SRC_MERGE - SRC_CG) // WPERM_TN
    n_src = N_IN // WPERM_TN
    return jnp.where(jd < n_merge, jd + SRC_MERGE // WPERM_TN,
                     jnp.where(jd < n_merge + n_ab, jd - n_merge,
                               jnp.where(jd < n_merge + n_ab + n_cg, jd - n_merge - n_ab + SRC_CG // WPERM_TN,
                                         jnp.minimum(jd - n_merge - n_cg, n_src - 1))))


def _permute_w_in(w_in, layer):
    assert all(off % WPERM_TN == 0 for off in (SRC_CS, SRC_CG, SRC_MERGE, N_IN)) and N_PAD % (2 * WPERM_TN) == 0

    def src(idx):
        return pl.BlockSpec((None, D_MODEL, WPERM_TN), lambda j: (layer, 0, _src_block(2 * j + idx)))

    return pl.pallas_call(
        functools.partial(_wperm_kernel, N_IN // WPERM_TN),
        grid=(N_PAD // (2 * WPERM_TN),),
        in_specs=[src(0), src(1)],
        out_specs=pl.BlockSpec((D_MODEL, 2 * WPERM_TN), lambda j: (0, j)),
        out_shape=jax.ShapeDtypeStruct((D_MODEL, N_PAD), BF16),
        compiler_params=_params("parallel"),
    )(w_in, w_in)


def _each(fn, *lists):
    return [fn(*items) for items in zip(*lists)]


def _hgrn_kernel(rev, final, tt, q_ref, f_ref, v_ref, lb_ref, tri_ref, *rest):
    if final:
        ob_ref, gate_ref, ng_ref, y_ref, st_ref = rest
    else:
        o_ref, st_ref = rest

    @pl.when(pl.program_id(2) == 0)
    def _():
        st_ref[...] = jnp.zeros_like(st_ref)

    n = CHUNK_A
    ri = lax.broadcasted_iota(jnp.int32, (n, n), 0)
    ci = lax.broadcasted_iota(jnp.int32, (n, n), 1)
    g_bits = int(np.log2(GROUP_A))
    diag_mask = ((ri >> g_bits) == (ci >> g_bits)) & ((ci >= ri) if rev else (ci <= ri))
    halves = [GROUP_A << i for i in range(int(np.log2(n // GROUP_A)))]
    level_masks = []
    for half in halves:
        hb = int(np.log2(half))
        q_side, k_side = (0, 1) if rev else (1, 0)
        level_masks.append(((ri >> (hb + 1)) == (ci >> (hb + 1)))
                           & (((ri >> hb) & 1) == q_side) & (((ci >> hb) & 1) == k_side))

    def ref_rows(b, blk, idx):
        picked = b.reshape(n // blk, blk, DKA)[:, idx:idx + 1, :]
        return jnp.broadcast_to(picked, (n // blk, blk, DKA)).reshape(n, DKA)

    nchunk = tt // n
    order = [(nchunk - 1 - i) if rev else i for i in range(nchunk)]
    items = [(pl.ds(c * n, n), pl.ds(hh * DKA, DKA), hh) for c in order for hh in range(HEADS_A)]
    sls = [it[0] for it in items]
    cls = [it[1] for it in items]
    lbs = [lb_ref[it[2]] for it in items]
    sig = _each(lambda sl, cl: _sigmoid(_f32(f_ref[0, sl, cl])), sls, cls)
    lf = _each(lambda z, lb: jnp.log(jnp.maximum(lb + (1.0 - lb) * z, F_MIN)), sig, lbs)
    k = _each(lambda z, lb: (1.0 - lb) * (1.0 - z), sig, lbs)
    q = _each(lambda sl, cl: _silu(_f32(q_ref[0, sl, cl])), sls, cls)
    v = _each(lambda sl, cl: v_ref[0, sl, cl], sls, cls)
    b = _each(lambda z: _split_dot(tri_ref[...], z), lf)
    d0 = _each(lambda z: z - ref_rows(z, GROUP_A, GROUP_A // 2 if rev else GROUP_A // 2 - 1), b)
    qb = _each(lambda z: z.astype(BF16), q)
    kb = _each(lambda z: z.astype(BF16), k)
    scores = _each(lambda qq, kk, d: jnp.where(
        diag_mask, _mm_nt(qq * jnp.exp(d).astype(BF16), kk * jnp.exp(-d).astype(BF16)), 0.0), qb, kb, d0)
    for half, mask in zip(halves, level_masks):
        e = _each(lambda z: jnp.exp(-jnp.abs(z - ref_rows(z, 2 * half, half if rev else half - 1))).astype(BF16),
                  b)
        scores = _each(lambda sc, qq, kk, ee: jnp.where(mask, _mm_nt(qq * ee, kk * ee), sc), scores, qb, kb, e)
    o_intra = _each(_mm, scores, v)
    b_end = _each(lambda z: z[0:1, :] if rev else z[n - 1:n, :], b)
    q_in = _each(lambda qq, z: qq * jnp.exp(z), q, b)
    kv = _each(lambda vv, kk, z, ze: _mm_tn(vv, kk * jnp.exp(ze - z)), v, k, b, b_end)
    e_end = _each(jnp.exp, b_end)

    outs = []
    st = [st_ref[hh] for hh in range(HEADS_A)]
    for (sl, cl, hh), oi, qi, kvi, ee in zip(items, o_intra, q_in, kv, e_end):
        outs.append((sl, cl, hh, oi + _mm_nt(qi, st[hh])))
        st[hh] = st[hh] * ee + kvi
    for hh in range(HEADS_A):
        st_ref[hh] = st[hh]

    for sl, cl, hh, o in outs:
        if final:
            o = o + ob_ref[0, sl, cl]
            ms = jnp.mean(o * o, axis=-1, keepdims=True)
            y = o * lax.rsqrt(ms + EPS) * ng_ref[hh] * _silu(_f32(gate_ref[0, sl, cl]))
            y_ref[0, sl, cl] = y.astype(y_ref.dtype)
        else:
            o_ref[0, sl, cl] = o


def _tri_blockdiag(n, blk, rev):
    i = np.arange(n)
    same = (i[:, None] // blk) == (i[None, :] // blk)
    tri = (i[None, :] >= i[:, None]) if rev else (i[None, :] <= i[:, None])
    return jnp.asarray((same & tri).astype(np.float32), dtype=BF16)


def _hgrn_pass(proj3, lb, rev, final, ob=None, norm_g=None):
    b, t, _ = proj3.shape
    tt = _pick(t, 1024)
    nt = t // tt
    tmap = (lambda i: nt - 1 - i) if rev else (lambda i: i)

    wblk = HEADS_A * DKA

    def col(off):
        return pl.BlockSpec((1, tt, wblk), lambda bi, h, ti: (bi, tmap(ti), off // wblk + h))

    head_vec = pl.BlockSpec((HEADS_A, 1, DKA), lambda bi, h, ti: (h, 0, 0))
    in_specs = [col(OFF_AQ), col(OFF_AFB if rev else OFF_AFF), col(OFF_AI), head_vec,
                pl.BlockSpec((LANE, LANE), lambda bi, h, ti: (0, 0))]
    args = [proj3, proj3, proj3, lb.reshape(HA, 1, DKA), _tri_blockdiag(LANE, CHUNK_A, rev)]
    out_block = pl.BlockSpec((1, tt, wblk), lambda bi, h, ti: (bi, tmap(ti), h))
    if final:
        in_specs += [out_block, col(OFF_AG), head_vec]
        args += [ob, proj3, norm_g.reshape(HA, 1, DKA)]
        out_dtype = BF16
    else:
        out_dtype = F32
    return pl.pallas_call(
        functools.partial(_hgrn_kernel, rev, final, tt),
        grid=(b, HA // HEADS_A, nt),
        in_specs=in_specs,
        out_specs=out_block,
        out_shape=jax.ShapeDtypeStruct((b, t, WA), out_dtype),
        scratch_shapes=[pltpu.VMEM((HEADS_A, DKA, DKA), F32)],
        compiler_params=_params("parallel", "parallel", "arbitrary"),
    )(*args)


def _hgrn_mixer(proj3, lbs, norm_g):
    ob = _hgrn_pass(proj3, lbs[1], rev=True, final=False)
    return _hgrn_pass(proj3, lbs[0], rev=False, final=True, ob=ob, norm_g=norm_g)


def _rotary(z, cos2, sin2):
    return z * cos2 + pltpu.roll(z, DKB // 2, axis=1) * sin2


def _ret_state_kernel(nsub, k_ref, v_ref, cos_ref, sin_ref, lg_ref, sb_ref, st_ref):
    @pl.when(pl.program_id(2) == 0)
    def _():
        st_ref[...] = jnp.zeros_like(st_ref)

    cb = CHUNK_B
    lg2 = lg_ref[0]
    lg = lg2[:, :DKB]
    pos = lax.broadcasted_iota(jnp.int32, (cb, DKB), 0).astype(F32)
    dec = jnp.exp(pos * lg) * (DKB ** -0.5)
    order = list(range(nsub - 1, -1, -1))
    sls = [pl.ds(c * cb, cb) for c in order]
    kd = _each(lambda sl: _rotary(_f32(k_ref[0, sl, :]), cos_ref[sl, :], sin_ref[sl, :]) * dec, sls)
    kv = _each(lambda kk, sl: _mm_tn(kk, v_ref[0, sl, :]), kd, sls)
    e_chunk = jnp.exp(cb * lg2)
    st = st_ref[...]
    for c, kvi in zip(order, kv):
        sb_ref[0, 0, c] = st
        st = st * e_chunk + kvi
    st_ref[...] = st


def _ret_out_kernel(nsub, q_ref, k_ref, v_ref, cos_ref, sin_ref, lg_ref, sb_ref, gate_ref, gg_ref,
                    gb_ref, y_ref, st_ref):
    @pl.when(pl.program_id(2) == 0)
    def _():
        st_ref[...] = jnp.zeros_like(st_ref)

    cb = CHUNK_B
    lg2 = lg_ref[0]
    lg = lg2[:, :DKB]
    pos = lax.broadcasted_iota(jnp.int32, (cb, DKB), 0).astype(F32)
    ri = lax.broadcasted_iota(jnp.int32, (cb, cb), 0)
    ci = lax.broadcasted_iota(jnp.int32, (cb, cb), 1)
    decay = jnp.exp(jnp.abs(ri - ci).astype(F32) * lg2)
    dec_fwd = jnp.exp((pos + 1.0) * lg)
    dec_bwd = jnp.exp((cb - pos) * lg)
    dec_key = jnp.exp((cb - 1.0 - pos) * lg)
    e_chunk = jnp.exp(cb * lg2)

    sls = [pl.ds(c * cb, cb) for c in range(nsub)]
    q = _each(lambda sl: _rotary(_f32(q_ref[0, sl, :]), cos_ref[sl, :], sin_ref[sl, :]), sls)
    k = _each(lambda sl: _rotary(_f32(k_ref[0, sl, :]), cos_ref[sl, :], sin_ref[sl, :]) * (DKB ** -0.5), sls)
    v = _each(lambda sl: v_ref[0, sl, :], sls)
    scores = _each(lambda qq, kk: _mm_nt(qq, kk) * decay, q, k)
    o = _each(_mm, scores, v)
    o = _each(lambda oo, qq, c: oo + _mm(qq * dec_bwd, sb_ref[0, 0, c]), o, q, list(range(nsub)))
    kv = _each(lambda kk, vv: _mm_tn(kk * dec_key, vv), k, v)
    q_fwd = _each(lambda qq: qq * dec_fwd, q)
    st = st_ref[...]
    outs = []
    for oo, qf, kvi in zip(o, q_fwd, kv):
        outs.append(oo + _mm(qf, st))
        st = st * e_chunk + kvi
    st_ref[...] = st

    for sl, oo in zip(sls, outs):
        mu = jnp.mean(oo, axis=-1, keepdims=True)
        cen = oo - mu
        var = jnp.mean(cen * cen, axis=-1, keepdims=True)
        y = (cen * lax.rsqrt(var + RET_GN_EPS) * gg_ref[0] + gb_ref[0]) * _silu(_f32(gate_ref[0, sl, :]))
        y_ref[0, sl, :] = y.astype(y_ref.dtype)


def _ret_mixer(proj3, gn_g, gn_b):
    b, t, _ = proj3.shape
    cb = CHUNK_B
    tt = _pick(t, 8 * cb)
    nsub = tt // cb
    nt = t // tt
    half = DKB // 2
    inv = ROPE_BASE ** (-jnp.arange(0, DKB, 2, dtype=F32) / DKB)
    ang_hi = (jnp.arange(t // cb, dtype=F32) * cb)[:, None, None] * inv
    ang_lo = jnp.arange(cb, dtype=F32)[None, :, None] * inv
    cos = (jnp.cos(ang_hi) * jnp.cos(ang_lo) - jnp.sin(ang_hi) * jnp.sin(ang_lo)).reshape(t, half)
    sin = (jnp.sin(ang_hi) * jnp.cos(ang_lo) + jnp.cos(ang_hi) * jnp.sin(ang_lo)).reshape(t, half)
    cos2 = jnp.concatenate([cos, cos], axis=-1)
    sin2 = jnp.concatenate([-sin, sin], axis=-1)
    log_gamma = jnp.log1p(-jnp.exp2(-5.0 - jnp.arange(HB, dtype=F32)))
    lg = jnp.broadcast_to(log_gamma[:, None, None], (HB, 1, DVB))

    def col(off, width, tmap):
        return pl.BlockSpec((1, tt, width), lambda bi, h, c: (bi, tmap(c), off // width + h))

    def tab(tmap):
        return pl.BlockSpec((tt, DKB), lambda bi, h, c: (tmap(c), 0))

    lg_spec = pl.BlockSpec((1, 1, DVB), lambda bi, h, c: (h, 0, 0))
    rmap = lambda c: nt - 1 - c
    fmap = lambda c: c

    sb = pl.pallas_call(
        functools.partial(_ret_state_kernel, nsub),
        grid=(b, HB, nt),
        in_specs=[col(OFF_BK, DKB, rmap), col(OFF_BV, DVB, rmap), tab(rmap), tab(rmap), lg_spec],
        out_specs=pl.BlockSpec((1, 1, nsub, DKB, DVB), lambda bi, h, c: (bi, h, rmap(c), 0, 0)),
        out_shape=jax.ShapeDtypeStruct((b, HB, t // cb, DKB, DVB), F32),
        scratch_shapes=[pltpu.VMEM((DKB, DVB), F32)],
        compiler_params=_params("parallel", "parallel", "arbitrary"),
    )(proj3, proj3, cos2, sin2, lg)

    head_vec = pl.BlockSpec((1, 1, DVB), lambda bi, h, c: (h, 0, 0))
    return pl.pallas_call(
        functools.partial(_ret_out_kernel, nsub),
        grid=(b, HB, nt),
        in_specs=[col(OFF_BQ, DKB, fmap), col(OFF_BK, DKB, fmap), col(OFF_BV, DVB, fmap),
                  tab(fmap), tab(fmap), lg_spec,
                  pl.BlockSpec((1, 1, nsub, DKB, DVB), lambda bi, h, c: (bi, h, c, 0, 0)),
                  col(OFF_BG, DVB, fmap), head_vec, head_vec],
        out_specs=pl.BlockSpec((1, tt, DVB), lambda bi, h, c: (bi, c, h)),
        out_shape=jax.ShapeDtypeStruct((b, t, WB_V), BF16),
        scratch_shapes=[pltpu.VMEM((DKB, DVB), F32)],
        compiler_params=_params("parallel", "parallel", "arbitrary"),
    )(proj3, proj3, proj3, cos2, sin2, lg, sb, proj3,
      gn_g.reshape(HB, 1, DVB), gn_b.reshape(HB, 1, DVB))


def _group_sum(x, gmat):
    cols = [_split_dot_right(x[:, i * LANE:(i + 1) * LANE], gmat) for i in range(x.shape[1] // LANE)]
    return jnp.concatenate(cols, axis=1)


def _rwkv_prep_kernel(tt, r_ref, k_ref, v_ref, c_ref, rp_ref, kp_ref, vp_ref, cp_ref,
                      rn_ref, kn_ref, vn_ref, cn_ref, mur_ref, muk_ref, muv_ref, muc_ref,
                      w0_ref, wl_ref, a0_ref, al_ref, kk_ref, ka_ref, rk_ref, gmat_ref, shift_ref,
                      r_o, v_o, kk_o, bonus_o, lw0_o, lw1_o, k0_o, k1_o, ka0_o, ka1_o):
    ti = pl.program_id(1)
    has_prev = (ti > 0).astype(F32)
    has_next = (ti < pl.num_programs(1) - 1).astype(F32)
    rows = lax.broadcasted_iota(jnp.int32, (tt, 1), 0)

    def shifted(p_ref, prev_ref, next_ref, mu_ref):
        both = jnp.dot(shift_ref[...], p_ref[0].astype(BF16), preferred_element_type=F32)
        p = _f32(p_ref[0])
        prev_row = _f32(prev_ref[0, HALO - 1:HALO, :]) * has_prev
        next_row = _f32(next_ref[0, 0:1, :]) * has_next
        before = jnp.where(rows == 0, prev_row, both[:tt])
        after = jnp.where(rows == tt - 1, next_row, both[tt:])
        return p + mu_ref[0:1, :] * (before - p) + mu_ref[1:2, :] * (after - p)

    r = shifted(r_ref, rp_ref, rn_ref, mur_ref)
    k = shifted(k_ref, kp_ref, kn_ref, muk_ref)
    v = shifted(v_ref, vp_ref, vn_ref, muv_ref)
    codes = shifted(c_ref, cp_ref, cn_ref, muc_ref)
    gmat = gmat_ref[...]

    kraw = k * kk_ref[...]
    kk = kraw * lax.rsqrt(_group_sum(kraw * kraw, gmat) + KK_NORM_EPS)
    r_o[0] = r.astype(r_o.dtype)
    v_o[0] = v.astype(v_o.dtype)
    kk_o[0] = kk.astype(kk_o.dtype)
    tcodes = jnp.tanh(codes)
    bonus = jnp.zeros_like(r)
    for d, (lw_o, kd_o, ka_o) in enumerate(((lw0_o, k0_o, ka0_o), (lw1_o, k1_o, ka1_o))):
        w = w0_ref[d:d + 1, :] + _mm(tcodes, wl_ref[d])
        lw_o[0] = -np.exp(-0.5).astype(np.float32) * _sigmoid(w)
        a = _sigmoid(a0_ref[d:d + 1, :] + _mm(codes, al_ref[d]))
        kd = k * (1.0 + (a - 1.0) * ka_ref[...])
        kd_o[0] = kd.astype(kd_o.dtype)
        ka_o[0] = (kk * a).astype(ka_o.dtype)
        bonus = bonus + _group_sum(r * kd * rk_ref[...], gmat) * v
    bonus_o[0] = bonus


def _rwkv_scan_kernel(rev, final, tt, r_ref, lw_ref, k_ref, v_ref, ka_ref, kk_ref, *rest):
    if final:
        ob_ref, bonus_ref, gate_ref, lng_ref, lnb_ref, y_ref, h_ref = rest
    else:
        o_ref, h_ref = rest

    @pl.when(pl.program_id(2) == 0)
    def _():
        h_ref[...] = jnp.zeros_like(h_ref)

    n2 = 2 * CHUNK_C
    lane = lax.broadcasted_iota(jnp.int32, (1, LANE), 1)
    head0 = lane < NC
    ri = lax.broadcasted_iota(jnp.int32, (n2, n2), 0)
    ci = lax.broadcasted_iota(jnp.int32, (n2, n2), 1)
    strict = (ci > ri) if rev else (ci < ri)
    incl = (ci >= ri) if rev else (ci <= ri)
    eye = ri == ci
    nchunk = tt // CHUNK_C

    def stack(z):
        zero = jnp.zeros_like(z)
        return jnp.concatenate([jnp.where(head0, z, zero), jnp.where(head0, zero, z)], axis=0)

    def stack_b(z):
        return stack(z.astype(BF16))

    order = [(nchunk - 1 - i) if rev else i for i in range(nchunk)]
    items = [(pl.ds(c * CHUNK_C, CHUNK_C), pl.ds(p * LANE, LANE), p) for c in order for p in range(PAIRS_C)]
    rows = [it[0] for it in items]
    cols = [it[1] for it in items]

    lws = _each(lambda sl, cl: lw_ref[0, sl, cl], rows, cols)
    g = _each(lambda lw: _cumsum_rows(lw, rev), lws)
    e_end = _each(lambda gu: jnp.exp(gu[0:1, :] if rev else gu[CHUNK_C - 1:CHUNK_C, :]), g)
    e_neg = _each(lambda gu: jnp.exp(-gu), g)
    bs = _each(lambda sl, cl, gu, lw: stack_b(-_f32(kk_ref[0, sl, cl]) * jnp.exp(gu - lw)), rows, cols, g, lws)
    a_sc = _each(lambda sl, cl, en: _f32(ka_ref[0, sl, cl]) * en, rows, cols, e_neg)
    k_sc = _each(lambda sl, cl, en: _f32(k_ref[0, sl, cl]) * en, rows, cols, e_neg)
    as_ = _each(stack_b, a_sc)
    ks = _each(stack_b, k_sc)
    ake = _each(lambda a, k, e: jnp.concatenate([stack_b(a * e), stack_b(k * e)], axis=0), a_sc, k_sc, e_end)
    rs = _each(lambda sl, cl, gu: stack(_f32(r_ref[0, sl, cl]) * jnp.exp(gu)), rows, cols, g)
    vs = _each(lambda sl, cl: stack_b(v_ref[0, sl, cl]), rows, cols)

    ak = _each(lambda a, k: jnp.concatenate([a, k], axis=0), as_, ks)
    sc_b = _each(_mm_nt, bs, ak)
    sc_r = _each(_mm_nt, rs, ak)
    a_ab = _each(lambda z: jnp.where(strict, z[:, :n2], 0.0), sc_b)
    a_ak = _each(lambda z: jnp.where(strict, z[:, n2:], 0.0), sc_b)
    m_rak = _each(lambda z: jnp.where(jnp.concatenate([incl, incl], axis=1), z, 0.0), sc_r)
    tinv = _each(lambda a: jnp.where(eye, 1.0, 0.0) + a, a_ab)
    pw = _each(lambda a: _mm(a, a), a_ab)
    akv = _each(_mm, a_ak, vs)
    nsteps = int(np.log2(CHUNK_C)) - 1
    for i in range(nsteps):
        if i < nsteps - 1:
            both = _each(lambda t, p: _mm(jnp.concatenate([t, p], axis=0), p), tinv, pw)
            tinv = _each(lambda t, b: t + b[:n2], tinv, both)
            pw = _each(lambda b: b[n2:], both)
        else:
            tinv = _each(lambda t, p: t + _mm(t, p), tinv, pw)
    pwv = _each(lambda t, b, x: _mm(t, jnp.concatenate([b, x.astype(BF16)], axis=1)), tinv, bs, akv)
    low = _each(lambda z, v: jnp.concatenate([z.astype(BF16), jnp.concatenate([jnp.zeros_like(v), v], axis=1)],
                                             axis=0), pwv, vs)
    x1 = _each(_mm, m_rak, low)
    x2 = _each(_mm_tn, ake, low)
    top = _each(lambda r, z, e, y: jnp.concatenate([r + z[:, :LANE], jnp.where(eye, e, 0.0) + y[:, :LANE]], axis=0),
                rs, x1, e_end, x2)
    o_in = _each(lambda z: z[:, LANE:], x1)
    dmat = _each(lambda y: y[:, LANE:], x2)

    outs = []
    h = [h_ref[p] for p in range(PAIRS_C)]
    for (sl, cl, p), tp, oi, dm in zip(items, top, o_in, dmat):
        both = _mm(tp, h[p])
        h[p] = both[n2:] + dm
        os_ = both[:n2] + oi
        outs.append((sl, cl, os_[:CHUNK_C, :] + os_[CHUNK_C:, :]))
    for p in range(PAIRS_C):
        h_ref[p] = h[p]

    for sl, cl, o in outs:
        if final:
            o = o + ob_ref[0, sl, cl]
            inv_n = 1.0 / NC
            s0 = jnp.sum(jnp.where(head0, o, 0.0), axis=-1, keepdims=True)
            s1 = jnp.sum(jnp.where(head0, 0.0, o), axis=-1, keepdims=True)
            cen = o - jnp.where(head0, s0, s1) * inv_n
            c2 = cen * cen
            v0 = jnp.sum(jnp.where(head0, c2, 0.0), axis=-1, keepdims=True)
            v1 = jnp.sum(jnp.where(head0, 0.0, c2), axis=-1, keepdims=True)
            var = jnp.where(head0, v0, v1) * inv_n
            y = cen * lax.rsqrt(var + RWKV_GN_EPS) * lng_ref[:, cl] + lnb_ref[:, cl] + bonus_ref[0, sl, cl]
            y_ref[0, sl, cl] = (y * _silu(_f32(gate_ref[0, sl, cl]))).astype(y_ref.dtype)
        else:
            o_ref[0, sl, cl] = o


def _rwkv_mixer(proj3, mu, w0, w_lora_b, a0, a_lora_b, k_k, k_a, r_k, ln_g, ln_b):
    b, t, _ = proj3.shape
    tt = _pick(t, 256)
    nt = t // tt
    sub = tt // HALO

    def cur(off, width):
        return pl.BlockSpec((1, tt, width), lambda bi, ti: (bi, ti, off // width))

    def prev(off, width):
        return pl.BlockSpec((1, HALO, width),
                            lambda bi, ti: (bi, jnp.maximum(ti * sub - 1, 0), off // width))

    def nxt(off, width):
        return pl.BlockSpec((1, HALO, width),
                            lambda bi, ti: (bi, jnp.minimum((ti + 1) * sub, nt * sub - 1), off // width))

    def full(shape):
        return pl.BlockSpec(shape, lambda bi, ti: (0,) * len(shape))

    segs = ((OFF_CR, WC), (OFF_CK, WC), (OFF_CV, WC), (OFF_CC, N_CODES))
    in_specs = ([cur(o, w) for o, w in segs] + [prev(o, w) for o, w in segs] + [nxt(o, w) for o, w in segs]
                + [full((2, WC))] * 3 + [full((2, N_CODES))]
                + [full((2, WC)), full((2, N_CODES, WC)), full((2, WC)), full((2, N_CODES, WC))]
                + [full((1, WC))] * 3 + [full((LANE, LANE)), full((2 * tt, tt))])
    wl = jnp.zeros((2, N_CODES, WC), F32)
    al = jnp.zeros((2, N_CODES, WC), F32)
    for d in range(2):
        wl = wl.at[d, d * W_LORA:(d + 1) * W_LORA].set(w_lora_b[d])
        al = al.at[d, 2 * W_LORA + d * A_LORA:2 * W_LORA + (d + 1) * A_LORA].set(a_lora_b[d])
    row_i = np.arange(tt)
    shift = np.concatenate([row_i[:, None] - 1 == row_i[None, :], row_i[:, None] + 1 == row_i[None, :]], axis=0)
    shift = jnp.asarray(shift.astype(np.float32), dtype=BF16)
    lane_i = np.arange(LANE)
    gmat = jnp.asarray((lane_i[:, None] // NC == lane_i[None, :] // NC).astype(np.float32), dtype=BF16)
    row_out = pl.BlockSpec((1, tt, WC), lambda bi, ti: (bi, ti, 0))
    outs = pl.pallas_call(
        functools.partial(_rwkv_prep_kernel, tt),
        grid=(b, nt),
        in_specs=in_specs,
        out_specs=[row_out] * 10,
        out_shape=[jax.ShapeDtypeStruct((b, t, WC), dt) for dt in (SCAN_DTYPE,) * 3 + (F32,) * 3 + (SCAN_DTYPE,) * 4],
        compiler_params=_params("parallel", "parallel"),
    )(*([proj3] * 12), mu[:, :WC], mu[:, WC:2 * WC], mu[:, 2 * WC:3 * WC], mu[:, 3 * WC:],
      w0, wl.astype(BF16), a0, al.astype(BF16), k_k.reshape(1, WC), k_a.reshape(1, WC),
      r_k.reshape(1, WC), gmat, shift)
    r, v, kk, bonus, lw0, lw1, k0, k1, ka0, ka1 = outs

    ts = _pick(t, 512)
    ns = t // ts

    def scan(rev, final, lw, kd, ka, ob=None):
        wblk = PAIRS_C * LANE
        tmap = (lambda i: ns - 1 - i) if rev else (lambda i: i)
        blk = pl.BlockSpec((1, ts, wblk), lambda bi, h, ti: (bi, tmap(ti), h))
        head_vec = pl.BlockSpec((1, wblk), lambda bi, h, ti: (0, h))
        in_specs = [blk] * 6
        args = [r, lw, kd, v, ka, kk]
        if final:
            gate = pl.BlockSpec((1, ts, wblk), lambda bi, h, ti: (bi, tmap(ti), OFF_CG // wblk + h))
            in_specs += [blk, blk, gate, head_vec, head_vec]
            args += [ob, bonus, proj3, ln_g.reshape(1, WC), ln_b.reshape(1, WC)]
        return pl.pallas_call(
            functools.partial(_rwkv_scan_kernel, rev, final, ts),
            grid=(b, WC // wblk, ns),
            in_specs=in_specs,
            out_specs=blk,
            out_shape=jax.ShapeDtypeStruct((b, t, WC), BF16 if final else F32),
            scratch_shapes=[pltpu.VMEM((PAIRS_C, LANE, LANE), F32)],
            compiler_params=_params("parallel", "parallel", "arbitrary"),
        )(*args)

    ob = scan(True, False, lw1, k1, ka1)
    return scan(False, True, lw0, k0, ka0, ob=ob)


def _merge_kernel(final, ya_ref, yb_ref, yc_ref, mg_ref, x_ref, wa_ref, wb_ref, wc_ref, wo_ref,
                  fg_ref, o_ref):
    za = jnp.dot(ya_ref[...], wa_ref[...], preferred_element_type=F32)
    zb = jnp.dot(yb_ref[...], wb_ref[...], preferred_element_type=F32)
    zc = jnp.dot(yc_ref[...], wc_ref[...], preferred_element_type=F32)
    mg = _f32(mg_ref[...])
    mixed = (_sigmoid(mg[:, :D_MODEL]) * za
             + _sigmoid(mg[:, D_MODEL:2 * D_MODEL]) * zb
             + _sigmoid(mg[:, 2 * D_MODEL:]) * zc)
    out = x_ref[...] + jnp.dot(mixed.astype(BF16), wo_ref[...], preferred_element_type=F32)
    if final:
        ms = jnp.mean(out * out, axis=-1, keepdims=True)
        out = out * lax.rsqrt(ms + EPS) * fg_ref[...]
    o_ref[...] = out


def _merge(ya, yb, yc, proj, x2, wa, wb, wc, wo, fg, final):
    m = x2.shape[0]
    tm = _pick(m, 512)

    def rows(width):
        return pl.BlockSpec((tm, width), lambda i: (i, 0))

    def full(shape):
        return pl.BlockSpec(shape, lambda i: (0, 0))

    return pl.pallas_call(
        functools.partial(_merge_kernel, final),
        grid=(m // tm,),
        in_specs=[rows(WA), rows(WB_V), rows(WC), rows(3 * D_MODEL), rows(D_MODEL),
                  full((WA, D_MODEL)), full((WB_V, D_MODEL)), full((WC, D_MODEL)),
                  full((D_MODEL, D_MODEL)), full((1, D_MODEL))],
        out_specs=rows(D_MODEL),
        out_shape=jax.ShapeDtypeStruct((m, D_MODEL), F32),
        compiler_params=_params("parallel", vmem=PROJ_VMEM_LIMIT),
    )(ya, yb, yc, proj, x2, wa.astype(BF16), wb.astype(BF16), wc.astype(BF16), wo.astype(BF16),
      fg.reshape(1, D_MODEL))


def kernel(x, norm_g, w_in, hgrn_lb_logits, hgrn_norm_g, ret_norm_g, ret_norm_b, rwkv_mu, rwkv_w0,
           rwkv_w_lora_b, rwkv_a0, rwkv_a_lora_b, rwkv_k_k, rwkv_k_a, rwkv_r_k, rwkv_ln_g,
           rwkv_ln_b, w_branch_a, w_branch_b, w_branch_c, w_out, final_norm_g):
    b, t, d = x.shape
    m = b * t
    depth = w_in.shape[0]
    p_lb = jax.nn.softmax(hgrn_lb_logits.astype(F32), axis=0)
    lbs = jnp.cumsum(p_lb, axis=0) - p_lb[0:1]
    x2 = x.reshape(m, d)
    for l in range(depth):
        proj = _inproj(x2, norm_g[l], _permute_w_in(w_in, l))
        proj3 = proj.reshape(b, t, N_PAD)
        ya = _hgrn_mixer(proj3, lbs[l], hgrn_norm_g[l])
        yb = _ret_mixer(proj3, ret_norm_g[l], ret_norm_b[l])
        yc = _rwkv_mixer(proj3, rwkv_mu[l], rwkv_w0[l], rwkv_w_lora_b[l], rwkv_a0[l],
                         rwkv_a_lora_b[l], rwkv_k_k[l], rwkv_k_a[l], rwkv_r_k[l],
                         rwkv_ln_g[l], rwkv_ln_b[l])
        x2 = _merge(ya.reshape(m, WA), yb.reshape(m, WB_V), yc.reshape(m, WC), proj, x2,
                    w_branch_a[l], w_branch_b[l], w_branch_c[l], w_out[l], final_norm_g,
                    final=(l == depth - 1))
    return x2.reshape(b, t, d)
```

```python
import functools

import numpy as np
import jax
import jax.numpy as jnp
from jax import lax
from jax.experimental import pallas as pl
from jax.experimental.pallas import tpu as pltpu

F32 = jnp.float32
BF16 = jnp.bfloat16

D_MODEL = 1024
EPS = 1e-6
F_MIN = 1e-6
DKA = 128
HA = D_MODEL // DKA
WA = HA * DKA
HB = 8
DKB = D_MODEL // HB
DVB = 2 * DKB
WB_QK = HB * DKB
WB_V = HB * DVB
ROPE_BASE = 10000.0
RET_GN_EPS = 1e-5
NC = 64
HC = D_MODEL // NC
WC = HC * NC
W_LORA = 64
A_LORA = 64
RWKV_GN_EPS = 64e-5
KK_NORM_EPS = 1e-12
N_CODES = 2 * W_LORA + 2 * A_LORA
C_SHIFT = 3 * WC + N_CODES

SRC_B = 5 * WA
SRC_CS = SRC_B + 2 * WB_QK + 2 * WB_V
SRC_CG = SRC_CS + C_SHIFT
SRC_MERGE = SRC_CG + WC
N_IN = SRC_MERGE + 3 * D_MODEL

OFF_MERGE = 0
OFF_AQ = OFF_MERGE + 3 * D_MODEL
OFF_AFF = OFF_AQ + WA
OFF_AFB = OFF_AFF + WA
OFF_AI = OFF_AFB + WA
OFF_AG = OFF_AI + WA
OFF_BQ = OFF_AG + WA
OFF_BK = OFF_BQ + WB_QK
OFF_BV = OFF_BK + WB_QK
OFF_BG = OFF_BV + WB_V
OFF_CG = OFF_BG + WB_V
OFF_CR = OFF_CG + WC
OFF_CK = OFF_CR + WC
OFF_CV = OFF_CK + WC
OFF_CC = OFF_CV + WC
LANE = 128
PROJ_DTYPE = BF16
SCAN_DTYPE = BF16
HALO = 16
PROJ_SPLIT = 4
N_PAD = -(-(OFF_CC + N_CODES) // (PROJ_SPLIT * LANE)) * PROJ_SPLIT * LANE
PROJ_TN = N_PAD // PROJ_SPLIT
MXU_N = 256
WPERM_TN = 256

CHUNK_A = 128
GROUP_A = 8
HEADS_A = 2
CHUNK_B = 256
CHUNK_C = 64
PAIRS_C = 4
VMEM_LIMIT = 48 * 1024 * 1024
PROJ_VMEM_LIMIT = 56 * 1024 * 1024

NT_DIMS = (((1,), (1,)), ((), ()))
TN_DIMS = (((0,), (0,)), ((), ()))


def _mm(a, b):
    return jnp.dot(a.astype(BF16), b.astype(BF16), preferred_element_type=F32)


def _mm_nt(a, b):
    return lax.dot_general(a.astype(BF16), b.astype(BF16), NT_DIMS, preferred_element_type=F32)


def _mm_tn(a, b):
    return lax.dot_general(a.astype(BF16), b.astype(BF16), TN_DIMS, preferred_element_type=F32)


def _split_dot(ones_mat, x):
    hi = x.astype(BF16)
    lo = (x - hi.astype(F32)).astype(BF16)
    return jnp.dot(ones_mat, hi, preferred_element_type=F32) + jnp.dot(ones_mat, lo, preferred_element_type=F32)


def _split_dot_right(x, ones_mat):
    hi = x.astype(BF16)
    lo = (x - hi.astype(F32)).astype(BF16)
    return jnp.dot(hi, ones_mat, preferred_element_type=F32) + jnp.dot(lo, ones_mat, preferred_element_type=F32)


def _f32(z):
    return z.astype(F32)


def _sigmoid(z):
    return 0.5 * jnp.tanh(0.5 * z) + 0.5


def _cumsum_rows(x, rev):
    n = x.shape[0]
    rows = lax.broadcasted_iota(jnp.int32, (n, 1), 0)
    step = 1
    while step < n:
        if rev:
            x = x + jnp.where(rows < n - step, pltpu.roll(x, n - step, axis=0), 0.0)
        else:
            x = x + jnp.where(rows >= step, pltpu.roll(x, step, axis=0), 0.0)
        step *= 2
    return x


def _silu(z):
    return z * _sigmoid(z)


def _pick(n, pref):
    t = min(n, pref)
    assert n % t == 0, (n, pref)
    return t


def _params(*sem, vmem=VMEM_LIMIT):
    return pltpu.CompilerParams(dimension_semantics=sem, vmem_limit_bytes=vmem)


def _inproj_kernel(x_ref, g_ref, w_ref, o_ref, h_ref):
    @pl.when(pl.program_id(1) == 0)
    def _():
        x = x_ref[...]
        ms = jnp.mean(x * x, axis=-1, keepdims=True)
        h_ref[...] = (x * lax.rsqrt(ms + EPS) * g_ref[...]).astype(BF16)

    def tile(cols):
        o_ref[:, cols] = jnp.dot(h_ref[...], w_ref[:, cols], preferred_element_type=F32).astype(o_ref.dtype)

    for start in range(0, PROJ_TN, MXU_N):
        tile(pl.ds(start, min(MXU_N, PROJ_TN - start)))


def _inproj(x2, g, w_bf16):
    m = x2.shape[0]
    tm = _pick(m, 1024)
    return pl.pallas_call(
        _inproj_kernel,
        grid=(m // tm, N_PAD // PROJ_TN),
        in_specs=[
            pl.BlockSpec((tm, D_MODEL), lambda i, j: (i, 0)),
            pl.BlockSpec((1, D_MODEL), lambda i, j: (0, 0)),
            pl.BlockSpec((D_MODEL, PROJ_TN), lambda i, j: (0, j)),
        ],
        out_specs=pl.BlockSpec((tm, PROJ_TN), lambda i, j: (i, j)),
        out_shape=jax.ShapeDtypeStruct((m, N_PAD), PROJ_DTYPE),
        scratch_shapes=[pltpu.VMEM((tm, D_MODEL), BF16)],
        compiler_params=_params("parallel", "arbitrary", vmem=PROJ_VMEM_LIMIT),
    )(x2, g.reshape(1, D_MODEL), w_bf16)


def _wperm_kernel(nvalid, a_ref, b_ref, o_ref):
    j = pl.program_id(0)
    for idx, src in enumerate((a_ref, b_ref)):
        cols = pl.ds(idx * WPERM_TN, WPERM_TN)
        blk = src[...].astype(BF16)
        o_ref[:, cols] = jnp.where(2 * j + idx < nvalid, blk, jnp.zeros_like(blk))


def _src_block(jd):
    n_merge = (N_IN - SRC_MERGE) // WPERM_TN
    n_ab = SRC_CS // WPERM_TN
    n_cg = (SRC_MERGE - SRC_CG) // WPERM_TN
    n_src = N_IN // WPERM_TN
    return jnp.where(jd < n_merge, jd + SRC_MERGE // WPERM_TN,
                     jnp.where(jd < n_merge + n_ab, jd - n_merge,
                               jnp.where(jd < n_merge + n_ab + n_cg, jd - n_merge - n_ab + SRC_CG // WPERM_TN,
                                         jnp.minimum(jd - n_merge - n_cg, n_src - 1))))


def _permute_w_in(w_in, layer):
    assert all(off % WPERM_TN == 0 for off in (SRC_CS, SRC_CG, SRC_MERGE, N_IN)) and N_PAD % (2 * WPERM_TN) == 0

    def src(idx):
        return pl.BlockSpec((None, D_MODEL, WPERM_TN), lambda j: (layer, 0, _src_block(2 * j + idx)))

    return pl.pallas_call(
        functools.partial(_wperm_kernel, N_IN // WPERM_TN),
        grid=(N_PAD // (2 * WPERM_TN),),
        in_specs=[src(0), src(1)],
        out_specs=pl.BlockSpec((D_MODEL, 2 * WPERM_TN), lambda j: (0, j)),
        out_shape=jax.ShapeDtypeStruct((D_MODEL, N_PAD), BF16),
        compiler_params=_params("parallel"),
    )(w_in, w_in)


def _each(fn, *lists):
    return [fn(*items) for items in zip(*lists)]


def _hgrn_kernel(rev, final, tt, q_ref, f_ref, v_ref, lb_ref, tri_ref, *rest):
    if final:
        ob_ref, gate_ref, ng_ref, y_ref, st_ref = rest
    else:
        o_ref, st_ref = rest

    @pl.when(pl.program_id(2) == 0)
    def _():
        st_ref[...] = jnp.zeros_like(st_ref)

    n = CHUNK_A
    ri = lax.broadcasted_iota(jnp.int32, (n, n), 0)
    ci = lax.broadcasted_iota(jnp.int32, (n, n), 1)
    g_bits = int(np.log2(GROUP_A))
    diag_mask = ((ri >> g_bits) == (ci >> g_bits)) & ((ci >= ri) if rev else (ci <= ri))
    halves = [GROUP_A << i for i in range(int(np.log2(n // GROUP_A)))]
    level_masks = []
    for half in halves:
        hb = int(np.log2(half))
        q_side, k_side = (0, 1) if rev else (1, 0)
        level_masks.append(((ri >> (hb + 1)) == (ci >> (hb + 1)))
                           & (((ri >> hb) & 1) == q_side) & (((ci >> hb) & 1) == k_side))

    def ref_rows(b, blk, idx):
        picked = b.reshape(n // blk, blk, DKA)[:, idx:idx + 1, :]
        return jnp.broadcast_to(picked, (n // blk, blk, DKA)).reshape(n, DKA)

    nchunk = tt // n
    order = [(nchunk - 1 - i) if rev else i for i in range(nchunk)]
    items = [(pl.ds(c * n, n), pl.ds(hh * DKA, DKA), hh) for c in order for hh in range(HEADS_A)]
    sls = [it[0] for it in items]
    cls = [it[1] for it in items]
    lbs = [lb_ref[it[2]] for it in items]
    sig = _each(lambda sl, cl: _sigmoid(_f32(f_ref[0, sl, cl])), sls, cls)
    lf = _each(lambda z, lb: jnp.log(jnp.maximum(lb + (1.0 - lb) * z, F_MIN)), sig, lbs)
    k = _each(lambda z, lb: (1.0 - lb) * (1.0 - z), sig, lbs)
    q = _each(lambda sl, cl: _silu(_f32(q_ref[0, sl, cl])), sls, cls)
    v = _each(lambda sl, cl: v_ref[0, sl, cl], sls, cls)
    b = _each(lambda z: _split_dot(tri_ref[...], z), lf)
    d0 = _each(lambda z: z - ref_rows(z, GROUP_A, GROUP_A // 2 if rev else GROUP_A // 2 - 1), b)
    qb = _each(lambda z: z.astype(BF16), q)
    kb = _each(lambda z: z.astype(BF16), k)
    scores = _each(lambda qq, kk, d: jnp.where(
        diag_mask, _mm_nt(qq * jnp.exp(d).astype(BF16), kk * jnp.exp(-d).astype(BF16)), 0.0), qb, kb, d0)
    for half, mask in zip(halves, level_masks):
        e = _each(lambda z: jnp.exp(-jnp.abs(z - ref_rows(z, 2 * half, half if rev else half - 1))).astype(BF16),
                  b)
        scores = _each(lambda sc, qq, kk, ee: jnp.where(mask, _mm_nt(qq * ee, kk * ee), sc), scores, qb, kb, e)
    o_intra = _each(_mm, scores, v)
    b_end = _each(lambda z: z[0:1, :] if rev else z[n - 1:n, :], b)
    q_in = _each(lambda qq, z: qq * jnp.exp(z), q, b)
    kv = _each(lambda vv, kk, z, ze: _mm_tn(vv, kk * jnp.exp(ze - z)), v, k, b, b_end)
    e_end = _each(jnp.exp, b_end)

    outs = []
    st = [st_ref[hh] for hh in range(HEADS_A)]
    for (sl, cl, hh), oi, qi, kvi, ee in zip(items, o_intra, q_in, kv, e_end):
        outs.append((sl, cl, hh, oi + _mm_nt(qi, st[hh])))
        st[hh] = st[hh] * ee + kvi
    for hh in range(HEADS_A):
        st_ref[hh] = st[hh]

    for sl, cl, hh, o in outs:
        if final:
            o = o + ob_ref[0, sl, cl]
            ms = jnp.mean(o * o, axis=-1, keepdims=True)
            y = o * lax.rsqrt(ms + EPS) * ng_ref[hh] * _silu(_f32(gate_ref[0, sl, cl]))
            y_ref[0, sl, cl] = y.astype(y_ref.dtype)
        else:
            o_ref[0, sl, cl] = o


def _tri_blockdiag(n, blk, rev):
    i = np.arange(n)
    same = (i[:, None] // blk) == (i[None, :] // blk)
    tri = (i[None, :] >= i[:, None]) if rev else (i[None, :] <= i[:, None])
    return jnp.asarray((same & tri).astype(np.float32), dtype=BF16)


def _hgrn_pass(proj3, lb, rev, final, ob=None, norm_g=None):
    b, t, _ = proj3.shape
    tt = _pick(t, 2048)
    nt = t // tt
    tmap = (lambda i: nt - 1 - i) if rev else (lambda i: i)

    wblk = HEADS_A * DKA

    def col(off):
        return pl.BlockSpec((1, tt, wblk), lambda bi, h, ti: (bi, tmap(ti), off // wblk + h))

    head_vec = pl.BlockSpec((HEADS_A, 1, DKA), lambda bi, h, ti: (h, 0, 0))
    in_specs = [col(OFF_AQ), col(OFF_AFB if rev else OFF_AFF), col(OFF_AI), head_vec,
                pl.BlockSpec((LANE, LANE), lambda bi, h, ti: (0, 0))]
    args = [proj3, proj3, proj3, lb.reshape(HA, 1, DKA), _tri_blockdiag(LANE, CHUNK_A, rev)]
    out_block = pl.BlockSpec((1, tt, wblk), lambda bi, h, ti: (bi, tmap(ti), h))
    if final:
        in_specs += [out_block, col(OFF_AG), head_vec]
        args += [ob, proj3, norm_g.reshape(HA, 1, DKA)]
        out_dtype = BF16
    else:
        out_dtype = F32
    return pl.pallas_call(
        functools.partial(_hgrn_kernel, rev, final, tt),
        grid=(b, HA // HEADS_A, nt),
        in_specs=in_specs,
        out_specs=out_block,
        out_shape=jax.ShapeDtypeStruct((b, t, WA), out_dtype),
        scratch_shapes=[pltpu.VMEM((HEADS_A, DKA, DKA), F32)],
        compiler_params=_params("parallel", "parallel", "arbitrary"),
    )(*args)


def _hgrn_mixer(proj3, lbs, norm_g):
    ob = _hgrn_pass(proj3, lbs[1], rev=True, final=False)
    return _hgrn_pass(proj3, lbs[0], rev=False, final=True, ob=ob, norm_g=norm_g)


def _rotary(z, cos2, sin2):
    return z * cos2 + pltpu.roll(z, DKB // 2, axis=1) * sin2


def _ret_state_kernel(nsub, k_ref, v_ref, cos_ref, sin_ref, lg_ref, sb_ref, st_ref):
    @pl.when(pl.program_id(2) == 0)
    def _():
        st_ref[...] = jnp.zeros_like(st_ref)

    cb = CHUNK_B
    lg2 = lg_ref[0]
    lg = lg2[:, :DKB]
    pos = lax.broadcasted_iota(jnp.int32, (cb, DKB), 0).astype(F32)
    dec = jnp.exp(pos * lg) * (DKB ** -0.5)
    order = list(range(nsub - 1, -1, -1))
    sls = [pl.ds(c * cb, cb) for c in order]
    kd = _each(lambda sl: _rotary(_f32(k_ref[0, sl, :]), cos_ref[sl, :], sin_ref[sl, :]) * dec, sls)
    kv = _each(lambda kk, sl: _mm_tn(kk, v_ref[0, sl, :]), kd, sls)
    e_chunk = jnp.exp(cb * lg2)
    st = st_ref[...]
    for c, kvi in zip(order, kv):
        sb_ref[0, 0, c] = st
        st = st * e_chunk + kvi
    st_ref[...] = st


def _ret_out_kernel(nsub, q_ref, k_ref, v_ref, cos_ref, sin_ref, lg_ref, sb_ref, gate_ref, gg_ref,
                    gb_ref, y_ref, st_ref):
    @pl.when(pl.program_id(2) == 0)
    def _():
        st_ref[...] = jnp.zeros_like(st_ref)

    cb = CHUNK_B
    lg2 = lg_ref[0]
    lg = lg2[:, :DKB]
    pos = lax.broadcasted_iota(jnp.int32, (cb, DKB), 0).astype(F32)
    ri = lax.broadcasted_iota(jnp.int32, (cb, cb), 0)
    ci = lax.broadcasted_iota(jnp.int32, (cb, cb), 1)
    decay = jnp.exp(jnp.abs(ri - ci).astype(F32) * lg2)
    dec_fwd = jnp.exp((pos + 1.0) * lg)
    dec_bwd = jnp.exp((cb - pos) * lg)
    dec_key = jnp.exp((cb - 1.0 - pos) * lg)
    e_chunk = jnp.exp(cb * lg2)

    sls = [pl.ds(c * cb, cb) for c in range(nsub)]
    q = _each(lambda sl: _rotary(_f32(q_ref[0, sl, :]), cos_ref[sl, :], sin_ref[sl, :]), sls)
    k = _each(lambda sl: _rotary(_f32(k_ref[0, sl, :]), cos_ref[sl, :], sin_ref[sl, :]) * (DKB ** -0.5), sls)
    v = _each(lambda sl: v_ref[0, sl, :], sls)
    scores = _each(lambda qq, kk: _mm_nt(qq, kk) * decay, q, k)
    o = _each(_mm, scores, v)
    o = _each(lambda oo, qq, c: oo + _mm(qq * dec_bwd, sb_ref[0, 0, c]), o, q, list(range(nsub)))
    kv = _each(lambda kk, vv: _mm_tn(kk * dec_key, vv), k, v)
    q_fwd = _each(lambda qq: qq * dec_fwd, q)
    st = st_ref[...]
    outs = []
    for oo, qf, kvi in zip(o, q_fwd, kv):
        outs.append(oo + _mm(qf, st))
        st = st * e_chunk + kvi
    st_ref[...] = st

    for sl, oo in zip(sls, outs):
        mu = jnp.mean(oo, axis=-1, keepdims=True)
        cen = oo - mu
        var = jnp.mean(cen * cen, axis=-1, keepdims=True)
        y = (cen * lax.rsqrt(var + RET_GN_EPS) * gg_ref[0] + gb_ref[0]) * _silu(_f32(gate_ref[0, sl, :]))
        y_ref[0, sl, :] = y.astype(y_ref.dtype)


def _ret_mixer(proj3, gn_g, gn_b):
    b, t, _ = proj3.shape
    cb = CHUNK_B
    tt = _pick(t, 16 * cb)
    nsub = tt // cb
    nt = t // tt
    half = DKB // 2
    inv = ROPE_BASE ** (-jnp.arange(0, DKB, 2, dtype=F32) / DKB)
    ang_hi = (jnp.arange(t // cb, dtype=F32) * cb)[:, None, None] * inv
    ang_lo = jnp.arange(cb, dtype=F32)[None, :, None] * inv
    cos = (jnp.cos(ang_hi) * jnp.cos(ang_lo) - jnp.sin(ang_hi) * jnp.sin(ang_lo)).reshape(t, half)
    sin = (jnp.sin(ang_hi) * jnp.cos(ang_lo) + jnp.cos(ang_hi) * jnp.sin(ang_lo)).reshape(t, half)
    cos2 = jnp.concatenate([cos, cos], axis=-1)
    sin2 = jnp.concatenate([-sin, sin], axis=-1)
    log_gamma = jnp.log1p(-jnp.exp2(-5.0 - jnp.arange(HB, dtype=F32)))
    lg = jnp.broadcast_to(log_gamma[:, None, None], (HB, 1, DVB))

    def col(off, width, tmap):
        return pl.BlockSpec((1, tt, width), lambda bi, h, c: (bi, tmap(c), off // width + h))

    def tab(tmap):
        return pl.BlockSpec((tt, DKB), lambda bi, h, c: (tmap(c), 0))

    lg_spec = pl.BlockSpec((1, 1, DVB), lambda bi, h, c: (h, 0, 0))
    rmap = lambda c: nt - 1 - c
    fmap = lambda c: c

    sb = pl.pallas_call(
        functools.partial(_ret_state_kernel, nsub),
        grid=(b, HB, nt),
        in_specs=[col(OFF_BK, DKB, rmap), col(OFF_BV, DVB, rmap), tab(rmap), tab(rmap), lg_spec],
        out_specs=pl.BlockSpec((1, 1, nsub, DKB, DVB), lambda bi, h, c: (bi, h, rmap(c), 0, 0)),
        out_shape=jax.ShapeDtypeStruct((b, HB, t // cb, DKB, DVB), F32),
        scratch_shapes=[pltpu.VMEM((DKB, DVB), F32)],
        compiler_params=_params("parallel", "parallel", "arbitrary"),
    )(proj3, proj3, cos2, sin2, lg)

    head_vec = pl.BlockSpec((1, 1, DVB), lambda bi, h, c: (h, 0, 0))
    return pl.pallas_call(
        functools.partial(_ret_out_kernel, nsub),
        grid=(b, HB, nt),
        in_specs=[col(OFF_BQ, DKB, fmap), col(OFF_BK, DKB, fmap), col(OFF_BV, DVB, fmap),
                  tab(fmap), tab(fmap), lg_spec,
                  pl.BlockSpec((1, 1, nsub, DKB, DVB), lambda bi, h, c: (bi, h, c, 0, 0)),
                  col(OFF_BG, DVB, fmap), head_vec, head_vec],
        out_specs=pl.BlockSpec((1, tt, DVB), lambda bi, h, c: (bi, c, h)),
        out_shape=jax.ShapeDtypeStruct((b, t, WB_V), BF16),
        scratch_shapes=[pltpu.VMEM((DKB, DVB), F32)],
        compiler_params=_params("parallel", "parallel", "arbitrary"),
    )(proj3, proj3, proj3, cos2, sin2, lg, sb, proj3,
      gn_g.reshape(HB, 1, DVB), gn_b.reshape(HB, 1, DVB))


def _group_sum(x, gmat):
    cols = [_split_dot_right(x[:, i * LANE:(i + 1) * LANE], gmat) for i in range(x.shape[1] // LANE)]
    return jnp.concatenate(cols, axis=1)


def _rwkv_prep_kernel(tt, r_ref, k_ref, v_ref, c_ref, rp_ref, kp_ref, vp_ref, cp_ref,
                      rn_ref, kn_ref, vn_ref, cn_ref, mur_ref, muk_ref, muv_ref, muc_ref,
                      w0_ref, wl_ref, a0_ref, al_ref, kk_ref, ka_ref, rk_ref, gmat_ref, shift_ref,
                      r_o, v_o, kk_o, bonus_o, lw0_o, lw1_o, k0_o, k1_o, ka0_o, ka1_o):
    ti = pl.program_id(1)
    has_prev = (ti > 0).astype(F32)
    has_next = (ti < pl.num_programs(1) - 1).astype(F32)
    rows = lax.broadcasted_iota(jnp.int32, (tt, 1), 0)

    def shifted(p_ref, prev_ref, next_ref, mu_ref):
        both = jnp.dot(shift_ref[...], p_ref[0].astype(BF16), preferred_element_type=F32)
        p = _f32(p_ref[0])
        prev_row = _f32(prev_ref[0, HALO - 1:HALO, :]) * has_prev
        next_row = _f32(next_ref[0, 0:1, :]) * has_next
        before = jnp.where(rows == 0, prev_row, both[:tt])
        after = jnp.where(rows == tt - 1, next_row, both[tt:])
        return p + mu_ref[0:1, :] * (before - p) + mu_ref[1:2, :] * (after - p)

    r = shifted(r_ref, rp_ref, rn_ref, mur_ref)
    k = shifted(k_ref, kp_ref, kn_ref, muk_ref)
    v = shifted(v_ref, vp_ref, vn_ref, muv_ref)
    codes = shifted(c_ref, cp_ref, cn_ref, muc_ref)
    gmat = gmat_ref[...]

    kraw = k * kk_ref[...]
    kk = kraw * lax.rsqrt(_group_sum(kraw * kraw, gmat) + KK_NORM_EPS)
    r_o[0] = r.astype(r_o.dtype)
    v_o[0] = v.astype(v_o.dtype)
    kk_o[0] = kk.astype(kk_o.dtype)
    tcodes = jnp.tanh(codes)
    bonus = jnp.zeros_like(r)
    for d, (lw_o, kd_o, ka_o) in enumerate(((lw0_o, k0_o, ka0_o), (lw1_o, k1_o, ka1_o))):
        w = w0_ref[d:d + 1, :] + _mm(tcodes, wl_ref[d])
        lw_o[0] = -np.exp(-0.5).astype(np.float32) * _sigmoid(w)
        a = _sigmoid(a0_ref[d:d + 1, :] + _mm(codes, al_ref[d]))
        kd = k * (1.0 + (a - 1.0) * ka_ref[...])
        kd_o[0] = kd.astype(kd_o.dtype)
        ka_o[0] = (kk * a).astype(ka_o.dtype)
        bonus = bonus + _group_sum(r * kd * rk_ref[...], gmat) * v
    bonus_o[0] = bonus


def _rwkv_scan_kernel(rev, final, tt, r_ref, lw_ref, k_ref, v_ref, ka_ref, kk_ref, *rest):
    if final:
        ob_ref, bonus_ref, gate_ref, lng_ref, lnb_ref, y_ref, h_ref = rest
    else:
        o_ref, h_ref = rest

    @pl.when(pl.program_id(2) == 0)
    def _():
        h_ref[...] = jnp.zeros_like(h_ref)

    n2 = 2 * CHUNK_C
    lane = lax.broadcasted_iota(jnp.int32, (1, LANE), 1)
    head0 = lane < NC
    ri = lax.broadcasted_iota(jnp.int32, (n2, n2), 0)
    ci = lax.broadcasted_iota(jnp.int32, (n2, n2), 1)
    strict = (ci > ri) if rev else (ci < ri)
    incl = (ci >= ri) if rev else (ci <= ri)
    eye = ri == ci
    nchunk = tt // CHUNK_C

    def stack(z):
        zero = jnp.zeros_like(z)
        return jnp.concatenate([jnp.where(head0, z, zero), jnp.where(head0, zero, z)], axis=0)

    def stack_b(z):
        return stack(z.astype(BF16))

    order = [(nchunk - 1 - i) if rev else i for i in range(nchunk)]
    items = [(pl.ds(c * CHUNK_C, CHUNK_C), pl.ds(p * LANE, LANE), p) for c in order for p in range(PAIRS_C)]
    rows = [it[0] for it in items]
    cols = [it[1] for it in items]

    lws = _each(lambda sl, cl: lw_ref[0, sl, cl], rows, cols)
    g = _each(lambda lw: _cumsum_rows(lw, rev), lws)
    e_end = _each(lambda gu: jnp.exp(gu[0:1, :] if rev else gu[CHUNK_C - 1:CHUNK_C, :]), g)
    e_neg = _each(lambda gu: jnp.exp(-gu), g)
    bs = _each(lambda sl, cl, gu, lw: stack_b(-_f32(kk_ref[0, sl, cl]) * jnp.exp(gu - lw)), rows, cols, g, lws)
    a_sc = _each(lambda sl, cl, en: _f32(ka_ref[0, sl, cl]) * en, rows, cols, e_neg)
    k_sc = _each(lambda sl, cl, en: _f32(k_ref[0, sl, cl]) * en, rows, cols, e_neg)
    as_ = _each(stack_b, a_sc)
    ks = _each(stack_b, k_sc)
    ake = _each(lambda a, k, e: jnp.concatenate([stack_b(a * e), stack_b(k * e)], axis=0), a_sc, k_sc, e_end)
    rs = _each(lambda sl, cl, gu: stack(_f32(r_ref[0, sl, cl]) * jnp.exp(gu)), rows, cols, g)
    vs = _each(lambda sl, cl: stack_b(v_ref[0, sl, cl]), rows, cols)

    ak = _each(lambda a, k: jnp.concatenate([a, k], axis=0), as_, ks)
    sc_b = _each(_mm_nt, bs, ak)
    sc_r = _each(_mm_nt, rs, ak)
    a_ab = _each(lambda z: jnp.where(strict, z[:, :n2], 0.0), sc_b)
    a_ak = _each(lambda z: jnp.where(strict, z[:, n2:], 0.0), sc_b)
    m_rak = _each(lambda z: jnp.where(jnp.concatenate([incl, incl], axis=1), z, 0.0), sc_r)
    tinv = _each(lambda a: jnp.where(eye, 1.0, 0.0) + a, a_ab)
    pw = _each(lambda a: _mm(a, a), a_ab)
    akv = _each(_mm, a_ak, vs)
    nsteps = int(np.log2(CHUNK_C)) - 1
    for i in range(nsteps):
        if i < nsteps - 1:
            both = _each(lambda t, p: _mm(jnp.concatenate([t, p], axis=0), p), tinv, pw)
            tinv = _each(lambda t, b: t + b[:n2], tinv, both)
            pw = _each(lambda b: b[n2:], both)
        else:
            tinv = _each(lambda t, p: t + _mm(t, p), tinv, pw)
    pwv = _each(lambda t, b, x: _mm(t, jnp.concatenate([b, x.astype(BF16)], axis=1)), tinv, bs, akv)
    low = _each(lambda z, v: jnp.concatenate([z.astype(BF16), jnp.concatenate([jnp.zeros_like(v), v], axis=1)],
                                             axis=0), pwv, vs)
    x1 = _each(_mm, m_rak, low)
    x2 = _each(_mm_tn, ake, low)
    top = _each(lambda r, z, e, y: jnp.concatenate([r + z[:, :LANE], jnp.where(eye, e, 0.0) + y[:, :LANE]], axis=0),
                rs, x1, e_end, x2)
    o_in = _each(lambda z: z[:, LANE:], x1)
    dmat = _each(lambda y: y[:, LANE:], x2)

    outs = []
    h = [h_ref[p] for p in range(PAIRS_C)]
    for (sl, cl, p), tp, oi, dm in zip(items, top, o_in, dmat):
        both = _mm(tp, h[p])
        h[p] = both[n2:] + dm
        os_ = both[:n2] + oi
        outs.append((sl, cl, os_[:CHUNK_C, :] + os_[CHUNK_C:, :]))
    for p in range(PAIRS_C):
        h_ref[p] = h[p]

    for sl, cl, o in outs:
        if final:
            o = o + ob_ref[0, sl, cl]
            inv_n = 1.0 / NC
            s0 = jnp.sum(jnp.where(head0, o, 0.0), axis=-1, keepdims=True)
            s1 = jnp.sum(jnp.where(head0, 0.0, o), axis=-1, keepdims=True)
            cen = o - jnp.where(head0, s0, s1) * inv_n
            c2 = cen * cen
            v0 = jnp.sum(jnp.where(head0, c2, 0.0), axis=-1, keepdims=True)
            v1 = jnp.sum(jnp.where(head0, 0.0, c2), axis=-1, keepdims=True)
            var = jnp.where(head0, v0, v1) * inv_n
            y = cen * lax.rsqrt(var + RWKV_GN_EPS) * lng_ref[:, cl] + lnb_ref[:, cl] + bonus_ref[0, sl, cl]
            y_ref[0, sl, cl] = (y * _silu(_f32(gate_ref[0, sl, cl]))).astype(y_ref.dtype)
        else:
            o_ref[0, sl, cl] = o


def _rwkv_mixer(proj3, mu, w0, w_lora_b, a0, a_lora_b, k_k, k_a, r_k, ln_g, ln_b):
    b, t, _ = proj3.shape
    tt = _pick(t, 256)
    nt = t // tt
    sub = tt // HALO

    def cur(off, width):
        return pl.BlockSpec((1, tt, width), lambda bi, ti: (bi, ti, off // width))

    def prev(off, width):
        return pl.BlockSpec((1, HALO, width),
                            lambda bi, ti: (bi, jnp.maximum(ti * sub - 1, 0), off // width))

    def nxt(off, width):
        return pl.BlockSpec((1, HALO, width),
                            lambda bi, ti: (bi, jnp.minimum((ti + 1) * sub, nt * sub - 1), off // width))

    def full(shape):
        return pl.BlockSpec(shape, lambda bi, ti: (0,) * len(shape))

    segs = ((OFF_CR, WC), (OFF_CK, WC), (OFF_CV, WC), (OFF_CC, N_CODES))
    in_specs = ([cur(o, w) for o, w in segs] + [prev(o, w) for o, w in segs] + [nxt(o, w) for o, w in segs]
                + [full((2, WC))] * 3 + [full((2, N_CODES))]
                + [full((2, WC)), full((2, N_CODES, WC)), full((2, WC)), full((2, N_CODES, WC))]
                + [full((1, WC))] * 3 + [full((LANE, LANE)), full((2 * tt, tt))])
    wl = jnp.zeros((2, N_CODES, WC), F32)
    al = jnp.zeros((2, N_CODES, WC), F32)
    for d in range(2):
        wl = wl.at[d, d * W_LORA:(d + 1) * W_LORA].set(w_lora_b[d])
        al = al.at[d, 2 * W_LORA + d * A_LORA:2 * W_LORA + (d + 1) * A_LORA].set(a_lora_b[d])
    row_i = np.arange(tt)
    shift = np.concatenate([row_i[:, None] - 1 == row_i[None, :], row_i[:, None] + 1 == row_i[None, :]], axis=0)
    shift = jnp.asarray(shift.astype(np.float32), dtype=BF16)
    lane_i = np.arange(LANE)
    gmat = jnp.asarray((lane_i[:, None] // NC == lane_i[None, :] // NC).astype(np.float32), dtype=BF16)
    row_out = pl.BlockSpec((1, tt, WC), lambda bi, ti: (bi, ti, 0))
    outs = pl.pallas_call(
        functools.partial(_rwkv_prep_kernel, tt),
        grid=(b, nt),
        in_specs=in_specs,
        out_specs=[row_out] * 10,
        out_shape=[jax.ShapeDtypeStruct((b, t, WC), dt) for dt in (SCAN_DTYPE,) * 3 + (F32,) * 3 + (SCAN_DTYPE,) * 4],
        compiler_params=_params("parallel", "parallel"),
    )(*([proj3] * 12), mu[:, :WC], mu[:, WC:2 * WC], mu[:, 2 * WC:3 * WC], mu[:, 3 * WC:],
      w0, wl.astype(BF16), a0, al.astype(BF16), k_k.reshape(1, WC), k_a.reshape(1, WC),
      r_k.reshape(1, WC), gmat, shift)
    r, v, kk, bonus, lw0, lw1, k0, k1, ka0, ka1 = outs

    ts = _pick(t, 512)
    ns = t // ts

    def scan(rev, final, lw, kd, ka, ob=None):
        wblk = PAIRS_C * LANE
        tmap = (lambda i: ns - 1 - i) if rev else (lambda i: i)
        blk = pl.BlockSpec((1, ts, wblk), lambda bi, h, ti: (bi, tmap(ti), h))
        head_vec = pl.BlockSpec((1, wblk), lambda bi, h, ti: (0, h))
        in_specs = [blk] * 6
        args = [r, lw, kd, v, ka, kk]
        if final:
            gate = pl.BlockSpec((1, ts, wblk), lambda bi, h, ti: (bi, tmap(ti), OFF_CG // wblk + h))
            in_specs += [blk, blk, gate, head_vec, head_vec]
            args += [ob, bonus, proj3, ln_g.reshape(1, WC), ln_b.reshape(1, WC)]
        return pl.pallas_call(
            functools.partial(_rwkv_scan_kernel, rev, final, ts),
            grid=(b, WC // wblk, ns),
            in_specs=in_specs,
            out_specs=blk,
            out_shape=jax.ShapeDtypeStruct((b, t, WC), BF16 if final else F32),
            scratch_shapes=[pltpu.VMEM((PAIRS_C, LANE, LANE), F32)],
            compiler_params=_params("parallel", "parallel", "arbitrary"),
        )(*args)

    ob = scan(True, False, lw1, k1, ka1)
    return scan(False, True, lw0, k0, ka0, ob=ob)


def _merge_kernel(final, ya_ref, yb_ref, yc_ref, mg_ref, x_ref, wa_ref, wb_ref, wc_ref, wo_ref,
                  fg_ref, o_ref):
    za = jnp.dot(ya_ref[...], wa_ref[...], preferred_element_type=F32)
    zb = jnp.dot(yb_ref[...], wb_ref[...], preferred_element_type=F32)
    zc = jnp.dot(yc_ref[...], wc_ref[...], preferred_element_type=F32)
    mg = _f32(mg_ref[...])
    mixed = (_sigmoid(mg[:, :D_MODEL]) * za
             + _sigmoid(mg[:, D_MODEL:2 * D_MODEL]) * zb
             + _sigmoid(mg[:, 2 * D_MODEL:]) * zc)
    out = x_ref[...] + jnp.dot(mixed.astype(BF16), wo_ref[...], preferred_element_type=F32)
    if final:
        ms = jnp.mean(out * out, axis=-1, keepdims=True)
        out = out * lax.rsqrt(ms + EPS) * fg_ref[...]
    o_ref[...] = out


def _merge(ya, yb, yc, proj, x2, wa, wb, wc, wo, fg, final):
    m = x2.shape[0]
    tm = _pick(m, 512)

    def rows(width):
        return pl.BlockSpec((tm, width), lambda i: (i, 0))

    def full(shape):
        return pl.BlockSpec(shape, lambda i: (0, 0))

    return pl.pallas_call(
        functools.partial(_merge_kernel, final),
        grid=(m // tm,),
        in_specs=[rows(WA), rows(WB_V), rows(WC), rows(3 * D_MODEL), rows(D_MODEL),
                  full((WA, D_MODEL)), full((WB_V, D_MODEL)), full((WC, D_MODEL)),
                  full((D_MODEL, D_MODEL)), full((1, D_MODEL))],
        out_specs=rows(D_MODEL),
        out_shape=jax.ShapeDtypeStruct((m, D_MODEL), F32),
        compiler_params=_params("parallel", vmem=PROJ_VMEM_LIMIT),
    )(ya, yb, yc, proj, x2, wa.astype(BF16), wb.astype(BF16), wc.astype(BF16), wo.astype(BF16),
      fg.reshape(1, D_MODEL))


def kernel(x, norm_g, w_in, hgrn_lb_logits, hgrn_norm_g, ret_norm_g, ret_norm_b, rwkv_mu, rwkv_w0,
           rwkv_w_lora_b, rwkv_a0, rwkv_a_lora_b, rwkv_k_k, rwkv_k_a, rwkv_r_k, rwkv_ln_g,
           rwkv_ln_b, w_branch_a, w_branch_b, w_branch_c, w_out, final_norm_g):
    b, t, d = x.shape
    m = b * t
    depth = w_in.shape[0]
    p_lb = jax.nn.softmax(hgrn_lb_logits.astype(F32), axis=0)
    lbs = jnp.cumsum(p_lb, axis=0) - p_lb[0:1]
    x2 = x.reshape(m, d)
    for l in range(depth):
        proj = _inproj(x2, norm_g[l], _permute_w_in(w_in, l))
        proj3 = proj.reshape(b, t, N_PAD)
        ya = _hgrn_mixer(proj3, lbs[l], hgrn_norm_g[l])
        yb = _ret_mixer(proj3, ret_norm_g[l], ret_norm_b[l])
        yc = _rwkv_mixer(proj3, rwkv_mu[l], rwkv_w0[l], rwkv_w_lora_b[l], rwkv_a0[l],
                         rwkv_a_lora_b[l], rwkv_k_k[l], rwkv_k_a[l], rwkv_r_k[l],
                         rwkv_ln_g[l], rwkv_ln_b[l])
        x2 = _merge(ya.reshape(m, WA), yb.reshape(m, WB_V), yc.reshape(m, WC), proj, x2,
                    w_branch_a[l], w_branch_b[l], w_branch_c[l], w_out[l], final_norm_g,
                    final=(l == depth - 1))
    return x2.reshape(b, t, d)
```

```python
import functools

import numpy as np
import jax
import jax.numpy as jnp
from jax import lax
from jax.experimental import pallas as pl
from jax.experimental.pallas import tpu as pltpu

F32 = jnp.float32
BF16 = jnp.bfloat16

D_MODEL = 1024
EPS = 1e-6
F_MIN = 1e-6
DKA = 128
HA = D_MODEL // DKA
WA = HA * DKA
HB = 8
DKB = D_MODEL // HB
DVB = 2 * DKB
WB_QK = HB * DKB
WB_V = HB * DVB
ROPE_BASE = 10000.0
RET_GN_EPS = 1e-5
NC = 64
HC = D_MODEL // NC
WC = HC * NC
W_LORA = 64
A_LORA = 64
RWKV_GN_EPS = 64e-5
KK_NORM_EPS = 1e-12
N_CODES = 2 * W_LORA + 2 * A_LORA
C_SHIFT = 3 * WC + N_CODES

SRC_B = 5 * WA
SRC_CS = SRC_B + 2 * WB_QK + 2 * WB_V
SRC_CG = SRC_CS + C_SHIFT
SRC_MERGE = SRC_CG + WC
N_IN = SRC_MERGE + 3 * D_MODEL

OFF_MERGE = 0
OFF_AQ = OFF_MERGE + 3 * D_MODEL
OFF_AFF = OFF_AQ + WA
OFF_AFB = OFF_AFF + WA
OFF_AI = OFF_AFB + WA
OFF_AG = OFF_AI + WA
OFF_BQ = OFF_AG + WA
OFF_BK = OFF_BQ + WB_QK
OFF_BV = OFF_BK + WB_QK
OFF_BG = OFF_BV + WB_V
OFF_CG = OFF_BG + WB_V
OFF_CR = OFF_CG + WC
OFF_CK = OFF_CR + WC
OFF_CV = OFF_CK + WC
OFF_CC = OFF_CV + WC
LANE = 128
PROJ_DTYPE = BF16
SCAN_DTYPE = BF16
HALO = 16
PROJ_SPLIT = 4
N_PAD = -(-(OFF_CC + N_CODES) // (PROJ_SPLIT * LANE)) * PROJ_SPLIT * LANE
PROJ_TN = N_PAD // PROJ_SPLIT
MXU_N = 256
WPERM_TN = 256

CHUNK_A = 128
GROUP_A = 8
HEADS_A = 2
CHUNK_B = 256
CHUNK_C = 64
PAIRS_C = 4
VMEM_LIMIT = 48 * 1024 * 1024
PROJ_VMEM_LIMIT = 56 * 1024 * 1024

NT_DIMS = (((1,), (1,)), ((), ()))
TN_DIMS = (((0,), (0,)), ((), ()))


def _mm(a, b):
    return jnp.dot(a.astype(BF16), b.astype(BF16), preferred_element_type=F32)


def _mm_nt(a, b):
    return lax.dot_general(a.astype(BF16), b.astype(BF16), NT_DIMS, preferred_element_type=F32)


def _mm_tn(a, b):
    return lax.dot_general(a.astype(BF16), b.astype(BF16), TN_DIMS, preferred_element_type=F32)


def _split_dot(ones_mat, x):
    hi = x.astype(BF16)
    lo = (x - hi.astype(F32)).astype(BF16)
    return jnp.dot(ones_mat, hi, preferred_element_type=F32) + jnp.dot(ones_mat, lo, preferred_element_type=F32)


def _split_dot_right(x, ones_mat):
    hi = x.astype(BF16)
    lo = (x - hi.astype(F32)).astype(BF16)
    return jnp.dot(hi, ones_mat, preferred_element_type=F32) + jnp.dot(lo, ones_mat, preferred_element_type=F32)


def _f32(z):
    return z.astype(F32)


def _sigmoid(z):
    return 0.5 * jnp.tanh(0.5 * z) + 0.5


def _cumsum_rows(x, rev):
    n = x.shape[0]
    rows = lax.broadcasted_iota(jnp.int32, (n, 1), 0)
    step = 1
    while step < n:
        if rev:
            x = x + jnp.where(rows < n - step, pltpu.roll(x, n - step, axis=0), 0.0)
        else:
            x = x + jnp.where(rows >= step, pltpu.roll(x, step, axis=0), 0.0)
        step *= 2
    return x


def _silu(z):
    return z * _sigmoid(z)


def _pick(n, pref):
    t = min(n, pref)
    assert n % t == 0, (n, pref)
    return t


def _params(*sem, vmem=VMEM_LIMIT):
    return pltpu.CompilerParams(dimension_semantics=sem, vmem_limit_bytes=vmem)


def _inproj_kernel(x_ref, g_ref, w_ref, o_ref, h_ref):
    @pl.when(pl.program_id(1) == 0)
    def _():
        x = x_ref[...]
        ms = jnp.mean(x * x, axis=-1, keepdims=True)
        h_ref[...] = (x * lax.rsqrt(ms + EPS) * g_ref[...]).astype(BF16)

    def tile(cols):
        o_ref[:, cols] = jnp.dot(h_ref[...], w_ref[:, cols], preferred_element_type=F32).astype(o_ref.dtype)

    for start in range(0, PROJ_TN, MXU_N):
        tile(pl.ds(start, min(MXU_N, PROJ_TN - start)))


def _inproj(x2, g, w_bf16):
    m = x2.shape[0]
    tm = _pick(m, 1024)
    return pl.pallas_call(
        _inproj_kernel,
        grid=(m // tm, N_PAD // PROJ_TN),
        in_specs=[
            pl.BlockSpec((tm, D_MODEL), lambda i, j: (i, 0)),
            pl.BlockSpec((1, D_MODEL), lambda i, j: (0, 0)),
            pl.BlockSpec((D_MODEL, PROJ_TN), lambda i, j: (0, j)),
        ],
        out_specs=pl.BlockSpec((tm, PROJ_TN), lambda i, j: (i, j)),
        out_shape=jax.ShapeDtypeStruct((m, N_PAD), PROJ_DTYPE),
        scratch_shapes=[pltpu.VMEM((tm, D_MODEL), BF16)],
        compiler_params=_params("parallel", "arbitrary", vmem=PROJ_VMEM_LIMIT),
    )(x2, g.reshape(1, D_MODEL), w_bf16)


def _wperm_kernel(nvalid, a_ref, b_ref, o_ref):
    j = pl.program_id(0)
    for idx, src in enumerate((a_ref, b_ref)):
        cols = pl.ds(idx * WPERM_TN, WPERM_TN)
        blk = src[...].astype(BF16)
        o_ref[:, cols] = jnp.where(2 * j + idx < nvalid, blk, jnp.zeros_like(blk))


def _src_block(jd):
    n_merge = (N_IN - SRC_MERGE) // WPERM_TN
    n_ab = SRC_CS // WPERM_TN
    n_cg = (SRC_MERGE - SRC_CG) // WPERM_TN
    n_src = N_IN // WPERM_TN
    return jnp.where(jd < n_merge, jd + SRC_MERGE // WPERM_TN,
                     jnp.where(jd < n_merge + n_ab, jd - n_merge,
                               jnp.where(jd < n_merge + n_ab + n_cg, jd - n_merge - n_ab + SRC_CG // WPERM_TN,
                                         jnp.minimum(jd - n_merge - n_cg, n_src - 1))))


def _permute_w_in(w_in, layer):
    assert all(off % WPERM_TN == 0 for off in (SRC_CS, SRC_CG, SRC_MERGE, N_IN)) and N_PAD % (2 * WPERM_TN) == 0

    def src(idx):
        return pl.BlockSpec((None, D_MODEL, WPERM_TN), lambda j: (layer, 0, _src_block(2 * j + idx)))

    return pl.pallas_call(
        functools.partial(_wperm_kernel, N_IN // WPERM_TN),
        grid=(N_PAD // (2 * WPERM_TN),),
        in_specs=[src(0), src(1)],
        out_specs=pl.BlockSpec((D_MODEL, 2 * WPERM_TN), lambda j: (0, j)),
        out_shape=jax.ShapeDtypeStruct((D_MODEL, N_PAD), BF16),
        compiler_params=_params("parallel"),
    )(w_in, w_in)


def _each(fn, *lists):
    return [fn(*items) for items in zip(*lists)]


def _hgrn_kernel(rev, final, tt, q_ref, f_ref, v_ref, lb_ref, tri_ref, *rest):
    if final:
        ob_ref, gate_ref, ng_ref, y_ref, st_ref = rest
    else:
        o_ref, st_ref = rest

    @pl.when(pl.program_id(2) == 0)
    def _():
        st_ref[...] = jnp.zeros_like(st_ref)

    n = CHUNK_A
    ri = lax.broadcasted_iota(jnp.int32, (n, n), 0)
    ci = lax.broadcasted_iota(jnp.int32, (n, n), 1)
    g_bits = int(np.log2(GROUP_A))
    diag_mask = ((ri >> g_bits) == (ci >> g_bits)) & ((ci >= ri) if rev else (ci <= ri))
    halves = [GROUP_A << i for i in range(int(np.log2(n // GROUP_A)))]
    level_masks = []
    for half in halves:
        hb = int(np.log2(half))
        q_side, k_side = (0, 1) if rev else (1, 0)
        level_masks.append(((ri >> (hb + 1)) == (ci >> (hb + 1)))
                           & (((ri >> hb) & 1) == q_side) & (((ci >> hb) & 1) == k_side))

    def ref_rows(b, blk, idx):
        picked = b.reshape(n // blk, blk, DKA)[:, idx:idx + 1, :]
        return jnp.broadcast_to(picked, (n // blk, blk, DKA)).reshape(n, DKA)

    nchunk = tt // n
    order = [(nchunk - 1 - i) if rev else i for i in range(nchunk)]
    items = [(pl.ds(c * n, n), pl.ds(hh * DKA, DKA), hh) for c in order for hh in range(HEADS_A)]
    sls = [it[0] for it in items]
    cls = [it[1] for it in items]
    lbs = [lb_ref[it[2]] for it in items]
    sig = _each(lambda sl, cl: _sigmoid(_f32(f_ref[0, sl, cl])), sls, cls)
    lf = _each(lambda z, lb: jnp.log2(jnp.maximum(lb + (1.0 - lb) * z, F_MIN)), sig, lbs)
    k = _each(lambda z, lb: (1.0 - lb) * (1.0 - z), sig, lbs)
    q = _each(lambda sl, cl: _silu(_f32(q_ref[0, sl, cl])), sls, cls)
    v = _each(lambda sl, cl: v_ref[0, sl, cl], sls, cls)
    b = _each(lambda z: _split_dot(tri_ref[...], z), lf)
    d0 = _each(lambda z: z - ref_rows(z, GROUP_A, GROUP_A // 2 if rev else GROUP_A // 2 - 1), b)
    qb = _each(lambda z: z.astype(BF16), q)
    kb = _each(lambda z: z.astype(BF16), k)
    scores = _each(lambda qq, kk, d: jnp.where(
        diag_mask, _mm_nt(qq * jnp.exp2(d).astype(BF16), kk * jnp.exp2(-d).astype(BF16)), 0.0), qb, kb, d0)
    for half, mask in zip(halves, level_masks):
        e = _each(lambda z: jnp.exp2(-jnp.abs(z - ref_rows(z, 2 * half, half if rev else half - 1))).astype(BF16),
                  b)
        scores = _each(lambda sc, qq, kk, ee: jnp.where(mask, _mm_nt(qq * ee, kk * ee), sc), scores, qb, kb, e)
    o_intra = _each(_mm, scores, v)
    b_end = _each(lambda z: z[0:1, :] if rev else z[n - 1:n, :], b)
    q_in = _each(lambda qq, z: qq * jnp.exp2(z), q, b)
    kv = _each(lambda vv, kk, z, ze: _mm_tn(vv, kk * jnp.exp2(ze - z)), v, k, b, b_end)
    e_end = _each(jnp.exp2, b_end)

    outs = []
    st = [st_ref[hh] for hh in range(HEADS_A)]
    for (sl, cl, hh), oi, qi, kvi, ee in zip(items, o_intra, q_in, kv, e_end):
        outs.append((sl, cl, hh, oi + _mm_nt(qi, st[hh])))
        st[hh] = st[hh] * ee + kvi
    for hh in range(HEADS_A):
        st_ref[hh] = st[hh]

    for sl, cl, hh, o in outs:
        if final:
            o = o + ob_ref[0, sl, cl]
            ms = jnp.mean(o * o, axis=-1, keepdims=True)
            y = o * lax.rsqrt(ms + EPS) * ng_ref[hh] * _silu(_f32(gate_ref[0, sl, cl]))
            y_ref[0, sl, cl] = y.astype(y_ref.dtype)
        else:
            o_ref[0, sl, cl] = o


def _tri_blockdiag(n, blk, rev):
    i = np.arange(n)
    same = (i[:, None] // blk) == (i[None, :] // blk)
    tri = (i[None, :] >= i[:, None]) if rev else (i[None, :] <= i[:, None])
    return jnp.asarray((same & tri).astype(np.float32), dtype=BF16)


def _hgrn_pass(proj3, lb, rev, final, ob=None, norm_g=None):
    b, t, _ = proj3.shape
    tt = _pick(t, 2048)
    nt = t // tt
    tmap = (lambda i: nt - 1 - i) if rev else (lambda i: i)

    wblk = HEADS_A * DKA

    def col(off):
        return pl.BlockSpec((1, tt, wblk), lambda bi, h, ti: (bi, tmap(ti), off // wblk + h))

    head_vec = pl.BlockSpec((HEADS_A, 1, DKA), lambda bi, h, ti: (h, 0, 0))
    in_specs = [col(OFF_AQ), col(OFF_AFB if rev else OFF_AFF), col(OFF_AI), head_vec,
                pl.BlockSpec((LANE, LANE), lambda bi, h, ti: (0, 0))]
    args = [proj3, proj3, proj3, lb.reshape(HA, 1, DKA), _tri_blockdiag(LANE, CHUNK_A, rev)]
    out_block = pl.BlockSpec((1, tt, wblk), lambda bi, h, ti: (bi, tmap(ti), h))
    if final:
        in_specs += [out_block, col(OFF_AG), head_vec]
        args += [ob, proj3, norm_g.reshape(HA, 1, DKA)]
        out_dtype = BF16
    else:
        out_dtype = F32
    return pl.pallas_call(
        functools.partial(_hgrn_kernel, rev, final, tt),
        grid=(b, HA // HEADS_A, nt),
        in_specs=in_specs,
        out_specs=out_block,
        out_shape=jax.ShapeDtypeStruct((b, t, WA), out_dtype),
        scratch_shapes=[pltpu.VMEM((HEADS_A, DKA, DKA), F32)],
        compiler_params=_params("parallel", "parallel", "arbitrary"),
    )(*args)


def _hgrn_mixer(proj3, lbs, norm_g):
    ob = _hgrn_pass(proj3, lbs[1], rev=True, final=False)
    return _hgrn_pass(proj3, lbs[0], rev=False, final=True, ob=ob, norm_g=norm_g)


def _rotary(z, cos2, sin2):
    return z * cos2 + pltpu.roll(z, DKB // 2, axis=1) * sin2


def _ret_state_kernel(nsub, k_ref, v_ref, cos_ref, sin_ref, lg_ref, sb_ref, st_ref):
    @pl.when(pl.program_id(2) == 0)
    def _():
        st_ref[...] = jnp.zeros_like(st_ref)

    cb = CHUNK_B
    lg2 = lg_ref[0]
    lg = lg2[:, :DKB]
    pos = lax.broadcasted_iota(jnp.int32, (cb, DKB), 0).astype(F32)
    dec = jnp.exp(pos * lg) * (DKB ** -0.5)
    order = list(range(nsub - 1, -1, -1))
    sls = [pl.ds(c * cb, cb) for c in order]
    kd = _each(lambda sl: _rotary(_f32(k_ref[0, sl, :]), cos_ref[sl, :], sin_ref[sl, :]) * dec, sls)
    kv = _each(lambda kk, sl: _mm_tn(kk, v_ref[0, sl, :]), kd, sls)
    e_chunk = jnp.exp(cb * lg2)
    st = st_ref[...]
    for c, kvi in zip(order, kv):
        sb_ref[0, 0, c] = st
        st = st * e_chunk + kvi
    st_ref[...] = st


def _ret_out_kernel(nsub, q_ref, k_ref, v_ref, cos_ref, sin_ref, lg_ref, sb_ref, gate_ref, gg_ref,
                    gb_ref, y_ref, st_ref):
    @pl.when(pl.program_id(2) == 0)
    def _():
        st_ref[...] = jnp.zeros_like(st_ref)

    cb = CHUNK_B
    lg2 = lg_ref[0]
    lg = lg2[:, :DKB]
    pos = lax.broadcasted_iota(jnp.int32, (cb, DKB), 0).astype(F32)
    ri = lax.broadcasted_iota(jnp.int32, (cb, cb), 0)
    ci = lax.broadcasted_iota(jnp.int32, (cb, cb), 1)
    decay = jnp.exp(jnp.abs(ri - ci).astype(F32) * lg2)
    dec_fwd = jnp.exp((pos + 1.0) * lg)
    dec_bwd = jnp.exp((cb - pos) * lg)
    dec_key = jnp.exp((cb - 1.0 - pos) * lg)
    e_chunk = jnp.exp(cb * lg2)

    sls = [pl.ds(c * cb, cb) for c in range(nsub)]
    q = _each(lambda sl: _rotary(_f32(q_ref[0, sl, :]), cos_ref[sl, :], sin_ref[sl, :]), sls)
    k = _each(lambda sl: _rotary(_f32(k_ref[0, sl, :]), cos_ref[sl, :], sin_ref[sl, :]) * (DKB ** -0.5), sls)
    v = _each(lambda sl: v_ref[0, sl, :], sls)
    scores = _each(lambda qq, kk: _mm_nt(qq, kk) * decay, q, k)
    o = _each(_mm, scores, v)
    o = _each(lambda oo, qq, c: oo + _mm(qq * dec_bwd, sb_ref[0, 0, c]), o, q, list(range(nsub)))
    kv = _each(lambda kk, vv: _mm_tn(kk * dec_key, vv), k, v)
    q_fwd = _each(lambda qq: qq * dec_fwd, q)
    st = st_ref[...]
    outs = []
    for oo, qf, kvi in zip(o, q_fwd, kv):
        outs.append(oo + _mm(qf, st))
        st = st * e_chunk + kvi
    st_ref[...] = st

    for sl, oo in zip(sls, outs):
        mu = jnp.mean(oo, axis=-1, keepdims=True)
        cen = oo - mu
        var = jnp.mean(cen * cen, axis=-1, keepdims=True)
        y = (cen * lax.rsqrt(var + RET_GN_EPS) * gg_ref[0] + gb_ref[0]) * _silu(_f32(gate_ref[0, sl, :]))
        y_ref[0, sl, :] = y.astype(y_ref.dtype)


def _ret_mixer(proj3, gn_g, gn_b):
    b, t, _ = proj3.shape
    cb = CHUNK_B
    tt = _pick(t, 16 * cb)
    nsub = tt // cb
    nt = t // tt
    half = DKB // 2
    inv = ROPE_BASE ** (-jnp.arange(0, DKB, 2, dtype=F32) / DKB)
    ang_hi = (jnp.arange(t // cb, dtype=F32) * cb)[:, None, None] * inv
    ang_lo = jnp.arange(cb, dtype=F32)[None, :, None] * inv
    cos = (jnp.cos(ang_hi) * jnp.cos(ang_lo) - jnp.sin(ang_hi) * jnp.sin(ang_lo)).reshape(t, half)
    sin = (jnp.sin(ang_hi) * jnp.cos(ang_lo) + jnp.cos(ang_hi) * jnp.sin(ang_lo)).reshape(t, half)
    cos2 = jnp.concatenate([cos, cos], axis=-1)
    sin2 = jnp.concatenate([-sin, sin], axis=-1)
    log_gamma = jnp.log1p(-jnp.exp2(-5.0 - jnp.arange(HB, dtype=F32)))
    lg = jnp.broadcast_to(log_gamma[:, None, None], (HB, 1, DVB))

    def col(off, width, tmap):
        return pl.BlockSpec((1, tt, width), lambda bi, h, c: (bi, tmap(c), off // width + h))

    def tab(tmap):
        return pl.BlockSpec((tt, DKB), lambda bi, h, c: (tmap(c), 0))

    lg_spec = pl.BlockSpec((1, 1, DVB), lambda bi, h, c: (h, 0, 0))
    rmap = lambda c: nt - 1 - c
    fmap = lambda c: c

    sb = pl.pallas_call(
        functools.partial(_ret_state_kernel, nsub),
        grid=(b, HB, nt),
        in_specs=[col(OFF_BK, DKB, rmap), col(OFF_BV, DVB, rmap), tab(rmap), tab(rmap), lg_spec],
        out_specs=pl.BlockSpec((1, 1, nsub, DKB, DVB), lambda bi, h, c: (bi, h, rmap(c), 0, 0)),
        out_shape=jax.ShapeDtypeStruct((b, HB, t // cb, DKB, DVB), F32),
        scratch_shapes=[pltpu.VMEM((DKB, DVB), F32)],
        compiler_params=_params("parallel", "parallel", "arbitrary"),
    )(proj3, proj3, cos2, sin2, lg)

    head_vec = pl.BlockSpec((1, 1, DVB), lambda bi, h, c: (h, 0, 0))
    return pl.pallas_call(
        functools.partial(_ret_out_kernel, nsub),
        grid=(b, HB, nt),
        in_specs=[col(OFF_BQ, DKB, fmap), col(OFF_BK, DKB, fmap), col(OFF_BV, DVB, fmap),
                  tab(fmap), tab(fmap), lg_spec,
                  pl.BlockSpec((1, 1, nsub, DKB, DVB), lambda bi, h, c: (bi, h, c, 0, 0)),
                  col(OFF_BG, DVB, fmap), head_vec, head_vec],
        out_specs=pl.BlockSpec((1, tt, DVB), lambda bi, h, c: (bi, c, h)),
        out_shape=jax.ShapeDtypeStruct((b, t, WB_V), BF16),
        scratch_shapes=[pltpu.VMEM((DKB, DVB), F32)],
        compiler_params=_params("parallel", "parallel", "arbitrary"),
    )(proj3, proj3, proj3, cos2, sin2, lg, sb, proj3,
      gn_g.reshape(HB, 1, DVB), gn_b.reshape(HB, 1, DVB))


def _group_sum(x, gmat):
    cols = [_split_dot_right(x[:, i * LANE:(i + 1) * LANE], gmat) for i in range(x.shape[1] // LANE)]
    return jnp.concatenate(cols, axis=1)


def _rwkv_prep_kernel(tt, r_ref, k_ref, v_ref, c_ref, rp_ref, kp_ref, vp_ref, cp_ref,
                      rn_ref, kn_ref, vn_ref, cn_ref, mur_ref, muk_ref, muv_ref, muc_ref,
                      w0_ref, wl_ref, a0_ref, al_ref, kk_ref, ka_ref, rk_ref, gmat_ref, shift_ref,
                      r_o, v_o, kk_o, bonus_o, lw0_o, lw1_o, k0_o, k1_o, ka0_o, ka1_o):
    ti = pl.program_id(1)
    has_prev = (ti > 0).astype(F32)
    has_next = (ti < pl.num_programs(1) - 1).astype(F32)
    rows = lax.broadcasted_iota(jnp.int32, (tt, 1), 0)

    def shifted(p_ref, prev_ref, next_ref, mu_ref):
        both = jnp.dot(shift_ref[...], p_ref[0].astype(BF16), preferred_element_type=F32)
        p = _f32(p_ref[0])
        prev_row = _f32(prev_ref[0, HALO - 1:HALO, :]) * has_prev
        next_row = _f32(next_ref[0, 0:1, :]) * has_next
        before = jnp.where(rows == 0, prev_row, both[:tt])
        after = jnp.where(rows == tt - 1, next_row, both[tt:])
        return p + mu_ref[0:1, :] * (before - p) + mu_ref[1:2, :] * (after - p)

    r = shifted(r_ref, rp_ref, rn_ref, mur_ref)
    k = shifted(k_ref, kp_ref, kn_ref, muk_ref)
    v = shifted(v_ref, vp_ref, vn_ref, muv_ref)
    codes = shifted(c_ref, cp_ref, cn_ref, muc_ref)
    gmat = gmat_ref[...]

    kraw = k * kk_ref[...]
    kk = kraw * lax.rsqrt(_group_sum(kraw * kraw, gmat) + KK_NORM_EPS)
    r_o[0] = r.astype(r_o.dtype)
    v_o[0] = v.astype(v_o.dtype)
    kk_o[0] = kk.astype(kk_o.dtype)
    tcodes = jnp.tanh(codes)
    bonus = jnp.zeros_like(r)
    for d, (lw_o, kd_o, ka_o) in enumerate(((lw0_o, k0_o, ka0_o), (lw1_o, k1_o, ka1_o))):
        w = w0_ref[d:d + 1, :] + _mm(tcodes, wl_ref[d])
        lw_o[0] = np.float32(-np.exp(-0.5) * np.log2(np.e)) * _sigmoid(w)
        a = _sigmoid(a0_ref[d:d + 1, :] + _mm(codes, al_ref[d]))
        kd = k * (1.0 + (a - 1.0) * ka_ref[...])
        kd_o[0] = kd.astype(kd_o.dtype)
        ka_o[0] = (kk * a).astype(ka_o.dtype)
        bonus = bonus + _group_sum(r * kd * rk_ref[...], gmat) * v
    bonus_o[0] = bonus


def _rwkv_scan_kernel(rev, final, tt, r_ref, lw_ref, k_ref, v_ref, ka_ref, kk_ref, *rest):
    if final:
        ob_ref, bonus_ref, gate_ref, lng_ref, lnb_ref, y_ref, h_ref = rest
    else:
        o_ref, h_ref = rest

    @pl.when(pl.program_id(2) == 0)
    def _():
        h_ref[...] = jnp.zeros_like(h_ref)

    n2 = 2 * CHUNK_C
    lane = lax.broadcasted_iota(jnp.int32, (1, LANE), 1)
    head0 = lane < NC
    ri = lax.broadcasted_iota(jnp.int32, (n2, n2), 0)
    ci = lax.broadcasted_iota(jnp.int32, (n2, n2), 1)
    strict = (ci > ri) if rev else (ci < ri)
    incl = (ci >= ri) if rev else (ci <= ri)
    eye = ri == ci
    nchunk = tt // CHUNK_C

    def stack(z):
        zero = jnp.zeros_like(z)
        return jnp.concatenate([jnp.where(head0, z, zero), jnp.where(head0, zero, z)], axis=0)

    def stack_b(z):
        return stack(z.astype(BF16))

    order = [(nchunk - 1 - i) if rev else i for i in range(nchunk)]
    items = [(pl.ds(c * CHUNK_C, CHUNK_C), pl.ds(p * LANE, LANE), p) for c in order for p in range(PAIRS_C)]
    rows = [it[0] for it in items]
    cols = [it[1] for it in items]

    lws = _each(lambda sl, cl: lw_ref[0, sl, cl], rows, cols)
    g = _each(lambda lw: _cumsum_rows(lw, rev), lws)
    e_end = _each(lambda gu: jnp.exp2(gu[0:1, :] if rev else gu[CHUNK_C - 1:CHUNK_C, :]), g)
    e_neg = _each(lambda gu: jnp.exp2(-gu), g)
    bs = _each(lambda sl, cl, gu, lw: stack_b(-_f32(kk_ref[0, sl, cl]) * jnp.exp2(gu - lw)), rows, cols, g, lws)
    a_sc = _each(lambda sl, cl, en: _f32(ka_ref[0, sl, cl]) * en, rows, cols, e_neg)
    k_sc = _each(lambda sl, cl, en: _f32(k_ref[0, sl, cl]) * en, rows, cols, e_neg)
    as_ = _each(stack_b, a_sc)
    ks = _each(stack_b, k_sc)
    ake = _each(lambda a, k, e: jnp.concatenate([stack_b(a * e), stack_b(k * e)], axis=0), a_sc, k_sc, e_end)
    rs = _each(lambda sl, cl, gu: stack(_f32(r_ref[0, sl, cl]) * jnp.exp2(gu)), rows, cols, g)
    vs = _each(lambda sl, cl: stack_b(v_ref[0, sl, cl]), rows, cols)

    ak = _each(lambda a, k: jnp.concatenate([a, k], axis=0), as_, ks)
    sc_b = _each(_mm_nt, bs, ak)
    sc_r = _each(_mm_nt, rs, ak)
    a_ab = _each(lambda z: jnp.where(strict, z[:, :n2], 0.0), sc_b)
    a_ak = _each(lambda z: jnp.where(strict, z[:, n2:], 0.0), sc_b)
    m_rak = _each(lambda z: jnp.where(jnp.concatenate([incl, incl], axis=1), z, 0.0), sc_r)
    tinv = _each(lambda a: jnp.where(eye, 1.0, 0.0) + a, a_ab)
    pw = _each(lambda a: _mm(a, a), a_ab)
    akv = _each(_mm, a_ak, vs)
    nsteps = int(np.log2(CHUNK_C)) - 1
    for i in range(nsteps):
        if i < nsteps - 1:
            both = _each(lambda t, p: _mm(jnp.concatenate([t, p], axis=0), p), tinv, pw)
            tinv = _each(lambda t, b: t + b[:n2], tinv, both)
            pw = _each(lambda b: b[n2:], both)
        else:
            tinv = _each(lambda t, p: t + _mm(t, p), tinv, pw)
    pwv = _each(lambda t, b, x: _mm(t, jnp.concatenate([b, x.astype(BF16)], axis=1)), tinv, bs, akv)
    low = _each(lambda z, v: jnp.concatenate([z.astype(BF16), jnp.concatenate([jnp.zeros_like(v), v], axis=1)],
                                             axis=0), pwv, vs)
    x1 = _each(_mm, m_rak, low)
    x2 = _each(_mm_tn, ake, low)
    top = _each(lambda r, z, e, y: jnp.concatenate([r + z[:, :LANE], jnp.where(eye, e, 0.0) + y[:, :LANE]], axis=0),
                rs, x1, e_end, x2)
    o_in = _each(lambda z: z[:, LANE:], x1)
    dmat = _each(lambda y: y[:, LANE:], x2)

    outs = []
    h = [h_ref[p] for p in range(PAIRS_C)]
    for (sl, cl, p), tp, oi, dm in zip(items, top, o_in, dmat):
        both = _mm(tp, h[p])
        h[p] = both[n2:] + dm
        os_ = both[:n2] + oi
        outs.append((sl, cl, os_[:CHUNK_C, :] + os_[CHUNK_C:, :]))
    for p in range(PAIRS_C):
        h_ref[p] = h[p]

    for sl, cl, o in outs:
        if final:
            o = o + ob_ref[0, sl, cl]
            inv_n = 1.0 / NC
            s0 = jnp.sum(jnp.where(head0, o, 0.0), axis=-1, keepdims=True)
            s1 = jnp.sum(jnp.where(head0, 0.0, o), axis=-1, keepdims=True)
            cen = o - jnp.where(head0, s0, s1) * inv_n
            c2 = cen * cen
            v0 = jnp.sum(jnp.where(head0, c2, 0.0), axis=-1, keepdims=True)
            v1 = jnp.sum(jnp.where(head0, 0.0, c2), axis=-1, keepdims=True)
            var = jnp.where(head0, v0, v1) * inv_n
            y = cen * lax.rsqrt(var + RWKV_GN_EPS) * lng_ref[:, cl] + lnb_ref[:, cl] + bonus_ref[0, sl, cl]
            y_ref[0, sl, cl] = (y * _silu(_f32(gate_ref[0, sl, cl]))).astype(y_ref.dtype)
        else:
            o_ref[0, sl, cl] = o


def _rwkv_mixer(proj3, mu, w0, w_lora_b, a0, a_lora_b, k_k, k_a, r_k, ln_g, ln_b):
    b, t, _ = proj3.shape
    tt = _pick(t, 256)
    nt = t // tt
    sub = tt // HALO

    def cur(off, width):
        return pl.BlockSpec((1, tt, width), lambda bi, ti: (bi, ti, off // width))

    def prev(off, width):
        return pl.BlockSpec((1, HALO, width),
                            lambda bi, ti: (bi, jnp.maximum(ti * sub - 1, 0), off // width))

    def nxt(off, width):
        return pl.BlockSpec((1, HALO, width),
                            lambda bi, ti: (bi, jnp.minimum((ti + 1) * sub, nt * sub - 1), off // width))

    def full(shape):
        return pl.BlockSpec(shape, lambda bi, ti: (0,) * len(shape))

    segs = ((OFF_CR, WC), (OFF_CK, WC), (OFF_CV, WC), (OFF_CC, N_CODES))
    in_specs = ([cur(o, w) for o, w in segs] + [prev(o, w) for o, w in segs] + [nxt(o, w) for o, w in segs]
                + [full((2, WC))] * 3 + [full((2, N_CODES))]
                + [full((2, WC)), full((2, N_CODES, WC)), full((2, WC)), full((2, N_CODES, WC))]
                + [full((1, WC))] * 3 + [full((LANE, LANE)), full((2 * tt, tt))])
    wl = jnp.zeros((2, N_CODES, WC), F32)
    al = jnp.zeros((2, N_CODES, WC), F32)
    for d in range(2):
        wl = wl.at[d, d * W_LORA:(d + 1) * W_LORA].set(w_lora_b[d])
        al = al.at[d, 2 * W_LORA + d * A_LORA:2 * W_LORA + (d + 1) * A_LORA].set(a_lora_b[d])
    row_i = np.arange(tt)
    shift = np.concatenate([row_i[:, None] - 1 == row_i[None, :], row_i[:, None] + 1 == row_i[None, :]], axis=0)
    shift = jnp.asarray(shift.astype(np.float32), dtype=BF16)
    lane_i = np.arange(LANE)
    gmat = jnp.asarray((lane_i[:, None] // NC == lane_i[None, :] // NC).astype(np.float32), dtype=BF16)
    row_out = pl.BlockSpec((1, tt, WC), lambda bi, ti: (bi, ti, 0))
    outs = pl.pallas_call(
        functools.partial(_rwkv_prep_kernel, tt),
        grid=(b, nt),
        in_specs=in_specs,
        out_specs=[row_out] * 10,
        out_shape=[jax.ShapeDtypeStruct((b, t, WC), dt) for dt in (SCAN_DTYPE,) * 3 + (F32,) * 3 + (SCAN_DTYPE,) * 4],
        compiler_params=_params("parallel", "parallel"),
    )(*([proj3] * 12), mu[:, :WC], mu[:, WC:2 * WC], mu[:, 2 * WC:3 * WC], mu[:, 3 * WC:],
      w0, wl.astype(BF16), a0, al.astype(BF16), k_k.reshape(1, WC), k_a.reshape(1, WC),
      r_k.reshape(1, WC), gmat, shift)
    r, v, kk, bonus, lw0, lw1, k0, k1, ka0, ka1 = outs

    ts = _pick(t, 512)
    ns = t // ts

    def scan(rev, final, lw, kd, ka, ob=None):
        wblk = PAIRS_C * LANE
        tmap = (lambda i: ns - 1 - i) if rev else (lambda i: i)
        blk = pl.BlockSpec((1, ts, wblk), lambda bi, h, ti: (bi, tmap(ti), h))
        head_vec = pl.BlockSpec((1, wblk), lambda bi, h, ti: (0, h))
        in_specs = [blk] * 6
        args = [r, lw, kd, v, ka, kk]
        if final:
            gate = pl.BlockSpec((1, ts, wblk), lambda bi, h, ti: (bi, tmap(ti), OFF_CG // wblk + h))
            in_specs += [blk, blk, gate, head_vec, head_vec]
            args += [ob, bonus, proj3, ln_g.reshape(1, WC), ln_b.reshape(1, WC)]
        return pl.pallas_call(
            functools.partial(_rwkv_scan_kernel, rev, final, ts),
            grid=(b, WC // wblk, ns),
            in_specs=in_specs,
            out_specs=blk,
            out_shape=jax.ShapeDtypeStruct((b, t, WC), BF16 if final else F32),
            scratch_shapes=[pltpu.VMEM((PAIRS_C, LANE, LANE), F32)],
            compiler_params=_params("parallel", "parallel", "arbitrary"),
        )(*args)

    ob = scan(True, False, lw1, k1, ka1)
    return scan(False, True, lw0, k0, ka0, ob=ob)


def _merge_kernel(final, ya_ref, yb_ref, yc_ref, mg_ref, x_ref, wa_ref, wb_ref, wc_ref, wo_ref,
                  fg_ref, o_ref):
    za = jnp.dot(ya_ref[...], wa_ref[...], preferred_element_type=F32)
    zb = jnp.dot(yb_ref[...], wb_ref[...], preferred_element_type=F32)
    zc = jnp.dot(yc_ref[...], wc_ref[...], preferred_element_type=F32)
    mg = _f32(mg_ref[...])
    mixed = (_sigmoid(mg[:, :D_MODEL]) * za
             + _sigmoid(mg[:, D_MODEL:2 * D_MODEL]) * zb
             + _sigmoid(mg[:, 2 * D_MODEL:]) * zc)
    out = x_ref[...] + jnp.dot(mixed.astype(BF16), wo_ref[...], preferred_element_type=F32)
    if final:
        ms = jnp.mean(out * out, axis=-1, keepdims=True)
        out = out * lax.rsqrt(ms + EPS) * fg_ref[...]
    o_ref[...] = out


def _merge(ya, yb, yc, proj, x2, wa, wb, wc, wo, fg, final):
    m = x2.shape[0]
    tm = _pick(m, 512)

    def rows(width):
        return pl.BlockSpec((tm, width), lambda i: (i, 0))

    def full(shape):
        return pl.BlockSpec(shape, lambda i: (0, 0))

    return pl.pallas_call(
        functools.partial(_merge_kernel, final),
        grid=(m // tm,),
        in_specs=[rows(WA), rows(WB_V), rows(WC), rows(3 * D_MODEL), rows(D_MODEL),
                  full((WA, D_MODEL)), full((WB_V, D_MODEL)), full((WC, D_MODEL)),
                  full((D_MODEL, D_MODEL)), full((1, D_MODEL))],
        out_specs=rows(D_MODEL),
        out_shape=jax.ShapeDtypeStruct((m, D_MODEL), F32),
        compiler_params=_params("parallel", vmem=PROJ_VMEM_LIMIT),
    )(ya, yb, yc, proj, x2, wa.astype(BF16), wb.astype(BF16), wc.astype(BF16), wo.astype(BF16),
      fg.reshape(1, D_MODEL))


def kernel(x, norm_g, w_in, hgrn_lb_logits, hgrn_norm_g, ret_norm_g, ret_norm_b, rwkv_mu, rwkv_w0,
           rwkv_w_lora_b, rwkv_a0, rwkv_a_lora_b, rwkv_k_k, rwkv_k_a, rwkv_r_k, rwkv_ln_g,
           rwkv_ln_b, w_branch_a, w_branch_b, w_branch_c, w_out, final_norm_g):
    b, t, d = x.shape
    m = b * t
    depth = w_in.shape[0]
    p_lb = jax.nn.softmax(hgrn_lb_logits.astype(F32), axis=0)
    lbs = jnp.cumsum(p_lb, axis=0) - p_lb[0:1]
    x2 = x.reshape(m, d)
    for l in range(depth):
        proj = _inproj(x2, norm_g[l], _permute_w_in(w_in, l))
        proj3 = proj.reshape(b, t, N_PAD)
        ya = _hgrn_mixer(proj3, lbs[l], hgrn_norm_g[l])
        yb = _ret_mixer(proj3, ret_norm_g[l], ret_norm_b[l])
        yc = _rwkv_mixer(proj3, rwkv_mu[l], rwkv_w0[l], rwkv_w_lora_b[l], rwkv_a0[l],
                         rwkv_a_lora_b[l], rwkv_k_k[l], rwkv_k_a[l], rwkv_r_k[l],
                         rwkv_ln_g[l], rwkv_ln_b[l])
        x2 = _merge(ya.reshape(m, WA), yb.reshape(m, WB_V), yc.reshape(m, WC), proj, x2,
                    w_branch_a[l], w_branch_b[l], w_branch_c[l], w_out[l], final_norm_g,
                    final=(l == depth - 1))
    return x2.reshape(b, t, d)
```

```python
import functools

import numpy as np
import jax
import jax.numpy as jnp
from jax import lax
from jax.experimental import pallas as pl
from jax.experimental.pallas import tpu as pltpu

F32 = jnp.float32
BF16 = jnp.bfloat16

D_MODEL = 1024
EPS = 1e-6
F_MIN = 1e-6
DKA = 128
HA = D_MODEL // DKA
WA = HA * DKA
HB = 8
DKB = D_MODEL // HB
DVB = 2 * DKB
WB_QK = HB * DKB
WB_V = HB * DVB
ROPE_BASE = 10000.0
RET_GN_EPS = 1e-5
NC = 64
HC = D_MODEL // NC
WC = HC * NC
W_LORA = 64
A_LORA = 64
RWKV_GN_EPS = 64e-5
KK_NORM_EPS = 1e-12
N_CODES = 2 * W_LORA + 2 * A_LORA
C_SHIFT = 3 * WC + N_CODES

SRC_B = 5 * WA
SRC_CS = SRC_B + 2 * WB_QK + 2 * WB_V
SRC_CG = SRC_CS + C_SHIFT
SRC_MERGE = SRC_CG + WC
N_IN = SRC_MERGE + 3 * D_MODEL

OFF_MERGE = 0
OFF_AQ = OFF_MERGE + 3 * D_MODEL
OFF_AFF = OFF_AQ + WA
OFF_AFB = OFF_AFF + WA
OFF_AI = OFF_AFB + WA
OFF_AG = OFF_AI + WA
OFF_BQ = OFF_AG + WA
OFF_BK = OFF_BQ + WB_QK
OFF_BV = OFF_BK + WB_QK
OFF_BG = OFF_BV + WB_V
OFF_CG = OFF_BG + WB_V
OFF_CR = OFF_CG + WC
OFF_CK = OFF_CR + WC
OFF_CV = OFF_CK + WC
OFF_CC = OFF_CV + WC
LANE = 128
PROJ_DTYPE = BF16
SCAN_DTYPE = BF16
HALO = 16
PROJ_SPLIT = 4
N_PAD = -(-(OFF_CC + N_CODES) // (PROJ_SPLIT * LANE)) * PROJ_SPLIT * LANE
PROJ_TN = N_PAD // PROJ_SPLIT
MXU_N = 256
WPERM_TN = 256

CHUNK_A = 128
GROUP_A = 8
HEADS_A = 2
CHUNK_B = 256
CHUNK_C = 64
PAIRS_C = 4
VMEM_LIMIT = 48 * 1024 * 1024
PROJ_VMEM_LIMIT = 56 * 1024 * 1024

NT_DIMS = (((1,), (1,)), ((), ()))
TN_DIMS = (((0,), (0,)), ((), ()))


def _mm(a, b):
    return jnp.dot(a.astype(BF16), b.astype(BF16), preferred_element_type=F32)


def _mm_nt(a, b):
    return lax.dot_general(a.astype(BF16), b.astype(BF16), NT_DIMS, preferred_element_type=F32)


def _mm_tn(a, b):
    return lax.dot_general(a.astype(BF16), b.astype(BF16), TN_DIMS, preferred_element_type=F32)


def _split_dot(ones_mat, x):
    hi = x.astype(BF16)
    lo = (x - hi.astype(F32)).astype(BF16)
    return jnp.dot(ones_mat, hi, preferred_element_type=F32) + jnp.dot(ones_mat, lo, preferred_element_type=F32)


def _split_dot_right(x, ones_mat):
    hi = x.astype(BF16)
    lo = (x - hi.astype(F32)).astype(BF16)
    return jnp.dot(hi, ones_mat, preferred_element_type=F32) + jnp.dot(lo, ones_mat, preferred_element_type=F32)


def _f32(z):
    return z.astype(F32)


def _sigmoid(z):
    return 0.5 * jnp.tanh(0.5 * z) + 0.5


def _cumsum_rows(x, rev):
    n = x.shape[0]
    rows = lax.broadcasted_iota(jnp.int32, (n, 1), 0)
    step = 1
    while step < n:
        if rev:
            x = x + jnp.where(rows < n - step, pltpu.roll(x, n - step, axis=0), 0.0)
        else:
            x = x + jnp.where(rows >= step, pltpu.roll(x, step, axis=0), 0.0)
        step *= 2
    return x


def _silu(z):
    h = 0.5 * z
    return h * jnp.tanh(h) + h


def _pick(n, pref):
    t = min(n, pref)
    assert n % t == 0, (n, pref)
    return t


def _params(*sem, vmem=VMEM_LIMIT):
    return pltpu.CompilerParams(dimension_semantics=sem, vmem_limit_bytes=vmem)


def _inproj_kernel(x_ref, g_ref, w_ref, o_ref, h_ref):
    @pl.when(pl.program_id(1) == 0)
    def _():
        x = x_ref[...]
        ms = jnp.mean(x * x, axis=-1, keepdims=True)
        h_ref[...] = (x * lax.rsqrt(ms + EPS) * g_ref[...]).astype(BF16)

    def tile(cols):
        o_ref[:, cols] = jnp.dot(h_ref[...], w_ref[:, cols], preferred_element_type=F32).astype(o_ref.dtype)

    for start in range(0, PROJ_TN, MXU_N):
        tile(pl.ds(start, min(MXU_N, PROJ_TN - start)))


def _inproj(x2, g, w_bf16):
    m = x2.shape[0]
    tm = _pick(m, 1024)
    return pl.pallas_call(
        _inproj_kernel,
        grid=(m // tm, N_PAD // PROJ_TN),
        in_specs=[
            pl.BlockSpec((tm, D_MODEL), lambda i, j: (i, 0)),
            pl.BlockSpec((1, D_MODEL), lambda i, j: (0, 0)),
            pl.BlockSpec((D_MODEL, PROJ_TN), lambda i, j: (0, j)),
        ],
        out_specs=pl.BlockSpec((tm, PROJ_TN), lambda i, j: (i, j)),
        out_shape=jax.ShapeDtypeStruct((m, N_PAD), PROJ_DTYPE),
        scratch_shapes=[pltpu.VMEM((tm, D_MODEL), BF16)],
        compiler_params=_params("parallel", "arbitrary", vmem=PROJ_VMEM_LIMIT),
    )(x2, g.reshape(1, D_MODEL), w_bf16)


def _wperm_kernel(nvalid, a_ref, b_ref, o_ref):
    j = pl.program_id(0)
    for idx, src in enumerate((a_ref, b_ref)):
        cols = pl.ds(idx * WPERM_TN, WPERM_TN)
        blk = src[...].astype(BF16)
        o_ref[:, cols] = jnp.where(2 * j + idx < nvalid, blk, jnp.zeros_like(blk))


def _src_block(jd):
    n_merge = (N_IN - SRC_MERGE) // WPERM_TN
    n_ab = SRC_CS // WPERM_TN
    n_cg = (SRC_MERGE - SRC_CG) // WPERM_TN
    n_src = N_IN // WPERM_TN
    return jnp.where(jd < n_merge, jd + SRC_MERGE // WPERM_TN,
                     jnp.where(jd < n_merge + n_ab, jd - n_merge,
                               jnp.where(jd < n_merge + n_ab + n_cg, jd - n_merge - n_ab + SRC_CG // WPERM_TN,
                                         jnp.minimum(jd - n_merge - n_cg, n_src - 1))))


def _permute_w_in(w_in, layer):
    assert all(off % WPERM_TN == 0 for off in (SRC_CS, SRC_CG, SRC_MERGE, N_IN)) and N_PAD % (2 * WPERM_TN) == 0

    def src(idx):
        return pl.BlockSpec((None, D_MODEL, WPERM_TN), lambda j: (layer, 0, _src_block(2 * j + idx)))

    return pl.pallas_call(
        functools.partial(_wperm_kernel, N_IN // WPERM_TN),
        grid=(N_PAD // (2 * WPERM_TN),),
        in_specs=[src(0), src(1)],
        out_specs=pl.BlockSpec((D_MODEL, 2 * WPERM_TN), lambda j: (0, j)),
        out_shape=jax.ShapeDtypeStruct((D_MODEL, N_PAD), BF16),
        compiler_params=_params("parallel"),
    )(w_in, w_in)


def _each(fn, *lists):
    return [fn(*items) for items in zip(*lists)]


def _hgrn_kernel(rev, final, tt, q_ref, f_ref, v_ref, lb_ref, tri_ref, *rest):
    if final:
        ob_ref, gate_ref, ng_ref, y_ref, st_ref = rest
    else:
        o_ref, st_ref = rest

    @pl.when(pl.program_id(2) == 0)
    def _():
        st_ref[...] = jnp.zeros_like(st_ref)

    n = CHUNK_A
    ri = lax.broadcasted_iota(jnp.int32, (n, n), 0)
    ci = lax.broadcasted_iota(jnp.int32, (n, n), 1)
    g_bits = int(np.log2(GROUP_A))
    diag_mask = ((ri >> g_bits) == (ci >> g_bits)) & ((ci >= ri) if rev else (ci <= ri))
    halves = [GROUP_A << i for i in range(int(np.log2(n // GROUP_A)))]
    level_masks = []
    for half in halves:
        hb = int(np.log2(half))
        q_side, k_side = (0, 1) if rev else (1, 0)
        level_masks.append(((ri >> (hb + 1)) == (ci >> (hb + 1)))
                           & (((ri >> hb) & 1) == q_side) & (((ci >> hb) & 1) == k_side))

    def ref_rows(b, blk, idx):
        picked = b.reshape(n // blk, blk, DKA)[:, idx:idx + 1, :]
        return jnp.broadcast_to(picked, (n // blk, blk, DKA)).reshape(n, DKA)

    nchunk = tt // n
    order = [(nchunk - 1 - i) if rev else i for i in range(nchunk)]
    items = [(pl.ds(c * n, n), pl.ds(hh * DKA, DKA), hh) for c in order for hh in range(HEADS_A)]
    sls = [it[0] for it in items]
    cls = [it[1] for it in items]
    c0 = [0.5 + 0.5 * lb_ref[it[2]] for it in items]
    c1 = [0.5 - 0.5 * lb_ref[it[2]] for it in items]
    gm = _each(lambda sl, cl, c: c * jnp.tanh(0.5 * _f32(f_ref[0, sl, cl])), sls, cls, c1)
    lf = _each(lambda m, c: jnp.log2(jnp.maximum(c + m, F_MIN)), gm, c0)
    k = _each(lambda m, c: c - m, gm, c1)
    q = _each(lambda sl, cl: _silu(_f32(q_ref[0, sl, cl])), sls, cls)
    v = _each(lambda sl, cl: v_ref[0, sl, cl], sls, cls)
    b = _each(lambda z: _split_dot(tri_ref[...], z), lf)
    d0 = _each(lambda z: z - ref_rows(z, GROUP_A, GROUP_A // 2 if rev else GROUP_A // 2 - 1), b)
    qb = _each(lambda z: z.astype(BF16), q)
    kb = _each(lambda z: z.astype(BF16), k)
    scores = _each(lambda qq, kk, d: jnp.where(
        diag_mask, _mm_nt(qq * jnp.exp2(d).astype(BF16), kk * jnp.exp2(-d).astype(BF16)), 0.0), qb, kb, d0)
    for half, mask in zip(halves, level_masks):
        e = _each(lambda z: jnp.exp2(-jnp.abs(z - ref_rows(z, 2 * half, half if rev else half - 1))).astype(BF16),
                  b)
        scores = _each(lambda sc, qq, kk, ee: jnp.where(mask, _mm_nt(qq * ee, kk * ee), sc), scores, qb, kb, e)
    o_intra = _each(_mm, scores, v)
    b_end = _each(lambda z: z[0:1, :] if rev else z[n - 1:n, :], b)
    q_in = _each(lambda qq, z: qq * jnp.exp2(z), q, b)
    kv = _each(lambda vv, kk, z, ze: _mm_tn(vv, kk * jnp.exp2(ze - z)), v, k, b, b_end)
    e_end = _each(jnp.exp2, b_end)

    outs = []
    st = [st_ref[hh] for hh in range(HEADS_A)]
    for (sl, cl, hh), oi, qi, kvi, ee in zip(items, o_intra, q_in, kv, e_end):
        outs.append((sl, cl, hh, oi + _mm_nt(qi, st[hh])))
        st[hh] = st[hh] * ee + kvi
    for hh in range(HEADS_A):
        st_ref[hh] = st[hh]

    for sl, cl, hh, o in outs:
        if final:
            o = o + ob_ref[0, sl, cl]
            ms = jnp.mean(o * o, axis=-1, keepdims=True)
            y = o * lax.rsqrt(ms + EPS) * ng_ref[hh] * _silu(_f32(gate_ref[0, sl, cl]))
            y_ref[0, sl, cl] = y.astype(y_ref.dtype)
        else:
            o_ref[0, sl, cl] = o


def _tri_blockdiag(n, blk, rev):
    i = np.arange(n)
    same = (i[:, None] // blk) == (i[None, :] // blk)
    tri = (i[None, :] >= i[:, None]) if rev else (i[None, :] <= i[:, None])
    return jnp.asarray((same & tri).astype(np.float32), dtype=BF16)


def _hgrn_pass(proj3, lb, rev, final, ob=None, norm_g=None):
    b, t, _ = proj3.shape
    tt = _pick(t, 2048)
    nt = t // tt
    tmap = (lambda i: nt - 1 - i) if rev else (lambda i: i)

    wblk = HEADS_A * DKA

    def col(off):
        return pl.BlockSpec((1, tt, wblk), lambda bi, h, ti: (bi, tmap(ti), off // wblk + h))

    head_vec = pl.BlockSpec((HEADS_A, 1, DKA), lambda bi, h, ti: (h, 0, 0))
    in_specs = [col(OFF_AQ), col(OFF_AFB if rev else OFF_AFF), col(OFF_AI), head_vec,
                pl.BlockSpec((LANE, LANE), lambda bi, h, ti: (0, 0))]
    args = [proj3, proj3, proj3, lb.reshape(HA, 1, DKA), _tri_blockdiag(LANE, CHUNK_A, rev)]
    out_block = pl.BlockSpec((1, tt, wblk), lambda bi, h, ti: (bi, tmap(ti), h))
    if final:
        in_specs += [out_block, col(OFF_AG), head_vec]
        args += [ob, proj3, norm_g.reshape(HA, 1, DKA)]
        out_dtype = BF16
    else:
        out_dtype = F32
    return pl.pallas_call(
        functools.partial(_hgrn_kernel, rev, final, tt),
        grid=(b, HA // HEADS_A, nt),
        in_specs=in_specs,
        out_specs=out_block,
        out_shape=jax.ShapeDtypeStruct((b, t, WA), out_dtype),
        scratch_shapes=[pltpu.VMEM((HEADS_A, DKA, DKA), F32)],
        compiler_params=_params("parallel", "parallel", "arbitrary"),
    )(*args)


def _hgrn_mixer(proj3, lbs, norm_g):
    ob = _hgrn_pass(proj3, lbs[1], rev=True, final=False)
    return _hgrn_pass(proj3, lbs[0], rev=False, final=True, ob=ob, norm_g=norm_g)


def _rotary(z, cos2, sin2):
    return z * cos2 + pltpu.roll(z, DKB // 2, axis=1) * sin2


def _ret_state_kernel(nsub, k_ref, v_ref, cos_ref, sin_ref, lg_ref, sb_ref, st_ref):
    @pl.when(pl.program_id(2) == 0)
    def _():
        st_ref[...] = jnp.zeros_like(st_ref)

    cb = CHUNK_B
    lg2 = lg_ref[0]
    lg = lg2[:, :DKB]
    pos = lax.broadcasted_iota(jnp.int32, (cb, DKB), 0).astype(F32)
    dec = jnp.exp(pos * lg) * (DKB ** -0.5)
    order = list(range(nsub - 1, -1, -1))
    sls = [pl.ds(c * cb, cb) for c in order]
    kd = _each(lambda sl: _rotary(_f32(k_ref[0, sl, :]), cos_ref[sl, :], sin_ref[sl, :]) * dec, sls)
    kv = _each(lambda kk, sl: _mm_tn(kk, v_ref[0, sl, :]), kd, sls)
    e_chunk = jnp.exp(cb * lg2)
    st = st_ref[...]
    for c, kvi in zip(order, kv):
        sb_ref[0, 0, c] = st
        st = st * e_chunk + kvi
    st_ref[...] = st


def _ret_out_kernel(nsub, q_ref, k_ref, v_ref, cos_ref, sin_ref, lg_ref, sb_ref, gate_ref, gg_ref,
                    gb_ref, y_ref, st_ref):
    @pl.when(pl.program_id(2) == 0)
    def _():
        st_ref[...] = jnp.zeros_like(st_ref)

    cb = CHUNK_B
    lg2 = lg_ref[0]
    lg = lg2[:, :DKB]
    pos = lax.broadcasted_iota(jnp.int32, (cb, DKB), 0).astype(F32)
    ri = lax.broadcasted_iota(jnp.int32, (cb, cb), 0)
    ci = lax.broadcasted_iota(jnp.int32, (cb, cb), 1)
    decay = jnp.exp(jnp.abs(ri - ci).astype(F32) * lg2)
    dec_fwd = jnp.exp((pos + 1.0) * lg)
    dec_bwd = jnp.exp((cb - pos) * lg)
    dec_key = jnp.exp((cb - 1.0 - pos) * lg)
    e_chunk = jnp.exp(cb * lg2)

    sls = [pl.ds(c * cb, cb) for c in range(nsub)]
    q = _each(lambda sl: _rotary(_f32(q_ref[0, sl, :]), cos_ref[sl, :], sin_ref[sl, :]), sls)
    k = _each(lambda sl: _rotary(_f32(k_ref[0, sl, :]), cos_ref[sl, :], sin_ref[sl, :]) * (DKB ** -0.5), sls)
    v = _each(lambda sl: v_ref[0, sl, :], sls)
    scores = _each(lambda qq, kk: _mm_nt(qq, kk) * decay, q, k)
    o = _each(_mm, scores, v)
    o = _each(lambda oo, qq, c: oo + _mm(qq * dec_bwd, sb_ref[0, 0, c]), o, q, list(range(nsub)))
    kv = _each(lambda kk, vv: _mm_tn(kk * dec_key, vv), k, v)
    q_fwd = _each(lambda qq: qq * dec_fwd, q)
    st = st_ref[...]
    outs = []
    for oo, qf, kvi in zip(o, q_fwd, kv):
        outs.append(oo + _mm(qf, st))
        st = st * e_chunk + kvi
    st_ref[...] = st

    for sl, oo in zip(sls, outs):
        mu = jnp.mean(oo, axis=-1, keepdims=True)
        cen = oo - mu
        var = jnp.mean(cen * cen, axis=-1, keepdims=True)
        y = (cen * lax.rsqrt(var + RET_GN_EPS) * gg_ref[0] + gb_ref[0]) * _silu(_f32(gate_ref[0, sl, :]))
        y_ref[0, sl, :] = y.astype(y_ref.dtype)


def _ret_mixer(proj3, gn_g, gn_b):
    b, t, _ = proj3.shape
    cb = CHUNK_B
    tt = _pick(t, 16 * cb)
    nsub = tt // cb
    nt = t // tt
    half = DKB // 2
    inv = ROPE_BASE ** (-jnp.arange(0, DKB, 2, dtype=F32) / DKB)
    ang_hi = (jnp.arange(t // cb, dtype=F32) * cb)[:, None, None] * inv
    ang_lo = jnp.arange(cb, dtype=F32)[None, :, None] * inv
    cos = (jnp.cos(ang_hi) * jnp.cos(ang_lo) - jnp.sin(ang_hi) * jnp.sin(ang_lo)).reshape(t, half)
    sin = (jnp.sin(ang_hi) * jnp.cos(ang_lo) + jnp.cos(ang_hi) * jnp.sin(ang_lo)).reshape(t, half)
    cos2 = jnp.concatenate([cos, cos], axis=-1)
    sin2 = jnp.concatenate([-sin, sin], axis=-1)
    log_gamma = jnp.log1p(-jnp.exp2(-5.0 - jnp.arange(HB, dtype=F32)))
    lg = jnp.broadcast_to(log_gamma[:, None, None], (HB, 1, DVB))

    def col(off, width, tmap):
        return pl.BlockSpec((1, tt, width), lambda bi, h, c: (bi, tmap(c), off // width + h))

    def tab(tmap):
        return pl.BlockSpec((tt, DKB), lambda bi, h, c: (tmap(c), 0))

    lg_spec = pl.BlockSpec((1, 1, DVB), lambda bi, h, c: (h, 0, 0))
    rmap = lambda c: nt - 1 - c
    fmap = lambda c: c

    sb = pl.pallas_call(
        functools.partial(_ret_state_kernel, nsub),
        grid=(b, HB, nt),
        in_specs=[col(OFF_BK, DKB, rmap), col(OFF_BV, DVB, rmap), tab(rmap), tab(rmap), lg_spec],
        out_specs=pl.BlockSpec((1, 1, nsub, DKB, DVB), lambda bi, h, c: (bi, h, rmap(c), 0, 0)),
        out_shape=jax.ShapeDtypeStruct((b, HB, t // cb, DKB, DVB), F32),
        scratch_shapes=[pltpu.VMEM((DKB, DVB), F32)],
        compiler_params=_params("parallel", "parallel", "arbitrary"),
    )(proj3, proj3, cos2, sin2, lg)

    head_vec = pl.BlockSpec((1, 1, DVB), lambda bi, h, c: (h, 0, 0))
    return pl.pallas_call(
        functools.partial(_ret_out_kernel, nsub),
        grid=(b, HB, nt),
        in_specs=[col(OFF_BQ, DKB, fmap), col(OFF_BK, DKB, fmap), col(OFF_BV, DVB, fmap),
                  tab(fmap), tab(fmap), lg_spec,
                  pl.BlockSpec((1, 1, nsub, DKB, DVB), lambda bi, h, c: (bi, h, c, 0, 0)),
                  col(OFF_BG, DVB, fmap), head_vec, head_vec],
        out_specs=pl.BlockSpec((1, tt, DVB), lambda bi, h, c: (bi, c, h)),
        out_shape=jax.ShapeDtypeStruct((b, t, WB_V), BF16),
        scratch_shapes=[pltpu.VMEM((DKB, DVB), F32)],
        compiler_params=_params("parallel", "parallel", "arbitrary"),
    )(proj3, proj3, proj3, cos2, sin2, lg, sb, proj3,
      gn_g.reshape(HB, 1, DVB), gn_b.reshape(HB, 1, DVB))


def _group_sum(x, gmat):
    cols = [_split_dot_right(x[:, i * LANE:(i + 1) * LANE], gmat) for i in range(x.shape[1] // LANE)]
    return jnp.concatenate(cols, axis=1)


def _rwkv_prep_kernel(tt, r_ref, k_ref, v_ref, c_ref, rp_ref, kp_ref, vp_ref, cp_ref,
                      rn_ref, kn_ref, vn_ref, cn_ref, mur_ref, muk_ref, muv_ref, muc_ref,
                      w0_ref, wl_ref, a0_ref, al_ref, kk_ref, ka_ref, rk_ref, gmat_ref, shift_ref,
                      r_o, v_o, kk_o, bonus_o, lw0_o, lw1_o, k0_o, k1_o, ka0_o, ka1_o):
    ti = pl.program_id(1)
    has_prev = (ti > 0).astype(F32)
    has_next = (ti < pl.num_programs(1) - 1).astype(F32)
    rows = lax.broadcasted_iota(jnp.int32, (tt, 1), 0)

    def shifted(p_ref, prev_ref, next_ref, mu_ref):
        both = jnp.dot(shift_ref[...], p_ref[0].astype(BF16), preferred_element_type=F32)
        p = _f32(p_ref[0])
        prev_row = _f32(prev_ref[0, HALO - 1:HALO, :]) * has_prev
        next_row = _f32(next_ref[0, 0:1, :]) * has_next
        before = jnp.where(rows == 0, prev_row, both[:tt])
        after = jnp.where(rows == tt - 1, next_row, both[tt:])
        return p + mu_ref[0:1, :] * (before - p) + mu_ref[1:2, :] * (after - p)

    r = shifted(r_ref, rp_ref, rn_ref, mur_ref)
    k = shifted(k_ref, kp_ref, kn_ref, muk_ref)
    v = shifted(v_ref, vp_ref, vn_ref, muv_ref)
    codes = shifted(c_ref, cp_ref, cn_ref, muc_ref)
    gmat = gmat_ref[...]

    kraw = k * kk_ref[...]
    kk = kraw * lax.rsqrt(_group_sum(kraw * kraw, gmat) + KK_NORM_EPS)
    r_o[0] = r.astype(r_o.dtype)
    v_o[0] = v.astype(v_o.dtype)
    kk_o[0] = kk.astype(kk_o.dtype)
    tcodes = jnp.tanh(codes)
    bonus = jnp.zeros_like(r)
    for d, (lw_o, kd_o, ka_o) in enumerate(((lw0_o, k0_o, ka0_o), (lw1_o, k1_o, ka1_o))):
        w = w0_ref[d:d + 1, :] + _mm(tcodes, wl_ref[d])
        half_c = np.float32(-0.5 * np.exp(-0.5) * np.log2(np.e))
        lw_o[0] = half_c * jnp.tanh(0.5 * w) + half_c
        a = _sigmoid(a0_ref[d:d + 1, :] + _mm(codes, al_ref[d]))
        kd = k * (1.0 + (a - 1.0) * ka_ref[...])
        kd_o[0] = kd.astype(kd_o.dtype)
        ka_o[0] = (kk * a).astype(ka_o.dtype)
        bonus = bonus + _group_sum(r * kd * rk_ref[...], gmat) * v
    bonus_o[0] = bonus


def _rwkv_scan_kernel(rev, final, tt, r_ref, lw_ref, k_ref, v_ref, ka_ref, kk_ref, *rest):
    if final:
        ob_ref, bonus_ref, gate_ref, lng_ref, lnb_ref, y_ref, h_ref = rest
    else:
        o_ref, h_ref = rest

    @pl.when(pl.program_id(2) == 0)
    def _():
        h_ref[...] = jnp.zeros_like(h_ref)

    n2 = 2 * CHUNK_C
    lane = lax.broadcasted_iota(jnp.int32, (1, LANE), 1)
    head0 = lane < NC
    ri = lax.broadcasted_iota(jnp.int32, (n2, n2), 0)
    ci = lax.broadcasted_iota(jnp.int32, (n2, n2), 1)
    strict = (ci > ri) if rev else (ci < ri)
    incl = (ci >= ri) if rev else (ci <= ri)
    eye = ri == ci
    nchunk = tt // CHUNK_C

    def stack(z):
        zero = jnp.zeros_like(z)
        return jnp.concatenate([jnp.where(head0, z, zero), jnp.where(head0, zero, z)], axis=0)

    def stack_b(z):
        return stack(z.astype(BF16))

    order = [(nchunk - 1 - i) if rev else i for i in range(nchunk)]
    items = [(pl.ds(c * CHUNK_C, CHUNK_C), pl.ds(p * LANE, LANE), p) for c in order for p in range(PAIRS_C)]
    rows = [it[0] for it in items]
    cols = [it[1] for it in items]

    lws = _each(lambda sl, cl: lw_ref[0, sl, cl], rows, cols)
    g = _each(lambda lw: _cumsum_rows(lw, rev), lws)
    e_end = _each(lambda gu: jnp.exp2(gu[0:1, :] if rev else gu[CHUNK_C - 1:CHUNK_C, :]), g)
    e_neg = _each(lambda gu: jnp.exp2(-gu), g)
    bs = _each(lambda sl, cl, gu, lw: stack_b(-_f32(kk_ref[0, sl, cl]) * jnp.exp2(gu - lw)), rows, cols, g, lws)
    a_sc = _each(lambda sl, cl, en: _f32(ka_ref[0, sl, cl]) * en, rows, cols, e_neg)
    k_sc = _each(lambda sl, cl, en: _f32(k_ref[0, sl, cl]) * en, rows, cols, e_neg)
    as_ = _each(stack_b, a_sc)
    ks = _each(stack_b, k_sc)
    ake = _each(lambda a, k, e: jnp.concatenate([stack_b(a * e), stack_b(k * e)], axis=0), a_sc, k_sc, e_end)
    rs = _each(lambda sl, cl, gu: stack(_f32(r_ref[0, sl, cl]) * jnp.exp2(gu)), rows, cols, g)
    vs = _each(lambda sl, cl: stack_b(v_ref[0, sl, cl]), rows, cols)

    ak = _each(lambda a, k: jnp.concatenate([a, k], axis=0), as_, ks)
    sc_b = _each(_mm_nt, bs, ak)
    sc_r = _each(_mm_nt, rs, ak)
    a_ab = _each(lambda z: jnp.where(strict, z[:, :n2], 0.0), sc_b)
    a_ak = _each(lambda z: jnp.where(strict, z[:, n2:], 0.0), sc_b)
    m_rak = _each(lambda z: jnp.where(jnp.concatenate([incl, incl], axis=1), z, 0.0), sc_r)
    tinv = _each(lambda a: jnp.where(eye, 1.0, 0.0) + a, a_ab)
    pw = _each(lambda a: _mm(a, a), a_ab)
    akv = _each(_mm, a_ak, vs)
    nsteps = int(np.log2(CHUNK_C)) - 1
    for i in range(nsteps):
        if i < nsteps - 1:
            both = _each(lambda t, p: _mm(jnp.concatenate([t, p], axis=0), p), tinv, pw)
            tinv = _each(lambda t, b: t + b[:n2], tinv, both)
            pw = _each(lambda b: b[n2:], both)
        else:
            tinv = _each(lambda t, p: t + _mm(t, p), tinv, pw)
    pwv = _each(lambda t, b, x: _mm(t, jnp.concatenate([b, x.astype(BF16)], axis=1)), tinv, bs, akv)
    low = _each(lambda z, v: jnp.concatenate([z.astype(BF16), jnp.concatenate([jnp.zeros_like(v), v], axis=1)],
                                             axis=0), pwv, vs)
    x1 = _each(_mm, m_rak, low)
    x2 = _each(_mm_tn, ake, low)
    top = _each(lambda r, z, e, y: jnp.concatenate([r + z[:, :LANE], jnp.where(eye, e, 0.0) + y[:, :LANE]], axis=0),
                rs, x1, e_end, x2)
    o_in = _each(lambda z: z[:, LANE:], x1)
    dmat = _each(lambda y: y[:, LANE:], x2)

    outs = []
    h = [h_ref[p] for p in range(PAIRS_C)]
    for (sl, cl, p), tp, oi, dm in zip(items, top, o_in, dmat):
        both = _mm(tp, h[p])
        h[p] = both[n2:] + dm
        os_ = both[:n2] + oi
        outs.append((sl, cl, os_[:CHUNK_C, :] + os_[CHUNK_C:, :]))
    for p in range(PAIRS_C):
        h_ref[p] = h[p]

    for sl, cl, o in outs:
        if final:
            o = o + ob_ref[0, sl, cl]
            inv_n = 1.0 / NC
            s0 = jnp.sum(jnp.where(head0, o, 0.0), axis=-1, keepdims=True)
            s1 = jnp.sum(jnp.where(head0, 0.0, o), axis=-1, keepdims=True)
            cen = o - jnp.where(head0, s0, s1) * inv_n
            c2 = cen * cen
            v0 = jnp.sum(jnp.where(head0, c2, 0.0), axis=-1, keepdims=True)
            v1 = jnp.sum(jnp.where(head0, 0.0, c2), axis=-1, keepdims=True)
            var = jnp.where(head0, v0, v1) * inv_n
            y = cen * lax.rsqrt(var + RWKV_GN_EPS) * lng_ref[:, cl] + lnb_ref[:, cl] + bonus_ref[0, sl, cl]
            y_ref[0, sl, cl] = (y * _silu(_f32(gate_ref[0, sl, cl]))).astype(y_ref.dtype)
        else:
            o_ref[0, sl, cl] = o


def _rwkv_mixer(proj3, mu, w0, w_lora_b, a0, a_lora_b, k_k, k_a, r_k, ln_g, ln_b):
    b, t, _ = proj3.shape
    tt = _pick(t, 256)
    nt = t // tt
    sub = tt // HALO

    def cur(off, width):
        return pl.BlockSpec((1, tt, width), lambda bi, ti: (bi, ti, off // width))

    def prev(off, width):
        return pl.BlockSpec((1, HALO, width),
                            lambda bi, ti: (bi, jnp.maximum(ti * sub - 1, 0), off // width))

    def nxt(off, width):
        return pl.BlockSpec((1, HALO, width),
                            lambda bi, ti: (bi, jnp.minimum((ti + 1) * sub, nt * sub - 1), off // width))

    def full(shape):
        return pl.BlockSpec(shape, lambda bi, ti: (0,) * len(shape))

    segs = ((OFF_CR, WC), (OFF_CK, WC), (OFF_CV, WC), (OFF_CC, N_CODES))
    in_specs = ([cur(o, w) for o, w in segs] + [prev(o, w) for o, w in segs] + [nxt(o, w) for o, w in segs]
                + [full((2, WC))] * 3 + [full((2, N_CODES))]
                + [full((2, WC)), full((2, N_CODES, WC)), full((2, WC)), full((2, N_CODES, WC))]
                + [full((1, WC))] * 3 + [full((LANE, LANE)), full((2 * tt, tt))])
    wl = jnp.zeros((2, N_CODES, WC), F32)
    al = jnp.zeros((2, N_CODES, WC), F32)
    for d in range(2):
        wl = wl.at[d, d * W_LORA:(d + 1) * W_LORA].set(w_lora_b[d])
        al = al.at[d, 2 * W_LORA + d * A_LORA:2 * W_LORA + (d + 1) * A_LORA].set(a_lora_b[d])
    row_i = np.arange(tt)
    shift = np.concatenate([row_i[:, None] - 1 == row_i[None, :], row_i[:, None] + 1 == row_i[None, :]], axis=0)
    shift = jnp.asarray(shift.astype(np.float32), dtype=BF16)
    lane_i = np.arange(LANE)
    gmat = jnp.asarray((lane_i[:, None] // NC == lane_i[None, :] // NC).astype(np.float32), dtype=BF16)
    row_out = pl.BlockSpec((1, tt, WC), lambda bi, ti: (bi, ti, 0))
    outs = pl.pallas_call(
        functools.partial(_rwkv_prep_kernel, tt),
        grid=(b, nt),
        in_specs=in_specs,
        out_specs=[row_out] * 10,
        out_shape=[jax.ShapeDtypeStruct((b, t, WC), dt) for dt in (SCAN_DTYPE,) * 3 + (F32,) * 3 + (SCAN_DTYPE,) * 4],
        compiler_params=_params("parallel", "parallel"),
    )(*([proj3] * 12), mu[:, :WC], mu[:, WC:2 * WC], mu[:, 2 * WC:3 * WC], mu[:, 3 * WC:],
      w0, wl.astype(BF16), a0, al.astype(BF16), k_k.reshape(1, WC), k_a.reshape(1, WC),
      r_k.reshape(1, WC), gmat, shift)
    r, v, kk, bonus, lw0, lw1, k0, k1, ka0, ka1 = outs

    ts = _pick(t, 512)
    ns = t // ts

    def scan(rev, final, lw, kd, ka, ob=None):
        wblk = PAIRS_C * LANE
        tmap = (lambda i: ns - 1 - i) if rev else (lambda i: i)
        blk = pl.BlockSpec((1, ts, wblk), lambda bi, h, ti: (bi, tmap(ti), h))
        head_vec = pl.BlockSpec((1, wblk), lambda bi, h, ti: (0, h))
        in_specs = [blk] * 6
        args = [r, lw, kd, v, ka, kk]
        if final:
            gate = pl.BlockSpec((1, ts, wblk), lambda bi, h, ti: (bi, tmap(ti), OFF_CG // wblk + h))
            in_specs += [blk, blk, gate, head_vec, head_vec]
            args += [ob, bonus, proj3, ln_g.reshape(1, WC), ln_b.reshape(1, WC)]
        return pl.pallas_call(
            functools.partial(_rwkv_scan_kernel, rev, final, ts),
            grid=(b, WC // wblk, ns),
            in_specs=in_specs,
            out_specs=blk,
            out_shape=jax.ShapeDtypeStruct((b, t, WC), BF16 if final else F32),
            scratch_shapes=[pltpu.VMEM((PAIRS_C, LANE, LANE), F32)],
            compiler_params=_params("parallel", "parallel", "arbitrary"),
        )(*args)

    ob = scan(True, False, lw1, k1, ka1)
    return scan(False, True, lw0, k0, ka0, ob=ob)


def _merge_kernel(final, ya_ref, yb_ref, yc_ref, mg_ref, x_ref, wa_ref, wb_ref, wc_ref, wo_ref,
                  fg_ref, o_ref):
    za = jnp.dot(ya_ref[...], wa_ref[...], preferred_element_type=F32)
    zb = jnp.dot(yb_ref[...], wb_ref[...], preferred_element_type=F32)
    zc = jnp.dot(yc_ref[...], wc_ref[...], preferred_element_type=F32)
    mg = _f32(mg_ref[...])
    mixed = (_sigmoid(mg[:, :D_MODEL]) * za
             + _sigmoid(mg[:, D_MODEL:2 * D_MODEL]) * zb
             + _sigmoid(mg[:, 2 * D_MODEL:]) * zc)
    out = x_ref[...] + jnp.dot(mixed.astype(BF16), wo_ref[...], preferred_element_type=F32)
    if final:
        ms = jnp.mean(out * out, axis=-1, keepdims=True)
        out = out * lax.rsqrt(ms + EPS) * fg_ref[...]
    o_ref[...] = out


def _merge(ya, yb, yc, proj, x2, wa, wb, wc, wo, fg, final):
    m = x2.shape[0]
    tm = _pick(m, 512)

    def rows(width):
        return pl.BlockSpec((tm, width), lambda i: (i, 0))

    def full(shape):
        return pl.BlockSpec(shape, lambda i: (0, 0))

    return pl.pallas_call(
        functools.partial(_merge_kernel, final),
        grid=(m // tm,),
        in_specs=[rows(WA), rows(WB_V), rows(WC), rows(3 * D_MODEL), rows(D_MODEL),
                  full((WA, D_MODEL)), full((WB_V, D_MODEL)), full((WC, D_MODEL)),
                  full((D_MODEL, D_MODEL)), full((1, D_MODEL))],
        out_specs=rows(D_MODEL),
        out_shape=jax.ShapeDtypeStruct((m, D_MODEL), F32),
        compiler_params=_params("parallel", vmem=PROJ_VMEM_LIMIT),
    )(ya, yb, yc, proj, x2, wa.astype(BF16), wb.astype(BF16), wc.astype(BF16), wo.astype(BF16),
      fg.reshape(1, D_MODEL))


def kernel(x, norm_g, w_in, hgrn_lb_logits, hgrn_norm_g, ret_norm_g, ret_norm_b, rwkv_mu, rwkv_w0,
           rwkv_w_lora_b, rwkv_a0, rwkv_a_lora_b, rwkv_k_k, rwkv_k_a, rwkv_r_k, rwkv_ln_g,
           rwkv_ln_b, w_branch_a, w_branch_b, w_branch_c, w_out, final_norm_g):
    b, t, d = x.shape
    m = b * t
    depth = w_in.shape[0]
    p_lb = jax.nn.softmax(hgrn_lb_logits.astype(F32), axis=0)
    lbs = jnp.cumsum(p_lb, axis=0) - p_lb[0:1]
    x2 = x.reshape(m, d)
    for l in range(depth):
        proj = _inproj(x2, norm_g[l], _permute_w_in(w_in, l))
        proj3 = proj.reshape(b, t, N_PAD)
        ya = _hgrn_mixer(proj3, lbs[l], hgrn_norm_g[l])
        yb = _ret_mixer(proj3, ret_norm_g[l], ret_norm_b[l])
        yc = _rwkv_mixer(proj3, rwkv_mu[l], rwkv_w0[l], rwkv_w_lora_b[l], rwkv_a0[l],
                         rwkv_a_lora_b[l], rwkv_k_k[l], rwkv_k_a[l], rwkv_r_k[l],
                         rwkv_ln_g[l], rwkv_ln_b[l])
        x2 = _merge(ya.reshape(m, WA), yb.reshape(m, WB_V), yc.reshape(m, WC), proj, x2,
                    w_branch_a[l], w_branch_b[l], w_branch_c[l], w_out[l], final_norm_g,
                    final=(l == depth - 1))
    return x2.reshape(b, t, d)
```

```python
import functools

import numpy as np
import jax
import jax.numpy as jnp
from jax import lax
from jax.experimental import pallas as pl
from jax.experimental.pallas import tpu as pltpu

F32 = jnp.float32
BF16 = jnp.bfloat16

D_MODEL = 1024
EPS = 1e-6
F_MIN = 1e-6
DKA = 128
HA = D_MODEL // DKA
WA = HA * DKA
HB = 8
DKB = D_MODEL // HB
DVB = 2 * DKB
WB_QK = HB * DKB
WB_V = HB * DVB
ROPE_BASE = 10000.0
RET_GN_EPS = 1e-5
NC = 64
HC = D_MODEL // NC
WC = HC * NC
W_LORA = 64
A_LORA = 64
RWKV_GN_EPS = 64e-5
KK_NORM_EPS = 1e-12
N_CODES = 2 * W_LORA + 2 * A_LORA
C_SHIFT = 3 * WC + N_CODES

SRC_B = 5 * WA
SRC_CS = SRC_B + 2 * WB_QK + 2 * WB_V
SRC_CG = SRC_CS + C_SHIFT
SRC_MERGE = SRC_CG + WC
N_IN = SRC_MERGE + 3 * D_MODEL

OFF_MERGE = 0
OFF_AQ = OFF_MERGE + 3 * D_MODEL
OFF_AFF = OFF_AQ + WA
OFF_AFB = OFF_AFF + WA
OFF_AI = OFF_AFB + WA
OFF_AG = OFF_AI + WA
OFF_BQ = OFF_AG + WA
OFF_BK = OFF_BQ + WB_QK
OFF_BV = OFF_BK + WB_QK
OFF_BG = OFF_BV + WB_V
OFF_CG = OFF_BG + WB_V
OFF_CR = OFF_CG + WC
OFF_CK = OFF_CR + WC
OFF_CV = OFF_CK + WC
OFF_CC = OFF_CV + WC
LANE = 128
PROJ_DTYPE = BF16
SCAN_DTYPE = BF16
HALO = 16
PROJ_SPLIT = 4
N_PAD = -(-(OFF_CC + N_CODES) // (PROJ_SPLIT * LANE)) * PROJ_SPLIT * LANE
PROJ_TN = N_PAD // PROJ_SPLIT
MXU_N = 256
WPERM_TN = 256

CHUNK_A = 128
GROUP_A = 8
HEADS_A = 2
CHUNK_B = 256
CHUNK_C = 64
PAIRS_C = 4
VMEM_LIMIT = 48 * 1024 * 1024
PROJ_VMEM_LIMIT = 56 * 1024 * 1024

NT_DIMS = (((1,), (1,)), ((), ()))
TN_DIMS = (((0,), (0,)), ((), ()))


def _mm(a, b):
    return jnp.dot(a.astype(BF16), b.astype(BF16), preferred_element_type=F32)


def _mm_nt(a, b):
    return lax.dot_general(a.astype(BF16), b.astype(BF16), NT_DIMS, preferred_element_type=F32)


def _mm_tn(a, b):
    return lax.dot_general(a.astype(BF16), b.astype(BF16), TN_DIMS, preferred_element_type=F32)


def _split_dot(ones_mat, x):
    hi = x.astype(BF16)
    lo = (x - hi.astype(F32)).astype(BF16)
    return jnp.dot(ones_mat, hi, preferred_element_type=F32) + jnp.dot(ones_mat, lo, preferred_element_type=F32)


def _split_dot_right(x, ones_mat):
    hi = x.astype(BF16)
    lo = (x - hi.astype(F32)).astype(BF16)
    return jnp.dot(hi, ones_mat, preferred_element_type=F32) + jnp.dot(lo, ones_mat, preferred_element_type=F32)


def _f32(z):
    return z.astype(F32)


def _sigmoid(z):
    return 0.5 * jnp.tanh(0.5 * z) + 0.5


def _cumsum_rows(x, rev):
    n = x.shape[0]
    rows = lax.broadcasted_iota(jnp.int32, (n, 1), 0)
    step = 1
    while step < n:
        if rev:
            x = x + jnp.where(rows < n - step, pltpu.roll(x, n - step, axis=0), 0.0)
        else:
            x = x + jnp.where(rows >= step, pltpu.roll(x, step, axis=0), 0.0)
        step *= 2
    return x


def _silu(z):
    h = 0.5 * z
    return h * jnp.tanh(h) + h


def _pick(n, pref):
    t = min(n, pref)
    assert n % t == 0, (n, pref)
    return t


def _params(*sem, vmem=VMEM_LIMIT):
    return pltpu.CompilerParams(dimension_semantics=sem, vmem_limit_bytes=vmem)


def _inproj_kernel(x_ref, g_ref, w_ref, o_ref, h_ref):
    @pl.when(pl.program_id(1) == 0)
    def _():
        x = x_ref[...]
        ms = jnp.mean(x * x, axis=-1, keepdims=True)
        h_ref[...] = (x * lax.rsqrt(ms + EPS) * g_ref[...]).astype(BF16)

    def tile(cols):
        o_ref[:, cols] = jnp.dot(h_ref[...], w_ref[:, cols], preferred_element_type=F32).astype(o_ref.dtype)

    for start in range(0, PROJ_TN, MXU_N):
        tile(pl.ds(start, min(MXU_N, PROJ_TN - start)))


def _inproj(x2, g, w_bf16):
    m = x2.shape[0]
    tm = _pick(m, 1024)
    return pl.pallas_call(
        _inproj_kernel,
        grid=(m // tm, N_PAD // PROJ_TN),
        in_specs=[
            pl.BlockSpec((tm, D_MODEL), lambda i, j: (i, 0)),
            pl.BlockSpec((1, D_MODEL), lambda i, j: (0, 0)),
            pl.BlockSpec((D_MODEL, PROJ_TN), lambda i, j: (0, j)),
        ],
        out_specs=pl.BlockSpec((tm, PROJ_TN), lambda i, j: (i, j)),
        out_shape=jax.ShapeDtypeStruct((m, N_PAD), PROJ_DTYPE),
        scratch_shapes=[pltpu.VMEM((tm, D_MODEL), BF16)],
        compiler_params=_params("parallel", "arbitrary", vmem=PROJ_VMEM_LIMIT),
    )(x2, g.reshape(1, D_MODEL), w_bf16)


def _wperm_kernel(nvalid, a_ref, b_ref, o_ref):
    j = pl.program_id(0)
    for idx, src in enumerate((a_ref, b_ref)):
        cols = pl.ds(idx * WPERM_TN, WPERM_TN)
        blk = src[...].astype(BF16)
        o_ref[:, cols] = jnp.where(2 * j + idx < nvalid, blk, jnp.zeros_like(blk))


def _src_block(jd):
    n_merge = (N_IN - SRC_MERGE) // WPERM_TN
    n_ab = SRC_CS // WPERM_TN
    n_cg = (SRC_MERGE - SRC_CG) // WPERM_TN
    n_src = N_IN // WPERM_TN
    return jnp.where(jd < n_merge, jd + SRC_MERGE // WPERM_TN,
                     jnp.where(jd < n_merge + n_ab, jd - n_merge,
                               jnp.where(jd < n_merge + n_ab + n_cg, jd - n_merge - n_ab + SRC_CG // WPERM_TN,
                                         jnp.minimum(jd - n_merge - n_cg, n_src - 1))))


def _permute_w_in(w_in, layer):
    assert all(off % WPERM_TN == 0 for off in (SRC_CS, SRC_CG, SRC_MERGE, N_IN)) and N_PAD % (2 * WPERM_TN) == 0

    def src(idx):
        return pl.BlockSpec((None, D_MODEL, WPERM_TN), lambda j: (layer, 0, _src_block(2 * j + idx)))

    return pl.pallas_call(
        functools.partial(_wperm_kernel, N_IN // WPERM_TN),
        grid=(N_PAD // (2 * WPERM_TN),),
        in_specs=[src(0), src(1)],
        out_specs=pl.BlockSpec((D_MODEL, 2 * WPERM_TN), lambda j: (0, j)),
        out_shape=jax.ShapeDtypeStruct((D_MODEL, N_PAD), BF16),
        compiler_params=_params("parallel"),
    )(w_in, w_in)


def _each(fn, *lists):
    return [fn(*items) for items in zip(*lists)]


def _hgrn_kernel(rev, final, tt, q_ref, f_ref, v_ref, lb_ref, tri_ref, *rest):
    if final:
        ob_ref, gate_ref, ng_ref, y_ref, st_ref = rest
    else:
        o_ref, st_ref = rest

    @pl.when(pl.program_id(2) == 0)
    def _():
        st_ref[...] = jnp.zeros_like(st_ref)

    n = CHUNK_A
    ri = lax.broadcasted_iota(jnp.int32, (n, n), 0)
    ci = lax.broadcasted_iota(jnp.int32, (n, n), 1)
    g_bits = int(np.log2(GROUP_A))
    diag_mask = ((ri >> g_bits) == (ci >> g_bits)) & ((ci >= ri) if rev else (ci <= ri))
    halves = [GROUP_A << i for i in range(int(np.log2(n // GROUP_A)))]
    level_masks = []
    for half in halves:
        hb = int(np.log2(half))
        q_side, k_side = (0, 1) if rev else (1, 0)
        level_masks.append(((ri >> (hb + 1)) == (ci >> (hb + 1)))
                           & (((ri >> hb) & 1) == q_side) & (((ci >> hb) & 1) == k_side))

    def ref_rows(b, blk, idx):
        picked = b.reshape(n // blk, blk, DKA)[:, idx:idx + 1, :]
        return jnp.broadcast_to(picked, (n // blk, blk, DKA)).reshape(n, DKA)

    nchunk = tt // n
    order = [(nchunk - 1 - i) if rev else i for i in range(nchunk)]
    items = [(pl.ds(c * n, n), pl.ds(hh * DKA, DKA), hh) for c in order for hh in range(HEADS_A)]
    sls = [it[0] for it in items]
    cls = [it[1] for it in items]
    c0 = [0.5 + 0.5 * lb_ref[it[2]] for it in items]
    c1 = [0.5 - 0.5 * lb_ref[it[2]] for it in items]
    gm = _each(lambda sl, cl, c: c * jnp.tanh(0.5 * _f32(f_ref[0, sl, cl])), sls, cls, c1)
    lf = _each(lambda m, c: jnp.log2(jnp.maximum(c + m, F_MIN)), gm, c0)
    k = _each(lambda m, c: c - m, gm, c1)
    q = _each(lambda sl, cl: _silu(_f32(q_ref[0, sl, cl])), sls, cls)
    v = _each(lambda sl, cl: v_ref[0, sl, cl], sls, cls)
    b = _each(lambda z: _split_dot(tri_ref[...], z), lf)
    d0 = _each(lambda z: z - ref_rows(z, GROUP_A, GROUP_A // 2 if rev else GROUP_A // 2 - 1), b)
    qb = _each(lambda z: z.astype(BF16), q)
    kb = _each(lambda z: z.astype(BF16), k)
    scores = _each(lambda qq, kk, d: jnp.where(
        diag_mask, _mm_nt(qq * jnp.exp2(d).astype(BF16), kk * jnp.exp2(-d).astype(BF16)), 0.0), qb, kb, d0)
    for half, mask in zip(halves, level_masks):
        e = _each(lambda z: jnp.exp2(-jnp.abs(z - ref_rows(z, 2 * half, half if rev else half - 1))).astype(BF16),
                  b)
        scores = _each(lambda sc, qq, kk, ee: jnp.where(mask, _mm_nt(qq * ee, kk * ee), sc), scores, qb, kb, e)
    o_intra = _each(_mm, scores, v)
    b_end = _each(lambda z: z[0:1, :] if rev else z[n - 1:n, :], b)
    q_in = _each(lambda qq, z: qq * jnp.exp2(z), q, b)
    kv = _each(lambda vv, kk, z, ze: _mm_tn(vv, kk * jnp.exp2(ze - z)), v, k, b, b_end)
    e_end = _each(jnp.exp2, b_end)

    outs = []
    st = [st_ref[hh] for hh in range(HEADS_A)]
    for (sl, cl, hh), oi, qi, kvi, ee in zip(items, o_intra, q_in, kv, e_end):
        outs.append((sl, cl, hh, oi + _mm_nt(qi, st[hh])))
        st[hh] = st[hh] * ee + kvi
    for hh in range(HEADS_A):
        st_ref[hh] = st[hh]

    for sl, cl, hh, o in outs:
        if final:
            o = o + ob_ref[0, sl, cl]
            ms = jnp.mean(o * o, axis=-1, keepdims=True)
            y = o * lax.rsqrt(ms + EPS) * ng_ref[hh] * _silu(_f32(gate_ref[0, sl, cl]))
            y_ref[0, sl, cl] = y.astype(y_ref.dtype)
        else:
            o_ref[0, sl, cl] = o


def _tri_blockdiag(n, blk, rev):
    i = np.arange(n)
    same = (i[:, None] // blk) == (i[None, :] // blk)
    tri = (i[None, :] >= i[:, None]) if rev else (i[None, :] <= i[:, None])
    return jnp.asarray((same & tri).astype(np.float32), dtype=BF16)


def _hgrn_pass(proj3, lb, rev, final, ob=None, norm_g=None):
    b, t, _ = proj3.shape
    tt = _pick(t, 2048)
    nt = t // tt
    tmap = (lambda i: nt - 1 - i) if rev else (lambda i: i)

    wblk = HEADS_A * DKA

    def col(off):
        return pl.BlockSpec((1, tt, wblk), lambda bi, h, ti: (bi, tmap(ti), off // wblk + h))

    head_vec = pl.BlockSpec((HEADS_A, 1, DKA), lambda bi, h, ti: (h, 0, 0))
    in_specs = [col(OFF_AQ), col(OFF_AFB if rev else OFF_AFF), col(OFF_AI), head_vec,
                pl.BlockSpec((LANE, LANE), lambda bi, h, ti: (0, 0))]
    args = [proj3, proj3, proj3, lb.reshape(HA, 1, DKA), _tri_blockdiag(LANE, CHUNK_A, rev)]
    out_block = pl.BlockSpec((1, tt, wblk), lambda bi, h, ti: (bi, tmap(ti), h))
    if final:
        in_specs += [out_block, col(OFF_AG), head_vec]
        args += [ob, proj3, norm_g.reshape(HA, 1, DKA)]
        out_dtype = BF16
    else:
        out_dtype = F32
    return pl.pallas_call(
        functools.partial(_hgrn_kernel, rev, final, tt),
        grid=(b, HA // HEADS_A, nt),
        in_specs=in_specs,
        out_specs=out_block,
        out_shape=jax.ShapeDtypeStruct((b, t, WA), out_dtype),
        scratch_shapes=[pltpu.VMEM((HEADS_A, DKA, DKA), F32)],
        compiler_params=_params("parallel", "parallel", "arbitrary"),
    )(*args)


def _hgrn_mixer(proj3, lbs, norm_g):
    ob = _hgrn_pass(proj3, lbs[1], rev=True, final=False)
    return _hgrn_pass(proj3, lbs[0], rev=False, final=True, ob=ob, norm_g=norm_g)


def _rotary(z, cos2, sin2):
    return z * cos2 + pltpu.roll(z, DKB // 2, axis=1) * sin2


def _ret_state_kernel(nsub, k_ref, v_ref, cos_ref, sin_ref, lg_ref, sb_ref, st_ref):
    @pl.when(pl.program_id(2) == 0)
    def _():
        st_ref[...] = jnp.zeros_like(st_ref)

    cb = CHUNK_B
    lg2 = lg_ref[0]
    lg = lg2[:, :DKB]
    pos = lax.broadcasted_iota(jnp.int32, (cb, DKB), 0).astype(F32)
    dec = jnp.exp(pos * lg) * (DKB ** -0.5)
    order = list(range(nsub - 1, -1, -1))
    sls = [pl.ds(c * cb, cb) for c in order]
    kd = _each(lambda sl: _rotary(_f32(k_ref[0, sl, :]), cos_ref[sl, :], sin_ref[sl, :]) * dec, sls)
    kv = _each(lambda kk, sl: _mm_tn(kk, v_ref[0, sl, :]), kd, sls)
    e_chunk = jnp.exp(cb * lg2)
    st = st_ref[...]
    for c, kvi in zip(order, kv):
        sb_ref[0, 0, c] = st
        st = st * e_chunk + kvi
    st_ref[...] = st


def _ret_out_kernel(nsub, q_ref, k_ref, v_ref, cos_ref, sin_ref, lg_ref, sb_ref, gate_ref, gg_ref,
                    gb_ref, y_ref, st_ref):
    @pl.when(pl.program_id(2) == 0)
    def _():
        st_ref[...] = jnp.zeros_like(st_ref)

    cb = CHUNK_B
    lg2 = lg_ref[0]
    lg = lg2[:, :DKB]
    pos = lax.broadcasted_iota(jnp.int32, (cb, DKB), 0).astype(F32)
    ri = lax.broadcasted_iota(jnp.int32, (cb, cb), 0)
    ci = lax.broadcasted_iota(jnp.int32, (cb, cb), 1)
    decay = jnp.exp(jnp.abs(ri - ci).astype(F32) * lg2) * (DKB ** -0.5)
    dec_fwd = jnp.exp((pos + 1.0) * lg)
    dec_bwd = jnp.exp((cb - pos) * lg)
    dec_key = jnp.exp((cb - 1.0 - pos) * lg) * (DKB ** -0.5)
    e_chunk = jnp.exp(cb * lg2)

    sls = [pl.ds(c * cb, cb) for c in range(nsub)]
    q = _each(lambda sl: _rotary(_f32(q_ref[0, sl, :]), cos_ref[sl, :], sin_ref[sl, :]), sls)
    k = _each(lambda sl: _rotary(_f32(k_ref[0, sl, :]), cos_ref[sl, :], sin_ref[sl, :]), sls)
    v = _each(lambda sl: v_ref[0, sl, :], sls)
    scores = _each(lambda qq, kk: _mm_nt(qq, kk) * decay, q, k)
    o = _each(_mm, scores, v)
    o = _each(lambda oo, qq, c: oo + _mm(qq * dec_bwd, sb_ref[0, 0, c]), o, q, list(range(nsub)))
    kv = _each(lambda kk, vv: _mm_tn(kk * dec_key, vv), k, v)
    q_fwd = _each(lambda qq: qq * dec_fwd, q)
    st = st_ref[...]
    outs = []
    for oo, qf, kvi in zip(o, q_fwd, kv):
        outs.append(oo + _mm(qf, st))
        st = st * e_chunk + kvi
    st_ref[...] = st

    for sl, oo in zip(sls, outs):
        mu = jnp.mean(oo, axis=-1, keepdims=True)
        cen = oo - mu
        var = jnp.mean(cen * cen, axis=-1, keepdims=True)
        y = (cen * lax.rsqrt(var + RET_GN_EPS) * gg_ref[0] + gb_ref[0]) * _silu(_f32(gate_ref[0, sl, :]))
        y_ref[0, sl, :] = y.astype(y_ref.dtype)


def _ret_mixer(proj3, gn_g, gn_b):
    b, t, _ = proj3.shape
    cb = CHUNK_B
    tt = _pick(t, 16 * cb)
    nsub = tt // cb
    nt = t // tt
    half = DKB // 2
    inv = ROPE_BASE ** (-jnp.arange(0, DKB, 2, dtype=F32) / DKB)
    ang_hi = (jnp.arange(t // cb, dtype=F32) * cb)[:, None, None] * inv
    ang_lo = jnp.arange(cb, dtype=F32)[None, :, None] * inv
    cos = (jnp.cos(ang_hi) * jnp.cos(ang_lo) - jnp.sin(ang_hi) * jnp.sin(ang_lo)).reshape(t, half)
    sin = (jnp.sin(ang_hi) * jnp.cos(ang_lo) + jnp.cos(ang_hi) * jnp.sin(ang_lo)).reshape(t, half)
    cos2 = jnp.concatenate([cos, cos], axis=-1)
    sin2 = jnp.concatenate([-sin, sin], axis=-1)
    log_gamma = jnp.log1p(-jnp.exp2(-5.0 - jnp.arange(HB, dtype=F32)))
    lg = jnp.broadcast_to(log_gamma[:, None, None], (HB, 1, DVB))

    def col(off, width, tmap):
        return pl.BlockSpec((1, tt, width), lambda bi, h, c: (bi, tmap(c), off // width + h))

    def tab(tmap):
        return pl.BlockSpec((tt, DKB), lambda bi, h, c: (tmap(c), 0))

    lg_spec = pl.BlockSpec((1, 1, DVB), lambda bi, h, c: (h, 0, 0))
    rmap = lambda c: nt - 1 - c
    fmap = lambda c: c

    sb = pl.pallas_call(
        functools.partial(_ret_state_kernel, nsub),
        grid=(b, HB, nt),
        in_specs=[col(OFF_BK, DKB, rmap), col(OFF_BV, DVB, rmap), tab(rmap), tab(rmap), lg_spec],
        out_specs=pl.BlockSpec((1, 1, nsub, DKB, DVB), lambda bi, h, c: (bi, h, rmap(c), 0, 0)),
        out_shape=jax.ShapeDtypeStruct((b, HB, t // cb, DKB, DVB), F32),
        scratch_shapes=[pltpu.VMEM((DKB, DVB), F32)],
        compiler_params=_params("parallel", "parallel", "arbitrary"),
    )(proj3, proj3, cos2, sin2, lg)

    head_vec = pl.BlockSpec((1, 1, DVB), lambda bi, h, c: (h, 0, 0))
    return pl.pallas_call(
        functools.partial(_ret_out_kernel, nsub),
        grid=(b, HB, nt),
        in_specs=[col(OFF_BQ, DKB, fmap), col(OFF_BK, DKB, fmap), col(OFF_BV, DVB, fmap),
                  tab(fmap), tab(fmap), lg_spec,
                  pl.BlockSpec((1, 1, nsub, DKB, DVB), lambda bi, h, c: (bi, h, c, 0, 0)),
                  col(OFF_BG, DVB, fmap), head_vec, head_vec],
        out_specs=pl.BlockSpec((1, tt, DVB), lambda bi, h, c: (bi, c, h)),
        out_shape=jax.ShapeDtypeStruct((b, t, WB_V), BF16),
        scratch_shapes=[pltpu.VMEM((DKB, DVB), F32)],
        compiler_params=_params("parallel", "parallel", "arbitrary"),
    )(proj3, proj3, proj3, cos2, sin2, lg, sb, proj3,
      gn_g.reshape(HB, 1, DVB), gn_b.reshape(HB, 1, DVB))


def _group_sum(x, gmat):
    cols = [_split_dot_right(x[:, i * LANE:(i + 1) * LANE], gmat) for i in range(x.shape[1] // LANE)]
    return jnp.concatenate(cols, axis=1)


def _rwkv_prep_kernel(tt, r_ref, k_ref, v_ref, c_ref, rp_ref, kp_ref, vp_ref, cp_ref,
                      rn_ref, kn_ref, vn_ref, cn_ref, mur_ref, muk_ref, muv_ref, muc_ref,
                      w0_ref, wl_ref, a0_ref, al_ref, kk_ref, ka_ref, rk_ref, gmat_ref, shift_ref,
                      r_o, v_o, kk_o, bonus_o, lw0_o, lw1_o, k0_o, k1_o, ka0_o, ka1_o):
    ti = pl.program_id(1)
    has_prev = (ti > 0).astype(F32)
    has_next = (ti < pl.num_programs(1) - 1).astype(F32)
    rows = lax.broadcasted_iota(jnp.int32, (tt, 1), 0)

    def shifted(p_ref, prev_ref, next_ref, mu_ref):
        both = jnp.dot(shift_ref[...], p_ref[0].astype(BF16), preferred_element_type=F32)
        p = _f32(p_ref[0])
        prev_row = _f32(prev_ref[0, HALO - 1:HALO, :]) * has_prev
        next_row = _f32(next_ref[0, 0:1, :]) * has_next
        before = jnp.where(rows == 0, prev_row, both[:tt])
        after = jnp.where(rows == tt - 1, next_row, both[tt:])
        return p + mu_ref[0:1, :] * (before - p) + mu_ref[1:2, :] * (after - p)

    r = shifted(r_ref, rp_ref, rn_ref, mur_ref)
    k = shifted(k_ref, kp_ref, kn_ref, muk_ref)
    v = shifted(v_ref, vp_ref, vn_ref, muv_ref)
    codes = shifted(c_ref, cp_ref, cn_ref, muc_ref)
    gmat = gmat_ref[...]

    kraw = k * kk_ref[...]
    kk = kraw * lax.rsqrt(_group_sum(kraw * kraw, gmat) + KK_NORM_EPS)
    r_o[0] = r.astype(r_o.dtype)
    v_o[0] = v.astype(v_o.dtype)
    kk_o[0] = kk.astype(kk_o.dtype)
    tcodes = jnp.tanh(codes)
    bonus = jnp.zeros_like(r)
    half_kk = 0.5 * kk
    e1 = 0.5 * ka_ref[...]
    e0 = 1.0 - e1
    for d, (lw_o, kd_o, ka_o) in enumerate(((lw0_o, k0_o, ka0_o), (lw1_o, k1_o, ka1_o))):
        w = w0_ref[d:d + 1, :] + _mm(tcodes, wl_ref[d])
        half_c = np.float32(-0.5 * np.exp(-0.5) * np.log2(np.e))
        lw_o[0] = half_c * jnp.tanh(0.5 * w) + half_c
        th = jnp.tanh(0.5 * (a0_ref[d:d + 1, :] + _mm(codes, al_ref[d])))
        kd = k * (e0 + e1 * th)
        kd_o[0] = kd.astype(kd_o.dtype)
        ka_o[0] = (half_kk * th + half_kk).astype(ka_o.dtype)
        bonus = bonus + _group_sum(r * kd * rk_ref[...], gmat) * v
    bonus_o[0] = bonus


def _rwkv_scan_kernel(rev, final, tt, r_ref, lw_ref, k_ref, v_ref, ka_ref, kk_ref, *rest):
    if final:
        ob_ref, bonus_ref, gate_ref, lng_ref, lnb_ref, y_ref, h_ref = rest
    else:
        o_ref, h_ref = rest

    @pl.when(pl.program_id(2) == 0)
    def _():
        h_ref[...] = jnp.zeros_like(h_ref)

    n2 = 2 * CHUNK_C
    lane = lax.broadcasted_iota(jnp.int32, (1, LANE), 1)
    head0 = lane < NC
    ri = lax.broadcasted_iota(jnp.int32, (n2, n2), 0)
    ci = lax.broadcasted_iota(jnp.int32, (n2, n2), 1)
    strict = (ci > ri) if rev else (ci < ri)
    incl = (ci >= ri) if rev else (ci <= ri)
    eye = ri == ci
    nchunk = tt // CHUNK_C

    def stack(z):
        zero = jnp.zeros_like(z)
        return jnp.concatenate([jnp.where(head0, z, zero), jnp.where(head0, zero, z)], axis=0)

    def stack_b(z):
        return stack(z.astype(BF16))

    order = [(nchunk - 1 - i) if rev else i for i in range(nchunk)]
    items = [(pl.ds(c * CHUNK_C, CHUNK_C), pl.ds(p * LANE, LANE), p) for c in order for p in range(PAIRS_C)]
    rows = [it[0] for it in items]
    cols = [it[1] for it in items]

    lws = _each(lambda sl, cl: lw_ref[0, sl, cl], rows, cols)
    g = _each(lambda lw: _cumsum_rows(lw, rev), lws)
    e_end = _each(lambda gu: jnp.exp2(gu[0:1, :] if rev else gu[CHUNK_C - 1:CHUNK_C, :]), g)
    e_neg = _each(lambda gu: jnp.exp2(-gu), g)
    bs = _each(lambda sl, cl, gu, lw: stack_b(-_f32(kk_ref[0, sl, cl]) * jnp.exp2(gu - lw)), rows, cols, g, lws)
    a_sc = _each(lambda sl, cl, en: _f32(ka_ref[0, sl, cl]) * en, rows, cols, e_neg)
    k_sc = _each(lambda sl, cl, en: _f32(k_ref[0, sl, cl]) * en, rows, cols, e_neg)
    as_ = _each(stack_b, a_sc)
    ks = _each(stack_b, k_sc)
    ake = _each(lambda a, k, e: jnp.concatenate([stack_b(a * e), stack_b(k * e)], axis=0), a_sc, k_sc, e_end)
    rs = _each(lambda sl, cl, gu: stack(_f32(r_ref[0, sl, cl]) * jnp.exp2(gu)), rows, cols, g)
    vs = _each(lambda sl, cl: stack_b(v_ref[0, sl, cl]), rows, cols)

    ak = _each(lambda a, k: jnp.concatenate([a, k], axis=0), as_, ks)
    sc_b = _each(_mm_nt, bs, ak)
    sc_r = _each(_mm_nt, rs, ak)
    a_ab = _each(lambda z: jnp.where(strict, z[:, :n2], 0.0), sc_b)
    a_ak = _each(lambda z: jnp.where(strict, z[:, n2:], 0.0), sc_b)
    m_rak = _each(lambda z: jnp.where(jnp.concatenate([incl, incl], axis=1), z, 0.0), sc_r)
    tinv = _each(lambda a: jnp.where(eye, 1.0, 0.0) + a, a_ab)
    pw = _each(lambda a: _mm(a, a), a_ab)
    akv = _each(_mm, a_ak, vs)
    nsteps = int(np.log2(CHUNK_C)) - 1
    for i in range(nsteps):
        if i < nsteps - 1:
            both = _each(lambda t, p: _mm(jnp.concatenate([t, p], axis=0), p), tinv, pw)
            tinv = _each(lambda t, b: t + b[:n2], tinv, both)
            pw = _each(lambda b: b[n2:], both)
        else:
            tinv = _each(lambda t, p: t + _mm(t, p), tinv, pw)
    pwv = _each(lambda t, b, x: _mm(t, jnp.concatenate([b, x.astype(BF16)], axis=1)), tinv, bs, akv)
    low = _each(lambda z, v: jnp.concatenate([z.astype(BF16), jnp.concatenate([jnp.zeros_like(v), v], axis=1)],
                                             axis=0), pwv, vs)
    x1 = _each(_mm, m_rak, low)
    x2 = _each(_mm_tn, ake, low)
    top = _each(lambda r, z, e, y: jnp.concatenate([r + z[:, :LANE], jnp.where(eye, e, 0.0) + y[:, :LANE]], axis=0),
                rs, x1, e_end, x2)
    o_in = _each(lambda z: z[:, LANE:], x1)
    dmat = _each(lambda y: y[:, LANE:], x2)

    outs = []
    h = [h_ref[p] for p in range(PAIRS_C)]
    for (sl, cl, p), tp, oi, dm in zip(items, top, o_in, dmat):
        both = _mm(tp, h[p])
        h[p] = both[n2:] + dm
        os_ = both[:n2] + oi
        outs.append((sl, cl, os_[:CHUNK_C, :] + os_[CHUNK_C:, :]))
    for p in range(PAIRS_C):
        h_ref[p] = h[p]

    for sl, cl, o in outs:
        if final:
            o = o + ob_ref[0, sl, cl]
            inv_n = 1.0 / NC
            s0 = jnp.sum(jnp.where(head0, o, 0.0), axis=-1, keepdims=True)
            s1 = jnp.sum(jnp.where(head0, 0.0, o), axis=-1, keepdims=True)
            cen = o - jnp.where(head0, s0, s1) * inv_n
            c2 = cen * cen
            v0 = jnp.sum(jnp.where(head0, c2, 0.0), axis=-1, keepdims=True)
            v1 = jnp.sum(jnp.where(head0, 0.0, c2), axis=-1, keepdims=True)
            var = jnp.where(head0, v0, v1) * inv_n
            y = cen * lax.rsqrt(var + RWKV_GN_EPS) * lng_ref[:, cl] + lnb_ref[:, cl] + bonus_ref[0, sl, cl]
            y_ref[0, sl, cl] = (y * _silu(_f32(gate_ref[0, sl, cl]))).astype(y_ref.dtype)
        else:
            o_ref[0, sl, cl] = o


def _rwkv_mixer(proj3, mu, w0, w_lora_b, a0, a_lora_b, k_k, k_a, r_k, ln_g, ln_b):
    b, t, _ = proj3.shape
    tt = _pick(t, 256)
    nt = t // tt
    sub = tt // HALO

    def cur(off, width):
        return pl.BlockSpec((1, tt, width), lambda bi, ti: (bi, ti, off // width))

    def prev(off, width):
        return pl.BlockSpec((1, HALO, width),
                            lambda bi, ti: (bi, jnp.maximum(ti * sub - 1, 0), off // width))

    def nxt(off, width):
        return pl.BlockSpec((1, HALO, width),
                            lambda bi, ti: (bi, jnp.minimum((ti + 1) * sub, nt * sub - 1), off // width))

    def full(shape):
        return pl.BlockSpec(shape, lambda bi, ti: (0,) * len(shape))

    segs = ((OFF_CR, WC), (OFF_CK, WC), (OFF_CV, WC), (OFF_CC, N_CODES))
    in_specs = ([cur(o, w) for o, w in segs] + [prev(o, w) for o, w in segs] + [nxt(o, w) for o, w in segs]
                + [full((2, WC))] * 3 + [full((2, N_CODES))]
                + [full((2, WC)), full((2, N_CODES, WC)), full((2, WC)), full((2, N_CODES, WC))]
                + [full((1, WC))] * 3 + [full((LANE, LANE)), full((2 * tt, tt))])
    wl = jnp.zeros((2, N_CODES, WC), F32)
    al = jnp.zeros((2, N_CODES, WC), F32)
    for d in range(2):
        wl = wl.at[d, d * W_LORA:(d + 1) * W_LORA].set(w_lora_b[d])
        al = al.at[d, 2 * W_LORA + d * A_LORA:2 * W_LORA + (d + 1) * A_LORA].set(a_lora_b[d])
    row_i = np.arange(tt)
    shift = np.concatenate([row_i[:, None] - 1 == row_i[None, :], row_i[:, None] + 1 == row_i[None, :]], axis=0)
    shift = jnp.asarray(shift.astype(np.float32), dtype=BF16)
    lane_i = np.arange(LANE)
    gmat = jnp.asarray((lane_i[:, None] // NC == lane_i[None, :] // NC).astype(np.float32), dtype=BF16)
    row_out = pl.BlockSpec((1, tt, WC), lambda bi, ti: (bi, ti, 0))
    outs = pl.pallas_call(
        functools.partial(_rwkv_prep_kernel, tt),
        grid=(b, nt),
        in_specs=in_specs,
        out_specs=[row_out] * 10,
        out_shape=[jax.ShapeDtypeStruct((b, t, WC), dt) for dt in (SCAN_DTYPE,) * 3 + (F32,) * 3 + (SCAN_DTYPE,) * 4],
        compiler_params=_params("parallel", "parallel"),
    )(*([proj3] * 12), mu[:, :WC], mu[:, WC:2 * WC], mu[:, 2 * WC:3 * WC], mu[:, 3 * WC:],
      w0, wl.astype(BF16), a0, al.astype(BF16), k_k.reshape(1, WC), k_a.reshape(1, WC),
      r_k.reshape(1, WC), gmat, shift)
    r, v, kk, bonus, lw0, lw1, k0, k1, ka0, ka1 = outs

    ts = _pick(t, 512)
    ns = t // ts

    def scan(rev, final, lw, kd, ka, ob=None):
        wblk = PAIRS_C * LANE
        tmap = (lambda i: ns - 1 - i) if rev else (lambda i: i)
        blk = pl.BlockSpec((1, ts, wblk), lambda bi, h, ti: (bi, tmap(ti), h))
        head_vec = pl.BlockSpec((1, wblk), lambda bi, h, ti: (0, h))
        in_specs = [blk] * 6
        args = [r, lw, kd, v, ka, kk]
        if final:
            gate = pl.BlockSpec((1, ts, wblk), lambda bi, h, ti: (bi, tmap(ti), OFF_CG // wblk + h))
            in_specs += [blk, blk, gate, head_vec, head_vec]
            args += [ob, bonus, proj3, ln_g.reshape(1, WC), ln_b.reshape(1, WC)]
        return pl.pallas_call(
            functools.partial(_rwkv_scan_kernel, rev, final, ts),
            grid=(b, WC // wblk, ns),
            in_specs=in_specs,
            out_specs=blk,
            out_shape=jax.ShapeDtypeStruct((b, t, WC), BF16 if final else F32),
            scratch_shapes=[pltpu.VMEM((PAIRS_C, LANE, LANE), F32)],
            compiler_params=_params("parallel", "parallel", "arbitrary"),
        )(*args)

    ob = scan(True, False, lw1, k1, ka1)
    return scan(False, True, lw0, k0, ka0, ob=ob)


def _merge_kernel(final, ya_ref, yb_ref, yc_ref, mg_ref, x_ref, wa_ref, wb_ref, wc_ref, wo_ref,
                  fg_ref, o_ref):
    za = jnp.dot(ya_ref[...], wa_ref[...], preferred_element_type=F32)
    zb = jnp.dot(yb_ref[...], wb_ref[...], preferred_element_type=F32)
    zc = jnp.dot(yc_ref[...], wc_ref[...], preferred_element_type=F32)
    mg = _f32(mg_ref[...])
    mixed = (_sigmoid(mg[:, :D_MODEL]) * za
             + _sigmoid(mg[:, D_MODEL:2 * D_MODEL]) * zb
             + _sigmoid(mg[:, 2 * D_MODEL:]) * zc)
    out = x_ref[...] + jnp.dot(mixed.astype(BF16), wo_ref[...], preferred_element_type=F32)
    if final:
        ms = jnp.mean(out * out, axis=-1, keepdims=True)
        out = out * lax.rsqrt(ms + EPS) * fg_ref[...]
    o_ref[...] = out


def _merge(ya, yb, yc, proj, x2, wa, wb, wc, wo, fg, final):
    m = x2.shape[0]
    tm = _pick(m, 512)

    def rows(width):
        return pl.BlockSpec((tm, width), lambda i: (i, 0))

    def full(shape):
        return pl.BlockSpec(shape, lambda i: (0, 0))

    return pl.pallas_call(
        functools.partial(_merge_kernel, final),
        grid=(m // tm,),
        in_specs=[rows(WA), rows(WB_V), rows(WC), rows(3 * D_MODEL), rows(D_MODEL),
                  full((WA, D_MODEL)), full((WB_V, D_MODEL)), full((WC, D_MODEL)),
                  full((D_MODEL, D_MODEL)), full((1, D_MODEL))],
        out_specs=rows(D_MODEL),
        out_shape=jax.ShapeDtypeStruct((m, D_MODEL), F32),
        compiler_params=_params("parallel", vmem=PROJ_VMEM_LIMIT),
    )(ya, yb, yc, proj, x2, wa.astype(BF16), wb.astype(BF16), wc.astype(BF16), wo.astype(BF16),
      fg.reshape(1, D_MODEL))


def kernel(x, norm_g, w_in, hgrn_lb_logits, hgrn_norm_g, ret_norm_g, ret_norm_b, rwkv_mu, rwkv_w0,
           rwkv_w_lora_b, rwkv_a0, rwkv_a_lora_b, rwkv_k_k, rwkv_k_a, rwkv_r_k, rwkv_ln_g,
           rwkv_ln_b, w_branch_a, w_branch_b, w_branch_c, w_out, final_norm_g):
    b, t, d = x.shape
    m = b * t
    depth = w_in.shape[0]
    p_lb = jax.nn.softmax(hgrn_lb_logits.astype(F32), axis=0)
    lbs = jnp.cumsum(p_lb, axis=0) - p_lb[0:1]
    x2 = x.reshape(m, d)
    for l in range(depth):
        proj = _inproj(x2, norm_g[l], _permute_w_in(w_in, l))
        proj3 = proj.reshape(b, t, N_PAD)
        ya = _hgrn_mixer(proj3, lbs[l], hgrn_norm_g[l])
        yb = _ret_mixer(proj3, ret_norm_g[l], ret_norm_b[l])
        yc = _rwkv_mixer(proj3, rwkv_mu[l], rwkv_w0[l], rwkv_w_lora_b[l], rwkv_a0[l],
                         rwkv_a_lora_b[l], rwkv_k_k[l], rwkv_k_a[l], rwkv_r_k[l],
                         rwkv_ln_g[l], rwkv_ln_b[l])
        x2 = _merge(ya.reshape(m, WA), yb.reshape(m, WB_V), yc.reshape(m, WC), proj, x2,
                    w_branch_a[l], w_branch_b[l], w_branch_c[l], w_out[l], final_norm_g,
                    final=(l == depth - 1))
    return x2.reshape(b, t, d)
```

```python
import functools

import numpy as np
import jax
import jax.numpy as jnp
from jax import lax
from jax.experimental import pallas as pl
from jax.experimental.pallas import tpu as pltpu

F32 = jnp.float32
BF16 = jnp.bfloat16

D_MODEL = 1024
EPS = 1e-6
F_MIN = 1e-6
DKA = 128
HA = D_MODEL // DKA
WA = HA * DKA
HB = 8
DKB = D_MODEL // HB
DVB = 2 * DKB
WB_QK = HB * DKB
WB_V = HB * DVB
ROPE_BASE = 10000.0
RET_GN_EPS = 1e-5
NC = 64
HC = D_MODEL // NC
WC = HC * NC
W_LORA = 64
A_LORA = 64
RWKV_GN_EPS = 64e-5
KK_NORM_EPS = 1e-12
N_CODES = 2 * W_LORA + 2 * A_LORA
C_SHIFT = 3 * WC + N_CODES

SRC_B = 5 * WA
SRC_CS = SRC_B + 2 * WB_QK + 2 * WB_V
SRC_CG = SRC_CS + C_SHIFT
SRC_MERGE = SRC_CG + WC
N_IN = SRC_MERGE + 3 * D_MODEL

OFF_MERGE = 0
OFF_AQ = OFF_MERGE + 3 * D_MODEL
OFF_AFF = OFF_AQ + WA
OFF_AFB = OFF_AFF + WA
OFF_AI = OFF_AFB + WA
OFF_AG = OFF_AI + WA
OFF_BQ = OFF_AG + WA
OFF_BK = OFF_BQ + WB_QK
OFF_BV = OFF_BK + WB_QK
OFF_BG = OFF_BV + WB_V
OFF_CG = OFF_BG + WB_V
OFF_CR = OFF_CG + WC
OFF_CK = OFF_CR + WC
OFF_CV = OFF_CK + WC
OFF_CC = OFF_CV + WC
LANE = 128
PROJ_DTYPE = BF16
SCAN_DTYPE = BF16
HALO = 16
PROJ_SPLIT = 4
N_PAD = -(-(OFF_CC + N_CODES) // (PROJ_SPLIT * LANE)) * PROJ_SPLIT * LANE
PROJ_TN = N_PAD // PROJ_SPLIT
MXU_N = 256
WPERM_TN = 256

CHUNK_A = 128
GROUP_A = 8
HEADS_A = 2
CHUNK_B = 256
CHUNK_C = 64
PAIRS_C = 4
VMEM_LIMIT = 48 * 1024 * 1024
PROJ_VMEM_LIMIT = 56 * 1024 * 1024

NT_DIMS = (((1,), (1,)), ((), ()))
TN_DIMS = (((0,), (0,)), ((), ()))


def _mm(a, b):
    return jnp.dot(a.astype(BF16), b.astype(BF16), preferred_element_type=F32)


def _mm_nt(a, b):
    return lax.dot_general(a.astype(BF16), b.astype(BF16), NT_DIMS, preferred_element_type=F32)


def _mm_tn(a, b):
    return lax.dot_general(a.astype(BF16), b.astype(BF16), TN_DIMS, preferred_element_type=F32)


def _split_dot(ones_mat, x):
    hi = x.astype(BF16)
    lo = (x - hi.astype(F32)).astype(BF16)
    return jnp.dot(ones_mat, hi, preferred_element_type=F32) + jnp.dot(ones_mat, lo, preferred_element_type=F32)


def _split_dot_right(x, ones_mat):
    hi = x.astype(BF16)
    lo = (x - hi.astype(F32)).astype(BF16)
    return jnp.dot(hi, ones_mat, preferred_element_type=F32) + jnp.dot(lo, ones_mat, preferred_element_type=F32)


def _f32(z):
    return z.astype(F32)


def _sigmoid(z):
    return 0.5 * jnp.tanh(0.5 * z) + 0.5


def _cumsum_rows(x, rev):
    n = x.shape[0]
    rows = lax.broadcasted_iota(jnp.int32, (n, 1), 0)
    step = 1
    while step < n:
        if rev:
            x = x + jnp.where(rows < n - step, pltpu.roll(x, n - step, axis=0), 0.0)
        else:
            x = x + jnp.where(rows >= step, pltpu.roll(x, step, axis=0), 0.0)
        step *= 2
    return x


def _silu(z):
    h = 0.5 * z
    return h * jnp.tanh(h) + h


def _pick(n, pref):
    t = min(n, pref)
    assert n % t == 0, (n, pref)
    return t


def _params(*sem, vmem=VMEM_LIMIT):
    return pltpu.CompilerParams(dimension_semantics=sem, vmem_limit_bytes=vmem)


def _inproj_kernel(x_ref, g_ref, w_ref, o_ref, h_ref):
    @pl.when(pl.program_id(1) == 0)
    def _():
        x = x_ref[...]
        ms = jnp.mean(x * x, axis=-1, keepdims=True)
        h_ref[...] = (x * lax.rsqrt(ms + EPS) * g_ref[...]).astype(BF16)

    def tile(cols):
        o_ref[:, cols] = jnp.dot(h_ref[...], w_ref[:, cols], preferred_element_type=F32).astype(o_ref.dtype)

    for start in range(0, PROJ_TN, MXU_N):
        tile(pl.ds(start, min(MXU_N, PROJ_TN - start)))


def _inproj(x2, g, w_bf16):
    m = x2.shape[0]
    tm = _pick(m, 1024)
    return pl.pallas_call(
        _inproj_kernel,
        grid=(m // tm, N_PAD // PROJ_TN),
        in_specs=[
            pl.BlockSpec((tm, D_MODEL), lambda i, j: (i, 0)),
            pl.BlockSpec((1, D_MODEL), lambda i, j: (0, 0)),
            pl.BlockSpec((D_MODEL, PROJ_TN), lambda i, j: (0, j)),
        ],
        out_specs=pl.BlockSpec((tm, PROJ_TN), lambda i, j: (i, j)),
        out_shape=jax.ShapeDtypeStruct((m, N_PAD), PROJ_DTYPE),
        scratch_shapes=[pltpu.VMEM((tm, D_MODEL), BF16)],
        compiler_params=_params("parallel", "arbitrary", vmem=PROJ_VMEM_LIMIT),
    )(x2, g.reshape(1, D_MODEL), w_bf16)


def _wperm_kernel(nvalid, a_ref, b_ref, o_ref):
    j = pl.program_id(0)
    for idx, src in enumerate((a_ref, b_ref)):
        cols = pl.ds(idx * WPERM_TN, WPERM_TN)
        blk = src[...].astype(BF16)
        o_ref[:, cols] = jnp.where(2 * j + idx < nvalid, blk, jnp.zeros_like(blk))


def _src_block(jd):
    n_merge = (N_IN - SRC_MERGE) // WPERM_TN
    n_ab = SRC_CS // WPERM_TN
    n_cg = (SRC_MERGE - SRC_CG) // WPERM_TN
    n_src = N_IN // WPERM_TN
    return jnp.where(jd < n_merge, jd + SRC_MERGE // WPERM_TN,
                     jnp.where(jd < n_merge + n_ab, jd - n_merge,
                               jnp.where(jd < n_merge + n_ab + n_cg, jd - n_merge - n_ab + SRC_CG // WPERM_TN,
                                         jnp.minimum(jd - n_merge - n_cg, n_src - 1))))


def _permute_w_in(w_in, layer):
    assert all(off % WPERM_TN == 0 for off in (SRC_CS, SRC_CG, SRC_MERGE, N_IN)) and N_PAD % (2 * WPERM_TN) == 0

    def src(idx):
        return pl.BlockSpec((None, D_MODEL, WPERM_TN), lambda j: (layer, 0, _src_block(2 * j + idx)))

    return pl.pallas_call(
        functools.partial(_wperm_kernel, N_IN // WPERM_TN),
        grid=(N_PAD // (2 * WPERM_TN),),
        in_specs=[src(0), src(1)],
        out_specs=pl.BlockSpec((D_MODEL, 2 * WPERM_TN), lambda j: (0, j)),
        out_shape=jax.ShapeDtypeStruct((D_MODEL, N_PAD), BF16),
        compiler_params=_params("parallel"),
    )(w_in, w_in)


def _each(fn, *lists):
    return [fn(*items) for items in zip(*lists)]


def _hgrn_kernel(rev, final, tt, q_ref, f_ref, v_ref, lb_ref, tri_ref, *rest):
    if final:
        ob_ref, gate_ref, ng_ref, y_ref, st_ref = rest
    else:
        o_ref, st_ref = rest

    @pl.when(pl.program_id(2) == 0)
    def _():
        st_ref[...] = jnp.zeros_like(st_ref)

    n = CHUNK_A
    ri = lax.broadcasted_iota(jnp.int32, (n, n), 0)
    ci = lax.broadcasted_iota(jnp.int32, (n, n), 1)
    g_bits = int(np.log2(GROUP_A))
    diag_mask = ((ri >> g_bits) == (ci >> g_bits)) & ((ci >= ri) if rev else (ci <= ri))
    halves = [GROUP_A << i for i in range(int(np.log2(n // GROUP_A)))]
    level_masks = []
    for half in halves:
        hb = int(np.log2(half))
        q_side, k_side = (0, 1) if rev else (1, 0)
        level_masks.append(((ri >> (hb + 1)) == (ci >> (hb + 1)))
                           & (((ri >> hb) & 1) == q_side) & (((ci >> hb) & 1) == k_side))

    def ref_rows(b, blk, idx):
        picked = b.reshape(n // blk, blk, DKA)[:, idx:idx + 1, :]
        return jnp.broadcast_to(picked, (n // blk, blk, DKA)).reshape(n, DKA)

    nchunk = tt // n
    order = [(nchunk - 1 - i) if rev else i for i in range(nchunk)]
    items = [(pl.ds(c * n, n), pl.ds(hh * DKA, DKA), hh) for c in order for hh in range(HEADS_A)]
    sls = [it[0] for it in items]
    cls = [it[1] for it in items]
    c0 = [0.5 + 0.5 * lb_ref[it[2]] for it in items]
    c1 = [0.5 - 0.5 * lb_ref[it[2]] for it in items]
    gm = _each(lambda sl, cl, c: c * jnp.tanh(0.5 * _f32(f_ref[0, sl, cl])), sls, cls, c1)
    lf = _each(lambda m, c: jnp.log2(jnp.maximum(c + m, F_MIN)), gm, c0)
    k = _each(lambda m, c: c - m, gm, c1)
    q = _each(lambda sl, cl: _silu(_f32(q_ref[0, sl, cl])), sls, cls)
    v = _each(lambda sl, cl: v_ref[0, sl, cl], sls, cls)
    b = _each(lambda z: _split_dot(tri_ref[...], z), lf)
    d0 = _each(lambda z: z - ref_rows(z, GROUP_A, GROUP_A // 2 if rev else GROUP_A // 2 - 1), b)
    qb = _each(lambda z: z.astype(BF16), q)
    kb = _each(lambda z: z.astype(BF16), k)
    scores = _each(lambda qq, kk, d: jnp.where(
        diag_mask, _mm_nt(qq * jnp.exp2(d).astype(BF16), kk * jnp.exp2(-d).astype(BF16)), 0.0), qb, kb, d0)
    for half, mask in zip(halves, level_masks):
        e = _each(lambda z: jnp.exp2(-jnp.abs(z - ref_rows(z, 2 * half, half if rev else half - 1))).astype(BF16),
                  b)
        scores = _each(lambda sc, qq, kk, ee: jnp.where(mask, _mm_nt(qq * ee, kk * ee), sc), scores, qb, kb, e)
    o_intra = _each(_mm, scores, v)
    b_end = _each(lambda z: z[0:1, :] if rev else z[n - 1:n, :], b)
    q_in = _each(lambda qq, z: qq * jnp.exp2(z), q, b)
    kv = _each(lambda vv, kk, z, ze: _mm_tn(vv, kk * jnp.exp2(ze - z)), v, k, b, b_end)
    e_end = _each(jnp.exp2, b_end)

    outs = []
    st = [st_ref[hh] for hh in range(HEADS_A)]
    for (sl, cl, hh), oi, qi, kvi, ee in zip(items, o_intra, q_in, kv, e_end):
        outs.append((sl, cl, hh, oi + _mm_nt(qi, st[hh])))
        st[hh] = st[hh] * ee + kvi
    for hh in range(HEADS_A):
        st_ref[hh] = st[hh]

    for sl, cl, hh, o in outs:
        if final:
            o = o + ob_ref[0, sl, cl]
            ms = jnp.mean(o * o, axis=-1, keepdims=True)
            y = o * lax.rsqrt(ms + EPS) * ng_ref[hh] * _silu(_f32(gate_ref[0, sl, cl]))
            y_ref[0, sl, cl] = y.astype(y_ref.dtype)
        else:
            o_ref[0, sl, cl] = o


def _tri_blockdiag(n, blk, rev):
    i = np.arange(n)
    same = (i[:, None] // blk) == (i[None, :] // blk)
    tri = (i[None, :] >= i[:, None]) if rev else (i[None, :] <= i[:, None])
    return jnp.asarray((same & tri).astype(np.float32), dtype=BF16)


def _hgrn_pass(proj3, lb, rev, final, ob=None, norm_g=None):
    b, t, _ = proj3.shape
    tt = _pick(t, 2048)
    nt = t // tt
    tmap = (lambda i: nt - 1 - i) if rev else (lambda i: i)

    wblk = HEADS_A * DKA

    def col(off):
        return pl.BlockSpec((1, tt, wblk), lambda bi, h, ti: (bi, tmap(ti), off // wblk + h))

    head_vec = pl.BlockSpec((HEADS_A, 1, DKA), lambda bi, h, ti: (h, 0, 0))
    in_specs = [col(OFF_AQ), col(OFF_AFB if rev else OFF_AFF), col(OFF_AI), head_vec,
                pl.BlockSpec((LANE, LANE), lambda bi, h, ti: (0, 0))]
    args = [proj3, proj3, proj3, lb.reshape(HA, 1, DKA), _tri_blockdiag(LANE, CHUNK_A, rev)]
    out_block = pl.BlockSpec((1, tt, wblk), lambda bi, h, ti: (bi, tmap(ti), h))
    if final:
        in_specs += [out_block, col(OFF_AG), head_vec]
        args += [ob, proj3, norm_g.reshape(HA, 1, DKA)]
        out_dtype = BF16
    else:
        out_dtype = F32
    return pl.pallas_call(
        functools.partial(_hgrn_kernel, rev, final, tt),
        grid=(b, HA // HEADS_A, nt),
        in_specs=in_specs,
        out_specs=out_block,
        out_shape=jax.ShapeDtypeStruct((b, t, WA), out_dtype),
        scratch_shapes=[pltpu.VMEM((HEADS_A, DKA, DKA), F32)],
        compiler_params=_params("parallel", "parallel", "arbitrary"),
    )(*args)


def _hgrn_mixer(proj3, lbs, norm_g):
    ob = _hgrn_pass(proj3, lbs[1], rev=True, final=False)
    return _hgrn_pass(proj3, lbs[0], rev=False, final=True, ob=ob, norm_g=norm_g)


def _rotary(z, cos2, sin2):
    return z * cos2 + pltpu.roll(z, DKB // 2, axis=1) * sin2


def _ret_state_kernel(nsub, k_ref, v_ref, cos_ref, sin_ref, lg_ref, sb_ref, st_ref):
    @pl.when(pl.program_id(2) == 0)
    def _():
        st_ref[...] = jnp.zeros_like(st_ref)

    cb = CHUNK_B
    lg2 = lg_ref[0]
    lg = lg2[:, :DKB]
    pos = lax.broadcasted_iota(jnp.int32, (cb, DKB), 0).astype(F32)
    dec = jnp.exp(pos * lg) * (DKB ** -0.5)
    order = list(range(nsub - 1, -1, -1))
    sls = [pl.ds(c * cb, cb) for c in order]
    kd = _each(lambda sl: _rotary(_f32(k_ref[0, sl, :]), cos_ref[sl, :], sin_ref[sl, :]) * dec, sls)
    kv = _each(lambda kk, sl: _mm_tn(kk, v_ref[0, sl, :]), kd, sls)
    e_chunk = jnp.exp(cb * lg2)
    st = st_ref[...]
    for c, kvi in zip(order, kv):
        sb_ref[0, 0, c] = st
        st = st * e_chunk + kvi
    st_ref[...] = st


def _ret_out_kernel(nsub, q_ref, k_ref, v_ref, cos_ref, sin_ref, lg_ref, sb_ref, gate_ref, gg_ref,
                    gb_ref, y_ref, st_ref):
    @pl.when(pl.program_id(2) == 0)
    def _():
        st_ref[...] = jnp.zeros_like(st_ref)

    cb = CHUNK_B
    lg2 = lg_ref[0]
    lg = lg2[:, :DKB]
    pos = lax.broadcasted_iota(jnp.int32, (cb, DKB), 0).astype(F32)
    ri = lax.broadcasted_iota(jnp.int32, (cb, cb), 0)
    ci = lax.broadcasted_iota(jnp.int32, (cb, cb), 1)
    decay = jnp.exp(jnp.abs(ri - ci).astype(F32) * lg2) * (DKB ** -0.5)
    dec_fwd = jnp.exp((pos + 1.0) * lg)
    dec_bwd = jnp.exp((cb - pos) * lg)
    dec_key = jnp.exp((cb - 1.0 - pos) * lg) * (DKB ** -0.5)
    e_chunk = jnp.exp(cb * lg2)

    sls = [pl.ds(c * cb, cb) for c in range(nsub)]
    q = _each(lambda sl: _rotary(_f32(q_ref[0, sl, :]), cos_ref[sl, :], sin_ref[sl, :]), sls)
    k = _each(lambda sl: _rotary(_f32(k_ref[0, sl, :]), cos_ref[sl, :], sin_ref[sl, :]), sls)
    v = _each(lambda sl: v_ref[0, sl, :], sls)
    scores = _each(lambda qq, kk: _mm_nt(qq, kk) * decay, q, k)
    o = _each(_mm, scores, v)
    o = _each(lambda oo, qq, c: oo + _mm(qq * dec_bwd, sb_ref[0, 0, c]), o, q, list(range(nsub)))
    kv = _each(lambda kk, vv: _mm_tn(kk * dec_key, vv), k, v)
    q_fwd = _each(lambda qq: qq * dec_fwd, q)
    st = st_ref[...]
    outs = []
    for oo, qf, kvi in zip(o, q_fwd, kv):
        outs.append(oo + _mm(qf, st))
        st = st * e_chunk + kvi
    st_ref[...] = st

    for sl, oo in zip(sls, outs):
        mu = jnp.mean(oo, axis=-1, keepdims=True)
        cen = oo - mu
        var = jnp.mean(cen * cen, axis=-1, keepdims=True)
        y = (cen * lax.rsqrt(var + RET_GN_EPS) * gg_ref[0] + gb_ref[0]) * _silu(_f32(gate_ref[0, sl, :]))
        y_ref[0, sl, :] = y.astype(y_ref.dtype)


def _ret_mixer(proj3, gn_g, gn_b):
    b, t, _ = proj3.shape
    cb = CHUNK_B
    tt = _pick(t, 16 * cb)
    nsub = tt // cb
    nt = t // tt
    half = DKB // 2
    inv = ROPE_BASE ** (-jnp.arange(0, DKB, 2, dtype=F32) / DKB)
    ang_hi = (jnp.arange(t // cb, dtype=F32) * cb)[:, None, None] * inv
    ang_lo = jnp.arange(cb, dtype=F32)[None, :, None] * inv
    cos = (jnp.cos(ang_hi) * jnp.cos(ang_lo) - jnp.sin(ang_hi) * jnp.sin(ang_lo)).reshape(t, half)
    sin = (jnp.sin(ang_hi) * jnp.cos(ang_lo) + jnp.cos(ang_hi) * jnp.sin(ang_lo)).reshape(t, half)
    cos2 = jnp.concatenate([cos, cos], axis=-1)
    sin2 = jnp.concatenate([-sin, sin], axis=-1)
    log_gamma = jnp.log1p(-jnp.exp2(-5.0 - jnp.arange(HB, dtype=F32)))
    lg = jnp.broadcast_to(log_gamma[:, None, None], (HB, 1, DVB))

    def col(off, width, tmap):
        return pl.BlockSpec((1, tt, width), lambda bi, h, c: (bi, tmap(c), off // width + h))

    def tab(tmap):
        return pl.BlockSpec((tt, DKB), lambda bi, h, c: (tmap(c), 0))

    lg_spec = pl.BlockSpec((1, 1, DVB), lambda bi, h, c: (h, 0, 0))
    rmap = lambda c: nt - 1 - c
    fmap = lambda c: c

    sb = pl.pallas_call(
        functools.partial(_ret_state_kernel, nsub),
        grid=(b, HB, nt),
        in_specs=[col(OFF_BK, DKB, rmap), col(OFF_BV, DVB, rmap), tab(rmap), tab(rmap), lg_spec],
        out_specs=pl.BlockSpec((1, 1, nsub, DKB, DVB), lambda bi, h, c: (bi, h, rmap(c), 0, 0)),
        out_shape=jax.ShapeDtypeStruct((b, HB, t // cb, DKB, DVB), F32),
        scratch_shapes=[pltpu.VMEM((DKB, DVB), F32)],
        compiler_params=_params("parallel", "parallel", "arbitrary"),
    )(proj3, proj3, cos2, sin2, lg)

    head_vec = pl.BlockSpec((1, 1, DVB), lambda bi, h, c: (h, 0, 0))
    return pl.pallas_call(
        functools.partial(_ret_out_kernel, nsub),
        grid=(b, HB, nt),
        in_specs=[col(OFF_BQ, DKB, fmap), col(OFF_BK, DKB, fmap), col(OFF_BV, DVB, fmap),
                  tab(fmap), tab(fmap), lg_spec,
                  pl.BlockSpec((1, 1, nsub, DKB, DVB), lambda bi, h, c: (bi, h, c, 0, 0)),
                  col(OFF_BG, DVB, fmap), head_vec, head_vec],
        out_specs=pl.BlockSpec((1, tt, DVB), lambda bi, h, c: (bi, c, h)),
        out_shape=jax.ShapeDtypeStruct((b, t, WB_V), BF16),
        scratch_shapes=[pltpu.VMEM((DKB, DVB), F32)],
        compiler_params=_params("parallel", "parallel", "arbitrary"),
    )(proj3, proj3, proj3, cos2, sin2, lg, sb, proj3,
      gn_g.reshape(HB, 1, DVB), gn_b.reshape(HB, 1, DVB))


def _group_sum(x, gmat):
    cols = [_split_dot_right(x[:, i * LANE:(i + 1) * LANE], gmat) for i in range(x.shape[1] // LANE)]
    return jnp.concatenate(cols, axis=1)


def _rwkv_prep_kernel(tt, r_ref, k_ref, v_ref, c_ref, rp_ref, kp_ref, vp_ref, cp_ref,
                      rn_ref, kn_ref, vn_ref, cn_ref, mur_ref, muk_ref, muv_ref, muc_ref,
                      w0_ref, wl_ref, a0_ref, al_ref, kk_ref, ka_ref, rk_ref, gmat_ref, shift_ref,
                      r_o, v_o, kk_o, bonus_o, lw0_o, lw1_o, k0_o, k1_o, ka0_o, ka1_o):
    ti = pl.program_id(1)
    has_prev = (ti > 0).astype(F32)
    has_next = (ti < pl.num_programs(1) - 1).astype(F32)
    rows = lax.broadcasted_iota(jnp.int32, (tt, 1), 0)

    def shifted(p_ref, prev_ref, next_ref, mu_ref):
        both = jnp.dot(shift_ref[...], p_ref[0].astype(BF16), preferred_element_type=F32)
        p = _f32(p_ref[0])
        prev_row = _f32(prev_ref[0, HALO - 1:HALO, :]) * has_prev
        next_row = _f32(next_ref[0, 0:1, :]) * has_next
        before = jnp.where(rows == 0, prev_row, both[:tt])
        after = jnp.where(rows == tt - 1, next_row, both[tt:])
        return p + mu_ref[0:1, :] * (before - p) + mu_ref[1:2, :] * (after - p)

    r = shifted(r_ref, rp_ref, rn_ref, mur_ref)
    k = shifted(k_ref, kp_ref, kn_ref, muk_ref)
    v = shifted(v_ref, vp_ref, vn_ref, muv_ref)
    codes = shifted(c_ref, cp_ref, cn_ref, muc_ref)
    gmat = gmat_ref[...]

    kraw = k * kk_ref[...]
    kk = kraw * lax.rsqrt(_group_sum(kraw * kraw, gmat) + KK_NORM_EPS)
    r_o[0] = r.astype(r_o.dtype)
    v_o[0] = v.astype(v_o.dtype)
    kk_o[0] = kk.astype(kk_o.dtype)
    tcodes = jnp.tanh(codes)
    bonus = jnp.zeros_like(r)
    half_kk = 0.5 * kk
    e1 = 0.5 * ka_ref[...]
    e0 = 1.0 - e1
    for d, (lw_o, kd_o, ka_o) in enumerate(((lw0_o, k0_o, ka0_o), (lw1_o, k1_o, ka1_o))):
        w = w0_ref[d:d + 1, :] + _mm(tcodes, wl_ref[d])
        half_c = np.float32(-0.5 * np.exp(-0.5) * np.log2(np.e))
        lw_o[0] = half_c * jnp.tanh(0.5 * w) + half_c
        th = jnp.tanh(0.5 * (a0_ref[d:d + 1, :] + _mm(codes, al_ref[d])))
        kd = k * (e0 + e1 * th)
        kd_o[0] = kd.astype(kd_o.dtype)
        ka_o[0] = (half_kk * th + half_kk).astype(ka_o.dtype)
        bonus = bonus + _group_sum(r * kd * rk_ref[...], gmat) * v
    bonus_o[0] = bonus


def _rwkv_scan_kernel(rev, final, tt, r_ref, lw_ref, k_ref, v_ref, ka_ref, kk_ref, *rest):
    if final:
        ob_ref, bonus_ref, gate_ref, lng_ref, lnb_ref, y_ref, h_ref = rest
    else:
        o_ref, h_ref = rest

    @pl.when(pl.program_id(2) == 0)
    def _():
        h_ref[...] = jnp.zeros_like(h_ref)

    n2 = 2 * CHUNK_C
    lane = lax.broadcasted_iota(jnp.int32, (1, LANE), 1)
    head0 = lane < NC
    ri = lax.broadcasted_iota(jnp.int32, (n2, n2), 0)
    ci = lax.broadcasted_iota(jnp.int32, (n2, n2), 1)
    strict = (ci > ri) if rev else (ci < ri)
    incl = (ci >= ri) if rev else (ci <= ri)
    eye = ri == ci
    nchunk = tt // CHUNK_C

    def stack(z):
        zero = jnp.zeros_like(z)
        return jnp.concatenate([jnp.where(head0, z, zero), jnp.where(head0, zero, z)], axis=0)

    def stack_b(z):
        return stack(z.astype(BF16))

    order = [(nchunk - 1 - i) if rev else i for i in range(nchunk)]
    items = [(pl.ds(c * CHUNK_C, CHUNK_C), pl.ds(p * LANE, LANE), p) for c in order for p in range(PAIRS_C)]
    rows = [it[0] for it in items]
    cols = [it[1] for it in items]

    lws = _each(lambda sl, cl: lw_ref[0, sl, cl], rows, cols)
    g = _each(lambda lw: _cumsum_rows(lw, rev), lws)
    e_end = _each(lambda gu: jnp.exp2(gu[0:1, :] if rev else gu[CHUNK_C - 1:CHUNK_C, :]), g)
    e_neg = _each(lambda gu: jnp.exp2(-gu), g)
    bs = _each(lambda sl, cl, gu, lw: stack_b(-_f32(kk_ref[0, sl, cl]) * jnp.exp2(gu - lw)), rows, cols, g, lws)
    a_sc = _each(lambda sl, cl, en: _f32(ka_ref[0, sl, cl]) * en, rows, cols, e_neg)
    k_sc = _each(lambda sl, cl, en: _f32(k_ref[0, sl, cl]) * en, rows, cols, e_neg)
    as_ = _each(stack_b, a_sc)
    ks = _each(stack_b, k_sc)
    ake = _each(lambda a, k, e: jnp.concatenate([stack_b(a * e), stack_b(k * e)], axis=0), a_sc, k_sc, e_end)
    rs = _each(lambda sl, cl, gu: stack(_f32(r_ref[0, sl, cl]) * jnp.exp2(gu)), rows, cols, g)
    vs = _each(lambda sl, cl: stack_b(v_ref[0, sl, cl]), rows, cols)

    ak = _each(lambda a, k: jnp.concatenate([a, k], axis=0), as_, ks)
    sc_b = _each(_mm_nt, bs, ak)
    sc_r = _each(_mm_nt, rs, ak)
    a_ab = _each(lambda z: jnp.where(strict, z[:, :n2], 0.0), sc_b)
    a_ak = _each(lambda z: jnp.where(strict, z[:, n2:], 0.0), sc_b)
    m_rak = _each(lambda z: jnp.where(jnp.concatenate([incl, incl], axis=1), z, 0.0), sc_r)
    tinv = _each(lambda a: jnp.where(eye, 1.0, 0.0) + a, a_ab)
    pw = _each(lambda a: _mm(a, a), a_ab)
    akv = _each(_mm, a_ak, vs)
    nsteps = int(np.log2(CHUNK_C)) - 1
    for i in range(nsteps):
        if i < nsteps - 1:
            both = _each(lambda t, p: _mm(jnp.concatenate([t, p], axis=0), p), tinv, pw)
            tinv = _each(lambda t, b: t + b[:n2], tinv, both)
            pw = _each(lambda b: b[n2:], both)
        else:
            tinv = _each(lambda t, p: t + _mm(t, p), tinv, pw)
    pwv = _each(lambda t, b, x: _mm(t, jnp.concatenate([b, x.astype(BF16)], axis=1)), tinv, bs, akv)
    low = _each(lambda z, v: jnp.concatenate([z.astype(BF16), jnp.concatenate([jnp.zeros_like(v), v], axis=1)],
                                             axis=0), pwv, vs)
    x1 = _each(_mm, m_rak, low)
    x2 = _each(_mm_tn, ake, low)
    top = _each(lambda r, z, e, y: jnp.concatenate([jnp.where(eye, e, 0.0) + y[:, :LANE], r + z[:, :LANE]], axis=0),
                rs, x1, e_end, x2)
    o_in = _each(lambda z: z[:, LANE:], x1)
    dmat = _each(lambda y: y[:, LANE:], x2)

    outs = []
    h = [h_ref[p] for p in range(PAIRS_C)]
    for (sl, cl, p), tp, oi, dm in zip(items, top, o_in, dmat):
        both = _mm(tp, h[p])
        h[p] = both[:n2] + dm
        os_ = both[n2:] + oi
        outs.append((sl, cl, os_[:CHUNK_C, :] + os_[CHUNK_C:, :]))
    for p in range(PAIRS_C):
        h_ref[p] = h[p]

    for sl, cl, o in outs:
        if final:
            o = o + ob_ref[0, sl, cl]
            inv_n = 1.0 / NC
            s0 = jnp.sum(jnp.where(head0, o, 0.0), axis=-1, keepdims=True)
            s1 = jnp.sum(jnp.where(head0, 0.0, o), axis=-1, keepdims=True)
            cen = o - jnp.where(head0, s0, s1) * inv_n
            c2 = cen * cen
            v0 = jnp.sum(jnp.where(head0, c2, 0.0), axis=-1, keepdims=True)
            v1 = jnp.sum(jnp.where(head0, 0.0, c2), axis=-1, keepdims=True)
            var = jnp.where(head0, v0, v1) * inv_n
            y = cen * lax.rsqrt(var + RWKV_GN_EPS) * lng_ref[:, cl] + lnb_ref[:, cl] + bonus_ref[0, sl, cl]
            y_ref[0, sl, cl] = (y * _silu(_f32(gate_ref[0, sl, cl]))).astype(y_ref.dtype)
        else:
            o_ref[0, sl, cl] = o


def _rwkv_mixer(proj3, mu, w0, w_lora_b, a0, a_lora_b, k_k, k_a, r_k, ln_g, ln_b):
    b, t, _ = proj3.shape
    tt = _pick(t, 256)
    nt = t // tt
    sub = tt // HALO

    def cur(off, width):
        return pl.BlockSpec((1, tt, width), lambda bi, ti: (bi, ti, off // width))

    def prev(off, width):
        return pl.BlockSpec((1, HALO, width),
                            lambda bi, ti: (bi, jnp.maximum(ti * sub - 1, 0), off // width))

    def nxt(off, width):
        return pl.BlockSpec((1, HALO, width),
                            lambda bi, ti: (bi, jnp.minimum((ti + 1) * sub, nt * sub - 1), off // width))

    def full(shape):
        return pl.BlockSpec(shape, lambda bi, ti: (0,) * len(shape))

    segs = ((OFF_CR, WC), (OFF_CK, WC), (OFF_CV, WC), (OFF_CC, N_CODES))
    in_specs = ([cur(o, w) for o, w in segs] + [prev(o, w) for o, w in segs] + [nxt(o, w) for o, w in segs]
                + [full((2, WC))] * 3 + [full((2, N_CODES))]
                + [full((2, WC)), full((2, N_CODES, WC)), full((2, WC)), full((2, N_CODES, WC))]
                + [full((1, WC))] * 3 + [full((LANE, LANE)), full((2 * tt, tt))])
    wl = jnp.zeros((2, N_CODES, WC), F32)
    al = jnp.zeros((2, N_CODES, WC), F32)
    for d in range(2):
        wl = wl.at[d, d * W_LORA:(d + 1) * W_LORA].set(w_lora_b[d])
        al = al.at[d, 2 * W_LORA + d * A_LORA:2 * W_LORA + (d + 1) * A_LORA].set(a_lora_b[d])
    row_i = np.arange(tt)
    shift = np.concatenate([row_i[:, None] - 1 == row_i[None, :], row_i[:, None] + 1 == row_i[None, :]], axis=0)
    shift = jnp.asarray(shift.astype(np.float32), dtype=BF16)
    lane_i = np.arange(LANE)
    gmat = jnp.asarray((lane_i[:, None] // NC == lane_i[None, :] // NC).astype(np.float32), dtype=BF16)
    row_out = pl.BlockSpec((1, tt, WC), lambda bi, ti: (bi, ti, 0))
    outs = pl.pallas_call(
        functools.partial(_rwkv_prep_kernel, tt),
        grid=(b, nt),
        in_specs=in_specs,
        out_specs=[row_out] * 10,
        out_shape=[jax.ShapeDtypeStruct((b, t, WC), dt) for dt in (SCAN_DTYPE,) * 3 + (F32,) * 3 + (SCAN_DTYPE,) * 4],
        compiler_params=_params("parallel", "parallel"),
    )(*([proj3] * 12), mu[:, :WC], mu[:, WC:2 * WC], mu[:, 2 * WC:3 * WC], mu[:, 3 * WC:],
      w0, wl.astype(BF16), a0, al.astype(BF16), k_k.reshape(1, WC), k_a.reshape(1, WC),
      r_k.reshape(1, WC), gmat, shift)
    r, v, kk, bonus, lw0, lw1, k0, k1, ka0, ka1 = outs

    ts = _pick(t, 512)
    ns = t // ts

    def scan(rev, final, lw, kd, ka, ob=None):
        wblk = PAIRS_C * LANE
        tmap = (lambda i: ns - 1 - i) if rev else (lambda i: i)
        blk = pl.BlockSpec((1, ts, wblk), lambda bi, h, ti: (bi, tmap(ti), h))
        head_vec = pl.BlockSpec((1, wblk), lambda bi, h, ti: (0, h))
        in_specs = [blk] * 6
        args = [r, lw, kd, v, ka, kk]
        if final:
            gate = pl.BlockSpec((1, ts, wblk), lambda bi, h, ti: (bi, tmap(ti), OFF_CG // wblk + h))
            in_specs += [blk, blk, gate, head_vec, head_vec]
            args += [ob, bonus, proj3, ln_g.reshape(1, WC), ln_b.reshape(1, WC)]
        return pl.pallas_call(
            functools.partial(_rwkv_scan_kernel, rev, final, ts),
            grid=(b, WC // wblk, ns),
            in_specs=in_specs,
            out_specs=blk,
            out_shape=jax.ShapeDtypeStruct((b, t, WC), BF16 if final else F32),
            scratch_shapes=[pltpu.VMEM((PAIRS_C, LANE, LANE), F32)],
            compiler_params=_params("parallel", "parallel", "arbitrary"),
        )(*args)

    ob = scan(True, False, lw1, k1, ka1)
    return scan(False, True, lw0, k0, ka0, ob=ob)


def _merge_kernel(final, ya_ref, yb_ref, yc_ref, mg_ref, x_ref, wa_ref, wb_ref, wc_ref, wo_ref,
                  fg_ref, o_ref):
    za = jnp.dot(ya_ref[...], wa_ref[...], preferred_element_type=F32)
    zb = jnp.dot(yb_ref[...], wb_ref[...], preferred_element_type=F32)
    zc = jnp.dot(yc_ref[...], wc_ref[...], preferred_element_type=F32)
    mg = _f32(mg_ref[...])
    mixed = (_sigmoid(mg[:, :D_MODEL]) * za
             + _sigmoid(mg[:, D_MODEL:2 * D_MODEL]) * zb
             + _sigmoid(mg[:, 2 * D_MODEL:]) * zc)
    out = x_ref[...] + jnp.dot(mixed.astype(BF16), wo_ref[...], preferred_element_type=F32)
    if final:
        ms = jnp.mean(out * out, axis=-1, keepdims=True)
        out = out * lax.rsqrt(ms + EPS) * fg_ref[...]
    o_ref[...] = out


def _merge(ya, yb, yc, proj, x2, wa, wb, wc, wo, fg, final):
    m = x2.shape[0]
    tm = _pick(m, 512)

    def rows(width):
        return pl.BlockSpec((tm, width), lambda i: (i, 0))

    def full(shape):
        return pl.BlockSpec(shape, lambda i: (0, 0))

    return pl.pallas_call(
        functools.partial(_merge_kernel, final),
        grid=(m // tm,),
        in_specs=[rows(WA), rows(WB_V), rows(WC), rows(3 * D_MODEL), rows(D_MODEL),
                  full((WA, D_MODEL)), full((WB_V, D_MODEL)), full((WC, D_MODEL)),
                  full((D_MODEL, D_MODEL)), full((1, D_MODEL))],
        out_specs=rows(D_MODEL),
        out_shape=jax.ShapeDtypeStruct((m, D_MODEL), F32),
        compiler_params=_params("parallel", vmem=PROJ_VMEM_LIMIT),
    )(ya, yb, yc, proj, x2, wa.astype(BF16), wb.astype(BF16), wc.astype(BF16), wo.astype(BF16),
      fg.reshape(1, D_MODEL))


def kernel(x, norm_g, w_in, hgrn_lb_logits, hgrn_norm_g, ret_norm_g, ret_norm_b, rwkv_mu, rwkv_w0,
           rwkv_w_lora_b, rwkv_a0, rwkv_a_lora_b, rwkv_k_k, rwkv_k_a, rwkv_r_k, rwkv_ln_g,
           rwkv_ln_b, w_branch_a, w_branch_b, w_branch_c, w_out, final_norm_g):
    b, t, d = x.shape
    m = b * t
    depth = w_in.shape[0]
    p_lb = jax.nn.softmax(hgrn_lb_logits.astype(F32), axis=0)
    lbs = jnp.cumsum(p_lb, axis=0) - p_lb[0:1]
    x2 = x.reshape(m, d)
    for l in range(depth):
        proj = _inproj(x2, norm_g[l], _permute_w_in(w_in, l))
        proj3 = proj.reshape(b, t, N_PAD)
        ya = _hgrn_mixer(proj3, lbs[l], hgrn_norm_g[l])
        yb = _ret_mixer(proj3, ret_norm_g[l], ret_norm_b[l])
        yc = _rwkv_mixer(proj3, rwkv_mu[l], rwkv_w0[l], rwkv_w_lora_b[l], rwkv_a0[l],
                         rwkv_a_lora_b[l], rwkv_k_k[l], rwkv_k_a[l], rwkv_r_k[l],
                         rwkv_ln_g[l], rwkv_ln_b[l])
        x2 = _merge(ya.reshape(m, WA), yb.reshape(m, WB_V), yc.reshape(m, WC), proj, x2,
                    w_branch_a[l], w_branch_b[l], w_branch_c[l], w_out[l], final_norm_g,
                    final=(l == depth - 1))
    return x2.reshape(b, t, d)
```

```python
import functools

import numpy as np
import jax
import jax.numpy as jnp
from jax import lax
from jax.experimental import pallas as pl
from jax.experimental.pallas import tpu as pltpu

F32 = jnp.float32
BF16 = jnp.bfloat16

D_MODEL = 1024
EPS = 1e-6
F_MIN = 1e-6
DKA = 128
HA = D_MODEL // DKA
WA = HA * DKA
HB = 8
DKB = D_MODEL // HB
DVB = 2 * DKB
WB_QK = HB * DKB
WB_V = HB * DVB
ROPE_BASE = 10000.0
RET_GN_EPS = 1e-5
NC = 64
HC = D_MODEL // NC
WC = HC * NC
W_LORA = 64
A_LORA = 64
RWKV_GN_EPS = 64e-5
KK_NORM_EPS = 1e-12
N_CODES = 2 * W_LORA + 2 * A_LORA
C_SHIFT = 3 * WC + N_CODES

SRC_B = 5 * WA
SRC_CS = SRC_B + 2 * WB_QK + 2 * WB_V
SRC_CG = SRC_CS + C_SHIFT
SRC_MERGE = SRC_CG + WC
N_IN = SRC_MERGE + 3 * D_MODEL

OFF_MERGE = 0
OFF_AQ = OFF_MERGE + 3 * D_MODEL
OFF_AFF = OFF_AQ + WA
OFF_AFB = OFF_AFF + WA
OFF_AI = OFF_AFB + WA
OFF_AG = OFF_AI + WA
OFF_BQ = OFF_AG + WA
OFF_BK = OFF_BQ + WB_QK
OFF_BV = OFF_BK + WB_QK
OFF_BG = OFF_BV + WB_V
OFF_CG = OFF_BG + WB_V
OFF_CR = OFF_CG + WC
OFF_CK = OFF_CR + WC
OFF_CV = OFF_CK + WC
OFF_CC = OFF_CV + WC
LANE = 128
PROJ_DTYPE = BF16
SCAN_DTYPE = BF16
HALO = 16
PROJ_SPLIT = 4
N_PAD = -(-(OFF_CC + N_CODES) // (PROJ_SPLIT * LANE)) * PROJ_SPLIT * LANE
PROJ_TN = N_PAD // PROJ_SPLIT
MXU_N = 256
WPERM_TN = 256

CHUNK_A = 128
GROUP_A = 8
HEADS_A = 2
CHUNK_B = 256
CHUNK_C = 64
PAIRS_C = 4
VMEM_LIMIT = 48 * 1024 * 1024
PROJ_VMEM_LIMIT = 56 * 1024 * 1024

NT_DIMS = (((1,), (1,)), ((), ()))
TN_DIMS = (((0,), (0,)), ((), ()))


def _mm(a, b):
    return jnp.dot(a.astype(BF16), b.astype(BF16), preferred_element_type=F32)


def _mm_nt(a, b):
    return lax.dot_general(a.astype(BF16), b.astype(BF16), NT_DIMS, preferred_element_type=F32)


def _mm_tn(a, b):
    return lax.dot_general(a.astype(BF16), b.astype(BF16), TN_DIMS, preferred_element_type=F32)


def _split_dot(ones_mat, x):
    hi = x.astype(BF16)
    lo = (x - hi.astype(F32)).astype(BF16)
    return jnp.dot(ones_mat, hi, preferred_element_type=F32) + jnp.dot(ones_mat, lo, preferred_element_type=F32)


def _split_dot_right(x, ones_mat):
    hi = x.astype(BF16)
    lo = (x - hi.astype(F32)).astype(BF16)
    return jnp.dot(hi, ones_mat, preferred_element_type=F32) + jnp.dot(lo, ones_mat, preferred_element_type=F32)


def _f32(z):
    return z.astype(F32)


def _sigmoid(z):
    return 0.5 * jnp.tanh(0.5 * z) + 0.5


def _cumsum_rows(x, rev):
    n = x.shape[0]
    rows = lax.broadcasted_iota(jnp.int32, (n, 1), 0)
    step = 1
    while step < n:
        if rev:
            x = x + jnp.where(rows < n - step, pltpu.roll(x, n - step, axis=0), 0.0)
        else:
            x = x + jnp.where(rows >= step, pltpu.roll(x, step, axis=0), 0.0)
        step *= 2
    return x


def _silu(z):
    h = 0.5 * z
    return h * jnp.tanh(h) + h


def _pick(n, pref):
    t = min(n, pref)
    assert n % t == 0, (n, pref)
    return t


def _params(*sem, vmem=VMEM_LIMIT):
    return pltpu.CompilerParams(dimension_semantics=sem, vmem_limit_bytes=vmem)


def _inproj_kernel(x_ref, g_ref, w_ref, o_ref, h_ref):
    @pl.when(pl.program_id(1) == 0)
    def _():
        x = x_ref[...]
        ms = jnp.mean(x * x, axis=-1, keepdims=True)
        h_ref[...] = (x * lax.rsqrt(ms + EPS) * g_ref[...]).astype(BF16)

    def tile(cols):
        o_ref[:, cols] = jnp.dot(h_ref[...], w_ref[:, cols], preferred_element_type=F32).astype(o_ref.dtype)

    for start in range(0, PROJ_TN, MXU_N):
        tile(pl.ds(start, min(MXU_N, PROJ_TN - start)))


def _inproj(x2, g, w_bf16):
    m = x2.shape[0]
    tm = _pick(m, 1024)
    return pl.pallas_call(
        _inproj_kernel,
        grid=(m // tm, N_PAD // PROJ_TN),
        in_specs=[
            pl.BlockSpec((tm, D_MODEL), lambda i, j: (i, 0)),
            pl.BlockSpec((1, D_MODEL), lambda i, j: (0, 0)),
            pl.BlockSpec((D_MODEL, PROJ_TN), lambda i, j: (0, j)),
        ],
        out_specs=pl.BlockSpec((tm, PROJ_TN), lambda i, j: (i, j)),
        out_shape=jax.ShapeDtypeStruct((m, N_PAD), PROJ_DTYPE),
        scratch_shapes=[pltpu.VMEM((tm, D_MODEL), BF16)],
        compiler_params=_params("parallel", "arbitrary", vmem=PROJ_VMEM_LIMIT),
    )(x2, g.reshape(1, D_MODEL), w_bf16)


def _wperm_kernel(nvalid, a_ref, b_ref, o_ref):
    j = pl.program_id(0)
    for idx, src in enumerate((a_ref, b_ref)):
        cols = pl.ds(idx * WPERM_TN, WPERM_TN)
        blk = src[...].astype(BF16)
        o_ref[:, cols] = jnp.where(2 * j + idx < nvalid, blk, jnp.zeros_like(blk))


def _src_block(jd):
    n_merge = (N_IN - SRC_MERGE) // WPERM_TN
    n_ab = SRC_CS // WPERM_TN
    n_cg = (SRC_MERGE - SRC_CG) // WPERM_TN
    n_src = N_IN // WPERM_TN
    return jnp.where(jd < n_merge, jd + SRC_MERGE // WPERM_TN,
                     jnp.where(jd < n_merge + n_ab, jd - n_merge,
                               jnp.where(jd < n_merge + n_ab + n_cg, jd - n_merge - n_ab + SRC_CG // WPERM_TN,
                                         jnp.minimum(jd - n_merge - n_cg, n_src - 1))))


def _permute_w_in(w_in, layer):
    assert all(off % WPERM_TN == 0 for off in (SRC_CS, SRC_CG, SRC_MERGE, N_IN)) and N_PAD % (2 * WPERM_TN) == 0

    def src(idx):
        return pl.BlockSpec((None, D_MODEL, WPERM_TN), lambda j: (layer, 0, _src_block(2 * j + idx)))

    return pl.pallas_call(
        functools.partial(_wperm_kernel, N_IN // WPERM_TN),
        grid=(N_PAD // (2 * WPERM_TN),),
        in_specs=[src(0), src(1)],
        out_specs=pl.BlockSpec((D_MODEL, 2 * WPERM_TN), lambda j: (0, j)),
        out_shape=jax.ShapeDtypeStruct((D_MODEL, N_PAD), BF16),
        compiler_params=_params("parallel"),
    )(w_in, w_in)


def _each(fn, *lists):
    return [fn(*items) for items in zip(*lists)]


def _hgrn_kernel(rev, final, tt, q_ref, f_ref, v_ref, lb_ref, tri_ref, *rest):
    if final:
        ob_ref, gate_ref, ng_ref, y_ref, st_ref = rest
    else:
        o_ref, st_ref = rest

    @pl.when(pl.program_id(2) == 0)
    def _():
        st_ref[...] = jnp.zeros_like(st_ref)

    n = CHUNK_A
    ri = lax.broadcasted_iota(jnp.int32, (n, n), 0)
    ci = lax.broadcasted_iota(jnp.int32, (n, n), 1)
    g_bits = int(np.log2(GROUP_A))
    diag_mask = ((ri >> g_bits) == (ci >> g_bits)) & ((ci >= ri) if rev else (ci <= ri))
    halves = [GROUP_A << i for i in range(int(np.log2(n // GROUP_A)))]
    level_masks = []
    for half in halves:
        hb = int(np.log2(half))
        q_side, k_side = (0, 1) if rev else (1, 0)
        level_masks.append(((ri >> (hb + 1)) == (ci >> (hb + 1)))
                           & (((ri >> hb) & 1) == q_side) & (((ci >> hb) & 1) == k_side))

    def ref_rows(b, blk, idx):
        picked = b.reshape(n // blk, blk, DKA)[:, idx:idx + 1, :]
        return jnp.broadcast_to(picked, (n // blk, blk, DKA)).reshape(n, DKA)

    nchunk = tt // n
    order = [(nchunk - 1 - i) if rev else i for i in range(nchunk)]
    items = [(pl.ds(c * n, n), pl.ds(hh * DKA, DKA), hh) for c in order for hh in range(HEADS_A)]
    sls = [it[0] for it in items]
    cls = [it[1] for it in items]
    c0 = [0.5 + 0.5 * lb_ref[it[2]] for it in items]
    c1 = [0.5 - 0.5 * lb_ref[it[2]] for it in items]
    gm = _each(lambda sl, cl, c: c * jnp.tanh(0.5 * _f32(f_ref[0, sl, cl])), sls, cls, c1)
    lf = _each(lambda m, c: jnp.log2(jnp.maximum(c + m, F_MIN)), gm, c0)
    k = _each(lambda m, c: c - m, gm, c1)
    q = _each(lambda sl, cl: _silu(_f32(q_ref[0, sl, cl])), sls, cls)
    v = _each(lambda sl, cl: v_ref[0, sl, cl], sls, cls)
    b = _each(lambda z: _split_dot(tri_ref[...], z), lf)
    d0 = _each(lambda z: z - ref_rows(z, GROUP_A, GROUP_A // 2 if rev else GROUP_A // 2 - 1), b)
    qb = _each(lambda z: z.astype(BF16), q)
    kb = _each(lambda z: z.astype(BF16), k)
    scores = _each(lambda qq, kk, d: jnp.where(
        diag_mask, _mm_nt(qq * jnp.exp2(d).astype(BF16), kk * jnp.exp2(-d).astype(BF16)), 0.0), qb, kb, d0)
    for half, mask in zip(halves, level_masks):
        e = _each(lambda z: jnp.exp2(-jnp.abs(z - ref_rows(z, 2 * half, half if rev else half - 1))).astype(BF16),
                  b)
        scores = _each(lambda sc, qq, kk, ee: jnp.where(mask, _mm_nt(qq * ee, kk * ee), sc), scores, qb, kb, e)
    o_intra = _each(_mm, scores, v)
    b_end = _each(lambda z: z[0:1, :] if rev else z[n - 1:n, :], b)
    q_in = _each(lambda qq, z: qq * jnp.exp2(z), q, b)
    kv = _each(lambda vv, kk, z, ze: _mm_tn(vv, kk * jnp.exp2(ze - z)), v, k, b, b_end)
    e_end = _each(jnp.exp2, b_end)

    outs = []
    st = [st_ref[hh] for hh in range(HEADS_A)]
    for (sl, cl, hh), oi, qi, kvi, ee in zip(items, o_intra, q_in, kv, e_end):
        outs.append((sl, cl, hh, oi + _mm_nt(qi, st[hh])))
        st[hh] = st[hh] * ee + kvi
    for hh in range(HEADS_A):
        st_ref[hh] = st[hh]

    for sl, cl, hh, o in outs:
        if final:
            o = o + ob_ref[0, sl, cl]
            ms = jnp.mean(o * o, axis=-1, keepdims=True)
            y = o * lax.rsqrt(ms + EPS) * ng_ref[hh] * _silu(_f32(gate_ref[0, sl, cl]))
            y_ref[0, sl, cl] = y.astype(y_ref.dtype)
        else:
            o_ref[0, sl, cl] = o


def _tri_blockdiag(n, blk, rev):
    i = np.arange(n)
    same = (i[:, None] // blk) == (i[None, :] // blk)
    tri = (i[None, :] >= i[:, None]) if rev else (i[None, :] <= i[:, None])
    return jnp.asarray((same & tri).astype(np.float32), dtype=BF16)


def _hgrn_pass(proj3, lb, rev, final, ob=None, norm_g=None):
    b, t, _ = proj3.shape
    tt = _pick(t, 2048)
    nt = t // tt
    tmap = (lambda i: nt - 1 - i) if rev else (lambda i: i)

    wblk = HEADS_A * DKA

    def col(off):
        return pl.BlockSpec((1, tt, wblk), lambda bi, h, ti: (bi, tmap(ti), off // wblk + h))

    head_vec = pl.BlockSpec((HEADS_A, 1, DKA), lambda bi, h, ti: (h, 0, 0))
    in_specs = [col(OFF_AQ), col(OFF_AFB if rev else OFF_AFF), col(OFF_AI), head_vec,
                pl.BlockSpec((LANE, LANE), lambda bi, h, ti: (0, 0))]
    args = [proj3, proj3, proj3, lb.reshape(HA, 1, DKA), _tri_blockdiag(LANE, CHUNK_A, rev)]
    out_block = pl.BlockSpec((1, tt, wblk), lambda bi, h, ti: (bi, tmap(ti), h))
    if final:
        in_specs += [out_block, col(OFF_AG), head_vec]
        args += [ob, proj3, norm_g.reshape(HA, 1, DKA)]
        out_dtype = BF16
    else:
        out_dtype = F32
    return pl.pallas_call(
        functools.partial(_hgrn_kernel, rev, final, tt),
        grid=(b, HA // HEADS_A, nt),
        in_specs=in_specs,
        out_specs=out_block,
        out_shape=jax.ShapeDtypeStruct((b, t, WA), out_dtype),
        scratch_shapes=[pltpu.VMEM((HEADS_A, DKA, DKA), F32)],
        compiler_params=_params("parallel", "parallel", "arbitrary"),
    )(*args)


def _hgrn_mixer(proj3, lbs, norm_g):
    ob = _hgrn_pass(proj3, lbs[1], rev=True, final=False)
    return _hgrn_pass(proj3, lbs[0], rev=False, final=True, ob=ob, norm_g=norm_g)


def _rotary(z, cos2, sin2):
    return z * cos2 + pltpu.roll(z, DKB // 2, axis=1) * sin2


def _ret_state_kernel(nsub, k_ref, v_ref, cos_ref, sin_ref, lg_ref, sb_ref, st_ref):
    @pl.when(pl.program_id(2) == 0)
    def _():
        st_ref[...] = jnp.zeros_like(st_ref)

    cb = CHUNK_B
    lg2 = lg_ref[0]
    lg = lg2[:, :DKB]
    pos = lax.broadcasted_iota(jnp.int32, (cb, DKB), 0).astype(F32)
    dec = jnp.exp(pos * lg) * (DKB ** -0.5)
    order = list(range(nsub - 1, -1, -1))
    sls = [pl.ds(c * cb, cb) for c in order]
    kd = _each(lambda sl: _rotary(_f32(k_ref[0, sl, :]), cos_ref[sl, :], sin_ref[sl, :]) * dec, sls)
    kv = _each(lambda kk, sl: _mm_tn(kk, v_ref[0, sl, :]), kd, sls)
    e_chunk = jnp.exp(cb * lg2)
    st = st_ref[...]
    for c, kvi in zip(order, kv):
        sb_ref[0, 0, c] = st
        st = st * e_chunk + kvi
    st_ref[...] = st


def _ret_out_kernel(nsub, q_ref, k_ref, v_ref, cos_ref, sin_ref, lg_ref, sb_ref, gate_ref, gg_ref,
                    gb_ref, y_ref, st_ref):
    @pl.when(pl.program_id(2) == 0)
    def _():
        st_ref[...] = jnp.zeros_like(st_ref)

    cb = CHUNK_B
    lg2 = lg_ref[0]
    lg = lg2[:, :DKB]
    pos = lax.broadcasted_iota(jnp.int32, (cb, DKB), 0).astype(F32)
    ri = lax.broadcasted_iota(jnp.int32, (cb, cb), 0)
    ci = lax.broadcasted_iota(jnp.int32, (cb, cb), 1)
    decay = jnp.exp(jnp.abs(ri - ci).astype(F32) * lg2) * (DKB ** -0.5)
    dec_fwd = jnp.exp((pos + 1.0) * lg)
    dec_bwd = jnp.exp((cb - pos) * lg)
    dec_key = jnp.exp((cb - 1.0 - pos) * lg) * (DKB ** -0.5)
    e_chunk = jnp.exp(cb * lg2)

    sls = [pl.ds(c * cb, cb) for c in range(nsub)]
    q = _each(lambda sl: _rotary(_f32(q_ref[0, sl, :]), cos_ref[sl, :], sin_ref[sl, :]), sls)
    k = _each(lambda sl: _rotary(_f32(k_ref[0, sl, :]), cos_ref[sl, :], sin_ref[sl, :]), sls)
    v = _each(lambda sl: v_ref[0, sl, :], sls)
    scores = _each(lambda qq, kk: _mm_nt(qq, kk) * decay, q, k)
    o = _each(_mm, scores, v)
    o = _each(lambda oo, qq, c: oo + _mm(qq * dec_bwd, sb_ref[0, 0, c]), o, q, list(range(nsub)))
    kv = _each(lambda kk, vv: _mm_tn(kk * dec_key, vv), k, v)
    q_fwd = _each(lambda qq: qq * dec_fwd, q)
    st = st_ref[...]
    outs = []
    for oo, qf, kvi in zip(o, q_fwd, kv):
        outs.append(oo + _mm(qf, st))
        st = st * e_chunk + kvi
    st_ref[...] = st

    for sl, oo in zip(sls, outs):
        mu = jnp.mean(oo, axis=-1, keepdims=True)
        cen = oo - mu
        var = jnp.mean(cen * cen, axis=-1, keepdims=True)
        y = (cen * lax.rsqrt(var + RET_GN_EPS) * gg_ref[0] + gb_ref[0]) * _silu(_f32(gate_ref[0, sl, :]))
        y_ref[0, sl, :] = y.astype(y_ref.dtype)


def _ret_mixer(proj3, gn_g, gn_b):
    b, t, _ = proj3.shape
    cb = CHUNK_B
    tt = _pick(t, 16 * cb)
    nsub = tt // cb
    nt = t // tt
    half = DKB // 2
    inv = ROPE_BASE ** (-jnp.arange(0, DKB, 2, dtype=F32) / DKB)
    ang_hi = (jnp.arange(t // cb, dtype=F32) * cb)[:, None, None] * inv
    ang_lo = jnp.arange(cb, dtype=F32)[None, :, None] * inv
    cos = (jnp.cos(ang_hi) * jnp.cos(ang_lo) - jnp.sin(ang_hi) * jnp.sin(ang_lo)).reshape(t, half)
    sin = (jnp.sin(ang_hi) * jnp.cos(ang_lo) + jnp.cos(ang_hi) * jnp.sin(ang_lo)).reshape(t, half)
    cos2 = jnp.concatenate([cos, cos], axis=-1)
    sin2 = jnp.concatenate([-sin, sin], axis=-1)
    log_gamma = jnp.log1p(-jnp.exp2(-5.0 - jnp.arange(HB, dtype=F32)))
    lg = jnp.broadcast_to(log_gamma[:, None, None], (HB, 1, DVB))

    def col(off, width, tmap):
        return pl.BlockSpec((1, tt, width), lambda bi, h, c: (bi, tmap(c), off // width + h))

    def tab(tmap):
        return pl.BlockSpec((tt, DKB), lambda bi, h, c: (tmap(c), 0))

    lg_spec = pl.BlockSpec((1, 1, DVB), lambda bi, h, c: (h, 0, 0))
    rmap = lambda c: nt - 1 - c
    fmap = lambda c: c

    sb = pl.pallas_call(
        functools.partial(_ret_state_kernel, nsub),
        grid=(b, HB, nt),
        in_specs=[col(OFF_BK, DKB, rmap), col(OFF_BV, DVB, rmap), tab(rmap), tab(rmap), lg_spec],
        out_specs=pl.BlockSpec((1, 1, nsub, DKB, DVB), lambda bi, h, c: (bi, h, rmap(c), 0, 0)),
        out_shape=jax.ShapeDtypeStruct((b, HB, t // cb, DKB, DVB), F32),
        scratch_shapes=[pltpu.VMEM((DKB, DVB), F32)],
        compiler_params=_params("parallel", "parallel", "arbitrary"),
    )(proj3, proj3, cos2, sin2, lg)

    head_vec = pl.BlockSpec((1, 1, DVB), lambda bi, h, c: (h, 0, 0))
    return pl.pallas_call(
        functools.partial(_ret_out_kernel, nsub),
        grid=(b, HB, nt),
        in_specs=[col(OFF_BQ, DKB, fmap), col(OFF_BK, DKB, fmap), col(OFF_BV, DVB, fmap),
                  tab(fmap), tab(fmap), lg_spec,
                  pl.BlockSpec((1, 1, nsub, DKB, DVB), lambda bi, h, c: (bi, h, c, 0, 0)),
                  col(OFF_BG, DVB, fmap), head_vec, head_vec],
        out_specs=pl.BlockSpec((1, tt, DVB), lambda bi, h, c: (bi, c, h)),
        out_shape=jax.ShapeDtypeStruct((b, t, WB_V), BF16),
        scratch_shapes=[pltpu.VMEM((DKB, DVB), F32)],
        compiler_params=_params("parallel", "parallel", "arbitrary"),
    )(proj3, proj3, proj3, cos2, sin2, lg, sb, proj3,
      gn_g.reshape(HB, 1, DVB), gn_b.reshape(HB, 1, DVB))


def _group_sum(x, gmat):
    cols = [_split_dot_right(x[:, i * LANE:(i + 1) * LANE], gmat) for i in range(x.shape[1] // LANE)]
    return jnp.concatenate(cols, axis=1)


def _rwkv_prep_kernel(tt, r_ref, k_ref, v_ref, c_ref, rp_ref, kp_ref, vp_ref, cp_ref,
                      rn_ref, kn_ref, vn_ref, cn_ref, mur_ref, muk_ref, muv_ref, muc_ref,
                      w0_ref, wl_ref, a0_ref, al_ref, kk_ref, ka_ref, rk_ref, gmat_ref, shift_ref,
                      r_o, v_o, kk_o, bonus_o, lw0_o, lw1_o, k0_o, k1_o, ka0_o, ka1_o):
    ti = pl.program_id(1)
    has_prev = (ti > 0).astype(F32)
    has_next = (ti < pl.num_programs(1) - 1).astype(F32)
    rows = lax.broadcasted_iota(jnp.int32, (tt, 1), 0)

    def shifted(p_ref, prev_ref, next_ref, mu_ref):
        both = jnp.dot(shift_ref[...], p_ref[0].astype(BF16), preferred_element_type=F32)
        p = _f32(p_ref[0])
        prev_row = _f32(prev_ref[0, HALO - 1:HALO, :]) * has_prev
        next_row = _f32(next_ref[0, 0:1, :]) * has_next
        before = jnp.where(rows == 0, prev_row, both[:tt])
        after = jnp.where(rows == tt - 1, next_row, both[tt:])
        return p + mu_ref[0:1, :] * (before - p) + mu_ref[1:2, :] * (after - p)

    r = shifted(r_ref, rp_ref, rn_ref, mur_ref)
    k = shifted(k_ref, kp_ref, kn_ref, muk_ref)
    v = shifted(v_ref, vp_ref, vn_ref, muv_ref)
    codes = shifted(c_ref, cp_ref, cn_ref, muc_ref)
    gmat = gmat_ref[...]

    kraw = k * kk_ref[...]
    kk = kraw * lax.rsqrt(_group_sum(kraw * kraw, gmat) + KK_NORM_EPS)
    r_o[0] = r.astype(r_o.dtype)
    v_o[0] = v.astype(v_o.dtype)
    kk_o[0] = kk.astype(kk_o.dtype)
    tcodes = jnp.tanh(codes)
    bonus = jnp.zeros_like(r)
    half_kk = 0.5 * kk
    e1 = 0.5 * ka_ref[...]
    e0 = 1.0 - e1
    for d, (lw_o, kd_o, ka_o) in enumerate(((lw0_o, k0_o, ka0_o), (lw1_o, k1_o, ka1_o))):
        w = w0_ref[d:d + 1, :] + _mm(tcodes, wl_ref[d])
        half_c = np.float32(-0.5 * np.exp(-0.5) * np.log2(np.e))
        lw_o[0] = half_c * jnp.tanh(0.5 * w) + half_c
        th = jnp.tanh(0.5 * (a0_ref[d:d + 1, :] + _mm(codes, al_ref[d])))
        kd = k * (e0 + e1 * th)
        kd_o[0] = kd.astype(kd_o.dtype)
        ka_o[0] = (half_kk * th + half_kk).astype(ka_o.dtype)
        bonus = bonus + _group_sum(r * kd * rk_ref[...], gmat) * v
    bonus_o[0] = bonus


def _rwkv_scan_kernel(rev, final, tt, r_ref, lw_ref, k_ref, v_ref, ka_ref, kk_ref, *rest):
    if final:
        ob_ref, bonus_ref, gate_ref, lng_ref, lnb_ref, y_ref, h_ref = rest
    else:
        o_ref, h_ref = rest

    @pl.when(pl.program_id(1) == 0)
    def _():
        h_ref[...] = jnp.zeros_like(h_ref)

    n2 = 2 * CHUNK_C
    lane = lax.broadcasted_iota(jnp.int32, (1, LANE), 1)
    head0 = lane < NC
    ri = lax.broadcasted_iota(jnp.int32, (n2, n2), 0)
    ci = lax.broadcasted_iota(jnp.int32, (n2, n2), 1)
    strict = (ci > ri) if rev else (ci < ri)
    incl = (ci >= ri) if rev else (ci <= ri)
    eye = ri == ci
    nchunk = tt // CHUNK_C

    def stack(z):
        zero = jnp.zeros_like(z)
        return jnp.concatenate([jnp.where(head0, z, zero), jnp.where(head0, zero, z)], axis=0)

    def stack_b(z):
        return stack(z.astype(BF16))

    order = [(nchunk - 1 - i) if rev else i for i in range(nchunk)]
    nb = r_ref.shape[0]
    items = [((bi, pl.ds(c * CHUNK_C, CHUNK_C)), pl.ds(p * LANE, LANE), bi * PAIRS_C + p)
             for c in order for bi in range(nb) for p in range(PAIRS_C)]
    rows = [it[0] for it in items]
    cols = [it[1] for it in items]

    lws = _each(lambda sl, cl: lw_ref[sl[0], sl[1], cl], rows, cols)
    g = _each(lambda lw: _cumsum_rows(lw, rev), lws)
    e_end = _each(lambda gu: jnp.exp2(gu[0:1, :] if rev else gu[CHUNK_C - 1:CHUNK_C, :]), g)
    e_neg = _each(lambda gu: jnp.exp2(-gu), g)
    bs = _each(lambda sl, cl, gu, lw: stack_b(-_f32(kk_ref[sl[0], sl[1], cl]) * jnp.exp2(gu - lw)), rows, cols, g, lws)
    a_sc = _each(lambda sl, cl, en: _f32(ka_ref[sl[0], sl[1], cl]) * en, rows, cols, e_neg)
    k_sc = _each(lambda sl, cl, en: _f32(k_ref[sl[0], sl[1], cl]) * en, rows, cols, e_neg)
    as_ = _each(stack_b, a_sc)
    ks = _each(stack_b, k_sc)
    ake = _each(lambda a, k, e: jnp.concatenate([stack_b(a * e), stack_b(k * e)], axis=0), a_sc, k_sc, e_end)
    rs = _each(lambda sl, cl, gu: stack(_f32(r_ref[sl[0], sl[1], cl]) * jnp.exp2(gu)), rows, cols, g)
    vs = _each(lambda sl, cl: stack_b(v_ref[sl[0], sl[1], cl]), rows, cols)

    ak = _each(lambda a, k: jnp.concatenate([a, k], axis=0), as_, ks)
    sc_b = _each(_mm_nt, bs, ak)
    sc_r = _each(_mm_nt, rs, ak)
    a_ab = _each(lambda z: jnp.where(strict, z[:, :n2], 0.0), sc_b)
    a_ak = _each(lambda z: jnp.where(strict, z[:, n2:], 0.0), sc_b)
    m_rak = _each(lambda z: jnp.where(jnp.concatenate([incl, incl], axis=1), z, 0.0), sc_r)
    tinv = _each(lambda a: jnp.where(eye, 1.0, 0.0) + a, a_ab)
    pw = _each(lambda a: _mm(a, a), a_ab)
    akv = _each(_mm, a_ak, vs)
    nsteps = int(np.log2(CHUNK_C)) - 1
    for i in range(nsteps):
        if i < nsteps - 1:
            both = _each(lambda t, p: _mm(jnp.concatenate([t, p], axis=0), p), tinv, pw)
            tinv = _each(lambda t, b: t + b[:n2], tinv, both)
            pw = _each(lambda b: b[n2:], both)
        else:
            tinv = _each(lambda t, p: t + _mm(t, p), tinv, pw)
    pwv = _each(lambda t, b, x: _mm(t, jnp.concatenate([b, x.astype(BF16)], axis=1)), tinv, bs, akv)
    low = _each(lambda z, v: jnp.concatenate([z.astype(BF16), jnp.concatenate([jnp.zeros_like(v), v], axis=1)],
                                             axis=0), pwv, vs)
    x1 = _each(_mm, m_rak, low)
    x2 = _each(_mm_tn, ake, low)
    top = _each(lambda r, z, e, y: jnp.concatenate([jnp.where(eye, e, 0.0) + y[:, :LANE], r + z[:, :LANE]], axis=0),
                rs, x1, e_end, x2)
    o_in = _each(lambda z: z[:, LANE:], x1)
    dmat = _each(lambda y: y[:, LANE:], x2)

    outs = []
    h = [h_ref[p] for p in range(nb * PAIRS_C)]
    for (sl, cl, p), tp, oi, dm in zip(items, top, o_in, dmat):
        both = _mm(tp, h[p])
        h[p] = both[:n2] + dm
        os_ = both[n2:] + oi
        outs.append((sl, cl, os_[:CHUNK_C, :] + os_[CHUNK_C:, :]))
    for p in range(nb * PAIRS_C):
        h_ref[p] = h[p]

    for sl, cl, o in outs:
        if final:
            o = o + ob_ref[sl[0], sl[1], cl]
            inv_n = 1.0 / NC
            s0 = jnp.sum(jnp.where(head0, o, 0.0), axis=-1, keepdims=True)
            s1 = jnp.sum(jnp.where(head0, 0.0, o), axis=-1, keepdims=True)
            cen = o - jnp.where(head0, s0, s1) * inv_n
            c2 = cen * cen
            v0 = jnp.sum(jnp.where(head0, c2, 0.0), axis=-1, keepdims=True)
            v1 = jnp.sum(jnp.where(head0, 0.0, c2), axis=-1, keepdims=True)
            var = jnp.where(head0, v0, v1) * inv_n
            y = cen * lax.rsqrt(var + RWKV_GN_EPS) * lng_ref[:, cl] + lnb_ref[:, cl] + bonus_ref[sl[0], sl[1], cl]
            y_ref[sl[0], sl[1], cl] = (y * _silu(_f32(gate_ref[sl[0], sl[1], cl]))).astype(y_ref.dtype)
        else:
            o_ref[sl[0], sl[1], cl] = o


def _rwkv_mixer(proj3, mu, w0, w_lora_b, a0, a_lora_b, k_k, k_a, r_k, ln_g, ln_b):
    b, t, _ = proj3.shape
    tt = _pick(t, 256)
    nt = t // tt
    sub = tt // HALO

    def cur(off, width):
        return pl.BlockSpec((1, tt, width), lambda bi, ti: (bi, ti, off // width))

    def prev(off, width):
        return pl.BlockSpec((1, HALO, width),
                            lambda bi, ti: (bi, jnp.maximum(ti * sub - 1, 0), off // width))

    def nxt(off, width):
        return pl.BlockSpec((1, HALO, width),
                            lambda bi, ti: (bi, jnp.minimum((ti + 1) * sub, nt * sub - 1), off // width))

    def full(shape):
        return pl.BlockSpec(shape, lambda bi, ti: (0,) * len(shape))

    segs = ((OFF_CR, WC), (OFF_CK, WC), (OFF_CV, WC), (OFF_CC, N_CODES))
    in_specs = ([cur(o, w) for o, w in segs] + [prev(o, w) for o, w in segs] + [nxt(o, w) for o, w in segs]
                + [full((2, WC))] * 3 + [full((2, N_CODES))]
                + [full((2, WC)), full((2, N_CODES, WC)), full((2, WC)), full((2, N_CODES, WC))]
                + [full((1, WC))] * 3 + [full((LANE, LANE)), full((2 * tt, tt))])
    wl = jnp.zeros((2, N_CODES, WC), F32)
    al = jnp.zeros((2, N_CODES, WC), F32)
    for d in range(2):
        wl = wl.at[d, d * W_LORA:(d + 1) * W_LORA].set(w_lora_b[d])
        al = al.at[d, 2 * W_LORA + d * A_LORA:2 * W_LORA + (d + 1) * A_LORA].set(a_lora_b[d])
    row_i = np.arange(tt)
    shift = np.concatenate([row_i[:, None] - 1 == row_i[None, :], row_i[:, None] + 1 == row_i[None, :]], axis=0)
    shift = jnp.asarray(shift.astype(np.float32), dtype=BF16)
    lane_i = np.arange(LANE)
    gmat = jnp.asarray((lane_i[:, None] // NC == lane_i[None, :] // NC).astype(np.float32), dtype=BF16)
    row_out = pl.BlockSpec((1, tt, WC), lambda bi, ti: (bi, ti, 0))
    outs = pl.pallas_call(
        functools.partial(_rwkv_prep_kernel, tt),
        grid=(b, nt),
        in_specs=in_specs,
        out_specs=[row_out] * 10,
        out_shape=[jax.ShapeDtypeStruct((b, t, WC), dt) for dt in (SCAN_DTYPE,) * 3 + (F32,) * 3 + (SCAN_DTYPE,) * 4],
        compiler_params=_params("parallel", "parallel"),
    )(*([proj3] * 12), mu[:, :WC], mu[:, WC:2 * WC], mu[:, 2 * WC:3 * WC], mu[:, 3 * WC:],
      w0, wl.astype(BF16), a0, al.astype(BF16), k_k.reshape(1, WC), k_a.reshape(1, WC),
      r_k.reshape(1, WC), gmat, shift)
    r, v, kk, bonus, lw0, lw1, k0, k1, ka0, ka1 = outs

    ts = _pick(t, 256)
    ns = t // ts

    def scan(rev, final, lw, kd, ka, ob=None):
        wblk = PAIRS_C * LANE
        tmap = (lambda i: ns - 1 - i) if rev else (lambda i: i)
        blk = pl.BlockSpec((b, ts, wblk), lambda h, ti: (0, tmap(ti), h))
        head_vec = pl.BlockSpec((1, wblk), lambda h, ti: (0, h))
        in_specs = [blk] * 6
        args = [r, lw, kd, v, ka, kk]
        if final:
            gate = pl.BlockSpec((b, ts, wblk), lambda h, ti: (0, tmap(ti), OFF_CG // wblk + h))
            in_specs += [blk, blk, gate, head_vec, head_vec]
            args += [ob, bonus, proj3, ln_g.reshape(1, WC), ln_b.reshape(1, WC)]
        return pl.pallas_call(
            functools.partial(_rwkv_scan_kernel, rev, final, ts),
            grid=(WC // wblk, ns),
            in_specs=in_specs,
            out_specs=blk,
            out_shape=jax.ShapeDtypeStruct((b, t, WC), BF16 if final else F32),
            scratch_shapes=[pltpu.VMEM((b * PAIRS_C, LANE, LANE), F32)],
            compiler_params=_params("parallel", "arbitrary"),
        )(*args)

    ob = scan(True, False, lw1, k1, ka1)
    return scan(False, True, lw0, k0, ka0, ob=ob)


def _merge_kernel(final, ya_ref, yb_ref, yc_ref, mg_ref, x_ref, wa_ref, wb_ref, wc_ref, wo_ref,
                  fg_ref, o_ref):
    za = jnp.dot(ya_ref[...], wa_ref[...], preferred_element_type=F32)
    zb = jnp.dot(yb_ref[...], wb_ref[...], preferred_element_type=F32)
    zc = jnp.dot(yc_ref[...], wc_ref[...], preferred_element_type=F32)
    mg = _f32(mg_ref[...])
    mixed = (_sigmoid(mg[:, :D_MODEL]) * za
             + _sigmoid(mg[:, D_MODEL:2 * D_MODEL]) * zb
             + _sigmoid(mg[:, 2 * D_MODEL:]) * zc)
    out = x_ref[...] + jnp.dot(mixed.astype(BF16), wo_ref[...], preferred_element_type=F32)
    if final:
        ms = jnp.mean(out * out, axis=-1, keepdims=True)
        out = out * lax.rsqrt(ms + EPS) * fg_ref[...]
    o_ref[...] = out


def _merge(ya, yb, yc, proj, x2, wa, wb, wc, wo, fg, final):
    m = x2.shape[0]
    tm = _pick(m, 512)

    def rows(width):
        return pl.BlockSpec((tm, width), lambda i: (i, 0))

    def full(shape):
        return pl.BlockSpec(shape, lambda i: (0, 0))

    return pl.pallas_call(
        functools.partial(_merge_kernel, final),
        grid=(m // tm,),
        in_specs=[rows(WA), rows(WB_V), rows(WC), rows(3 * D_MODEL), rows(D_MODEL),
                  full((WA, D_MODEL)), full((WB_V, D_MODEL)), full((WC, D_MODEL)),
                  full((D_MODEL, D_MODEL)), full((1, D_MODEL))],
        out_specs=rows(D_MODEL),
        out_shape=jax.ShapeDtypeStruct((m, D_MODEL), F32),
        compiler_params=_params("parallel", vmem=PROJ_VMEM_LIMIT),
    )(ya, yb, yc, proj, x2, wa.astype(BF16), wb.astype(BF16), wc.astype(BF16), wo.astype(BF16),
      fg.reshape(1, D_MODEL))


def kernel(x, norm_g, w_in, hgrn_lb_logits, hgrn_norm_g, ret_norm_g, ret_norm_b, rwkv_mu, rwkv_w0,
           rwkv_w_lora_b, rwkv_a0, rwkv_a_lora_b, rwkv_k_k, rwkv_k_a, rwkv_r_k, rwkv_ln_g,
           rwkv_ln_b, w_branch_a, w_branch_b, w_branch_c, w_out, final_norm_g):
    b, t, d = x.shape
    m = b * t
    depth = w_in.shape[0]
    p_lb = jax.nn.softmax(hgrn_lb_logits.astype(F32), axis=0)
    lbs = jnp.cumsum(p_lb, axis=0) - p_lb[0:1]
    x2 = x.reshape(m, d)
    for l in range(depth):
        proj = _inproj(x2, norm_g[l], _permute_w_in(w_in, l))
        proj3 = proj.reshape(b, t, N_PAD)
        ya = _hgrn_mixer(proj3, lbs[l], hgrn_norm_g[l])
        yb = _ret_mixer(proj3, ret_norm_g[l], ret_norm_b[l])
        yc = _rwkv_mixer(proj3, rwkv_mu[l], rwkv_w0[l], rwkv_w_lora_b[l], rwkv_a0[l],
                         rwkv_a_lora_b[l], rwkv_k_k[l], rwkv_k_a[l], rwkv_r_k[l],
                         rwkv_ln_g[l], rwkv_ln_b[l])
        x2 = _merge(ya.reshape(m, WA), yb.reshape(m, WB_V), yc.reshape(m, WC), proj, x2,
                    w_branch_a[l], w_branch_b[l], w_branch_c[l], w_out[l], final_norm_g,
                    final=(l == depth - 1))
    return x2.reshape(b, t, d)
```
